```python
import math
import jax
import jax.numpy as jnp
from jax import lax
import numpy as np

D_MODEL = 1024
BATCH = 16
SEQ = 4096
DEPTH = 1
DEC_BATCH = 128
DEC_SEQ = 1
PAST_LEN = 8192
PAGE_SIZE = 128

NSA_HEADS = 8
KV_HEADS = 2
HEAD_DIM = 64
GQA = NSA_HEADS // KV_HEADS
CMP_BLK = 32
CMP_STRIDE = 16
CMP_HID = 64
SEL_BLK = 64
N_SEL = 16
WINDOW = 512
Q_BLOCK = 32
FORCE_CUR = 2.0e4
FORCE_SINK = 1.0e4
SSD_HEADS = 8
SSD_HEAD_DIM = 64
D_INNER = SSD_HEADS * SSD_HEAD_DIM
SSD_GROUPS = 2
D_STATE = 128
CONV_W = 4
CONV_DIM = D_INNER + 2 * SSD_GROUPS * D_STATE
SSD_CHUNK = 128
NSA_DIM = NSA_HEADS * HEAD_DIM
KV_DIM = KV_HEADS * HEAD_DIM
MIX_DIM = NSA_DIM + D_INNER
IN_DIM = NSA_DIM + 6 * KV_DIM + 3 * NSA_HEADS + D_INNER + CONV_DIM + SSD_HEADS
N_EXPERTS = 64
TOP_K = 8
N_EXPERT_GROUPS = 8
TOPK_GROUPS = 4
D_EXPERT = 256
D_SHARED = 256
ROUTED_SCALE = 2.5
MOE_TOKENS = 4096
MOE_BLOCK = 128
MOE_BLOCK_SMALL = 16
RMS_EPS = 1e-6
NEG_INF = -1e30

kernel_name = 'hymba_nsa_ssd_moe_adaln_step'


def rmsnorm(x, g):
    xf = x.astype(jnp.float32)
    y = xf * lax.rsqrt(jnp.mean(xf * xf, axis=-1, keepdims=True) + RMS_EPS)
    return (y * g.astype(jnp.float32)).astype(x.dtype)


def masked_softmax(s, mask):
    s = jnp.where(mask, s.astype(jnp.float32), NEG_INF)
    p = jnp.exp(s - jnp.max(s, axis=-1, keepdims=True)) * mask
    return p / jnp.maximum(jnp.sum(p, axis=-1, keepdims=True), 1e-30)


def alibi_slopes():
    h = jnp.arange(1, NSA_HEADS + 1, dtype=jnp.float32)
    return jnp.exp2(-8.0 * h / NSA_HEADS).reshape(KV_HEADS, GQA)


def modulation(c, w_ada, b_ada):
    m = jax.nn.silu(c) @ w_ada + b_ada
    return jnp.split(m[:, None, :], 6, axis=-1)


def split_projection(u):
    b, t = u.shape[:2]
    sizes = [NSA_DIM, 6 * KV_DIM, 3 * NSA_HEADS, D_INNER, CONV_DIM, SSD_HEADS]
    q, kv, g, z, xbc, dt = jnp.split(u, np.cumsum(sizes)[:-1].tolist(), axis=-1)
    q = q.reshape(b, t, KV_HEADS, GQA, HEAD_DIM)
    kv = kv.reshape(b, t, 6, KV_HEADS, HEAD_DIM)
    g = jax.nn.sigmoid(g.astype(jnp.float32)).reshape(b, t, KV_HEADS, GQA, 3)
    return q, kv, g, z, xbc, dt


def compress(rows, pe, w1, w2):
    b, t = rows.shape[:2]
    nh = t // CMP_STRIDE
    halves = rows[:, :nh * CMP_STRIDE].reshape(b, nh, CMP_STRIDE, KV_HEADS, HEAD_DIM)
    lead = jnp.einsum('bnlkd,ldh->bnkh', halves + pe[:CMP_STRIDE, None, :], w1[:CMP_STRIDE])
    tail = jnp.einsum('bnlkd,ldh->bnkh', halves + pe[CMP_STRIDE:, None, :], w1[CMP_STRIDE:])
    hid = jax.nn.silu(lead[:, :-1] + tail[:, 1:])
    return jnp.einsum('bnkh,hd->bnkd', hid, w2)


def nsa_attend(q, gates, qpos, kc, vc, cend, sel_gather, n_sel_blocks, kw, vw, kwpos):
    b, t = q.shape[:2]
    scale = HEAD_DIM ** -0.5
    slopes = alibi_slopes()
    s = jnp.einsum('btkgd,bnkd->bkgtn', q, kc).astype(jnp.float32) * scale
    s = s - slopes[:, :, None, None] * (qpos[:, None] - cend[None, :]).astype(jnp.float32)
    p_cmp = masked_softmax(s, cend[None, :] <= qpos[:, None])
    o_cmp = jnp.einsum('bkgtn,bnkd->btkgd', p_cmp.astype(vc.dtype), vc)
    nc = kc.shape[1]
    cst = jnp.arange(nc)[:, None] * CMP_STRIDE
    sst = jnp.arange(n_sel_blocks)[None, :] * SEL_BLK
    overlap = jnp.clip(jnp.minimum(cst + CMP_BLK, sst + SEL_BLK) - jnp.maximum(cst, sst), 0, None).astype(jnp.float32) / CMP_BLK
    imp = jnp.einsum('bkgtn,ns->bkts', p_cmp, overlap)
    cur = qpos // SEL_BLK
    jblk = jnp.arange(n_sel_blocks)
    imp = jnp.where(jblk[None, :] == cur[:, None], FORCE_CUR, jnp.where(jblk[None, :] == 0, FORCE_SINK, imp))
    imp = jnp.where(jblk[None, :] <= cur[:, None], imp, NEG_INF)
    _, idx = lax.top_k(imp, min(N_SEL, n_sel_blocks))
    ks, vs = sel_gather(idx)
    kpos = idx[..., None] * SEL_BLK + jnp.arange(SEL_BLK)
    qp = qpos[None, None, :, None, None]
    s = jnp.einsum('btkgd,bktjrd->bkgtjr', q, ks).astype(jnp.float32) * scale
    s = s - slopes[None, :, :, None, None, None] * (qp - kpos).astype(jnp.float32)[:, :, None]
    nk = ks.shape[3] * SEL_BLK
    p_sel = masked_softmax(s.reshape(b, KV_HEADS, GQA, t, nk), (kpos <= qp).reshape(b, KV_HEADS, 1, t, nk))
    o_sel = jnp.einsum('bkgtm,bktmd->btkgd', p_sel.astype(vs.dtype), vs.reshape(b, KV_HEADS, t, nk, HEAD_DIM))
    s = jnp.einsum('btkgd,bnkd->bkgtn', q, kw).astype(jnp.float32) * scale
    dist = qpos[:, None] - kwpos[None, :]
    s = s - slopes[:, :, None, None] * dist.astype(jnp.float32)
    p_win = masked_softmax(s, (dist >= 0) & (dist < WINDOW) & (kwpos[None, :] >= 0))
    o_win = jnp.einsum('bkgtn,bnkd->btkgd', p_win.astype(vw.dtype), vw)
    o = gates[..., 0:1] * o_cmp + gates[..., 1:2] * o_sel + gates[..., 2:3] * o_win
    return o.reshape(b, t, NSA_DIM).astype(q.dtype)


def nsa_prompt(q, gates, kv, cmp_k, cmp_v):
    b, t_len = q.shape[:2]
    kc = compress(kv[:, :, 0], *cmp_k)
    vc = compress(kv[:, :, 1], *cmp_v)
    cend = jnp.arange(kc.shape[1]) * CMP_STRIDE + CMP_BLK - 1
    n_blk = t_len // SEL_BLK
    ks_blocks = kv[:, :, 2].reshape(b, n_blk, SEL_BLK, KV_HEADS, HEAD_DIM).transpose(0, 3, 1, 2, 4)
    vs_blocks = kv[:, :, 3].reshape(b, n_blk, SEL_BLK, KV_HEADS, HEAD_DIM).transpose(0, 3, 1, 2, 4)
    bi = jnp.arange(b)[:, None, None, None]
    ki = jnp.arange(KV_HEADS)[None, :, None, None]

    def sel_gather(idx):
        return ks_blocks[bi, ki, idx], vs_blocks[bi, ki, idx]

    pad = ((0, 0), (WINDOW, 0), (0, 0), (0, 0))
    kw_pad = jnp.pad(kv[:, :, 4], pad)
    vw_pad = jnp.pad(kv[:, :, 5], pad)

    def one_block(i):
        t0 = i * Q_BLOCK
        q_i = lax.dynamic_slice_in_dim(q, t0, Q_BLOCK, axis=1)
        g_i = lax.dynamic_slice_in_dim(gates, t0, Q_BLOCK, axis=1)
        kw_i = lax.dynamic_slice_in_dim(kw_pad, t0, WINDOW + Q_BLOCK, axis=1)
        vw_i = lax.dynamic_slice_in_dim(vw_pad, t0, WINDOW + Q_BLOCK, axis=1)
        qpos = t0 + jnp.arange(Q_BLOCK)
        kwpos = t0 - WINDOW + jnp.arange(WINDOW + Q_BLOCK)
        return nsa_attend(q_i, g_i, qpos, kc, vc, cend, sel_gather, n_blk, kw_i, vw_i, kwpos)

    o = lax.map(one_block, jnp.arange(t_len // Q_BLOCK))
    return o.transpose(1, 0, 2, 3).reshape(b, t_len, NSA_DIM)


def nsa_sample(q, gates, kv, pool_cmp, pool_sel, win_buf, page_table, layer, cmp_k, cmp_v):
    b, t = q.shape[:2]
    n_pages = page_table.shape[1]
    past = n_pages * PAGE_SIZE
    qpos = past + jnp.arange(t)
    past_cmp = pool_cmp[layer, page_table].reshape(b, past, 2, KV_HEADS, HEAD_DIM)
    rows_cmp = jnp.concatenate([past_cmp, kv[:, :, 0:2].astype(past_cmp.dtype)], axis=1)
    kc = compress(rows_cmp[:, :, 0], *cmp_k)
    vc = compress(rows_cmp[:, :, 1], *cmp_v)
    cend = jnp.arange(kc.shape[1]) * CMP_STRIDE + CMP_BLK - 1
    past_nb = past // SEL_BLK
    new_nb = -(-t // SEL_BLK)
    sub_per_page = PAGE_SIZE // SEL_BLK
    pool_blocks = pool_sel.reshape(pool_sel.shape[0], pool_sel.shape[1], sub_per_page, SEL_BLK, 2, KV_HEADS, HEAD_DIM)
    new_sel = jnp.pad(kv[:, :, 2:4], ((0, 0), (0, new_nb * SEL_BLK - t), (0, 0), (0, 0), (0, 0)))
    new_blocks = new_sel.reshape(b, new_nb, SEL_BLK, 2, KV_HEADS, HEAD_DIM)
    bi = jnp.arange(b)[:, None, None, None]
    ki = jnp.arange(KV_HEADS)[None, :, None, None]

    def sel_gather(idx):
        is_past = idx < past_nb
        pidx = jnp.minimum(idx, past_nb - 1)
        phys = page_table[bi, pidx // sub_per_page]
        past_rows = pool_blocks[layer, phys, pidx % sub_per_page, :, :, ki, :]
        nidx = jnp.clip(idx - past_nb, 0, new_nb - 1)
        new_rows = new_blocks[bi, nidx, :, :, ki, :]
        rows = jnp.where(is_past[..., None, None, None], past_rows, new_rows.astype(past_rows.dtype))
        return rows[..., 0, :], rows[..., 1, :]

    wb = win_buf.shape[1]
    rows_win = jnp.concatenate([win_buf, kv[:, :, 4:6].astype(win_buf.dtype)], axis=1)
    kwpos = past - wb + jnp.arange(wb + t)
    o = nsa_attend(q, gates, qpos, kc, vc, cend, sel_gather, past_nb + new_nb,
                   rows_win[:, :, 0], rows_win[:, :, 1], kwpos)
    return o, rows_win[:, t:]


def ssd_scan(x, dt, a, bm, cm, h0):
    b, t, h, p = x.shape
    n = bm.shape[-1]
    e = h // SSD_GROUPS
    L = min(SSD_CHUNK, t)
    nc = -(-t // L)
    pad = nc * L - t
    if pad:
        x = jnp.pad(x, ((0, 0), (0, pad), (0, 0), (0, 0)))
        dt = jnp.pad(dt, ((0, 0), (0, pad), (0, 0)))
        bm = jnp.pad(bm, ((0, 0), (0, pad), (0, 0), (0, 0)))
        cm = jnp.pad(cm, ((0, 0), (0, pad), (0, 0), (0, 0)))
    xc = x.reshape(b, nc, L, SSD_GROUPS, e, p)
    dtc = dt.reshape(b, nc, L, SSD_GROUPS, e)
    bc = bm.reshape(b, nc, L, SSD_GROUPS, n)
    cc = cm.reshape(b, nc, L, SSD_GROUPS, n)
    acum = jnp.cumsum(dtc * a.reshape(SSD_GROUPS, e), axis=2)
    seg = acum[:, :, :, None] - acum[:, :, None]
    causal = jnp.tril(jnp.ones((L, L), dtype=bool))[:, :, None, None]
    decay = jnp.where(causal, jnp.exp(jnp.where(causal, seg, 0.0)), 0.0)
    xdt = xc * dtc[..., None]
    cb = jnp.einsum('bclgn,bcsgn->bclsg', cc, bc)
    y_diag = jnp.einsum('bclsg,bclsge,bcsgep->bclgep', cb, decay, xdt)
    to_end = jnp.exp(acum[:, :, -1:] - acum)
    states = jnp.einsum('bclgn,bclge,bclgep->bcgepn', bc, to_end, xdt)
    chunk_decay = jnp.exp(acum[:, :, -1])

    def step(hc, inp):
        st, dec = inp
        return dec[..., None, None] * hc + st, hc

    h_last, h_prev = lax.scan(step, h0.reshape(b, SSD_GROUPS, e, p, n),
                              (states.swapaxes(0, 1), chunk_decay.swapaxes(0, 1)))
    y_off = jnp.einsum('bclgn,bclge,bcgepn->bclgep', cc, jnp.exp(acum), h_prev.swapaxes(0, 1))
    y = (y_diag + y_off).reshape(b, nc * L, h, p)[:, :t]
    return y, h_last.reshape(b, h, p, n)


def ssd_mixer(z, xbc, dt_raw, conv_prev, h0, conv_w, conv_b, dt_bias, a_log, d_skip, ssd_norm):
    b, t = xbc.shape[:2]
    xpad = jnp.concatenate([conv_prev.astype(xbc.dtype), xbc], axis=1)
    conv = conv_b + xpad[:, 0:t] * conv_w[0]
    for k in range(1, CONV_W):
        conv = conv + xpad[:, k:k + t] * conv_w[k]
    u = jax.nn.silu(conv)
    gn = SSD_GROUPS * D_STATE
    xs = u[..., :D_INNER].reshape(b, t, SSD_HEADS, SSD_HEAD_DIM)
    bm = u[..., D_INNER:D_INNER + gn].reshape(b, t, SSD_GROUPS, D_STATE)
    cm = u[..., D_INNER + gn:].reshape(b, t, SSD_GROUPS, D_STATE)
    dt = jax.nn.softplus(dt_raw.astype(jnp.float32) + dt_bias.astype(jnp.float32))
    a = -jnp.exp(a_log.astype(jnp.float32))
    y, h_new = ssd_scan(xs, dt, a, bm, cm, h0.astype(jnp.float32))
    y = y + d_skip[:, None] * xs
    y = rmsnorm(y.reshape(b, t, D_INNER).astype(z.dtype) * jax.nn.silu(z), ssd_norm)
    return y, xpad[:, t:], h_new


def swiglu(x, w1, w3, w2):
    return (jax.nn.silu(x @ w1) * (x @ w3)) @ w2


def routed_experts(x, eidx, wsel, w1, w3, w2):
    n, d = x.shape
    n_assign = n * TOP_K
    blk = MOE_BLOCK if n_assign >= MOE_BLOCK * N_EXPERTS else MOE_BLOCK_SMALL
    n_blocks = n_assign // blk + N_EXPERTS
    flat_e = eidx.reshape(-1)
    order = jnp.argsort(flat_e)
    e_sorted = flat_e[order]
    tok_sorted = order // TOP_K
    counts = jnp.zeros((N_EXPERTS,), jnp.int32).at[flat_e].add(1)
    padded = (counts + blk - 1) // blk * blk
    pstart = jnp.cumsum(padded) - padded
    rank = jnp.arange(n_assign) - (jnp.cumsum(counts) - counts)[e_sorted]
    dest = pstart[e_sorted] + rank
    xd = jnp.zeros((n_blocks * blk, d), x.dtype).at[dest].set(x[tok_sorted])
    block_e = jnp.minimum(jnp.searchsorted(pstart + padded, jnp.arange(n_blocks) * blk, side='right'), N_EXPERTS - 1)
    yd = lax.map(lambda a: swiglu(a[0], w1[a[1]], w3[a[1]], w2[a[1]]), (xd.reshape(n_blocks, blk, d), block_e))
    y = yd.reshape(n_blocks * blk, d)[dest] * wsel.reshape(-1)[order][:, None]
    return jax.ops.segment_sum(y, tok_sorted, num_segments=n)


def moe(h, router_w, router_bias, w1, w3, w2, sw1, sw3, sw2):
    b, t, d = h.shape
    m = b * t
    tok = math.gcd(m, MOE_TOKENS)
    per = N_EXPERTS // N_EXPERT_GROUPS

    def chunk(xc):
        s = jax.nn.sigmoid((xc @ router_w).astype(jnp.float32))
        sb = s + router_bias.astype(jnp.float32)
        gscore = jnp.sum(lax.top_k(sb.reshape(-1, N_EXPERT_GROUPS, per), 2)[0], axis=-1)
        _, gidx = lax.top_k(gscore, TOPK_GROUPS)
        gmask = jnp.any(gidx[:, :, None] == jnp.arange(N_EXPERT_GROUPS), axis=1)
        _, eidx = lax.top_k(jnp.where(jnp.repeat(gmask, per, axis=1), sb, NEG_INF), TOP_K)
        w = jnp.take_along_axis(s, eidx, axis=1)
        w = w / jnp.sum(w, axis=-1, keepdims=True) * ROUTED_SCALE
        return (routed_experts(xc, eidx, w, w1, w3, w2) + swiglu(xc, sw1, sw3, sw2)).astype(xc.dtype)

    return lax.map(chunk, h.reshape(m // tok, tok, d)).reshape(b, t, d)


def setup_inputs(seed: int = 0) -> dict:
    key = jax.random.key(seed)
    ks = iter(jax.random.split(key, 48))
    f32 = jnp.float32

    def nrm(shape, scale=1.0):
        return jax.random.normal(next(ks), shape, f32) * scale

    n_pages = PAST_LEN // PAGE_SIZE
    n_pool = (DEC_BATCH * n_pages * 5) // 4
    win_buf = min(WINDOW, PAST_LEN)
    page_table = jax.random.permutation(next(ks), n_pool)[:DEC_BATCH * n_pages].reshape(DEC_BATCH, n_pages).astype(jnp.int32)
    dt0 = jnp.exp(jax.random.uniform(next(ks), (DEPTH, SSD_HEADS), f32, math.log(1e-3), math.log(1e-1)))
    dt_bias = dt0 + jnp.log(-jnp.expm1(-dt0))
    a_log = jnp.log(jax.random.uniform(next(ks), (DEPTH, SSD_HEADS), f32, 1.0, 16.0))
    return {
        'x_prompt': nrm((BATCH, SEQ, D_MODEL)),
        'x_sample': nrm((DEC_BATCH, DEC_SEQ, D_MODEL)),
        'c_prompt': nrm((BATCH, D_MODEL)),
        'c_sample': nrm((DEC_BATCH, D_MODEL)),
        'cache_kv_cmp': nrm((DEPTH, n_pool, PAGE_SIZE, 2, KV_HEADS, HEAD_DIM)),
        'cache_kv_sel': nrm((DEPTH, n_pool, PAGE_SIZE, 2, KV_HEADS, HEAD_DIM)),
        'cache_kv_win': nrm((DEPTH, DEC_BATCH, win_buf, 2, KV_HEADS, HEAD_DIM)),
        'state_conv': nrm((DEPTH, DEC_BATCH, CONV_W - 1, CONV_DIM)),
        'state_ssm': nrm((DEPTH, DEC_BATCH, SSD_HEADS, SSD_HEAD_DIM, D_STATE), 0.1),
        'page_table': page_table,
        'w_ada': nrm((DEPTH, D_MODEL, 6 * D_MODEL), 0.5 * D_MODEL ** -0.5),
        'b_ada': nrm((DEPTH, 6 * D_MODEL), 0.02),
        'norm_mix': 1.0 + nrm((DEPTH, D_MODEL), 0.02),
        'norm_ffn': 1.0 + nrm((DEPTH, D_MODEL), 0.02),
        'w_in': nrm((DEPTH, D_MODEL, IN_DIM), D_MODEL ** -0.5),
        'cmp_pe_k': nrm((DEPTH, CMP_BLK, HEAD_DIM), 0.02),
        'cmp_w1_k': nrm((DEPTH, CMP_BLK, HEAD_DIM, CMP_HID), (CMP_BLK * HEAD_DIM) ** -0.5),
        'cmp_w2_k': nrm((DEPTH, CMP_HID, HEAD_DIM), CMP_HID ** -0.5),
        'cmp_pe_v': nrm((DEPTH, CMP_BLK, HEAD_DIM), 0.02),
        'cmp_w1_v': nrm((DEPTH, CMP_BLK, HEAD_DIM, CMP_HID), (CMP_BLK * HEAD_DIM) ** -0.5),
        'cmp_w2_v': nrm((DEPTH, CMP_HID, HEAD_DIM), CMP_HID ** -0.5),
        'conv_w': nrm((DEPTH, CONV_W, CONV_DIM), CONV_W ** -0.5),
        'conv_b': nrm((DEPTH, CONV_DIM), 0.02),
        'dt_bias': dt_bias,
        'a_log': a_log,
        'd_skip': 1.0 + nrm((DEPTH, SSD_HEADS), 0.02),
        'ssd_norm': 1.0 + nrm((DEPTH, D_INNER), 0.02),
        'w_out': nrm((DEPTH, MIX_DIM, D_MODEL), MIX_DIM ** -0.5),
        'router_w': nrm((DEPTH, D_MODEL, N_EXPERTS), D_MODEL ** -0.5),
        'router_bias': nrm((DEPTH, N_EXPERTS), 0.01),
        'exp_w1': nrm((DEPTH, N_EXPERTS, D_MODEL, D_EXPERT), D_MODEL ** -0.5),
        'exp_w3': nrm((DEPTH, N_EXPERTS, D_MODEL, D_EXPERT), D_MODEL ** -0.5),
        'exp_w2': nrm((DEPTH, N_EXPERTS, D_EXPERT, D_MODEL), D_EXPERT ** -0.5),
        'sh_w1': nrm((DEPTH, D_MODEL, D_SHARED), D_MODEL ** -0.5),
        'sh_w3': nrm((DEPTH, D_MODEL, D_SHARED), D_MODEL ** -0.5),
        'sh_w2': nrm((DEPTH, D_SHARED, D_MODEL), D_SHARED ** -0.5),
        'norm_f': 1.0 + nrm((D_MODEL,), 0.02),
    }


def reference(x_prompt, x_sample, c_prompt, c_sample, cache_kv_cmp, cache_kv_sel, cache_kv_win,
              state_conv, state_ssm, page_table, w_ada, b_ada, norm_mix, norm_ffn, w_in,
              cmp_pe_k, cmp_w1_k, cmp_w2_k, cmp_pe_v, cmp_w1_v, cmp_w2_v, conv_w, conv_b,
              dt_bias, a_log, d_skip, ssd_norm, w_out, router_w, router_bias, exp_w1, exp_w3,
              exp_w2, sh_w1, sh_w3, sh_w2, norm_f):
    names = ('kv_cmp_p', 'kv_sel_p', 'kv_win_p', 'conv_p', 'ssm_p',
             'kv_cmp_s', 'kv_sel_s', 'kv_win_s', 'conv_s', 'ssm_s')
    acc = {k: [] for k in names}
    xp, xs = x_prompt, x_sample
    for i in range(DEPTH):
        cmp_k = (cmp_pe_k[i], cmp_w1_k[i], cmp_w2_k[i])
        cmp_v = (cmp_pe_v[i], cmp_w1_v[i], cmp_w2_v[i])
        ssd_w = (conv_w[i], conv_b[i], dt_bias[i], a_log[i], d_skip[i], ssd_norm[i])
        moe_w = (router_w[i], router_bias[i], exp_w1[i], exp_w3[i], exp_w2[i], sh_w1[i], sh_w3[i], sh_w2[i])

        sh1, sc1, g1, sh2, sc2, g2 = modulation(c_prompt, w_ada[i], b_ada[i])
        q, kv, gates, z, xbc, dtr = split_projection((rmsnorm(xp, norm_mix[i]) * (1 + sc1) + sh1) @ w_in[i])
        o_nsa = nsa_prompt(q, gates, kv, cmp_k, cmp_v)
        conv0 = jnp.zeros((xp.shape[0], CONV_W - 1, CONV_DIM), xbc.dtype)
        h0 = jnp.zeros((xp.shape[0], SSD_HEADS, SSD_HEAD_DIM, D_STATE), jnp.float32)
        y_ssd, conv_p, ssm_p = ssd_mixer(z, xbc, dtr, conv0, h0, *ssd_w)
        xp = xp + g1 * (jnp.concatenate([o_nsa, y_ssd], axis=-1) @ w_out[i])
        xp = xp + g2 * moe(rmsnorm(xp, norm_ffn[i]) * (1 + sc2) + sh2, *moe_w)
        t_p = kv.shape[1]
        acc['kv_cmp_p'].append(kv[:, :, 0:2])
        acc['kv_sel_p'].append(kv[:, :, 2:4])
        acc['kv_win_p'].append(kv[:, t_p - min(WINDOW, t_p):, 4:6])
        acc['conv_p'].append(conv_p)
        acc['ssm_p'].append(ssm_p)

        sh1, sc1, g1, sh2, sc2, g2 = modulation(c_sample, w_ada[i], b_ada[i])
        q, kv, gates, z, xbc, dtr = split_projection((rmsnorm(xs, norm_mix[i]) * (1 + sc1) + sh1) @ w_in[i])
        o_nsa, win_s = nsa_sample(q, gates, kv, cache_kv_cmp, cache_kv_sel, cache_kv_win[i], page_table, i, cmp_k, cmp_v)
        y_ssd, conv_s, ssm_s = ssd_mixer(z, xbc, dtr, state_conv[i], state_ssm[i], *ssd_w)
        xs = xs + g1 * (jnp.concatenate([o_nsa, y_ssd], axis=-1) @ w_out[i])
        xs = xs + g2 * moe(rmsnorm(xs, norm_ffn[i]) * (1 + sc2) + sh2, *moe_w)
        acc['kv_cmp_s'].append(kv[:, :, 0:2])
        acc['kv_sel_s'].append(kv[:, :, 2:4])
        acc['kv_win_s'].append(win_s)
        acc['conv_s'].append(conv_s)
        acc['ssm_s'].append(ssm_s)

    y_prompt = rmsnorm(xp, norm_f)
    y_sample = rmsnorm(xs, norm_f)
    kv_cmp_prompt = jnp.stack(acc['kv_cmp_p'])
    kv_sel_prompt = jnp.stack(acc['kv_sel_p'])
    kv_win_prompt = jnp.stack(acc['kv_win_p'])
    conv_prompt = jnp.stack(acc['conv_p'])
    ssm_prompt = jnp.stack(acc['ssm_p'])
    kv_cmp_sample = jnp.stack(acc['kv_cmp_s'])
    kv_sel_sample = jnp.stack(acc['kv_sel_s'])
    kv_win_sample = jnp.stack(acc['kv_win_s'])
    conv_sample = jnp.stack(acc['conv_s'])
    ssm_sample = jnp.stack(acc['ssm_s'])
    return (y_prompt, y_sample, kv_cmp_prompt, kv_sel_prompt, kv_win_prompt, conv_prompt, ssm_prompt,
            kv_cmp_sample, kv_sel_sample, kv_win_sample, conv_sample, ssm_sample)
```

```python
import functools
import math

import jax
import jax.numpy as jnp
import numpy as np
from jax import lax
from jax.experimental import pallas as pl
from jax.experimental.pallas import tpu as pltpu

F32 = jnp.float32
BF16 = jnp.bfloat16
HIGHEST = lax.Precision.HIGHEST

D_MODEL = 1024
NSA_HEADS = 8
KV_HEADS = 2
HEAD_DIM = 64
GQA = NSA_HEADS // KV_HEADS
CMP_BLK = 32
CMP_STRIDE = 16
CMP_HID = 64
SEL_BLK = 64
N_SEL = 16
WINDOW = 512
FORCE_CUR = 2.0e4
FORCE_SINK = 1.0e4
SSD_HEADS = 8
SSD_HEAD_DIM = 64
D_INNER = SSD_HEADS * SSD_HEAD_DIM
SSD_GROUPS = 2
D_STATE = 128
CONV_W = 4
CONV_DIM = D_INNER + 2 * SSD_GROUPS * D_STATE
SSD_CHUNK = 128
NSA_DIM = NSA_HEADS * HEAD_DIM
KV_DIM = KV_HEADS * HEAD_DIM
N_EXPERTS = 64
TOP_K = 8
N_EXPERT_GROUPS = 8
TOPK_GROUPS = 4
D_EXPERT = 256
D_SHARED = 256
ROUTED_SCALE = 2.5
RMS_EPS = 1e-6
NEG_INF = -1e30
PAGE_SIZE = 128

LANES = 128
SUBLANES = 8
VMEM_LIMIT = 56 * 1024 * 1024


def _cparams(sem):
    return pltpu.CompilerParams(dimension_semantics=sem, vmem_limit_bytes=VMEM_LIMIT)


def _silu(v):
    return v * jax.nn.sigmoid(v)


def _mod_body(c_ref, w_ref, b_ref, o_ref):
    s = _silu(c_ref[...])
    o_ref[...] = jnp.dot(s, w_ref[...], preferred_element_type=F32, precision=HIGHEST) + b_ref[...]


def _modulation(c_all, w_ada, b_ada):
    n, d = c_all.shape
    nout = w_ada.shape[1]
    tn = 512
    return pl.pallas_call(
        _mod_body,
        grid=(nout // tn,),
        in_specs=[pl.BlockSpec((n, d), lambda j: (0, 0)),
                  pl.BlockSpec((d, tn), lambda j: (0, j)),
                  pl.BlockSpec((1, tn), lambda j: (0, j))],
        out_specs=pl.BlockSpec((n, tn), lambda j: (0, j)),
        out_shape=jax.ShapeDtypeStruct((n, nout), F32),
        compiler_params=_cparams(("arbitrary",)),
        name="modulation",
    )(c_all, w_ada, b_ada.reshape(1, nout))


def _nt_dot(a, b):
    return lax.dot_general(a, b, (((1,), (1,)), ((), ())), preferred_element_type=F32)


def _proj_body(x_ref, sc_ref, sh_ref, nw_ref, wq_ref, wkvT_ref, wg_ref, wz_ref, wx_ref, wdt_ref, wdtT_ref,
               q_ref, kcT_ref, ksT_ref, kwT_ref, g_ref, z_ref, xbc_ref, dt_ref, dtT_ref):
    x = x_ref[0]
    ms = jnp.mean(x * x, axis=-1, keepdims=True)
    h = x * lax.rsqrt(ms + RMS_EPS) * nw_ref[...]
    h = h * (1.0 + sc_ref[0]) + sh_ref[0]
    hb = h.astype(BF16)
    q_ref[0] = (jnp.dot(hb, wq_ref[...], preferred_element_type=F32) * (HEAD_DIM ** -0.5)).astype(BF16)
    kvT = _nt_dot(wkvT_ref[...], hb)
    kcT_ref[0] = kvT[0:256]
    ksT_ref[0] = kvT[256:512]
    kwT_ref[0] = kvT[512:768]
    g_ref[0] = jax.nn.sigmoid(jnp.dot(hb, wg_ref[...], preferred_element_type=F32))
    z_ref[0] = jnp.dot(hb, wz_ref[...], preferred_element_type=F32)
    xbc_ref[0] = jnp.dot(hb, wx_ref[...], preferred_element_type=F32)
    dt_ref[0] = jnp.dot(hb, wdt_ref[...], preferred_element_type=F32)
    dtT_ref[0] = _nt_dot(wdtT_ref[...], hb)


def _prep_w_in(w_in):
    w = w_in
    o = 0
    wq = w[:, o:o + NSA_DIM]; o += NSA_DIM
    wkv = w[:, o:o + 6 * KV_DIM]; o += 6 * KV_DIM
    wg = w[:, o:o + 3 * NSA_HEADS]; o += 3 * NSA_HEADS
    wz = w[:, o:o + D_INNER]; o += D_INNER
    wx = w[:, o:o + CONV_DIM]; o += CONV_DIM
    wdt = w[:, o:o + SSD_HEADS]; o += SSD_HEADS
    pad = lambda a: jnp.pad(a, ((0, 0), (0, LANES - a.shape[1])))
    per = 3 * GQA
    wg = jnp.concatenate([pad(wg[:, k * per:(k + 1) * per]) for k in range(KV_HEADS)], axis=1)
    return dict(wq=wq.astype(BF16), wkvT=wkv.T.astype(BF16), wg=wg.astype(BF16), wz=wz.astype(BF16),
                wx=wx.astype(BF16), wdt=pad(wdt).astype(BF16), wdtT=wdt.T.astype(BF16))


def _in_proj(x, sc, sh, norm_w, W, tm):
    nb, t, d = x.shape
    mt = sc.shape[1]
    assert t % tm == 0 and (mt == 1 or mt == t)
    if mt == 1:
        mod_spec = pl.BlockSpec((1, 1, d), lambda b, i: (b, 0, 0))
    else:
        mod_spec = pl.BlockSpec((1, tm, d), lambda b, i: (b, i, 0))
    full = lambda a: pl.BlockSpec(a.shape, lambda b, i: (0,) * a.ndim)
    row = lambda n: pl.BlockSpec((1, tm, n), lambda b, i: (b, i, 0))
    col = lambda n: pl.BlockSpec((1, n, tm), lambda b, i: (b, 0, i))
    ws = [W["wq"], W["wkvT"], W["wg"], W["wz"], W["wx"], W["wdt"], W["wdtT"]]
    outs = pl.pallas_call(
        _proj_body,
        grid=(nb, t // tm),
        in_specs=[row(d), mod_spec, mod_spec, pl.BlockSpec((1, d), lambda b, i: (0, 0))] + [full(a) for a in ws],
        out_specs=[row(NSA_DIM), col(256), col(256), col(256), row(KV_HEADS * LANES), row(D_INNER), row(CONV_DIM),
                   row(LANES), col(SSD_HEADS)],
        out_shape=[jax.ShapeDtypeStruct((nb, t, NSA_DIM), BF16),
                   jax.ShapeDtypeStruct((nb, 256, t), F32),
                   jax.ShapeDtypeStruct((nb, 256, t), F32),
                   jax.ShapeDtypeStruct((nb, 256, t), F32),
                   jax.ShapeDtypeStruct((nb, t, KV_HEADS * LANES), F32),
                   jax.ShapeDtypeStruct((nb, t, D_INNER), F32),
                   jax.ShapeDtypeStruct((nb, t, CONV_DIM), F32),
                   jax.ShapeDtypeStruct((nb, t, LANES), F32),
                   jax.ShapeDtypeStruct((nb, SSD_HEADS, t), F32)],
        compiler_params=_cparams(("arbitrary", "arbitrary")),
        name="in_proj",
    )(x, sc, sh, norm_w.reshape(1, d), *ws)
    names = ("q", "kcT", "ksT", "kwT", "g", "z", "xbc", "dt", "dtT")
    return dict(zip(names, outs))


def _prep_compress(cmp_pe_k, cmp_w1_k, cmp_w2_k, cmp_pe_v, cmp_w1_v, cmp_w2_v):
    w1s = jnp.stack([cmp_w1_k, cmp_w1_v]).reshape(2, 2, CMP_STRIDE, HEAD_DIM, CMP_HID)
    eye = jnp.eye(2, dtype=F32)
    wbd = jnp.einsum("ktldh,kK,vV->lkvdtKVh", w1s, eye, eye).reshape(CMP_STRIDE, 256, 512)
    w2s = jnp.stack([cmp_w2_k, cmp_w2_v])
    w2bd = jnp.einsum("khd,kK,vV->kvhKVd", w2s, eye, eye).reshape(256, 256)
    pes = jnp.stack([cmp_pe_k, cmp_pe_v]).reshape(2, 2, CMP_STRIDE, HEAD_DIM)
    pe_rows = jnp.broadcast_to(jnp.transpose(pes, (1, 2, 0, 3))[:, :, :, None, :],
                               (2, CMP_STRIDE, 2, KV_HEADS, HEAD_DIM)).reshape(2, CMP_STRIDE * 256)
    pecat = jnp.pad(pe_rows, ((0, SUBLANES - 2), (0, 0)))
    perm = np.zeros((LANES, LANES), np.float32)
    for l in range(CMP_STRIDE):
        for n in range(LANES // CMP_STRIDE):
            perm[(LANES // CMP_STRIDE) * l + n, CMP_STRIDE * n + l] = 1.0
    bias = pl.pallas_call(
        _cmp_bias_body,
        out_shape=jax.ShapeDtypeStruct((SUBLANES, 512), F32),
        name="cmp_bias",
    )(pecat, wbd.reshape(CMP_STRIDE * 256, 512))
    return dict(wbd=wbd.astype(BF16), w2bd=w2bd.astype(BF16), bias=bias, perm=jnp.asarray(perm, BF16))


def _cmp_bias_body(pe_ref, w_ref, o_ref):
    o_ref[...] = jnp.dot(pe_ref[...], w_ref[...], preferred_element_type=F32, precision=HIGHEST)


def _compress_body(n_pref, n_slab_refs, slabs_per_ref, *refs):
    refs = refs[n_pref:]
    slab_refs = refs[:n_slab_refs]
    perm_ref, wbd_ref, bias_ref, w2bd_ref, o_ref, z_scr = refs[n_slab_refs:]
    j = pl.program_id(1)
    nh = z_scr.shape[1]
    per_slab = LANES // CMP_STRIDE
    g_tot = n_slab_refs * slabs_per_ref
    base = pl.multiple_of(j * (per_slab * g_tot), per_slab)
    perm = perm_ref[...]
    for ri in range(n_slab_refs):
        for si in range(slabs_per_ref):
            slab = slab_refs[ri][0][:, si * LANES:(si + 1) * LANES].astype(BF16)
            xp = _nt_dot(perm, slab)
            g = ri * slabs_per_ref + si
            for l in range(CMP_STRIDE):
                z_scr[l, pl.ds(base + per_slab * g, per_slab), :] = xp[per_slab * l:per_slab * (l + 1), :]

    @pl.when(j == pl.num_programs(1) - 1)
    def _():
        acc = jnp.zeros((nh, 512), F32)
        for l in range(CMP_STRIDE):
            acc = acc + jnp.dot(z_scr[l].astype(BF16), wbd_ref[l], preferred_element_type=F32)
        lead = acc[:, :256] + bias_ref[0:1, :256]
        tail = acc[:, 256:] + bias_ref[1:2, 256:]
        hid = _silu(lead + pltpu.roll(tail, nh - 1, 0))
        out = jnp.dot(hid.astype(BF16), w2bd_ref[...], preferred_element_type=F32)
        row = lax.broadcasted_iota(jnp.int32, out.shape, 0)
        o_ref[0] = jnp.where(row < nh - 1, out, 0.0)


def _compress_prompt(kcT, C):
    nb, _, t = kcT.shape
    nh = t // CMP_STRIDE
    g = min(8, t // LANES)
    nsteps = t // (LANES * g)
    full = lambda a: pl.BlockSpec(a.shape, lambda b, j: (0,) * a.ndim)
    return pl.pallas_call(
        functools.partial(_compress_body, 0, 1, g),
        grid=(nb, nsteps),
        in_specs=[pl.BlockSpec((1, 256, LANES * g), lambda b, j: (b, 0, j)),
                  full(C["perm"]), full(C["wbd"]), full(C["bias"]), full(C["w2bd"])],
        out_specs=pl.BlockSpec((1, nh, 256), lambda b, j: (b, 0, 0)),
        out_shape=jax.ShapeDtypeStruct((nb, nh, 256), F32),
        scratch_shapes=[pltpu.VMEM((CMP_STRIDE, nh, 256), F32)],
        compiler_params=_cparams(("arbitrary", "arbitrary")),
        name="compress_prompt",
    )(kcT, C["perm"], C["wbd"], C["bias"], C["w2bd"])


def _compress_paged(pool, page_table, C):
    nb, n_pages = page_table.shape
    nh = n_pages * (PAGE_SIZE // CMP_STRIDE)
    g = min(16, n_pages)
    nsteps = n_pages // g
    full = lambda a: pl.BlockSpec(a.shape, lambda b, j, pt: (0,) * a.ndim)
    page_spec = lambda k: pl.BlockSpec((1, 256, LANES), lambda b, j, pt: (pt[b, j * g + k], 0, 0))
    return pl.pallas_call(
        functools.partial(_compress_body, 1, g, 1),
        grid_spec=pltpu.PrefetchScalarGridSpec(
            num_scalar_prefetch=1,
            grid=(nb, nsteps),
            in_specs=[page_spec(k) for k in range(g)] + [full(C["perm"]), full(C["wbd"]), full(C["bias"]), full(C["w2bd"])],
            out_specs=pl.BlockSpec((1, nh, 256), lambda b, j, pt: (b, 0, 0)),
            scratch_shapes=[pltpu.VMEM((CMP_STRIDE, nh, 256), F32)],
        ),
        out_shape=jax.ShapeDtypeStruct((nb, nh, 256), F32),
        compiler_params=_cparams(("arbitrary", "arbitrary")),
        name="compress_paged",
    )(page_table, *([pool] * g), C["perm"], C["wbd"], C["bias"], C["w2bd"])


def _alibi_slope(head):
    return float(2.0 ** (-8.0 * (head + 1) / NSA_HEADS))


def _overlap_T(nc, ns):
    cst = np.arange(nc)[None, :] * CMP_STRIDE
    sst = np.arange(ns)[:, None] * SEL_BLK
    ov = np.clip(np.minimum(cst + CMP_BLK, sst + SEL_BLK) - np.maximum(cst, sst), 0, None).astype(np.float32) / CMP_BLK
    return jnp.asarray(ov, F32)


def _masked_softmax(s, mask):
    s = jnp.where(mask, s, NEG_INF)
    p = jnp.exp(s - jnp.max(s, axis=-1, keepdims=True)) * mask.astype(F32)
    return p / jnp.maximum(jnp.sum(p, axis=-1, keepdims=True), 1e-30)


def _topk_mask(grp, k):
    ngrp = len(grp)
    cnt = [jnp.zeros(grp[0].shape, F32) for _ in range(ngrp)]
    sub = lax.broadcasted_iota(jnp.int32, grp[0].shape, 0)
    one, zero = jnp.float32(1.0), jnp.float32(0.0)
    for j in range(ngrp * SUBLANES):
        a, r = divmod(j, SUBLANES)
        row = grp[a][r:r + 1, :]
        for c in range(ngrp):
            if c < a:
                beats = jnp.where(row > grp[c], one, zero)
            elif c > a:
                beats = jnp.where(row >= grp[c], one, zero)
            else:
                beats = jnp.where(sub > r, jnp.where(row >= grp[c], one, zero), jnp.where(row > grp[c], one, zero))
            cnt[c] = cnt[c] + beats
    return [jnp.where(c < float(k), one, zero) for c in cnt]


def _select_blocks(imp, cur):
    ns = imp.shape[0]
    jrow = lax.broadcasted_iota(jnp.int32, imp.shape, 0)
    imp = jnp.where(jrow == cur, FORCE_CUR, jnp.where(jrow == 0, FORCE_SINK, imp))
    imp = jnp.where(jrow <= cur, imp, NEG_INF)
    assert ns % SUBLANES == 0
    grp = [imp[SUBLANES * a:SUBLANES * (a + 1)] for a in range(ns // SUBLANES)]
    return jnp.concatenate(_topk_mask(grp, N_SEL), axis=0)


def _cmp_attn_body(q_ref, kcvc_ref, ovT_ref, eye_ref, ocmp_ref, sel_ref):
    i = pl.program_id(1)
    tq = q_ref.shape[1]
    nh = kcvc_ref.shape[1]
    t0 = i * tq
    qpos_col = t0 + lax.broadcasted_iota(jnp.int32, (tq, 1), 0)
    cend = lax.broadcasted_iota(jnp.int32, (1, nh), 1) * CMP_STRIDE + (CMP_BLK - 1)
    mask = cend <= qpos_col
    dist = (qpos_col - cend).astype(F32)
    qpos_row = t0 + lax.broadcasted_iota(jnp.int32, (1, tq), 1)
    cur = qpos_row // SEL_BLK
    kcvc = kcvc_ref[0]
    for kvh in range(KV_HEADS):
        kc = kcvc[:, kvh * HEAD_DIM:(kvh + 1) * HEAD_DIM].astype(BF16)
        vc = kcvc[:, KV_DIM + kvh * HEAD_DIM:KV_DIM + (kvh + 1) * HEAD_DIM].astype(BF16)
        psum = jnp.zeros((tq, nh), F32)
        for g in range(GQA):
            head = kvh * GQA + g
            qg = q_ref[0, :, head * HEAD_DIM:(head + 1) * HEAD_DIM]
            s = _nt_dot(qg, kc) - _alibi_slope(head) * dist
            p = _masked_softmax(s, mask)
            ocmp_ref[0, :, head * HEAD_DIM:(head + 1) * HEAD_DIM] = jnp.dot(p.astype(BF16), vc, preferred_element_type=F32)
            psum = psum + p
        impT = lax.dot_general(ovT_ref[...], psum, (((1,), (1,)), ((), ())), preferred_element_type=F32,
                               precision=HIGHEST)
        selT = _select_blocks(impT, cur)
        sel = _nt_dot(eye_ref[...], selT.astype(BF16))
        sel_ref[0, kvh] = sel.astype(BF16)


def _cmp_attn_prompt(q, kcvc, tq):
    nb, t, _ = q.shape
    nh = kcvc.shape[1]
    ns = t // SEL_BLK
    ovT = _overlap_T(nh, ns)
    eye = jnp.eye(tq, dtype=BF16)
    return pl.pallas_call(
        _cmp_attn_body,
        grid=(nb, t // tq),
        in_specs=[pl.BlockSpec((1, tq, NSA_DIM), lambda b, i: (b, i, 0)),
                  pl.BlockSpec((1, nh, 256), lambda b, i: (b, 0, 0)),
                  pl.BlockSpec(ovT.shape, lambda b, i: (0, 0)),
                  pl.BlockSpec(eye.shape, lambda b, i: (0, 0))],
        out_specs=[pl.BlockSpec((1, tq, NSA_DIM), lambda b, i: (b, i, 0)),
                   pl.BlockSpec((1, KV_HEADS, tq, ns), lambda b, i: (b, 0, i, 0))],
        out_shape=[jax.ShapeDtypeStruct((nb, t, NSA_DIM), F32),
                   jax.ShapeDtypeStruct((nb, KV_HEADS, t, ns), BF16)],
        compiler_params=_cparams(("arbitrary", "arbitrary")),
        name="cmp_attn",
    )(q, kcvc, ovT, eye)


def _head_rows(qrow, kvh):
    rows = [qrow[:, (kvh * GQA + g) * HEAD_DIM:(kvh * GQA + g + 1) * HEAD_DIM] for g in range(GQA)]
    return jnp.concatenate(rows + [jnp.zeros((SUBLANES - GQA, HEAD_DIM), qrow.dtype)], axis=0)


def _slope_col(kvh):
    r = lax.broadcasted_iota(jnp.int32, (SUBLANES, 1), 0)
    col = jnp.zeros((SUBLANES, 1), F32)
    for g in range(GQA):
        col = jnp.where(r == g, _alibi_slope(kvh * GQA + g), col)
    return col


def _cmp_attn_sample_body(past, q_ref, kcvc_ref, ovT_ref, eye_ref, ocmp_ref, sel_ref):
    nh = kcvc_ref.shape[1]
    cend = lax.broadcasted_iota(jnp.int32, (1, nh), 1) * CMP_STRIDE + (CMP_BLK - 1)
    mask = cend <= past
    dist = (past - cend).astype(F32)
    cur = jnp.full((1, SUBLANES), past // SEL_BLK, jnp.int32)
    kcvc = kcvc_ref[0]
    qrow = q_ref[0]
    for kvh in range(KV_HEADS):
        kc = kcvc[:, kvh * HEAD_DIM:(kvh + 1) * HEAD_DIM].astype(BF16)
        vc = kcvc[:, KV_DIM + kvh * HEAD_DIM:KV_DIM + (kvh + 1) * HEAD_DIM].astype(BF16)
        s = _nt_dot(_head_rows(qrow, kvh), kc) - _slope_col(kvh) * dist
        p = _masked_softmax(s, mask)
        o = jnp.dot(p.astype(BF16), vc, preferred_element_type=F32)
        for g in range(GQA):
            head = kvh * GQA + g
            ocmp_ref[0, :, head * HEAD_DIM:(head + 1) * HEAD_DIM] = o[g:g + 1, :]
        psum = jnp.sum(p[0:GQA], axis=0, keepdims=True)
        impT = lax.dot_general(ovT_ref[...], jnp.broadcast_to(psum, (SUBLANES, nh)), (((1,), (1,)), ((), ())),
                               preferred_element_type=F32, precision=HIGHEST)
        selT = _select_blocks(impT, cur)
        sel_ref[0, kvh] = _nt_dot(eye_ref[...], selT.astype(BF16)).astype(jnp.int32)


def _cmp_attn_sample(q3, kcvc, past):
    nb = q3.shape[0]
    nh = kcvc.shape[1]
    ns = past // SEL_BLK + 1
    ns_pad = -(-ns // SUBLANES) * SUBLANES
    ovT = jnp.pad(_overlap_T(nh, ns), ((0, ns_pad - ns), (0, 0)))
    eye = jnp.eye(SUBLANES, dtype=BF16)
    ocmp, sel = pl.pallas_call(
        functools.partial(_cmp_attn_sample_body, past),
        grid=(nb,),
        in_specs=[pl.BlockSpec((1, 1, NSA_DIM), lambda b: (b, 0, 0)),
                  pl.BlockSpec((1, nh, 256), lambda b: (b, 0, 0)),
                  pl.BlockSpec(ovT.shape, lambda b: (0, 0)),
                  pl.BlockSpec(eye.shape, lambda b: (0, 0))],
        out_specs=[pl.BlockSpec((1, 1, NSA_DIM), lambda b: (b, 0, 0)),
                   pl.BlockSpec((1, KV_HEADS, SUBLANES, ns_pad), lambda b: (b, 0, 0, 0))],
        out_shape=[jax.ShapeDtypeStruct((nb, 1, NSA_DIM), F32),
                   jax.ShapeDtypeStruct((nb, KV_HEADS, SUBLANES, ns_pad), jnp.int32)],
        compiler_params=_cparams(("arbitrary",)),
        name="cmp_attn_sample",
    )(q3, kcvc, ovT, eye)
    return ocmp, sel[:, :, 0, :]


def _pos_rows(n, start=0):
    tab = np.zeros((HEAD_DIM, n), np.float32)
    k = start + np.arange(n)
    tab[0] = k // SEL_BLK
    tab[1] = k % SEL_BLK
    return jnp.asarray(tab, BF16)


def _slope_rows():
    tab = np.zeros((KV_HEADS, SUBLANES, HEAD_DIM), np.float32)
    for k in range(KV_HEADS):
        for g in range(GQA):
            s = _alibi_slope(k * GQA + g)
            tab[k, g, 0] = SEL_BLK * s
            tab[k, g, 1] = s
    return jnp.asarray(tab, F32)


def _block_expand(ns, n):
    return jnp.asarray((np.arange(n)[None, :] // SEL_BLK == np.arange(ns)[:, None]).astype(np.float32), BF16)


def _flash_step(q4, kT_aug, vT, bias4, m_scr, l_scr, acc_scr):
    s = jnp.dot(q4, kT_aug, preferred_element_type=F32) + bias4
    m_old = m_scr[...]
    m_new = jnp.maximum(m_old, jnp.max(s, axis=-1, keepdims=True))
    alpha = jnp.exp(m_old - m_new)
    p = jnp.exp(s - m_new)
    l_scr[...] = alpha * l_scr[...] + jnp.sum(p, axis=-1, keepdims=True)
    acc_scr[...] = alpha * acc_scr[...] + _nt_dot(p.astype(BF16), vT)
    m_scr[...] = m_new


def _sel_win_body(tk, q_ref, sel_ref, g_ref, ocmp_ref, ksT_ref, vsT_ref, kwT_ref, vwT_ref, pos_ref, exp_ref, slope_ref,
                  o_ref, m_scr, l_scr, acc_scr):
    i = pl.program_id(2)
    tq = q_ref.shape[1]
    t0 = i * tq
    qpos = t0 + lax.broadcasted_iota(jnp.int32, (tq, 1), 0)
    q4 = jnp.concatenate(
        [jnp.concatenate([q_ref[0, :, g * HEAD_DIM:(g + 1) * HEAD_DIM],
                          jnp.broadcast_to(slope_ref[0, g:g + 1, :], (tq, HEAD_DIM)).astype(BF16)], axis=1)
         for g in range(GQA)], axis=0)

    def reset():
        m_scr[...] = jnp.full(m_scr.shape, NEG_INF, F32)
        l_scr[...] = jnp.zeros(l_scr.shape, F32)
        acc_scr[...] = jnp.zeros(acc_scr.shape, F32)

    def finish():
        return acc_scr[...] / jnp.maximum(l_scr[...], 1e-30)

    reset()
    sel = sel_ref[0, 0]

    def sel_step(kt, carry):
        k0 = pl.multiple_of(kt * tk, tk)
        kpos = k0 + lax.broadcasted_iota(jnp.int32, (1, tk), 1)
        chosen = jnp.dot(sel, exp_ref[:, pl.ds(k0, tk)], preferred_element_type=F32)
        keep = (chosen > 0.5) & (kpos <= qpos)
        bias = jnp.where(keep, 0.0, NEG_INF)
        kT_aug = jnp.concatenate([ksT_ref[0, :, pl.ds(k0, tk)].astype(BF16), pos_ref[:, pl.ds(k0, tk)]], axis=0)
        _flash_step(q4, kT_aug, vsT_ref[0, :, pl.ds(k0, tk)].astype(BF16), jnp.concatenate([bias] * GQA, axis=0),
                    m_scr, l_scr, acc_scr)
        return carry

    lax.fori_loop(0, (t0 + tq + tk - 1) // tk, sel_step, 0)
    o_sel = finish()

    reset()
    for w in range(WINDOW // tq + 1):
        kt = i - w

        @pl.when(kt >= 0)
        def _():
            k0 = pl.multiple_of(kt * tq, tq)
            kpos = k0 + lax.broadcasted_iota(jnp.int32, (1, tq), 1)
            dist = qpos - kpos
            bias = jnp.where((dist >= 0) & (dist < WINDOW), 0.0, NEG_INF)
            kT_aug = jnp.concatenate([kwT_ref[0, :, pl.ds(k0, tq)].astype(BF16), pos_ref[:, pl.ds(k0, tq)]], axis=0)
            _flash_step(q4, kT_aug, vwT_ref[0, :, pl.ds(k0, tq)].astype(BF16), jnp.concatenate([bias] * GQA, axis=0),
                        m_scr, l_scr, acc_scr)

    o_win = finish()
    gates = g_ref[0]
    for g in range(GQA):
        rows = slice(g * tq, (g + 1) * tq)
        cols = slice(g * HEAD_DIM, (g + 1) * HEAD_DIM)
        o = (gates[:, 3 * g:3 * g + 1] * ocmp_ref[0, :, cols] + gates[:, 3 * g + 1:3 * g + 2] * o_sel[rows]
             + gates[:, 3 * g + 2:3 * g + 3] * o_win[rows])
        o_ref[0, :, cols] = o.astype(o_ref.dtype)


def _sel_win_prompt(q, sel, gates, ocmp, ksT, kwT, tq, tk):
    nb, t, _ = q.shape
    ns = sel.shape[-1]
    assert t % tk == 0 and t % tq == 0 and WINDOW % tq == 0
    pos, expand, slopes = _pos_rows(t), _block_expand(ns, t), _slope_rows()
    grp = GQA * HEAD_DIM
    kv_spec = lambda which: pl.BlockSpec((1, HEAD_DIM, t), lambda b, k, i: (b, which * KV_HEADS + k, 0))
    return pl.pallas_call(
        functools.partial(_sel_win_body, tk),
        grid=(nb, KV_HEADS, t // tq),
        in_specs=[pl.BlockSpec((1, tq, grp), lambda b, k, i: (b, i, k)),
                  pl.BlockSpec((1, 1, tq, ns), lambda b, k, i: (b, k, i, 0)),
                  pl.BlockSpec((1, tq, LANES), lambda b, k, i: (b, i, k)),
                  pl.BlockSpec((1, tq, grp), lambda b, k, i: (b, i, k)),
                  kv_spec(0), kv_spec(1), kv_spec(0), kv_spec(1),
                  pl.BlockSpec(pos.shape, lambda b, k, i: (0, 0)),
                  pl.BlockSpec(expand.shape, lambda b, k, i: (0, 0)),
                  pl.BlockSpec((1, SUBLANES, HEAD_DIM), lambda b, k, i: (k, 0, 0))],
        out_specs=pl.BlockSpec((1, tq, grp), lambda b, k, i: (b, i, k)),
        out_shape=jax.ShapeDtypeStruct((nb, t, NSA_DIM), BF16),
        scratch_shapes=[pltpu.VMEM((GQA * tq, 1), F32), pltpu.VMEM((GQA * tq, 1), F32),
                        pltpu.VMEM((GQA * tq, HEAD_DIM), F32)],
        compiler_params=_cparams(("arbitrary", "arbitrary", "arbitrary")),
        name="sel_win_attn",
    )(q, sel, gates, ocmp, ksT, ksT, kwT, kwT, pos, expand, slopes)


def _sel_win_sample_body(past, g_pages, pt_ref, sel_ref, q_ref, g_ref, ocmp_ref, *refs):
    page_refs = refs[:g_pages]
    (win_ref, ksn_ref, kwn_ref, pos_ref, wpos_ref, slope_ref, o_ref, wout_ref, m_scr, l_scr, acc_scr) = refs[g_pages:]
    b = pl.program_id(0)
    j = pl.program_id(1)
    ns_pad = sel_ref.shape[1] // KV_HEADS
    qrow = q_ref[0]
    q4 = [jnp.concatenate([_head_rows(qrow, k), slope_ref[k].astype(BF16)], axis=1) for k in range(KV_HEADS)]

    @pl.when(j == 0)
    def _():
        m_scr[...] = jnp.full(m_scr.shape, NEG_INF, F32)
        l_scr[...] = jnp.zeros(l_scr.shape, F32)
        acc_scr[...] = jnp.zeros(acc_scr.shape, F32)

    width = g_pages * PAGE_SIZE
    lane_blk = lax.broadcasted_iota(jnp.int32, (1, width), 1) // SEL_BLK
    row0 = lax.broadcasted_iota(jnp.int32, (HEAD_DIM, 1), 0) == 0
    blk0 = j * (width // SEL_BLK)
    pos = (pos_ref[...].astype(F32) + jnp.where(row0, blk0.astype(F32), 0.0)).astype(BF16)
    for k in range(KV_HEADS):
        kT = jnp.concatenate([r[0, k * HEAD_DIM:(k + 1) * HEAD_DIM, :] for r in page_refs], axis=1).astype(BF16)
        vT = jnp.concatenate([r[0, KV_DIM + k * HEAD_DIM:KV_DIM + (k + 1) * HEAD_DIM, :] for r in page_refs],
                             axis=1).astype(BF16)
        bias = jnp.full((1, width), NEG_INF, F32)
        for blk in range(width // SEL_BLK):
            chosen = sel_ref[b, k * ns_pad + blk0 + blk] > 0
            bias = jnp.where(lane_blk == blk, jnp.where(chosen, 0.0, NEG_INF), bias)
        _flash_step(q4[k], jnp.concatenate([kT, pos], axis=0), vT, bias, m_scr.at[k], l_scr.at[k], acc_scr.at[k])

    @pl.when(j == pl.num_programs(1) - 1)
    def _():
        nb = ksn_ref.shape[2]
        pick = lax.broadcasted_iota(jnp.int32, (1, nb), 1) == b
        ks_new = jnp.sum(jnp.where(pick, ksn_ref[0], 0.0), axis=1, keepdims=True)
        kw_new = jnp.sum(jnp.where(pick, kwn_ref[0], 0.0), axis=1, keepdims=True)
        lane = lax.broadcasted_iota(jnp.int32, (1, LANES), 1)
        tile_new = jnp.where(lane == 0, ks_new, 0.0).astype(BF16)
        pos_new = jnp.where(row0 & (lane == 0), float(past // SEL_BLK), 0.0).astype(BF16)
        bias_new = jnp.where(lane == 0, 0.0, NEG_INF)
        wlane = lax.broadcasted_iota(jnp.int32, (1, win_ref.shape[2]), 1)
        wout = jnp.where(wlane == win_ref.shape[2] - 1, kw_new, pltpu.roll(win_ref[0], win_ref.shape[2] - 1, 1))
        wout_ref[0] = wout
        woutb = wout.astype(BF16)
        gates = g_ref[0]
        for k in range(KV_HEADS):
            ksl = slice(k * HEAD_DIM, (k + 1) * HEAD_DIM)
            vsl = slice(KV_DIM + k * HEAD_DIM, KV_DIM + (k + 1) * HEAD_DIM)
            _flash_step(q4[k], jnp.concatenate([tile_new[ksl], pos_new], axis=0), tile_new[vsl], bias_new,
                        m_scr.at[k], l_scr.at[k], acc_scr.at[k])
            o_sel = acc_scr[k] / jnp.maximum(l_scr[k], 1e-30)
            m_scr[k] = jnp.full(m_scr.shape[1:], NEG_INF, F32)
            l_scr[k] = jnp.zeros(l_scr.shape[1:], F32)
            acc_scr[k] = jnp.zeros(acc_scr.shape[1:], F32)
            _flash_step(q4[k], jnp.concatenate([woutb[ksl], wpos_ref[...]], axis=0), woutb[vsl],
                        jnp.zeros((1, win_ref.shape[2]), F32), m_scr.at[k], l_scr.at[k], acc_scr.at[k])
            o_win = acc_scr[k] / jnp.maximum(l_scr[k], 1e-30)
            for g in range(GQA):
                head = k * GQA + g
                cols = slice(head * HEAD_DIM, (head + 1) * HEAD_DIM)
                c0 = k * LANES + 3 * g
                o = (gates[:, c0:c0 + 1] * ocmp_ref[0, :, cols] + gates[:, c0 + 1:c0 + 2] * o_sel[g:g + 1, :]
                     + gates[:, c0 + 2:c0 + 3] * o_win[g:g + 1, :])
                o_ref[0, :, cols] = o.astype(o_ref.dtype)


def _sel_win_sample(q3, g3, ocmp, sel, pool_sel, page_table, win, ksT_new, kwT_new, past):
    nb, n_pages = page_table.shape
    wlen = win.shape[2]
    assert wlen == WINDOW and past >= WINDOW
    g = min(16, n_pages)
    ns_pad = sel.shape[2]
    pos, wpos, slopes = _pos_rows(g * PAGE_SIZE), _pos_rows(wlen, past - wlen + 1), _slope_rows()
    full = lambda a: pl.BlockSpec(a.shape, lambda b, j, pt, sl: (0,) * a.ndim)
    row = lambda n: pl.BlockSpec((1, 1, n), lambda b, j, pt, sl: (b, 0, 0))
    page_spec = lambda k: pl.BlockSpec((1, 256, PAGE_SIZE), lambda b, j, pt, sl: (pt[b, j * g + k], 0, 0))
    wspec = pl.BlockSpec((1, 256, wlen), lambda b, j, pt, sl: (b, 0, 0))
    return pl.pallas_call(
        functools.partial(_sel_win_sample_body, past, g),
        grid_spec=pltpu.PrefetchScalarGridSpec(
            num_scalar_prefetch=2,
            grid=(nb, n_pages // g),
            in_specs=[row(NSA_DIM), row(KV_HEADS * LANES), row(NSA_DIM)] + [page_spec(k) for k in range(g)]
                     + [wspec, full(ksT_new), full(kwT_new), full(pos), full(wpos), full(slopes)],
            out_specs=[row(NSA_DIM), wspec],
            scratch_shapes=[pltpu.VMEM((KV_HEADS, SUBLANES, 1), F32), pltpu.VMEM((KV_HEADS, SUBLANES, 1), F32),
                            pltpu.VMEM((KV_HEADS, SUBLANES, HEAD_DIM), F32)],
        ),
        out_shape=[jax.ShapeDtypeStruct((nb, 1, NSA_DIM), BF16), jax.ShapeDtypeStruct((nb, 256, wlen), F32)],
        compiler_params=_cparams(("arbitrary", "arbitrary")),
        name="sel_win_sample",
    )(page_table, sel.reshape(nb, KV_HEADS * ns_pad), q3, g3, ocmp, *([pool_sel] * g), win, ksT_new, kwT_new, pos,
      wpos, slopes)


def _softplus(v):
    return jnp.maximum(v, 0.0) + jnp.log1p(jnp.exp(-jnp.abs(v)))


def _tn_dot(a, b):
    return lax.dot_general(a, b, (((0,), (0,)), ((), ())), preferred_element_type=F32)


def _prep_ssd(conv_w, conv_b, dt_bias, a_log, d_skip, ssd_norm):
    padl = lambda v: jnp.pad(v.reshape(1, -1), ((0, 0), (0, LANES - v.shape[0])))
    L = SSD_CHUNK
    tril = jnp.asarray(np.tril(np.ones((L, L), np.float32)))
    return dict(conv_w=conv_w, conv_b=conv_b.reshape(1, -1), dtb_row=padl(dt_bias), dtb_col=dt_bias.reshape(-1, 1),
                alog_row=padl(a_log), alog_col=a_log.reshape(-1, 1), dskip=padl(d_skip), norm=ssd_norm.reshape(1, -1),
                tril=tril, triu=tril.T)


def _ssd_chunk(u, z, dt, dtT, h_prev, P):
    a_row = -jnp.exp(P["alog_row"][...])
    a_col = -jnp.exp(P["alog_col"][...])
    acum = jnp.dot(P["tril"][...], dt * a_row, preferred_element_type=F32, precision=HIGHEST)
    acumT = jnp.dot(dtT * a_col, P["triu"][...], preferred_element_type=F32, precision=HIGHEST)
    L = u.shape[0]
    li = lax.broadcasted_iota(jnp.int32, (L, L), 0)
    si = lax.broadcasted_iota(jnp.int32, (L, L), 1)
    causal = li >= si
    gn = SSD_GROUPS * D_STATE
    ys, hs = [], []
    per = SSD_HEADS // SSD_GROUPS
    for g in range(SSD_GROUPS):
        bm = u[:, D_INNER + g * D_STATE:D_INNER + (g + 1) * D_STATE]
        cm = u[:, D_INNER + gn + g * D_STATE:D_INNER + gn + (g + 1) * D_STATE]
        bmb = bm.astype(BF16)
        cb = _nt_dot(cm.astype(BF16), bmb)
        for e in range(per):
            h = g * per + e
            ac = acum[:, h:h + 1]
            seg = ac - acumT[h:h + 1, :]
            decay = jnp.where(causal, jnp.exp(jnp.where(causal, seg, 0.0)), 0.0)
            xs = u[:, h * SSD_HEAD_DIM:(h + 1) * SSD_HEAD_DIM]
            xdt = xs * dt[:, h:h + 1]
            y = jnp.dot((cb * decay).astype(BF16), xdt.astype(BF16), preferred_element_type=F32)
            a_last = acum[L - 1:L, h:h + 1]
            st = _tn_dot((xdt * jnp.exp(a_last - ac)).astype(BF16), bmb)
            y = y + _nt_dot((cm * jnp.exp(ac)).astype(BF16), h_prev[h].astype(BF16))
            hs.append(jnp.exp(a_last) * h_prev[h] + st)
            ys.append(y + P["dskip"][:, h:h + 1] * xs)
    return ys, hs


def _ssd_finish(ys, z, norm_w):
    y = jnp.concatenate(ys, axis=1) * _silu(z)
    ms = jnp.mean(y * y, axis=-1, keepdims=True)
    return y * lax.rsqrt(ms + RMS_EPS) * norm_w


def _ssd_prompt_body(xbc_ref, z_ref, dt_ref, dtT_ref, cw_ref, cb_ref, dtbr_ref, dtbc_ref, alr_ref, alc_ref, dsk_ref,
                     nrm_ref, tril_ref, triu_ref, y_ref, hout_ref, xpad_scr, h_scr):
    c = pl.program_id(1)
    L = xbc_ref.shape[1]

    @pl.when(c == 0)
    def _():
        xpad_scr[0:SUBLANES, :] = jnp.zeros((SUBLANES, xpad_scr.shape[1]), F32)
        h_scr[...] = jnp.zeros(h_scr.shape, F32)

    xt = xbc_ref[0]
    xpad_scr[SUBLANES:SUBLANES + L, :] = xt
    conv = cb_ref[...] + xpad_scr[SUBLANES - (CONV_W - 1):SUBLANES - (CONV_W - 1) + L, :] * cw_ref[0:1, :]
    for k in range(1, CONV_W):
        o = SUBLANES - (CONV_W - 1) + k
        conv = conv + xpad_scr[o:o + L, :] * cw_ref[k:k + 1, :]
    xpad_scr[0:SUBLANES, :] = xt[L - SUBLANES:L, :]
    u = _silu(conv)
    dt = _softplus(dt_ref[0] + dtbr_ref[...])
    dtT = _softplus(dtT_ref[0] + dtbc_ref[...])
    P = dict(alog_row=alr_ref, alog_col=alc_ref, tril=tril_ref, triu=triu_ref, dskip=dsk_ref[...])
    ys, hs = _ssd_chunk(u, z_ref[0], dt, dtT, [h_scr[h] for h in range(SSD_HEADS)], P)
    for h in range(SSD_HEADS):
        h_scr[h] = hs[h]
    y_ref[0] = _ssd_finish(ys, z_ref[0], nrm_ref[...]).astype(y_ref.dtype)

    @pl.when(c == pl.num_programs(1) - 1)
    def _():
        hout_ref[0] = h_scr[...]


def _ssd_prompt(xbc, z, dt, dtT, SP):
    nb, t, cd = xbc.shape
    L = SSD_CHUNK
    assert t % L == 0
    full = lambda a: pl.BlockSpec(a.shape, lambda b, c: (0,) * a.ndim)
    names = ("conv_w", "conv_b", "dtb_row", "dtb_col", "alog_row", "alog_col", "dskip", "norm", "tril", "triu")
    ps = [SP[n] for n in names]
    return pl.pallas_call(
        _ssd_prompt_body,
        grid=(nb, t // L),
        in_specs=[pl.BlockSpec((1, L, cd), lambda b, c: (b, c, 0)),
                  pl.BlockSpec((1, L, D_INNER), lambda b, c: (b, c, 0)),
                  pl.BlockSpec((1, L, LANES), lambda b, c: (b, c, 0)),
                  pl.BlockSpec((1, SSD_HEADS, L), lambda b, c: (b, 0, c))] + [full(a) for a in ps],
        out_specs=[pl.BlockSpec((1, L, D_INNER), lambda b, c: (b, c, 0)),
                   pl.BlockSpec((1, SSD_HEADS, SSD_HEAD_DIM, D_STATE), lambda b, c: (b, 0, 0, 0))],
        out_shape=[jax.ShapeDtypeStruct((nb, t, D_INNER), BF16),
                   jax.ShapeDtypeStruct((nb, SSD_HEADS, SSD_HEAD_DIM, D_STATE), F32)],
        scratch_shapes=[pltpu.VMEM((SUBLANES + L, cd), F32), pltpu.VMEM((SSD_HEADS, SSD_HEAD_DIM, D_STATE), F32)],
        compiler_params=_cparams(("arbitrary", "arbitrary")),
        name="ssd_prompt",
    )(xbc, z, dt, dtT, *ps)


def _ssd_sample_body(cs_ref, xbc_ref, z_ref, dt_ref, h0_ref, cw_ref, cb_ref, dtb_ref, al_ref, dsk_ref, nrm_ref, eye_ref,
                     y_ref, cso_ref, h_ref):
    nseq = xbc_ref.shape[0]
    xn = xbc_ref[...]
    conv = cb_ref[...] + xn * cw_ref[CONV_W - 1:CONV_W, :]
    for k in range(CONV_W - 1):
        conv = conv + cs_ref[k] * cw_ref[k:k + 1, :]
        if k > 0:
            cso_ref[k - 1] = cs_ref[k]
    cso_ref[CONV_W - 2] = xn
    u = _silu(conv)
    dt = _softplus(dt_ref[...] + dtb_ref[...])
    decay = jnp.exp(dt * (-jnp.exp(al_ref[...])))
    eye = eye_ref[...]
    gn = SSD_GROUPS * D_STATE
    per = SSD_HEADS // SSD_GROUPS
    rows = []
    for s in range(nseq):
        ys = []
        for h in range(SSD_HEADS):
            g = h // per
            xs = u[s:s + 1, h * SSD_HEAD_DIM:(h + 1) * SSD_HEAD_DIM]
            bm = u[s:s + 1, D_INNER + g * D_STATE:D_INNER + (g + 1) * D_STATE]
            cm = u[s:s + 1, D_INNER + gn + g * D_STATE:D_INNER + gn + (g + 1) * D_STATE]
            xcol = jnp.sum(eye * xs, axis=1, keepdims=True)
            hn = decay[s:s + 1, h:h + 1] * h0_ref[s, h] + (dt[s:s + 1, h:h + 1] * xcol) * bm
            h_ref[s, h] = hn
            ycol = jnp.sum(hn * cm, axis=1, keepdims=True)
            ys.append(jnp.sum(eye * ycol, axis=0, keepdims=True) + dsk_ref[:, h:h + 1] * xs)
        rows.append(jnp.concatenate(ys, axis=1))
    y = jnp.concatenate(rows, axis=0) * _silu(z_ref[...])
    ms = jnp.mean(y * y, axis=-1, keepdims=True)
    y_ref[...] = (y * lax.rsqrt(ms + RMS_EPS) * nrm_ref[...]).astype(y_ref.dtype)


def _ssd_sample(conv_state, xbc, z, dt, h0, SP):
    nb, cd = xbc.shape
    ts = SUBLANES
    assert nb % ts == 0
    eye = jnp.eye(SSD_HEAD_DIM, dtype=F32)
    names = ("conv_w", "conv_b", "dtb_row", "alog_row", "dskip", "norm")
    ps = [SP[n] for n in names] + [eye]
    full = lambda a: pl.BlockSpec(a.shape, lambda i: (0,) * a.ndim)
    st = pl.BlockSpec((ts, SSD_HEADS, SSD_HEAD_DIM, D_STATE), lambda i: (i, 0, 0, 0))
    cs = pl.BlockSpec((CONV_W - 1, ts, cd), lambda i: (0, i, 0))
    row = lambda n: pl.BlockSpec((ts, n), lambda i: (i, 0))
    return pl.pallas_call(
        _ssd_sample_body,
        grid=(nb // ts,),
        in_specs=[cs, row(cd), row(D_INNER), row(LANES), st] + [full(a) for a in ps],
        out_specs=[row(D_INNER), cs, st],
        out_shape=[jax.ShapeDtypeStruct((nb, D_INNER), BF16),
                   jax.ShapeDtypeStruct((CONV_W - 1, nb, cd), F32),
                   jax.ShapeDtypeStruct(h0.shape, F32)],
        compiler_params=_cparams(("arbitrary",)),
        name="ssd_sample",
    )(conv_state, xbc, z, dt, h0, *ps)


def _pack_bf16_pairs(v):
    m = v.shape[1] // 2
    hi = pltpu.bitcast(v[:, :m].astype(BF16).astype(F32), jnp.uint32)
    lo = pltpu.bitcast(v[:, m:].astype(BF16).astype(F32), jnp.uint32)
    return hi | (lo >> 16)


def _unpack_bf16_pairs(w):
    hi = pltpu.bitcast(w & jnp.uint32(0xFFFF0000), F32)
    lo = pltpu.bitcast(w << 16, F32)
    return hi.astype(BF16), lo.astype(BF16)


def _route(logitsT, bias_col):
    s = jax.nn.sigmoid(logitsT)
    sb = s + bias_col
    per = N_EXPERTS // N_EXPERT_GROUPS
    assert per == SUBLANES
    grp = [sb[per * a:per * (a + 1)] for a in range(N_EXPERT_GROUPS)]
    sub = lax.broadcasted_iota(jnp.int32, grp[0].shape, 0)
    gs = []
    for ga in grp:
        m1 = jnp.max(ga, axis=0, keepdims=True)
        first = jnp.min(jnp.where(ga == m1, sub, per), axis=0, keepdims=True)
        m2 = jnp.max(jnp.where(sub == first, NEG_INF, ga), axis=0, keepdims=True)
        gs.append(m1 + m2)
    gmask = _topk_mask([jnp.concatenate(gs, axis=0)], TOPK_GROUPS)[0]
    masked = [jnp.where(gmask[a:a + 1, :] > 0.5, grp[a], NEG_INF) for a in range(N_EXPERT_GROUPS)]
    sel = jnp.concatenate(_topk_mask(masked, TOP_K), axis=0)
    w = s * sel
    w = w / jnp.sum(w, axis=0, keepdims=True) * ROUTED_SCALE
    return sel, w


def _post_mix_body(x_ref, on_ref, ys_ref, g1_ref, sc_ref, sh_ref, nw_ref, wo_ref, rw_ref, rb_ref,
                   x1_ref, hp_ref, selT_ref, wT_ref, cnt_ref):
    first = (pl.program_id(0) == 0) & (pl.program_id(1) == 0)
    half = wo_ref.shape[0] // 2
    mix = (jnp.dot(on_ref[0], wo_ref[0:half, :], preferred_element_type=F32)
           + jnp.dot(ys_ref[0], wo_ref[half:, :], preferred_element_type=F32))
    x1 = x_ref[0] + g1_ref[0] * mix
    x1_ref[0] = x1
    ms = jnp.mean(x1 * x1, axis=-1, keepdims=True)
    h = x1 * lax.rsqrt(ms + RMS_EPS) * nw_ref[...]
    h = h * (1.0 + sc_ref[0]) + sh_ref[0]
    hp_ref[0] = _pack_bf16_pairs(h)
    logitsT = lax.dot_general(rw_ref[...], h, (((1,), (1,)), ((), ())), preferred_element_type=F32,
                              precision=HIGHEST)
    sel, w = _route(logitsT, rb_ref[...])
    selT_ref[...] = sel.astype(selT_ref.dtype)
    wT_ref[...] = w

    @pl.when(first)
    def _():
        cnt_ref[...] = jnp.zeros(cnt_ref.shape, F32)

    cnt_ref[...] += jnp.broadcast_to(jnp.sum(sel, axis=1, keepdims=True), cnt_ref.shape)


def _post_mix(x, o_nsa, y_ssd, g1, sc2, sh2, norm_w, w_out_b, router_wT, router_bias, tm):
    nb, t, d = x.shape
    mt = g1.shape[1]
    nt = t // tm
    assert t % tm == 0 and (mt == 1 or mt == t)
    if mt == 1:
        mod_spec = pl.BlockSpec((1, 1, d), lambda b, i: (b, 0, 0))
    else:
        mod_spec = pl.BlockSpec((1, tm, d), lambda b, i: (b, i, 0))
    row = lambda n: pl.BlockSpec((1, tm, n), lambda b, i: (b, i, 0))
    full = lambda a: pl.BlockSpec(a.shape, lambda b, i: (0,) * a.ndim)
    tok = lambda: pl.BlockSpec((N_EXPERTS, tm), lambda b, i: (0, b * nt + i))
    rb = router_bias.reshape(N_EXPERTS, 1)
    nw = norm_w.reshape(1, d)
    return pl.pallas_call(
        _post_mix_body,
        grid=(nb, nt),
        in_specs=[row(d), row(NSA_DIM), row(D_INNER), mod_spec, mod_spec, mod_spec, full(nw), full(w_out_b),
                  full(router_wT), full(rb)],
        out_specs=[row(d), row(d // 2), tok(), tok(), pl.BlockSpec((N_EXPERTS, LANES), lambda b, i: (0, 0))],
        out_shape=[jax.ShapeDtypeStruct((nb, t, d), F32),
                   jax.ShapeDtypeStruct((nb, t, d // 2), jnp.uint32),
                   jax.ShapeDtypeStruct((N_EXPERTS, nb * t), BF16),
                   jax.ShapeDtypeStruct((N_EXPERTS, nb * t), F32),
                   jax.ShapeDtypeStruct((N_EXPERTS, LANES), F32)],
        compiler_params=_cparams(("arbitrary", "arbitrary")),
        name="post_mix",
    )(x, o_nsa, y_ssd, g1, sc2, sh2, nw, w_out_b, router_wT, rb)


MOE_BLOCK = 256
MOE_BLOCK_SHIFT = 8


def _moe_rows(n_tok):
    n_blocks = n_tok * TOP_K // MOE_BLOCK + N_EXPERTS
    n_blocks_pad = -(-n_blocks // LANES) * LANES
    return n_blocks, n_blocks_pad


def _plan_body(selT_ref, wT_ref, cnt_ref, triu_ref, tril_ref, eye_ref, dest_ref, w8_ref, be_ref, fill_ref,
               carry_scr, pstart_scr):
    step = pl.program_id(0)
    ne = N_EXPERTS

    @pl.when(step == 0)
    def _():
        cnt = cnt_ref[...]
        cnt_i = cnt.astype(jnp.int32)
        padded = (((cnt_i + (MOE_BLOCK - 1)) >> MOE_BLOCK_SHIFT) << MOE_BLOCK_SHIFT).astype(F32)
        pstart = jnp.dot(tril_ref[...].astype(F32), padded, preferred_element_type=F32, precision=HIGHEST)
        pstart_scr[...] = pstart
        carry_scr[...] = jnp.zeros(carry_scr.shape, F32)
        pend = pstart + padded
        nbp = be_ref.shape[1]
        starts = (lax.broadcasted_iota(jnp.int32, (1, nbp), 1) * MOE_BLOCK).astype(F32)
        below = jnp.where(pend[:, 0:1] <= starts, 1.0, 0.0)
        be_ref[...] = jnp.minimum(jnp.sum(below, axis=0, keepdims=True), float(ne - 1)).astype(jnp.int32)
        eye = eye_ref[...]
        to_row = lambda col: jnp.sum(col * eye, axis=0, keepdims=True)
        n_used = jnp.max(pend, axis=0, keepdims=True) * (1.0 / MOE_BLOCK)
        rows = jnp.concatenate([to_row(pstart + cnt), to_row(padded - cnt), n_used,
                                jnp.zeros((SUBLANES - 3, LANES), F32)], axis=0)
        fill_ref[...] = rows.astype(jnp.int32)

    sel = selT_ref[...]
    self32 = sel.astype(F32)
    rank = jnp.dot(sel, triu_ref[...], preferred_element_type=F32) + carry_scr[:, 0:1]
    carry_scr[...] += jnp.broadcast_to(jnp.sum(self32, axis=1, keepdims=True), carry_scr.shape)
    dest = pstart_scr[:, 0:1] + rank
    slot = jnp.dot(tril_ref[...], sel, preferred_element_type=F32)
    w = wT_ref[...]
    drows, wrows = [], []
    for k in range(TOP_K):
        pick = jnp.where(slot == float(k), self32, 0.0)
        drows.append(jnp.sum(pick * dest, axis=0, keepdims=True))
        wrows.append(jnp.sum(pick * w, axis=0, keepdims=True))
    dest_ref[...] = jnp.concatenate(drows, axis=0).astype(jnp.int32)
    w8_ref[...] = jnp.concatenate(wrows, axis=0)


def _moe_plan(selT, wT, cnt, tile):
    ne, n = selT.shape
    assert n % tile == 0
    _, nbp = _moe_rows(n)
    triu = jnp.asarray(np.triu(np.ones((tile, tile), np.float32), 1), BF16)
    tril = jnp.asarray(np.tril(np.ones((ne, ne), np.float32), -1), BF16)
    eye = jnp.asarray(np.eye(ne, LANES, dtype=np.float32))
    full = lambda a: pl.BlockSpec(a.shape, lambda i: (0,) * a.ndim)
    return pl.pallas_call(
        _plan_body,
        grid=(n // tile,),
        in_specs=[pl.BlockSpec((ne, tile), lambda i: (0, i)), pl.BlockSpec((ne, tile), lambda i: (0, i)),
                  full(cnt), full(triu), full(tril), full(eye)],
        out_specs=[pl.BlockSpec((TOP_K, tile), lambda i: (0, i)), pl.BlockSpec((TOP_K, tile), lambda i: (0, i)),
                   pl.BlockSpec((1, nbp), lambda i: (0, 0)), pl.BlockSpec((SUBLANES, LANES), lambda i: (0, 0))],
        out_shape=[jax.ShapeDtypeStruct((TOP_K, n), jnp.int32), jax.ShapeDtypeStruct((TOP_K, n), F32),
                   jax.ShapeDtypeStruct((1, nbp), jnp.int32), jax.ShapeDtypeStruct((SUBLANES, LANES), jnp.int32)],
        scratch_shapes=[pltpu.VMEM((ne, LANES), F32), pltpu.VMEM((ne, LANES), F32)],
        compiler_params=_cparams(("arbitrary",)),
        name="moe_plan",
    )(selT, wT, cnt, triu, tril, eye)


_FILL_PIECES = tuple(1 << s for s in reversed(range(MOE_BLOCK_SHIFT)))


def _dispatch_body(dest_ref, fill_ref, hp_ref, xd_ref, zero_scr, sem, zsem):
    step = pl.program_id(0)
    tile = hp_ref.shape[0]

    def row_copy(t, k):
        return pltpu.make_async_copy(hp_ref.at[pl.ds(t, 1)], xd_ref.at[pl.ds(dest_ref[k, t], 1)], sem)

    def fill_pass(wait):
        def per_expert(e, carry):
            start = fill_ref[0, e]
            n = fill_ref[1, e]
            head = n & (SUBLANES - 1)
            for r in range(SUBLANES - 1):
                @pl.when(r < head)
                def _():
                    cp = pltpu.make_async_copy(zero_scr.at[pl.ds(0, 1)], xd_ref.at[pl.ds(start + r, 1)], zsem)
                    cp.wait() if wait else cp.start()

            cur = start + head
            for p in _FILL_PIECES:
                if p < SUBLANES:
                    continue
                hit = (n & p) != 0

                @pl.when(hit)
                def _():
                    off = pl.multiple_of(cur, SUBLANES)
                    cp = pltpu.make_async_copy(zero_scr.at[pl.ds(0, p)], xd_ref.at[pl.ds(off, p)], zsem)
                    cp.wait() if wait else cp.start()

                cur = cur + jnp.where(hit, p, 0)
            return carry

        lax.fori_loop(0, N_EXPERTS, per_expert, 0)

    @pl.when(step == 0)
    def _():
        zero_scr[...] = jnp.zeros(zero_scr.shape, zero_scr.dtype)
        fill_pass(False)
        fill_pass(True)

    def issue(t, carry):
        for k in range(TOP_K):
            row_copy(t, k).start()
        return carry

    def drain(t, carry):
        for k in range(TOP_K):
            row_copy(t, k).wait()
        return carry

    lax.fori_loop(0, tile, issue, 0)
    lax.fori_loop(0, tile, drain, 0)


def _moe_dispatch(hp, dest8, fill, tile):
    n, m = hp.shape
    n_blocks, _ = _moe_rows(n)
    nr = n_blocks * MOE_BLOCK
    return pl.pallas_call(
        _dispatch_body,
        grid=(n // tile,),
        in_specs=[pl.BlockSpec((TOP_K, tile), lambda i: (0, i), memory_space=pltpu.SMEM),
                  pl.BlockSpec(memory_space=pltpu.SMEM),
                  pl.BlockSpec((tile, m), lambda i: (i, 0))],
        out_specs=pl.BlockSpec(memory_space=pl.ANY),
        out_shape=jax.ShapeDtypeStruct((nr, m), jnp.uint32),
        scratch_shapes=[pltpu.VMEM((_FILL_PIECES[0], m), jnp.uint32), pltpu.SemaphoreType.DMA(()),
                        pltpu.SemaphoreType.DMA(())],
        compiler_params=_cparams(("arbitrary",)),
        name="moe_dispatch",
    )(dest8, fill, hp)


def _swiglu_packed(xw, w1, w3, w2):
    xa, xb = _unpack_bf16_pairs(xw)
    half = xa.shape[1]
    mm = lambda w: (jnp.dot(xa, w[0:half, :], preferred_element_type=F32)
                    + jnp.dot(xb, w[half:, :], preferred_element_type=F32))
    hid = _silu(mm(w1)) * mm(w3)
    return jnp.dot(hid.astype(BF16), w2, preferred_element_type=F32)


def _experts_body(be_ref, nu_ref, xd_ref, w1_ref, w3_ref, w2_ref, yd_ref):
    i = pl.program_id(0)

    @pl.when(i < nu_ref[0])
    def _():
        yd_ref[...] = _pack_bf16_pairs(_swiglu_packed(xd_ref[...], w1_ref[0], w3_ref[0], w2_ref[0]))

    @pl.when(i >= nu_ref[0])
    def _():
        yd_ref[...] = jnp.zeros(yd_ref.shape, yd_ref.dtype)


def _moe_experts(xd, block_e, n_used, w1b, w3b, w2b):
    nr, m = xd.shape
    n_blocks = nr // MOE_BLOCK
    d, f = w1b.shape[1:]
    clamp = lambda i, nu: jnp.minimum(i, nu[0] - 1)
    return pl.pallas_call(
        _experts_body,
        grid_spec=pltpu.PrefetchScalarGridSpec(
            num_scalar_prefetch=2,
            grid=(n_blocks,),
            in_specs=[pl.BlockSpec((MOE_BLOCK, m), lambda i, be, nu: (clamp(i, nu), 0)),
                      pl.BlockSpec((1, d, f), lambda i, be, nu: (be[clamp(i, nu)], 0, 0)),
                      pl.BlockSpec((1, d, f), lambda i, be, nu: (be[clamp(i, nu)], 0, 0)),
                      pl.BlockSpec((1, f, d), lambda i, be, nu: (be[clamp(i, nu)], 0, 0))],
            out_specs=pl.BlockSpec((MOE_BLOCK, m), lambda i, be, nu: (i, 0)),
        ),
        out_shape=jax.ShapeDtypeStruct((nr, m), jnp.uint32),
        compiler_params=_cparams(("arbitrary",)),
        name="moe_experts",
    )(block_e, n_used, xd, w1b, w3b, w2b)


def _combine_body(dest_ref, w8_ref, hp_ref, x1_ref, g2_ref, eye_ref, sw1_ref, sw3_ref, sw2_ref, nf_ref, yd_ref,
                  o_ref, ybuf, sem):
    tile = hp_ref.shape[1]

    def row_copy(t, k):
        return pltpu.make_async_copy(yd_ref.at[pl.ds(dest_ref[k, t], 1)], ybuf.at[k, pl.ds(t, 1)], sem)

    def issue(t, carry):
        for k in range(TOP_K):
            row_copy(t, k).start()
        return carry

    def drain(t, carry):
        for k in range(TOP_K):
            row_copy(t, k).wait()
        return carry

    lax.fori_loop(0, tile, issue, 0)
    shared = _swiglu_packed(hp_ref[0], sw1_ref[...], sw3_ref[...], sw2_ref[...])
    w_rows = lax.dot_general(eye_ref[...], w8_ref[...], (((1,), (1,)), ((), ())), preferred_element_type=F32,
                             precision=HIGHEST)
    lax.fori_loop(0, tile, drain, 0)
    half = ybuf.shape[2]
    acc_a = jnp.zeros((tile, half), F32)
    acc_b = jnp.zeros((tile, half), F32)
    for k in range(TOP_K):
        ya, yb = _unpack_bf16_pairs(ybuf[k])
        wk = w_rows[:, k:k + 1]
        acc_a = acc_a + wk * ya.astype(F32)
        acc_b = acc_b + wk * yb.astype(F32)
    routed = jnp.concatenate([acc_a, acc_b], axis=1)
    x2 = x1_ref[0] + g2_ref[0] * (routed + shared)
    ms = jnp.mean(x2 * x2, axis=-1, keepdims=True)
    o_ref[0] = x2 * lax.rsqrt(ms + RMS_EPS) * nf_ref[...]


def _moe_combine(dest8, w8, hp, x1, g2, yd, sw1b, sw3b, sw2b, norm_f, tile):
    nb, t, d = x1.shape
    nt = t // tile
    mt = g2.shape[1]
    assert t % tile == 0 and (mt == 1 or mt == t)
    if mt == 1:
        mod_spec = pl.BlockSpec((1, 1, d), lambda b, i: (b, 0, 0))
    else:
        mod_spec = pl.BlockSpec((1, tile, d), lambda b, i: (b, i, 0))
    eye = jnp.eye(tile, dtype=F32)
    nf = norm_f.reshape(1, d)
    full = lambda a: pl.BlockSpec(a.shape, lambda b, i: (0,) * a.ndim)
    return pl.pallas_call(
        _combine_body,
        grid=(nb, nt),
        in_specs=[pl.BlockSpec((TOP_K, tile), lambda b, i: (0, b * nt + i), memory_space=pltpu.SMEM),
                  pl.BlockSpec((TOP_K, tile), lambda b, i: (0, b * nt + i)),
                  pl.BlockSpec((1, tile, d // 2), lambda b, i: (b, i, 0)),
                  pl.BlockSpec((1, tile, d), lambda b, i: (b, i, 0)),
                  mod_spec, full(eye), full(sw1b), full(sw3b), full(sw2b), full(nf),
                  pl.BlockSpec(memory_space=pl.ANY)],
        out_specs=pl.BlockSpec((1, tile, d), lambda b, i: (b, i, 0)),
        out_shape=jax.ShapeDtypeStruct((nb, t, d), F32),
        scratch_shapes=[pltpu.VMEM((TOP_K, tile, d // 2), jnp.uint32), pltpu.SemaphoreType.DMA(())],
        compiler_params=_cparams(("arbitrary", "arbitrary")),
        name="moe_combine",
    )(dest8, w8, hp, x1, g2, eye, sw1b, sw3b, sw2b, nf, yd)


def _moe(x1, hp, selT, wT, cnt, g2, EW, norm_f, tile):
    nb, t, d = x1.shape
    n = nb * t
    dest8, w8, block_e, fill = _moe_plan(selT, wT, cnt, tile)
    xd = _moe_dispatch(hp.reshape(n, d // 2), dest8, fill, tile)
    n_blocks, _ = _moe_rows(n)
    yd = _moe_experts(xd, block_e[0, :n_blocks], fill[2, 0:1], EW["w1"], EW["w3"], EW["w2"])
    return _moe_combine(dest8, w8, hp, x1, g2, yd, EW["sw1"], EW["sw3"], EW["sw2"], norm_f, tile)


def kernel(x_prompt, x_sample, c_prompt, c_sample, cache_kv_cmp, cache_kv_sel, cache_kv_win, state_conv, state_ssm, page_table, w_ada, b_ada, norm_mix, norm_ffn, w_in, cmp_pe_k, cmp_w1_k, cmp_w2_k, cmp_pe_v, cmp_w1_v, cmp_w2_v, conv_w, conv_b, dt_bias, a_log, d_skip, ssd_norm, w_out, router_w, router_bias, exp_w1, exp_w3, exp_w2, sh_w1, sh_w3, sh_w2, norm_f):
    nb, t, d = x_prompt.shape
    ndb = x_sample.shape[0]
    c_all = jnp.concatenate([c_prompt, c_sample], axis=0)
    mod = _modulation(c_all, w_ada[0], b_ada[0]).reshape(nb + ndb, 6, d)
    mod_p = [mod[:nb, k][:, None, :] for k in range(6)]
    mod_s = [mod[nb:, k][None, :, :] for k in range(6)]
    W = _prep_w_in(w_in[0])
    P = _in_proj(x_prompt, mod_p[1], mod_p[0], norm_mix[0], W, 512)
    S = _in_proj(x_sample.reshape(1, ndb, d), mod_s[1], mod_s[0], norm_mix[0], W, ndb)
    C = _prep_compress(cmp_pe_k[0], cmp_w1_k[0], cmp_w2_k[0], cmp_pe_v[0], cmp_w1_v[0], cmp_w2_v[0])
    kcvc_p = _compress_prompt(P["kcT"], C)
    ocmp_p, sel_p = _cmp_attn_prompt(P["q"], kcvc_p, 256)
    o_nsa_p = _sel_win_prompt(P["q"], sel_p, P["g"], ocmp_p, P["ksT"], P["kwT"], 256, 512)
    SP = _prep_ssd(conv_w[0], conv_b[0], dt_bias[0], a_log[0], d_skip[0], ssd_norm[0])
    y_ssd_p, ssm_p = _ssd_prompt(P["xbc"], P["z"], P["dt"], P["dtT"], SP)
    w_out_b = w_out[0].astype(BF16)
    router_wT = router_w[0].T
    EW = dict(w1=exp_w1[0].astype(BF16), w3=exp_w3[0].astype(BF16), w2=exp_w2[0].astype(BF16),
              sw1=sh_w1[0].astype(BF16), sw3=sh_w3[0].astype(BF16), sw2=sh_w2[0].astype(BF16))
    x1_p, hp_p, selT_p, wT_p, cnt_p = _post_mix(x_prompt, o_nsa_p, y_ssd_p, mod_p[2], mod_p[4], mod_p[3], norm_ffn[0],
                                                 w_out_b, router_wT, router_bias[0], 512)
    y_prompt = _moe(x1_p, hp_p, selT_p, wT_p, cnt_p, mod_p[5], EW, norm_f, 512)

    n_pool = cache_kv_cmp.shape[1]
    past = page_table.shape[1] * PAGE_SIZE
    to_pages = lambda c: jnp.transpose(c, (0, 2, 3, 4, 1)).reshape(c.shape[0], 256, c.shape[1])
    pool_cmp, pool_sel, win = to_pages(cache_kv_cmp[0]), to_pages(cache_kv_sel[0]), to_pages(cache_kv_win[0])
    kcvc_s = _compress_paged(pool_cmp, page_table, C)
    q3 = S["q"].reshape(ndb, 1, NSA_DIM)
    ocmp_s, sel_s = _cmp_attn_sample(q3, kcvc_s, past)
    o_nsa_s, win_s = _sel_win_sample(q3, S["g"].reshape(ndb, 1, KV_HEADS * LANES), ocmp_s, sel_s, pool_sel, page_table,
                                     win, S["ksT"], S["kwT"], past)
    y_ssd_s, conv_s, ssm_s = _ssd_sample(jnp.transpose(state_conv[0], (1, 0, 2)), S["xbc"][0], S["z"][0], S["dt"][0],
                                         state_ssm[0], SP)
    x1_s, hp_s, selT_s, wT_s, cnt_s = _post_mix(x_sample.reshape(1, ndb, d), o_nsa_s.reshape(1, ndb, NSA_DIM),
                                                 y_ssd_s.reshape(1, ndb, D_INNER), mod_s[2], mod_s[4], mod_s[3],
                                                 norm_ffn[0], w_out_b, router_wT, router_bias[0], ndb)
    y_sample = _moe(x1_s, hp_s, selT_s, wT_s, cnt_s, mod_s[5], EW, norm_f, ndb).reshape(ndb, 1, d)

    from_cm = lambda a: jnp.transpose(a.reshape(a.shape[0], 2, KV_HEADS, HEAD_DIM, a.shape[2]), (0, 4, 1, 2, 3))[None]
    tw = min(WINDOW, t)
    return (y_prompt, y_sample,
            from_cm(P["kcT"]), from_cm(P["ksT"]), from_cm(P["kwT"][:, :, t - tw:]),
            P["xbc"][:, t - (CONV_W - 1):, :][None], ssm_p[None],
            from_cm(jnp.transpose(S["kcT"], (2, 1, 0))), from_cm(jnp.transpose(S["ksT"], (2, 1, 0))), from_cm(win_s),
            jnp.transpose(conv_s, (1, 0, 2))[None], ssm_s[None])
```

```python
import functools
import math

import jax
import jax.numpy as jnp
import numpy as np
from jax import lax
from jax.experimental import pallas as pl
from jax.experimental.pallas import tpu as pltpu

F32 = jnp.float32
BF16 = jnp.bfloat16
HIGHEST = lax.Precision.HIGHEST

D_MODEL = 1024
NSA_HEADS = 8
KV_HEADS = 2
HEAD_DIM = 64
GQA = NSA_HEADS // KV_HEADS
CMP_BLK = 32
CMP_STRIDE = 16
CMP_HID = 64
SEL_BLK = 64
N_SEL = 16
WINDOW = 512
FORCE_CUR = 2.0e4
FORCE_SINK = 1.0e4
SSD_HEADS = 8
SSD_HEAD_DIM = 64
D_INNER = SSD_HEADS * SSD_HEAD_DIM
SSD_GROUPS = 2
D_STATE = 128
CONV_W = 4
CONV_DIM = D_INNER + 2 * SSD_GROUPS * D_STATE
SSD_CHUNK = 128
NSA_DIM = NSA_HEADS * HEAD_DIM
KV_DIM = KV_HEADS * HEAD_DIM
N_EXPERTS = 64
TOP_K = 8
N_EXPERT_GROUPS = 8
TOPK_GROUPS = 4
D_EXPERT = 256
D_SHARED = 256
ROUTED_SCALE = 2.5
RMS_EPS = 1e-6
NEG_INF = -1e30
PAGE_SIZE = 128

LANES = 128
SUBLANES = 8
VMEM_LIMIT = 56 * 1024 * 1024


def _cparams(sem):
    return pltpu.CompilerParams(dimension_semantics=sem, vmem_limit_bytes=VMEM_LIMIT)


def _silu(v):
    return v * jax.nn.sigmoid(v)


def _mod_body(c_ref, w_ref, b_ref, o_ref):
    s = _silu(c_ref[...])
    o_ref[...] = jnp.dot(s, w_ref[...], preferred_element_type=F32, precision=HIGHEST) + b_ref[...]


def _modulation(c_all, w_ada, b_ada):
    n, d = c_all.shape
    nout = w_ada.shape[1]
    tn = 512
    return pl.pallas_call(
        _mod_body,
        grid=(nout // tn,),
        in_specs=[pl.BlockSpec((n, d), lambda j: (0, 0)),
                  pl.BlockSpec((d, tn), lambda j: (0, j)),
                  pl.BlockSpec((1, tn), lambda j: (0, j))],
        out_specs=pl.BlockSpec((n, tn), lambda j: (0, j)),
        out_shape=jax.ShapeDtypeStruct((n, nout), F32),
        compiler_params=_cparams(("arbitrary",)),
        name="modulation",
    )(c_all, w_ada, b_ada.reshape(1, nout))


def _nt_dot(a, b):
    return lax.dot_general(a, b, (((1,), (1,)), ((), ())), preferred_element_type=F32)


def _proj_body(x_ref, sc_ref, sh_ref, nw_ref, wq_ref, wkvT_ref, wg_ref, wz_ref, wx_ref, wdt_ref, wdtT_ref,
               q_ref, kcT_ref, ksT_ref, kwT_ref, g_ref, z_ref, xbc_ref, dt_ref, dtT_ref):
    x = x_ref[0]
    ms = jnp.mean(x * x, axis=-1, keepdims=True)
    h = x * lax.rsqrt(ms + RMS_EPS) * nw_ref[...]
    h = h * (1.0 + sc_ref[0]) + sh_ref[0]
    hb = h.astype(BF16)
    q_ref[0] = (jnp.dot(hb, wq_ref[...], preferred_element_type=F32) * (HEAD_DIM ** -0.5)).astype(BF16)
    kvT = _nt_dot(wkvT_ref[...], hb)
    kcT_ref[0] = kvT[0:256]
    ksT_ref[0] = kvT[256:512]
    kwT_ref[0] = kvT[512:768]
    g_ref[0] = jax.nn.sigmoid(jnp.dot(hb, wg_ref[...], preferred_element_type=F32))
    z_ref[0] = jnp.dot(hb, wz_ref[...], preferred_element_type=F32)
    xbc_ref[0] = jnp.dot(hb, wx_ref[...], preferred_element_type=F32)
    dt_ref[0] = jnp.dot(hb, wdt_ref[...], preferred_element_type=F32)
    dtT_ref[0] = _nt_dot(wdtT_ref[...], hb)


def _prep_w_in(w_in):
    w = w_in
    o = 0
    wq = w[:, o:o + NSA_DIM]; o += NSA_DIM
    wkv = w[:, o:o + 6 * KV_DIM]; o += 6 * KV_DIM
    wg = w[:, o:o + 3 * NSA_HEADS]; o += 3 * NSA_HEADS
    wz = w[:, o:o + D_INNER]; o += D_INNER
    wx = w[:, o:o + CONV_DIM]; o += CONV_DIM
    wdt = w[:, o:o + SSD_HEADS]; o += SSD_HEADS
    pad = lambda a: jnp.pad(a, ((0, 0), (0, LANES - a.shape[1])))
    per = 3 * GQA
    wg = jnp.concatenate([pad(wg[:, k * per:(k + 1) * per]) for k in range(KV_HEADS)], axis=1)
    return dict(wq=wq.astype(BF16), wkvT=wkv.T.astype(BF16), wg=wg.astype(BF16), wz=wz.astype(BF16),
                wx=wx.astype(BF16), wdt=pad(wdt).astype(BF16), wdtT=wdt.T.astype(BF16))


def _in_proj(x, sc, sh, norm_w, W, tm):
    nb, t, d = x.shape
    mt = sc.shape[1]
    assert t % tm == 0 and (mt == 1 or mt == t)
    if mt == 1:
        mod_spec = pl.BlockSpec((1, 1, d), lambda b, i: (b, 0, 0))
    else:
        mod_spec = pl.BlockSpec((1, tm, d), lambda b, i: (b, i, 0))
    full = lambda a: pl.BlockSpec(a.shape, lambda b, i: (0,) * a.ndim)
    row = lambda n: pl.BlockSpec((1, tm, n), lambda b, i: (b, i, 0))
    col = lambda n: pl.BlockSpec((1, n, tm), lambda b, i: (b, 0, i))
    ws = [W["wq"], W["wkvT"], W["wg"], W["wz"], W["wx"], W["wdt"], W["wdtT"]]
    outs = pl.pallas_call(
        _proj_body,
        grid=(nb, t // tm),
        in_specs=[row(d), mod_spec, mod_spec, pl.BlockSpec((1, d), lambda b, i: (0, 0))] + [full(a) for a in ws],
        out_specs=[row(NSA_DIM), col(256), col(256), col(256), row(KV_HEADS * LANES), row(D_INNER), row(CONV_DIM),
                   row(LANES), col(SSD_HEADS)],
        out_shape=[jax.ShapeDtypeStruct((nb, t, NSA_DIM), BF16),
                   jax.ShapeDtypeStruct((nb, 256, t), F32),
                   jax.ShapeDtypeStruct((nb, 256, t), F32),
                   jax.ShapeDtypeStruct((nb, 256, t), F32),
                   jax.ShapeDtypeStruct((nb, t, KV_HEADS * LANES), F32),
                   jax.ShapeDtypeStruct((nb, t, D_INNER), F32),
                   jax.ShapeDtypeStruct((nb, t, CONV_DIM), F32),
                   jax.ShapeDtypeStruct((nb, t, LANES), F32),
                   jax.ShapeDtypeStruct((nb, SSD_HEADS, t), F32)],
        compiler_params=_cparams(("arbitrary", "arbitrary")),
        name="in_proj",
    )(x, sc, sh, norm_w.reshape(1, d), *ws)
    names = ("q", "kcT", "ksT", "kwT", "g", "z", "xbc", "dt", "dtT")
    return dict(zip(names, outs))


def _prep_compress(cmp_pe_k, cmp_w1_k, cmp_w2_k, cmp_pe_v, cmp_w1_v, cmp_w2_v):
    w1s = jnp.stack([cmp_w1_k, cmp_w1_v]).reshape(2, 2, CMP_STRIDE, HEAD_DIM, CMP_HID)
    eye = jnp.eye(2, dtype=F32)
    wbd = jnp.einsum("ktldh,kK,vV->lkvdtKVh", w1s, eye, eye).reshape(CMP_STRIDE, 256, 512)
    w2s = jnp.stack([cmp_w2_k, cmp_w2_v])
    w2bd = jnp.einsum("khd,kK,vV->kvhKVd", w2s, eye, eye).reshape(256, 256)
    pes = jnp.stack([cmp_pe_k, cmp_pe_v]).reshape(2, 2, CMP_STRIDE, HEAD_DIM)
    pe_rows = jnp.broadcast_to(jnp.transpose(pes, (1, 2, 0, 3))[:, :, :, None, :],
                               (2, CMP_STRIDE, 2, KV_HEADS, HEAD_DIM)).reshape(2, CMP_STRIDE * 256)
    pecat = jnp.pad(pe_rows, ((0, SUBLANES - 2), (0, 0)))
    perm = np.zeros((LANES, LANES), np.float32)
    for l in range(CMP_STRIDE):
        for n in range(LANES // CMP_STRIDE):
            perm[(LANES // CMP_STRIDE) * l + n, CMP_STRIDE * n + l] = 1.0
    bias = pl.pallas_call(
        _cmp_bias_body,
        out_shape=jax.ShapeDtypeStruct((SUBLANES, 512), F32),
        name="cmp_bias",
    )(pecat, wbd.reshape(CMP_STRIDE * 256, 512))
    return dict(wbd=wbd.astype(BF16), w2bd=w2bd.astype(BF16), bias=bias, perm=jnp.asarray(perm, BF16))


def _cmp_bias_body(pe_ref, w_ref, o_ref):
    o_ref[...] = jnp.dot(pe_ref[...], w_ref[...], preferred_element_type=F32, precision=HIGHEST)


def _compress_body(n_pref, n_slab_refs, slabs_per_ref, *refs):
    refs = refs[n_pref:]
    slab_refs = refs[:n_slab_refs]
    perm_ref, wbd_ref, bias_ref, w2bd_ref, o_ref, z_scr = refs[n_slab_refs:]
    j = pl.program_id(1)
    nh = z_scr.shape[1]
    per_slab = LANES // CMP_STRIDE
    g_tot = n_slab_refs * slabs_per_ref
    base = pl.multiple_of(j * (per_slab * g_tot), per_slab)
    perm = perm_ref[...]
    for ri in range(n_slab_refs):
        for si in range(slabs_per_ref):
            slab = slab_refs[ri][0][:, si * LANES:(si + 1) * LANES].astype(BF16)
            xp = _nt_dot(perm, slab)
            g = ri * slabs_per_ref + si
            for l in range(CMP_STRIDE):
                z_scr[l, pl.ds(base + per_slab * g, per_slab), :] = xp[per_slab * l:per_slab * (l + 1), :]

    @pl.when(j == pl.num_programs(1) - 1)
    def _():
        acc = jnp.zeros((nh, 512), F32)
        for l in range(CMP_STRIDE):
            acc = acc + jnp.dot(z_scr[l].astype(BF16), wbd_ref[l], preferred_element_type=F32)
        lead = acc[:, :256] + bias_ref[0:1, :256]
        tail = acc[:, 256:] + bias_ref[1:2, 256:]
        hid = _silu(lead + pltpu.roll(tail, nh - 1, 0))
        out = jnp.dot(hid.astype(BF16), w2bd_ref[...], preferred_element_type=F32)
        row = lax.broadcasted_iota(jnp.int32, out.shape, 0)
        o_ref[0] = jnp.where(row < nh - 1, out, 0.0)


def _compress_prompt(kcT, C):
    nb, _, t = kcT.shape
    nh = t // CMP_STRIDE
    g = min(8, t // LANES)
    nsteps = t // (LANES * g)
    full = lambda a: pl.BlockSpec(a.shape, lambda b, j: (0,) * a.ndim)
    return pl.pallas_call(
        functools.partial(_compress_body, 0, 1, g),
        grid=(nb, nsteps),
        in_specs=[pl.BlockSpec((1, 256, LANES * g), lambda b, j: (b, 0, j)),
                  full(C["perm"]), full(C["wbd"]), full(C["bias"]), full(C["w2bd"])],
        out_specs=pl.BlockSpec((1, nh, 256), lambda b, j: (b, 0, 0)),
        out_shape=jax.ShapeDtypeStruct((nb, nh, 256), F32),
        scratch_shapes=[pltpu.VMEM((CMP_STRIDE, nh, 256), F32)],
        compiler_params=_cparams(("arbitrary", "arbitrary")),
        name="compress_prompt",
    )(kcT, C["perm"], C["wbd"], C["bias"], C["w2bd"])


def _compress_paged(pool, page_table, C):
    nb, n_pages = page_table.shape
    nh = n_pages * (PAGE_SIZE // CMP_STRIDE)
    g = min(16, n_pages)
    nsteps = n_pages // g
    full = lambda a: pl.BlockSpec(a.shape, lambda b, j, pt: (0,) * a.ndim)
    page_spec = lambda k: pl.BlockSpec((1, 256, LANES), lambda b, j, pt: (pt[b, j * g + k], 0, 0))
    return pl.pallas_call(
        functools.partial(_compress_body, 1, g, 1),
        grid_spec=pltpu.PrefetchScalarGridSpec(
            num_scalar_prefetch=1,
            grid=(nb, nsteps),
            in_specs=[page_spec(k) for k in range(g)] + [full(C["perm"]), full(C["wbd"]), full(C["bias"]), full(C["w2bd"])],
            out_specs=pl.BlockSpec((1, nh, 256), lambda b, j, pt: (b, 0, 0)),
            scratch_shapes=[pltpu.VMEM((CMP_STRIDE, nh, 256), F32)],
        ),
        out_shape=jax.ShapeDtypeStruct((nb, nh, 256), F32),
        compiler_params=_cparams(("arbitrary", "arbitrary")),
        name="compress_paged",
    )(page_table, *([pool] * g), C["perm"], C["wbd"], C["bias"], C["w2bd"])


def _alibi_slope(head):
    return float(2.0 ** (-8.0 * (head + 1) / NSA_HEADS))


def _overlap_T(nc, ns):
    cst = np.arange(nc)[None, :] * CMP_STRIDE
    sst = np.arange(ns)[:, None] * SEL_BLK
    ov = np.clip(np.minimum(cst + CMP_BLK, sst + SEL_BLK) - np.maximum(cst, sst), 0, None).astype(np.float32) / CMP_BLK
    return jnp.asarray(ov, F32)


def _masked_softmax(s, mask):
    s = jnp.where(mask, s, NEG_INF)
    p = jnp.exp(s - jnp.max(s, axis=-1, keepdims=True)) * mask.astype(F32)
    return p / jnp.maximum(jnp.sum(p, axis=-1, keepdims=True), 1e-30)


def _topk_mask(grp, k):
    ngrp = len(grp)
    cnt = [jnp.zeros(grp[0].shape, F32) for _ in range(ngrp)]
    sub = lax.broadcasted_iota(jnp.int32, grp[0].shape, 0)
    one, zero = jnp.float32(1.0), jnp.float32(0.0)
    for j in range(ngrp * SUBLANES):
        a, r = divmod(j, SUBLANES)
        row = grp[a][r:r + 1, :]
        for c in range(ngrp):
            if c < a:
                beats = jnp.where(row > grp[c], one, zero)
            elif c > a:
                beats = jnp.where(row >= grp[c], one, zero)
            else:
                beats = jnp.where(sub > r, jnp.where(row >= grp[c], one, zero), jnp.where(row > grp[c], one, zero))
            cnt[c] = cnt[c] + beats
    return [jnp.where(c < float(k), one, zero) for c in cnt]


def _select_blocks(imp, cur):
    ns = imp.shape[0]
    jrow = lax.broadcasted_iota(jnp.int32, imp.shape, 0)
    imp = jnp.where(jrow == cur, FORCE_CUR, jnp.where(jrow == 0, FORCE_SINK, imp))
    imp = jnp.where(jrow <= cur, imp, NEG_INF)
    assert ns % SUBLANES == 0
    grp = [imp[SUBLANES * a:SUBLANES * (a + 1)] for a in range(ns // SUBLANES)]
    return jnp.concatenate(_topk_mask(grp, N_SEL), axis=0)


def _cmp_attn_body(q_ref, kcvc_ref, ovT_ref, eye_ref, ocmp_ref, sel_ref):
    i = pl.program_id(1)
    tq = q_ref.shape[1]
    nh = kcvc_ref.shape[1]
    t0 = i * tq
    qpos_col = t0 + lax.broadcasted_iota(jnp.int32, (tq, 1), 0)
    cend = lax.broadcasted_iota(jnp.int32, (1, nh), 1) * CMP_STRIDE + (CMP_BLK - 1)
    mask = cend <= qpos_col
    dist = (qpos_col - cend).astype(F32)
    qpos_row = t0 + lax.broadcasted_iota(jnp.int32, (1, tq), 1)
    cur = qpos_row // SEL_BLK
    kcvc = kcvc_ref[0]
    for kvh in range(KV_HEADS):
        kc = kcvc[:, kvh * HEAD_DIM:(kvh + 1) * HEAD_DIM].astype(BF16)
        vc = kcvc[:, KV_DIM + kvh * HEAD_DIM:KV_DIM + (kvh + 1) * HEAD_DIM].astype(BF16)
        psum = jnp.zeros((tq, nh), F32)
        for g in range(GQA):
            head = kvh * GQA + g
            qg = q_ref[0, :, head * HEAD_DIM:(head + 1) * HEAD_DIM]
            s = _nt_dot(qg, kc) - _alibi_slope(head) * dist
            p = _masked_softmax(s, mask)
            ocmp_ref[0, :, head * HEAD_DIM:(head + 1) * HEAD_DIM] = jnp.dot(p.astype(BF16), vc, preferred_element_type=F32)
            psum = psum + p
        impT = lax.dot_general(ovT_ref[...], psum, (((1,), (1,)), ((), ())), preferred_element_type=F32,
                               precision=HIGHEST)
        selT = _select_blocks(impT, cur)
        sel = _nt_dot(eye_ref[...], selT.astype(BF16))
        sel_ref[0, kvh] = sel.astype(BF16)


def _cmp_attn_prompt(q, kcvc, tq):
    nb, t, _ = q.shape
    nh = kcvc.shape[1]
    ns = t // SEL_BLK
    ovT = _overlap_T(nh, ns)
    eye = jnp.eye(tq, dtype=BF16)
    return pl.pallas_call(
        _cmp_attn_body,
        grid=(nb, t // tq),
        in_specs=[pl.BlockSpec((1, tq, NSA_DIM), lambda b, i: (b, i, 0)),
                  pl.BlockSpec((1, nh, 256), lambda b, i: (b, 0, 0)),
                  pl.BlockSpec(ovT.shape, lambda b, i: (0, 0)),
                  pl.BlockSpec(eye.shape, lambda b, i: (0, 0))],
        out_specs=[pl.BlockSpec((1, tq, NSA_DIM), lambda b, i: (b, i, 0)),
                   pl.BlockSpec((1, KV_HEADS, tq, ns), lambda b, i: (b, 0, i, 0))],
        out_shape=[jax.ShapeDtypeStruct((nb, t, NSA_DIM), F32),
                   jax.ShapeDtypeStruct((nb, KV_HEADS, t, ns), BF16)],
        compiler_params=_cparams(("arbitrary", "arbitrary")),
        name="cmp_attn",
    )(q, kcvc, ovT, eye)


def _head_rows(qrow, kvh):
    rows = [qrow[:, (kvh * GQA + g) * HEAD_DIM:(kvh * GQA + g + 1) * HEAD_DIM] for g in range(GQA)]
    return jnp.concatenate(rows + [jnp.zeros((SUBLANES - GQA, HEAD_DIM), qrow.dtype)], axis=0)


def _slope_col(kvh):
    r = lax.broadcasted_iota(jnp.int32, (SUBLANES, 1), 0)
    col = jnp.zeros((SUBLANES, 1), F32)
    for g in range(GQA):
        col = jnp.where(r == g, _alibi_slope(kvh * GQA + g), col)
    return col


def _cmp_attn_sample_body(past, q_ref, kcvc_ref, ovT_ref, eye_ref, ocmp_ref, sel_ref):
    nh = kcvc_ref.shape[1]
    cend = lax.broadcasted_iota(jnp.int32, (1, nh), 1) * CMP_STRIDE + (CMP_BLK - 1)
    mask = cend <= past
    dist = (past - cend).astype(F32)
    cur = jnp.full((1, SUBLANES), past // SEL_BLK, jnp.int32)
    kcvc = kcvc_ref[0]
    qrow = q_ref[0]
    for kvh in range(KV_HEADS):
        kc = kcvc[:, kvh * HEAD_DIM:(kvh + 1) * HEAD_DIM].astype(BF16)
        vc = kcvc[:, KV_DIM + kvh * HEAD_DIM:KV_DIM + (kvh + 1) * HEAD_DIM].astype(BF16)
        s = _nt_dot(_head_rows(qrow, kvh), kc) - _slope_col(kvh) * dist
        p = _masked_softmax(s, mask)
        o = jnp.dot(p.astype(BF16), vc, preferred_element_type=F32)
        for g in range(GQA):
            head = kvh * GQA + g
            ocmp_ref[0, :, head * HEAD_DIM:(head + 1) * HEAD_DIM] = o[g:g + 1, :]
        psum = jnp.sum(p[0:GQA], axis=0, keepdims=True)
        impT = lax.dot_general(ovT_ref[...], jnp.broadcast_to(psum, (SUBLANES, nh)), (((1,), (1,)), ((), ())),
                               preferred_element_type=F32, precision=HIGHEST)
        selT = _select_blocks(impT, cur)
        sel_ref[0, kvh] = _nt_dot(eye_ref[...], selT.astype(BF16)).astype(jnp.int32)


def _cmp_attn_sample(q3, kcvc, past):
    nb = q3.shape[0]
    nh = kcvc.shape[1]
    ns = past // SEL_BLK + 1
    ns_pad = -(-ns // SUBLANES) * SUBLANES
    ovT = jnp.pad(_overlap_T(nh, ns), ((0, ns_pad - ns), (0, 0)))
    eye = jnp.eye(SUBLANES, dtype=BF16)
    ocmp, sel = pl.pallas_call(
        functools.partial(_cmp_attn_sample_body, past),
        grid=(nb,),
        in_specs=[pl.BlockSpec((1, 1, NSA_DIM), lambda b: (b, 0, 0)),
                  pl.BlockSpec((1, nh, 256), lambda b: (b, 0, 0)),
                  pl.BlockSpec(ovT.shape, lambda b: (0, 0)),
                  pl.BlockSpec(eye.shape, lambda b: (0, 0))],
        out_specs=[pl.BlockSpec((1, 1, NSA_DIM), lambda b: (b, 0, 0)),
                   pl.BlockSpec((1, KV_HEADS, SUBLANES, ns_pad), lambda b: (b, 0, 0, 0))],
        out_shape=[jax.ShapeDtypeStruct((nb, 1, NSA_DIM), F32),
                   jax.ShapeDtypeStruct((nb, KV_HEADS, SUBLANES, ns_pad), jnp.int32)],
        compiler_params=_cparams(("arbitrary",)),
        name="cmp_attn_sample",
    )(q3, kcvc, ovT, eye)
    return ocmp, sel[:, :, 0, :]


def _pos_rows(n, start=0):
    tab = np.zeros((HEAD_DIM, n), np.float32)
    k = start + np.arange(n)
    tab[0] = k // SEL_BLK
    tab[1] = k % SEL_BLK
    return jnp.asarray(tab, BF16)


def _slope_rows():
    tab = np.zeros((KV_HEADS, SUBLANES, HEAD_DIM), np.float32)
    for k in range(KV_HEADS):
        for g in range(GQA):
            s = _alibi_slope(k * GQA + g)
            tab[k, g, 0] = SEL_BLK * s
            tab[k, g, 1] = s
    return jnp.asarray(tab, F32)


def _block_expand(ns, n):
    return jnp.asarray((np.arange(n)[None, :] // SEL_BLK == np.arange(ns)[:, None]).astype(np.float32), BF16)


def _flash_step(q4, kT_aug, vT, bias, m_scr, l_scr, acc_scr):
    n, tk = q4.shape[0], kT_aug.shape[1]
    s = jnp.dot(q4, kT_aug, preferred_element_type=F32)
    rb = bias.shape[0]
    if rb in (1, n):
        s = s + bias
    else:
        s = (s.reshape(n // rb, rb, tk) + bias[None]).reshape(n, tk)
    m_old = m_scr[...]
    m_new = jnp.maximum(m_old, jnp.max(s, axis=-1, keepdims=True))
    alpha = jnp.exp(m_old - m_new)
    p = jnp.exp(s - pltpu.repeat(m_new, tk // LANES, axis=1))
    l_scr[...] = alpha * l_scr[...] + jnp.sum(p, axis=-1, keepdims=True)
    acc_scr[...] = alpha[:, :HEAD_DIM] * acc_scr[...] + _nt_dot(p.astype(BF16), vT)
    m_scr[...] = m_new


def _flash_result(l_scr, acc_scr):
    return acc_scr[...] / jnp.maximum(l_scr[...][:, :HEAD_DIM], 1e-30)


def _sel_win_body(tk, q_ref, sel_ref, g_ref, ocmp_ref, ksT_ref, vsT_ref, kwT_ref, vwT_ref, pos_ref, exp_ref, slope_ref,
                  o_ref, m_scr, l_scr, acc_scr):
    i = pl.program_id(2)
    tq = q_ref.shape[1]
    t0 = i * tq
    qpos = t0 + lax.broadcasted_iota(jnp.int32, (tq, 1), 0)
    q4 = jnp.concatenate(
        [jnp.concatenate([q_ref[0, :, g * HEAD_DIM:(g + 1) * HEAD_DIM],
                          jnp.broadcast_to(slope_ref[0, g:g + 1, :], (tq, HEAD_DIM)).astype(BF16)], axis=1)
         for g in range(GQA)], axis=0)

    def reset():
        m_scr[...] = jnp.full(m_scr.shape, NEG_INF, F32)
        l_scr[...] = jnp.zeros(l_scr.shape, F32)
        acc_scr[...] = jnp.zeros(acc_scr.shape, F32)

    def finish():
        return _flash_result(l_scr, acc_scr)

    reset()
    sel = sel_ref[0, 0]

    def sel_step(kt, carry):
        k0 = pl.multiple_of(kt * tk, tk)
        kpos = k0 + lax.broadcasted_iota(jnp.int32, (1, tk), 1)
        chosen = jnp.dot(sel, exp_ref[:, pl.ds(k0, tk)], preferred_element_type=F32)
        bias = (jnp.where(kpos <= qpos, chosen, 0.0) - 1.0) * (-NEG_INF)
        kT_aug = jnp.concatenate([ksT_ref[0, :, pl.ds(k0, tk)].astype(BF16), pos_ref[:, pl.ds(k0, tk)]], axis=0)
        _flash_step(q4, kT_aug, vsT_ref[0, :, pl.ds(k0, tk)].astype(BF16), bias, m_scr, l_scr, acc_scr)
        return carry

    lax.fori_loop(0, (t0 + tq + tk - 1) // tk, sel_step, 0)
    o_sel = finish()

    reset()
    wk = WINDOW + tq
    k0 = pl.multiple_of(jnp.maximum(t0 - WINDOW, 0), tq)
    dist = qpos - (k0 + lax.broadcasted_iota(jnp.int32, (1, wk), 1))
    bias = jnp.where(lax.bitcast_convert_type(dist, jnp.uint32) < jnp.uint32(WINDOW), 0.0, NEG_INF)
    kT_aug = jnp.concatenate([kwT_ref[0, :, pl.ds(k0, wk)].astype(BF16), pos_ref[:, pl.ds(k0, wk)]], axis=0)
    _flash_step(q4, kT_aug, vwT_ref[0, :, pl.ds(k0, wk)].astype(BF16), bias, m_scr, l_scr, acc_scr)
    o_win = finish()
    gates = g_ref[0]
    for g in range(GQA):
        rows = slice(g * tq, (g + 1) * tq)
        cols = slice(g * HEAD_DIM, (g + 1) * HEAD_DIM)
        o = (gates[:, 3 * g:3 * g + 1] * ocmp_ref[0, :, cols] + gates[:, 3 * g + 1:3 * g + 2] * o_sel[rows]
             + gates[:, 3 * g + 2:3 * g + 3] * o_win[rows])
        o_ref[0, :, cols] = o.astype(o_ref.dtype)


def _sel_win_prompt(q, sel, gates, ocmp, ksT, kwT, tq, tk):
    nb, t, _ = q.shape
    ns = sel.shape[-1]
    assert t % tk == 0 and t % tq == 0 and WINDOW % tq == 0
    pos, expand, slopes = _pos_rows(t), _block_expand(ns, t), _slope_rows()
    grp = GQA * HEAD_DIM
    kv_spec = lambda which: pl.BlockSpec((1, HEAD_DIM, t), lambda b, k, i: (b, which * KV_HEADS + k, 0))
    return pl.pallas_call(
        functools.partial(_sel_win_body, tk),
        grid=(nb, KV_HEADS, t // tq),
        in_specs=[pl.BlockSpec((1, tq, grp), lambda b, k, i: (b, i, k)),
                  pl.BlockSpec((1, 1, tq, ns), lambda b, k, i: (b, k, i, 0)),
                  pl.BlockSpec((1, tq, LANES), lambda b, k, i: (b, i, k)),
                  pl.BlockSpec((1, tq, grp), lambda b, k, i: (b, i, k)),
                  kv_spec(0), kv_spec(1), kv_spec(0), kv_spec(1),
                  pl.BlockSpec(pos.shape, lambda b, k, i: (0, 0)),
                  pl.BlockSpec(expand.shape, lambda b, k, i: (0, 0)),
                  pl.BlockSpec((1, SUBLANES, HEAD_DIM), lambda b, k, i: (k, 0, 0))],
        out_specs=pl.BlockSpec((1, tq, grp), lambda b, k, i: (b, i, k)),
        out_shape=jax.ShapeDtypeStruct((nb, t, NSA_DIM), BF16),
        scratch_shapes=[pltpu.VMEM((GQA * tq, LANES), F32), pltpu.VMEM((GQA * tq, LANES), F32),
                        pltpu.VMEM((GQA * tq, HEAD_DIM), F32)],
        compiler_params=_cparams(("arbitrary", "arbitrary", "arbitrary")),
        name="sel_win_attn",
    )(q, sel, gates, ocmp, ksT, ksT, kwT, kwT, pos, expand, slopes)


def _sel_win_sample_body(past, g_pages, pt_ref, sel_ref, q_ref, g_ref, ocmp_ref, *refs):
    page_refs = refs[:g_pages]
    (win_ref, ksn_ref, kwn_ref, pos_ref, wpos_ref, slope_ref, o_ref, wout_ref, m_scr, l_scr, acc_scr) = refs[g_pages:]
    b = pl.program_id(0)
    j = pl.program_id(1)
    ns_pad = sel_ref.shape[1] // KV_HEADS
    qrow = q_ref[0]
    q4 = [jnp.concatenate([_head_rows(qrow, k), slope_ref[k].astype(BF16)], axis=1) for k in range(KV_HEADS)]

    @pl.when(j == 0)
    def _():
        m_scr[...] = jnp.full(m_scr.shape, NEG_INF, F32)
        l_scr[...] = jnp.zeros(l_scr.shape, F32)
        acc_scr[...] = jnp.zeros(acc_scr.shape, F32)

    width = g_pages * PAGE_SIZE
    lane_blk = lax.broadcasted_iota(jnp.int32, (1, width), 1) // SEL_BLK
    row0 = lax.broadcasted_iota(jnp.int32, (HEAD_DIM, 1), 0) == 0
    blk0 = j * (width // SEL_BLK)
    pos = (pos_ref[...].astype(F32) + jnp.where(row0, blk0.astype(F32), 0.0)).astype(BF16)
    for k in range(KV_HEADS):
        kT = jnp.concatenate([r[0, k * HEAD_DIM:(k + 1) * HEAD_DIM, :] for r in page_refs], axis=1).astype(BF16)
        vT = jnp.concatenate([r[0, KV_DIM + k * HEAD_DIM:KV_DIM + (k + 1) * HEAD_DIM, :] for r in page_refs],
                             axis=1).astype(BF16)
        bias = jnp.full((1, width), NEG_INF, F32)
        for blk in range(width // SEL_BLK):
            chosen = sel_ref[b, k * ns_pad + blk0 + blk] > 0
            bias = jnp.where(lane_blk == blk, jnp.where(chosen, 0.0, NEG_INF), bias)
        _flash_step(q4[k], jnp.concatenate([kT, pos], axis=0), vT, bias, m_scr.at[k], l_scr.at[k], acc_scr.at[k])

    @pl.when(j == pl.num_programs(1) - 1)
    def _():
        nb = ksn_ref.shape[2]
        pick = lax.broadcasted_iota(jnp.int32, (1, nb), 1) == b
        ks_new = jnp.sum(jnp.where(pick, ksn_ref[0], 0.0), axis=1, keepdims=True)
        kw_new = jnp.sum(jnp.where(pick, kwn_ref[0], 0.0), axis=1, keepdims=True)
        lane = lax.broadcasted_iota(jnp.int32, (1, LANES), 1)
        tile_new = jnp.where(lane == 0, ks_new, 0.0).astype(BF16)
        pos_new = jnp.where(row0 & (lane == 0), float(past // SEL_BLK), 0.0).astype(BF16)
        bias_new = jnp.where(lane == 0, 0.0, NEG_INF)
        wlane = lax.broadcasted_iota(jnp.int32, (1, win_ref.shape[2]), 1)
        wout = jnp.where(wlane == win_ref.shape[2] - 1, kw_new, pltpu.roll(win_ref[0], win_ref.shape[2] - 1, 1))
        wout_ref[0] = wout
        woutb = wout.astype(BF16)
        gates = g_ref[0]
        for k in range(KV_HEADS):
            ksl = slice(k * HEAD_DIM, (k + 1) * HEAD_DIM)
            vsl = slice(KV_DIM + k * HEAD_DIM, KV_DIM + (k + 1) * HEAD_DIM)
            _flash_step(q4[k], jnp.concatenate([tile_new[ksl], pos_new], axis=0), tile_new[vsl], bias_new,
                        m_scr.at[k], l_scr.at[k], acc_scr.at[k])
            o_sel = _flash_result(l_scr.at[k], acc_scr.at[k])
            m_scr[k] = jnp.full(m_scr.shape[1:], NEG_INF, F32)
            l_scr[k] = jnp.zeros(l_scr.shape[1:], F32)
            acc_scr[k] = jnp.zeros(acc_scr.shape[1:], F32)
            _flash_step(q4[k], jnp.concatenate([woutb[ksl], wpos_ref[...]], axis=0), woutb[vsl],
                        jnp.zeros((1, win_ref.shape[2]), F32), m_scr.at[k], l_scr.at[k], acc_scr.at[k])
            o_win = _flash_result(l_scr.at[k], acc_scr.at[k])
            for g in range(GQA):
                head = k * GQA + g
                cols = slice(head * HEAD_DIM, (head + 1) * HEAD_DIM)
                c0 = k * LANES + 3 * g
                o = (gates[:, c0:c0 + 1] * ocmp_ref[0, :, cols] + gates[:, c0 + 1:c0 + 2] * o_sel[g:g + 1, :]
                     + gates[:, c0 + 2:c0 + 3] * o_win[g:g + 1, :])
                o_ref[0, :, cols] = o.astype(o_ref.dtype)


def _sel_win_sample(q3, g3, ocmp, sel, pool_sel, page_table, win, ksT_new, kwT_new, past):
    nb, n_pages = page_table.shape
    wlen = win.shape[2]
    assert wlen == WINDOW and past >= WINDOW
    g = min(16, n_pages)
    ns_pad = sel.shape[2]
    pos, wpos, slopes = _pos_rows(g * PAGE_SIZE), _pos_rows(wlen, past - wlen + 1), _slope_rows()
    full = lambda a: pl.BlockSpec(a.shape, lambda b, j, pt, sl: (0,) * a.ndim)
    row = lambda n: pl.BlockSpec((1, 1, n), lambda b, j, pt, sl: (b, 0, 0))
    page_spec = lambda k: pl.BlockSpec((1, 256, PAGE_SIZE), lambda b, j, pt, sl: (pt[b, j * g + k], 0, 0))
    wspec = pl.BlockSpec((1, 256, wlen), lambda b, j, pt, sl: (b, 0, 0))
    return pl.pallas_call(
        functools.partial(_sel_win_sample_body, past, g),
        grid_spec=pltpu.PrefetchScalarGridSpec(
            num_scalar_prefetch=2,
            grid=(nb, n_pages // g),
            in_specs=[row(NSA_DIM), row(KV_HEADS * LANES), row(NSA_DIM)] + [page_spec(k) for k in range(g)]
                     + [wspec, full(ksT_new), full(kwT_new), full(pos), full(wpos), full(slopes)],
            out_specs=[row(NSA_DIM), wspec],
            scratch_shapes=[pltpu.VMEM((KV_HEADS, SUBLANES, LANES), F32), pltpu.VMEM((KV_HEADS, SUBLANES, LANES), F32),
                            pltpu.VMEM((KV_HEADS, SUBLANES, HEAD_DIM), F32)],
        ),
        out_shape=[jax.ShapeDtypeStruct((nb, 1, NSA_DIM), BF16), jax.ShapeDtypeStruct((nb, 256, wlen), F32)],
        compiler_params=_cparams(("arbitrary", "arbitrary")),
        name="sel_win_sample",
    )(page_table, sel.reshape(nb, KV_HEADS * ns_pad), q3, g3, ocmp, *([pool_sel] * g), win, ksT_new, kwT_new, pos,
      wpos, slopes)


def _softplus(v):
    return jnp.maximum(v, 0.0) + jnp.log1p(jnp.exp(-jnp.abs(v)))


def _tn_dot(a, b):
    return lax.dot_general(a, b, (((0,), (0,)), ((), ())), preferred_element_type=F32)


def _prep_ssd(conv_w, conv_b, dt_bias, a_log, d_skip, ssd_norm):
    padl = lambda v: jnp.pad(v.reshape(1, -1), ((0, 0), (0, LANES - v.shape[0])))
    L = SSD_CHUNK
    tril = jnp.asarray(np.tril(np.ones((L, L), np.float32)))
    return dict(conv_w=conv_w, conv_b=conv_b.reshape(1, -1), dtb_row=padl(dt_bias), dtb_col=dt_bias.reshape(-1, 1),
                alog_row=padl(a_log), alog_col=a_log.reshape(-1, 1), dskip=padl(d_skip), norm=ssd_norm.reshape(1, -1),
                tril=tril, triu=tril.T)


def _ssd_chunk(u, z, dt, dtT, h_prev, P):
    a_row = -jnp.exp(P["alog_row"][...])
    a_col = -jnp.exp(P["alog_col"][...])
    acum = jnp.dot(P["tril"][...], dt * a_row, preferred_element_type=F32, precision=HIGHEST)
    acumT = jnp.dot(dtT * a_col, P["triu"][...], preferred_element_type=F32, precision=HIGHEST)
    L = u.shape[0]
    li = lax.broadcasted_iota(jnp.int32, (L, L), 0)
    si = lax.broadcasted_iota(jnp.int32, (L, L), 1)
    causal = li >= si
    gn = SSD_GROUPS * D_STATE
    ys, hs = [], []
    per = SSD_HEADS // SSD_GROUPS
    for g in range(SSD_GROUPS):
        bm = u[:, D_INNER + g * D_STATE:D_INNER + (g + 1) * D_STATE]
        cm = u[:, D_INNER + gn + g * D_STATE:D_INNER + gn + (g + 1) * D_STATE]
        bmb = bm.astype(BF16)
        cb = _nt_dot(cm.astype(BF16), bmb)
        for e in range(per):
            h = g * per + e
            ac = acum[:, h:h + 1]
            seg = ac - acumT[h:h + 1, :]
            decay = jnp.where(causal, jnp.exp(jnp.where(causal, seg, 0.0)), 0.0)
            xs = u[:, h * SSD_HEAD_DIM:(h + 1) * SSD_HEAD_DIM]
            xdt = xs * dt[:, h:h + 1]
            y = jnp.dot((cb * decay).astype(BF16), xdt.astype(BF16), preferred_element_type=F32)
            a_last = acum[L - 1:L, h:h + 1]
            st = _tn_dot((xdt * jnp.exp(a_last - ac)).astype(BF16), bmb)
            y = y + _nt_dot((cm * jnp.exp(ac)).astype(BF16), h_prev[h].astype(BF16))
            hs.append(jnp.exp(a_last) * h_prev[h] + st)
            ys.append(y + P["dskip"][:, h:h + 1] * xs)
    return ys, hs


def _ssd_finish(ys, z, norm_w):
    y = jnp.concatenate(ys, axis=1) * _silu(z)
    ms = jnp.mean(y * y, axis=-1, keepdims=True)
    return y * lax.rsqrt(ms + RMS_EPS) * norm_w


def _ssd_prompt_body(xbc_ref, z_ref, dt_ref, dtT_ref, cw_ref, cb_ref, dtbr_ref, dtbc_ref, alr_ref, alc_ref, dsk_ref,
                     nrm_ref, tril_ref, triu_ref, y_ref, hout_ref, xpad_scr, h_scr):
    c = pl.program_id(1)
    L = xbc_ref.shape[1]

    @pl.when(c == 0)
    def _():
        xpad_scr[0:SUBLANES, :] = jnp.zeros((SUBLANES, xpad_scr.shape[1]), F32)
        h_scr[...] = jnp.zeros(h_scr.shape, F32)

    xt = xbc_ref[0]
    xpad_scr[SUBLANES:SUBLANES + L, :] = xt
    conv = cb_ref[...] + xpad_scr[SUBLANES - (CONV_W - 1):SUBLANES - (CONV_W - 1) + L, :] * cw_ref[0:1, :]
    for k in range(1, CONV_W):
        o = SUBLANES - (CONV_W - 1) + k
        conv = conv + xpad_scr[o:o + L, :] * cw_ref[k:k + 1, :]
    xpad_scr[0:SUBLANES, :] = xt[L - SUBLANES:L, :]
    u = _silu(conv)
    dt = _softplus(dt_ref[0] + dtbr_ref[...])
    dtT = _softplus(dtT_ref[0] + dtbc_ref[...])
    P = dict(alog_row=alr_ref, alog_col=alc_ref, tril=tril_ref, triu=triu_ref, dskip=dsk_ref[...])
    ys, hs = _ssd_chunk(u, z_ref[0], dt, dtT, [h_scr[h] for h in range(SSD_HEADS)], P)
    for h in range(SSD_HEADS):
        h_scr[h] = hs[h]
    y_ref[0] = _ssd_finish(ys, z_ref[0], nrm_ref[...]).astype(y_ref.dtype)

    @pl.when(c == pl.num_programs(1) - 1)
    def _():
        hout_ref[0] = h_scr[...]


def _ssd_prompt(xbc, z, dt, dtT, SP):
    nb, t, cd = xbc.shape
    L = SSD_CHUNK
    assert t % L == 0
    full = lambda a: pl.BlockSpec(a.shape, lambda b, c: (0,) * a.ndim)
    names = ("conv_w", "conv_b", "dtb_row", "dtb_col", "alog_row", "alog_col", "dskip", "norm", "tril", "triu")
    ps = [SP[n] for n in names]
    return pl.pallas_call(
        _ssd_prompt_body,
        grid=(nb, t // L),
        in_specs=[pl.BlockSpec((1, L, cd), lambda b, c: (b, c, 0)),
                  pl.BlockSpec((1, L, D_INNER), lambda b, c: (b, c, 0)),
                  pl.BlockSpec((1, L, LANES), lambda b, c: (b, c, 0)),
                  pl.BlockSpec((1, SSD_HEADS, L), lambda b, c: (b, 0, c))] + [full(a) for a in ps],
        out_specs=[pl.BlockSpec((1, L, D_INNER), lambda b, c: (b, c, 0)),
                   pl.BlockSpec((1, SSD_HEADS, SSD_HEAD_DIM, D_STATE), lambda b, c: (b, 0, 0, 0))],
        out_shape=[jax.ShapeDtypeStruct((nb, t, D_INNER), BF16),
                   jax.ShapeDtypeStruct((nb, SSD_HEADS, SSD_HEAD_DIM, D_STATE), F32)],
        scratch_shapes=[pltpu.VMEM((SUBLANES + L, cd), F32), pltpu.VMEM((SSD_HEADS, SSD_HEAD_DIM, D_STATE), F32)],
        compiler_params=_cparams(("arbitrary", "arbitrary")),
        name="ssd_prompt",
    )(xbc, z, dt, dtT, *ps)


def _ssd_sample_body(cs_ref, xbc_ref, z_ref, dt_ref, h0_ref, cw_ref, cb_ref, dtb_ref, al_ref, dsk_ref, nrm_ref, eye_ref,
                     y_ref, cso_ref, h_ref):
    nseq = xbc_ref.shape[0]
    xn = xbc_ref[...]
    conv = cb_ref[...] + xn * cw_ref[CONV_W - 1:CONV_W, :]
    for k in range(CONV_W - 1):
        conv = conv + cs_ref[k] * cw_ref[k:k + 1, :]
        if k > 0:
            cso_ref[k - 1] = cs_ref[k]
    cso_ref[CONV_W - 2] = xn
    u = _silu(conv)
    dt = _softplus(dt_ref[...] + dtb_ref[...])
    decay = jnp.exp(dt * (-jnp.exp(al_ref[...])))
    eye = eye_ref[...]
    gn = SSD_GROUPS * D_STATE
    per = SSD_HEADS // SSD_GROUPS
    rows = []
    for s in range(nseq):
        ys = []
        for h in range(SSD_HEADS):
            g = h // per
            xs = u[s:s + 1, h * SSD_HEAD_DIM:(h + 1) * SSD_HEAD_DIM]
            bm = u[s:s + 1, D_INNER + g * D_STATE:D_INNER + (g + 1) * D_STATE]
            cm = u[s:s + 1, D_INNER + gn + g * D_STATE:D_INNER + gn + (g + 1) * D_STATE]
            xcol = jnp.sum(eye * xs, axis=1, keepdims=True)
            hn = decay[s:s + 1, h:h + 1] * h0_ref[s, h] + (dt[s:s + 1, h:h + 1] * xcol) * bm
            h_ref[s, h] = hn
            ycol = jnp.sum(hn * cm, axis=1, keepdims=True)
            ys.append(jnp.sum(eye * ycol, axis=0, keepdims=True) + dsk_ref[:, h:h + 1] * xs)
        rows.append(jnp.concatenate(ys, axis=1))
    y = jnp.concatenate(rows, axis=0) * _silu(z_ref[...])
    ms = jnp.mean(y * y, axis=-1, keepdims=True)
    y_ref[...] = (y * lax.rsqrt(ms + RMS_EPS) * nrm_ref[...]).astype(y_ref.dtype)


def _ssd_sample(conv_state, xbc, z, dt, h0, SP):
    nb, cd = xbc.shape
    ts = SUBLANES
    assert nb % ts == 0
    eye = jnp.eye(SSD_HEAD_DIM, dtype=F32)
    names = ("conv_w", "conv_b", "dtb_row", "alog_row", "dskip", "norm")
    ps = [SP[n] for n in names] + [eye]
    full = lambda a: pl.BlockSpec(a.shape, lambda i: (0,) * a.ndim)
    st = pl.BlockSpec((ts, SSD_HEADS, SSD_HEAD_DIM, D_STATE), lambda i: (i, 0, 0, 0))
    cs = pl.BlockSpec((CONV_W - 1, ts, cd), lambda i: (0, i, 0))
    row = lambda n: pl.BlockSpec((ts, n), lambda i: (i, 0))
    return pl.pallas_call(
        _ssd_sample_body,
        grid=(nb // ts,),
        in_specs=[cs, row(cd), row(D_INNER), row(LANES), st] + [full(a) for a in ps],
        out_specs=[row(D_INNER), cs, st],
        out_shape=[jax.ShapeDtypeStruct((nb, D_INNER), BF16),
                   jax.ShapeDtypeStruct((CONV_W - 1, nb, cd), F32),
                   jax.ShapeDtypeStruct(h0.shape, F32)],
        compiler_params=_cparams(("arbitrary",)),
        name="ssd_sample",
    )(conv_state, xbc, z, dt, h0, *ps)


def _pack_bf16_pairs(v):
    m = v.shape[1] // 2
    hi = pltpu.bitcast(v[:, :m].astype(BF16).astype(F32), jnp.uint32)
    lo = pltpu.bitcast(v[:, m:].astype(BF16).astype(F32), jnp.uint32)
    return hi | (lo >> 16)


def _unpack_bf16_pairs(w):
    hi = pltpu.bitcast(w & jnp.uint32(0xFFFF0000), F32)
    lo = pltpu.bitcast(w << 16, F32)
    return hi.astype(BF16), lo.astype(BF16)


def _route(logitsT, bias_col):
    s = jax.nn.sigmoid(logitsT)
    sb = s + bias_col
    per = N_EXPERTS // N_EXPERT_GROUPS
    assert per == SUBLANES
    grp = [sb[per * a:per * (a + 1)] for a in range(N_EXPERT_GROUPS)]
    sub = lax.broadcasted_iota(jnp.int32, grp[0].shape, 0)
    gs = []
    for ga in grp:
        m1 = jnp.max(ga, axis=0, keepdims=True)
        first = jnp.min(jnp.where(ga == m1, sub, per), axis=0, keepdims=True)
        m2 = jnp.max(jnp.where(sub == first, NEG_INF, ga), axis=0, keepdims=True)
        gs.append(m1 + m2)
    gmask = _topk_mask([jnp.concatenate(gs, axis=0)], TOPK_GROUPS)[0]
    masked = [jnp.where(gmask[a:a + 1, :] > 0.5, grp[a], NEG_INF) for a in range(N_EXPERT_GROUPS)]
    sel = jnp.concatenate(_topk_mask(masked, TOP_K), axis=0)
    w = s * sel
    w = w / jnp.sum(w, axis=0, keepdims=True) * ROUTED_SCALE
    return sel, w


def _post_mix_body(x_ref, on_ref, ys_ref, g1_ref, sc_ref, sh_ref, nw_ref, wo_ref, rw_ref, rb_ref,
                   x1_ref, hp_ref, selT_ref, wT_ref, cnt_ref):
    first = (pl.program_id(0) == 0) & (pl.program_id(1) == 0)
    half = wo_ref.shape[0] // 2
    mix = (jnp.dot(on_ref[0], wo_ref[0:half, :], preferred_element_type=F32)
           + jnp.dot(ys_ref[0], wo_ref[half:, :], preferred_element_type=F32))
    x1 = x_ref[0] + g1_ref[0] * mix
    x1_ref[0] = x1
    ms = jnp.mean(x1 * x1, axis=-1, keepdims=True)
    h = x1 * lax.rsqrt(ms + RMS_EPS) * nw_ref[...]
    h = h * (1.0 + sc_ref[0]) + sh_ref[0]
    hp_ref[0] = _pack_bf16_pairs(h)
    logitsT = lax.dot_general(rw_ref[...], h, (((1,), (1,)), ((), ())), preferred_element_type=F32,
                              precision=HIGHEST)
    sel, w = _route(logitsT, rb_ref[...])
    selT_ref[...] = sel.astype(selT_ref.dtype)
    wT_ref[...] = w

    @pl.when(first)
    def _():
        cnt_ref[...] = jnp.zeros(cnt_ref.shape, F32)

    cnt_ref[...] += jnp.broadcast_to(jnp.sum(sel, axis=1, keepdims=True), cnt_ref.shape)


def _post_mix(x, o_nsa, y_ssd, g1, sc2, sh2, norm_w, w_out_b, router_wT, router_bias, tm):
    nb, t, d = x.shape
    mt = g1.shape[1]
    nt = t // tm
    assert t % tm == 0 and (mt == 1 or mt == t)
    if mt == 1:
        mod_spec = pl.BlockSpec((1, 1, d), lambda b, i: (b, 0, 0))
    else:
        mod_spec = pl.BlockSpec((1, tm, d), lambda b, i: (b, i, 0))
    row = lambda n: pl.BlockSpec((1, tm, n), lambda b, i: (b, i, 0))
    full = lambda a: pl.BlockSpec(a.shape, lambda b, i: (0,) * a.ndim)
    tok = lambda: pl.BlockSpec((N_EXPERTS, tm), lambda b, i: (0, b * nt + i))
    rb = router_bias.reshape(N_EXPERTS, 1)
    nw = norm_w.reshape(1, d)
    return pl.pallas_call(
        _post_mix_body,
        grid=(nb, nt),
        in_specs=[row(d), row(NSA_DIM), row(D_INNER), mod_spec, mod_spec, mod_spec, full(nw), full(w_out_b),
                  full(router_wT), full(rb)],
        out_specs=[row(d), row(d // 2), tok(), tok(), pl.BlockSpec((N_EXPERTS, LANES), lambda b, i: (0, 0))],
        out_shape=[jax.ShapeDtypeStruct((nb, t, d), F32),
                   jax.ShapeDtypeStruct((nb, t, d // 2), jnp.uint32),
                   jax.ShapeDtypeStruct((N_EXPERTS, nb * t), BF16),
                   jax.ShapeDtypeStruct((N_EXPERTS, nb * t), F32),
                   jax.ShapeDtypeStruct((N_EXPERTS, LANES), F32)],
        compiler_params=_cparams(("arbitrary", "arbitrary")),
        name="post_mix",
    )(x, o_nsa, y_ssd, g1, sc2, sh2, nw, w_out_b, router_wT, rb)


MOE_BLOCK = 256
MOE_BLOCK_SHIFT = 8


def _moe_rows(n_tok):
    n_blocks = n_tok * TOP_K // MOE_BLOCK + N_EXPERTS
    n_blocks_pad = -(-n_blocks // LANES) * LANES
    return n_blocks, n_blocks_pad


def _plan_body(selT_ref, wT_ref, cnt_ref, triu_ref, tril_ref, eye_ref, dest_ref, w8_ref, be_ref, fill_ref,
               carry_scr, pstart_scr):
    step = pl.program_id(0)
    ne = N_EXPERTS

    @pl.when(step == 0)
    def _():
        cnt = cnt_ref[...]
        cnt_i = cnt.astype(jnp.int32)
        padded = (((cnt_i + (MOE_BLOCK - 1)) >> MOE_BLOCK_SHIFT) << MOE_BLOCK_SHIFT).astype(F32)
        pstart = jnp.dot(tril_ref[...].astype(F32), padded, preferred_element_type=F32, precision=HIGHEST)
        pstart_scr[...] = pstart
        carry_scr[...] = jnp.zeros(carry_scr.shape, F32)
        pend = pstart + padded
        nbp = be_ref.shape[1]
        starts = (lax.broadcasted_iota(jnp.int32, (1, nbp), 1) * MOE_BLOCK).astype(F32)
        below = jnp.where(pend[:, 0:1] <= starts, 1.0, 0.0)
        be_ref[...] = jnp.minimum(jnp.sum(below, axis=0, keepdims=True), float(ne - 1)).astype(jnp.int32)
        eye = eye_ref[...]
        to_row = lambda col: jnp.sum(col * eye, axis=0, keepdims=True)
        n_used = jnp.max(pend, axis=0, keepdims=True) * (1.0 / MOE_BLOCK)
        rows = jnp.concatenate([to_row(pstart + cnt), to_row(padded - cnt), n_used,
                                jnp.zeros((SUBLANES - 3, LANES), F32)], axis=0)
        fill_ref[...] = rows.astype(jnp.int32)

    sel = selT_ref[...]
    self32 = sel.astype(F32)
    rank = jnp.dot(sel, triu_ref[...], preferred_element_type=F32) + carry_scr[:, 0:1]
    carry_scr[...] += jnp.broadcast_to(jnp.sum(self32, axis=1, keepdims=True), carry_scr.shape)
    dest = pstart_scr[:, 0:1] + rank
    slot = jnp.dot(tril_ref[...], sel, preferred_element_type=F32)
    w = wT_ref[...]
    drows, wrows = [], []
    for k in range(TOP_K):
        pick = jnp.where(slot == float(k), self32, 0.0)
        drows.append(jnp.sum(pick * dest, axis=0, keepdims=True))
        wrows.append(jnp.sum(pick * w, axis=0, keepdims=True))
    dest_ref[...] = jnp.concatenate(drows, axis=0).astype(jnp.int32)
    w8_ref[...] = jnp.concatenate(wrows, axis=0)


def _moe_plan(selT, wT, cnt, tile):
    ne, n = selT.shape
    assert n % tile == 0
    _, nbp = _moe_rows(n)
    triu = jnp.asarray(np.triu(np.ones((tile, tile), np.float32), 1), BF16)
    tril = jnp.asarray(np.tril(np.ones((ne, ne), np.float32), -1), BF16)
    eye = jnp.asarray(np.eye(ne, LANES, dtype=np.float32))
    full = lambda a: pl.BlockSpec(a.shape, lambda i: (0,) * a.ndim)
    return pl.pallas_call(
        _plan_body,
        grid=(n // tile,),
        in_specs=[pl.BlockSpec((ne, tile), lambda i: (0, i)), pl.BlockSpec((ne, tile), lambda i: (0, i)),
                  full(cnt), full(triu), full(tril), full(eye)],
        out_specs=[pl.BlockSpec((TOP_K, tile), lambda i: (0, i)), pl.BlockSpec((TOP_K, tile), lambda i: (0, i)),
                   pl.BlockSpec((1, nbp), lambda i: (0, 0)), pl.BlockSpec((SUBLANES, LANES), lambda i: (0, 0))],
        out_shape=[jax.ShapeDtypeStruct((TOP_K, n), jnp.int32), jax.ShapeDtypeStruct((TOP_K, n), F32),
                   jax.ShapeDtypeStruct((1, nbp), jnp.int32), jax.ShapeDtypeStruct((SUBLANES, LANES), jnp.int32)],
        scratch_shapes=[pltpu.VMEM((ne, LANES), F32), pltpu.VMEM((ne, LANES), F32)],
        compiler_params=_cparams(("arbitrary",)),
        name="moe_plan",
    )(selT, wT, cnt, triu, tril, eye)


_FILL_PIECES = tuple(1 << s for s in reversed(range(MOE_BLOCK_SHIFT)))


def _dispatch_body(dest_ref, fill_ref, hp_ref, xd_ref, zero_scr, sem, zsem):
    step = pl.program_id(0)
    tile = hp_ref.shape[0]

    def row_copy(t, k):
        return pltpu.make_async_copy(hp_ref.at[pl.ds(t, 1)], xd_ref.at[pl.ds(dest_ref[k, t], 1)], sem)

    def fill_pass(wait):
        def per_expert(e, carry):
            start = fill_ref[0, e]
            n = fill_ref[1, e]
            head = n & (SUBLANES - 1)
            for r in range(SUBLANES - 1):
                @pl.when(r < head)
                def _():
                    cp = pltpu.make_async_copy(zero_scr.at[pl.ds(0, 1)], xd_ref.at[pl.ds(start + r, 1)], zsem)
                    cp.wait() if wait else cp.start()

            cur = start + head
            for p in _FILL_PIECES:
                if p < SUBLANES:
                    continue
                hit = (n & p) != 0

                @pl.when(hit)
                def _():
                    off = pl.multiple_of(cur, SUBLANES)
                    cp = pltpu.make_async_copy(zero_scr.at[pl.ds(0, p)], xd_ref.at[pl.ds(off, p)], zsem)
                    cp.wait() if wait else cp.start()

                cur = cur + jnp.where(hit, p, 0)
            return carry

        lax.fori_loop(0, N_EXPERTS, per_expert, 0)

    @pl.when(step == 0)
    def _():
        zero_scr[...] = jnp.zeros(zero_scr.shape, zero_scr.dtype)
        fill_pass(False)
        fill_pass(True)

    def issue(t, carry):
        for k in range(TOP_K):
            row_copy(t, k).start(priority=k % 2)
        return carry

    def drain(t, carry):
        for k in range(TOP_K):
            row_copy(t, k).wait()
        return carry

    lax.fori_loop(0, tile, issue, 0)
    lax.fori_loop(0, tile, drain, 0)


def _moe_dispatch(hp, dest8, fill, tile):
    n, m = hp.shape
    n_blocks, _ = _moe_rows(n)
    nr = n_blocks * MOE_BLOCK
    return pl.pallas_call(
        _dispatch_body,
        grid=(n // tile,),
        in_specs=[pl.BlockSpec((TOP_K, tile), lambda i: (0, i), memory_space=pltpu.SMEM),
                  pl.BlockSpec(memory_space=pltpu.SMEM),
                  pl.BlockSpec((tile, m), lambda i: (i, 0))],
        out_specs=pl.BlockSpec(memory_space=pl.ANY),
        out_shape=jax.ShapeDtypeStruct((nr, m), jnp.uint32),
        scratch_shapes=[pltpu.VMEM((_FILL_PIECES[0], m), jnp.uint32), pltpu.SemaphoreType.DMA(()),
                        pltpu.SemaphoreType.DMA(())],
        compiler_params=_cparams(("arbitrary",)),
        name="moe_dispatch",
    )(dest8, fill, hp)


def _swiglu_packed(xw, w1, w3, w2):
    xa, xb = _unpack_bf16_pairs(xw)
    half = xa.shape[1]
    mm = lambda w: (jnp.dot(xa, w[0:half, :], preferred_element_type=F32)
                    + jnp.dot(xb, w[half:, :], preferred_element_type=F32))
    hid = _silu(mm(w1)) * mm(w3)
    return jnp.dot(hid.astype(BF16), w2, preferred_element_type=F32)


def _experts_body(be_ref, nu_ref, xd_ref, w1_ref, w3_ref, w2_ref, yd_ref):
    i = pl.program_id(0)

    @pl.when(i < nu_ref[0])
    def _():
        yd_ref[...] = _pack_bf16_pairs(_swiglu_packed(xd_ref[...], w1_ref[0], w3_ref[0], w2_ref[0]))

    @pl.when(i >= nu_ref[0])
    def _():
        yd_ref[...] = jnp.zeros(yd_ref.shape, yd_ref.dtype)


def _moe_experts(xd, block_e, n_used, w1b, w3b, w2b):
    nr, m = xd.shape
    n_blocks = nr // MOE_BLOCK
    d, f = w1b.shape[1:]
    clamp = lambda i, nu: jnp.minimum(i, nu[0] - 1)
    return pl.pallas_call(
        _experts_body,
        grid_spec=pltpu.PrefetchScalarGridSpec(
            num_scalar_prefetch=2,
            grid=(n_blocks,),
            in_specs=[pl.BlockSpec((MOE_BLOCK, m), lambda i, be, nu: (clamp(i, nu), 0)),
                      pl.BlockSpec((1, d, f), lambda i, be, nu: (be[clamp(i, nu)], 0, 0)),
                      pl.BlockSpec((1, d, f), lambda i, be, nu: (be[clamp(i, nu)], 0, 0)),
                      pl.BlockSpec((1, f, d), lambda i, be, nu: (be[clamp(i, nu)], 0, 0))],
            out_specs=pl.BlockSpec((MOE_BLOCK, m), lambda i, be, nu: (i, 0)),
        ),
        out_shape=jax.ShapeDtypeStruct((nr, m), jnp.uint32),
        compiler_params=_cparams(("arbitrary",)),
        name="moe_experts",
    )(block_e, n_used, xd, w1b, w3b, w2b)


def _combine_body(dest_ref, w8_ref, hp_ref, x1_ref, g2_ref, eye_ref, sw1_ref, sw3_ref, sw2_ref, nf_ref, yd_ref,
                  o_ref, ybuf, sem):
    tile = hp_ref.shape[1]

    def row_copy(t, k):
        return pltpu.make_async_copy(yd_ref.at[pl.ds(dest_ref[k, t], 1)], ybuf.at[k, pl.ds(t, 1)], sem)

    def issue(t, carry):
        for k in range(TOP_K):
            row_copy(t, k).start(priority=k % 2)
        return carry

    def drain(t, carry):
        for k in range(TOP_K):
            row_copy(t, k).wait()
        return carry

    lax.fori_loop(0, tile, issue, 0)
    shared = _swiglu_packed(hp_ref[0], sw1_ref[...], sw3_ref[...], sw2_ref[...])
    w_rows = lax.dot_general(eye_ref[...], w8_ref[...], (((1,), (1,)), ((), ())), preferred_element_type=F32,
                             precision=HIGHEST)
    lax.fori_loop(0, tile, drain, 0)
    half = ybuf.shape[2]
    acc_a = jnp.zeros((tile, half), F32)
    acc_b = jnp.zeros((tile, half), F32)
    for k in range(TOP_K):
        ya, yb = _unpack_bf16_pairs(ybuf[k])
        wk = w_rows[:, k:k + 1]
        acc_a = acc_a + wk * ya.astype(F32)
        acc_b = acc_b + wk * yb.astype(F32)
    routed = jnp.concatenate([acc_a, acc_b], axis=1)
    x2 = x1_ref[0] + g2_ref[0] * (routed + shared)
    ms = jnp.mean(x2 * x2, axis=-1, keepdims=True)
    o_ref[0] = x2 * lax.rsqrt(ms + RMS_EPS) * nf_ref[...]


def _moe_combine(dest8, w8, hp, x1, g2, yd, sw1b, sw3b, sw2b, norm_f, tile):
    nb, t, d = x1.shape
    nt = t // tile
    mt = g2.shape[1]
    assert t % tile == 0 and (mt == 1 or mt == t)
    if mt == 1:
        mod_spec = pl.BlockSpec((1, 1, d), lambda b, i: (b, 0, 0))
    else:
        mod_spec = pl.BlockSpec((1, tile, d), lambda b, i: (b, i, 0))
    eye = jnp.eye(tile, dtype=F32)
    nf = norm_f.reshape(1, d)
    full = lambda a: pl.BlockSpec(a.shape, lambda b, i: (0,) * a.ndim)
    return pl.pallas_call(
        _combine_body,
        grid=(nb, nt),
        in_specs=[pl.BlockSpec((TOP_K, tile), lambda b, i: (0, b * nt + i), memory_space=pltpu.SMEM),
                  pl.BlockSpec((TOP_K, tile), lambda b, i: (0, b * nt + i)),
                  pl.BlockSpec((1, tile, d // 2), lambda b, i: (b, i, 0)),
                  pl.BlockSpec((1, tile, d), lambda b, i: (b, i, 0)),
                  mod_spec, full(eye), full(sw1b), full(sw3b), full(sw2b), full(nf),
                  pl.BlockSpec(memory_space=pl.ANY)],
        out_specs=pl.BlockSpec((1, tile, d), lambda b, i: (b, i, 0)),
        out_shape=jax.ShapeDtypeStruct((nb, t, d), F32),
        scratch_shapes=[pltpu.VMEM((TOP_K, tile, d // 2), jnp.uint32), pltpu.SemaphoreType.DMA(())],
        compiler_params=_cparams(("arbitrary", "arbitrary")),
        name="moe_combine",
    )(dest8, w8, hp, x1, g2, eye, sw1b, sw3b, sw2b, nf, yd)


def _moe(x1, hp, selT, wT, cnt, g2, EW, norm_f, tile):
    nb, t, d = x1.shape
    n = nb * t
    dest8, w8, block_e, fill = _moe_plan(selT, wT, cnt, tile)
    xd = _moe_dispatch(hp.reshape(n, d // 2), dest8, fill, tile)
    n_blocks, _ = _moe_rows(n)
    yd = _moe_experts(xd, block_e[0, :n_blocks], fill[2, 0:1], EW["w1"], EW["w3"], EW["w2"])
    return _moe_combine(dest8, w8, hp, x1, g2, yd, EW["sw1"], EW["sw3"], EW["sw2"], norm_f, tile)


def kernel(x_prompt, x_sample, c_prompt, c_sample, cache_kv_cmp, cache_kv_sel, cache_kv_win, state_conv, state_ssm, page_table, w_ada, b_ada, norm_mix, norm_ffn, w_in, cmp_pe_k, cmp_w1_k, cmp_w2_k, cmp_pe_v, cmp_w1_v, cmp_w2_v, conv_w, conv_b, dt_bias, a_log, d_skip, ssd_norm, w_out, router_w, router_bias, exp_w1, exp_w3, exp_w2, sh_w1, sh_w3, sh_w2, norm_f):
    nb, t, d = x_prompt.shape
    ndb = x_sample.shape[0]
    c_all = jnp.concatenate([c_prompt, c_sample], axis=0)
    mod = _modulation(c_all, w_ada[0], b_ada[0]).reshape(nb + ndb, 6, d)
    mod_p = [mod[:nb, k][:, None, :] for k in range(6)]
    mod_s = [mod[nb:, k][None, :, :] for k in range(6)]
    W = _prep_w_in(w_in[0])
    P = _in_proj(x_prompt, mod_p[1], mod_p[0], norm_mix[0], W, 512)
    S = _in_proj(x_sample.reshape(1, ndb, d), mod_s[1], mod_s[0], norm_mix[0], W, ndb)
    C = _prep_compress(cmp_pe_k[0], cmp_w1_k[0], cmp_w2_k[0], cmp_pe_v[0], cmp_w1_v[0], cmp_w2_v[0])
    kcvc_p = _compress_prompt(P["kcT"], C)
    ocmp_p, sel_p = _cmp_attn_prompt(P["q"], kcvc_p, 256)
    o_nsa_p = _sel_win_prompt(P["q"], sel_p, P["g"], ocmp_p, P["ksT"], P["kwT"], 256, 512)
    SP = _prep_ssd(conv_w[0], conv_b[0], dt_bias[0], a_log[0], d_skip[0], ssd_norm[0])
    y_ssd_p, ssm_p = _ssd_prompt(P["xbc"], P["z"], P["dt"], P["dtT"], SP)
    w_out_b = w_out[0].astype(BF16)
    router_wT = router_w[0].T
    EW = dict(w1=exp_w1[0].astype(BF16), w3=exp_w3[0].astype(BF16), w2=exp_w2[0].astype(BF16),
              sw1=sh_w1[0].astype(BF16), sw3=sh_w3[0].astype(BF16), sw2=sh_w2[0].astype(BF16))
    x1_p, hp_p, selT_p, wT_p, cnt_p = _post_mix(x_prompt, o_nsa_p, y_ssd_p, mod_p[2], mod_p[4], mod_p[3], norm_ffn[0],
                                                 w_out_b, router_wT, router_bias[0], 512)
    y_prompt = _moe(x1_p, hp_p, selT_p, wT_p, cnt_p, mod_p[5], EW, norm_f, 512)

    n_pool = cache_kv_cmp.shape[1]
    past = page_table.shape[1] * PAGE_SIZE
    to_pages = lambda c: jnp.transpose(c, (0, 2, 3, 4, 1)).reshape(c.shape[0], 256, c.shape[1])
    pool_cmp, pool_sel, win = to_pages(cache_kv_cmp[0]), to_pages(cache_kv_sel[0]), to_pages(cache_kv_win[0])
    kcvc_s = _compress_paged(pool_cmp, page_table, C)
    q3 = S["q"].reshape(ndb, 1, NSA_DIM)
    ocmp_s, sel_s = _cmp_attn_sample(q3, kcvc_s, past)
    o_nsa_s, win_s = _sel_win_sample(q3, S["g"].reshape(ndb, 1, KV_HEADS * LANES), ocmp_s, sel_s, pool_sel, page_table,
                                     win, S["ksT"], S["kwT"], past)
    y_ssd_s, conv_s, ssm_s = _ssd_sample(jnp.transpose(state_conv[0], (1, 0, 2)), S["xbc"][0], S["z"][0], S["dt"][0],
                                         state_ssm[0], SP)
    x1_s, hp_s, selT_s, wT_s, cnt_s = _post_mix(x_sample.reshape(1, ndb, d), o_nsa_s.reshape(1, ndb, NSA_DIM),
                                                 y_ssd_s.reshape(1, ndb, D_INNER), mod_s[2], mod_s[4], mod_s[3],
                                                 norm_ffn[0], w_out_b, router_wT, router_bias[0], ndb)
    y_sample = _moe(x1_s, hp_s, selT_s, wT_s, cnt_s, mod_s[5], EW, norm_f, ndb).reshape(ndb, 1, d)

    from_cm = lambda a: jnp.transpose(a.reshape(a.shape[0], 2, KV_HEADS, HEAD_DIM, a.shape[2]), (0, 4, 1, 2, 3))[None]
    tw = min(WINDOW, t)
    return (y_prompt, y_sample,
            from_cm(P["kcT"]), from_cm(P["ksT"]), from_cm(P["kwT"][:, :, t - tw:]),
            P["xbc"][:, t - (CONV_W - 1):, :][None], ssm_p[None],
            from_cm(jnp.transpose(S["kcT"], (2, 1, 0))), from_cm(jnp.transpose(S["ksT"], (2, 1, 0))), from_cm(win_s),
            jnp.transpose(conv_s, (1, 0, 2))[None], ssm_s[None])
```

```python
import functools
import math

import jax
import jax.numpy as jnp
import numpy as np
from jax import lax
from jax.experimental import pallas as pl
from jax.experimental.pallas import tpu as pltpu
from jax.experimental.pallas import tpu_sc as plsc

F32 = jnp.float32
BF16 = jnp.bfloat16
HIGHEST = lax.Precision.HIGHEST

D_MODEL = 1024
NSA_HEADS = 8
KV_HEADS = 2
HEAD_DIM = 64
GQA = NSA_HEADS // KV_HEADS
CMP_BLK = 32
CMP_STRIDE = 16
CMP_HID = 64
SEL_BLK = 64
N_SEL = 16
WINDOW = 512
FORCE_CUR = 2.0e4
FORCE_SINK = 1.0e4
SSD_HEADS = 8
SSD_HEAD_DIM = 64
D_INNER = SSD_HEADS * SSD_HEAD_DIM
SSD_GROUPS = 2
D_STATE = 128
CONV_W = 4
CONV_DIM = D_INNER + 2 * SSD_GROUPS * D_STATE
SSD_CHUNK = 128
NSA_DIM = NSA_HEADS * HEAD_DIM
KV_DIM = KV_HEADS * HEAD_DIM
N_EXPERTS = 64
TOP_K = 8
N_EXPERT_GROUPS = 8
TOPK_GROUPS = 4
D_EXPERT = 256
D_SHARED = 256
ROUTED_SCALE = 2.5
RMS_EPS = 1e-6
NEG_INF = -1e30
PAGE_SIZE = 128

LANES = 128
SUBLANES = 8
VMEM_LIMIT = 56 * 1024 * 1024


def _cparams(sem):
    return pltpu.CompilerParams(dimension_semantics=sem, vmem_limit_bytes=VMEM_LIMIT)


def _silu(v):
    return v * jax.nn.sigmoid(v)


def _mod_body(c_ref, w_ref, b_ref, o_ref):
    s = _silu(c_ref[...])
    o_ref[...] = jnp.dot(s, w_ref[...], preferred_element_type=F32, precision=HIGHEST) + b_ref[...]


def _modulation(c_all, w_ada, b_ada):
    n, d = c_all.shape
    nout = w_ada.shape[1]
    tn = 512
    return pl.pallas_call(
        _mod_body,
        grid=(nout // tn,),
        in_specs=[pl.BlockSpec((n, d), lambda j: (0, 0)),
                  pl.BlockSpec((d, tn), lambda j: (0, j)),
                  pl.BlockSpec((1, tn), lambda j: (0, j))],
        out_specs=pl.BlockSpec((n, tn), lambda j: (0, j)),
        out_shape=jax.ShapeDtypeStruct((n, nout), F32),
        compiler_params=_cparams(("arbitrary",)),
        name="modulation",
    )(c_all, w_ada, b_ada.reshape(1, nout))


def _nt_dot(a, b):
    return lax.dot_general(a, b, (((1,), (1,)), ((), ())), preferred_element_type=F32)


def _proj_body(x_ref, sc_ref, sh_ref, nw_ref, wq_ref, wkvT_ref, wg_ref, wz_ref, wx_ref, wdt_ref, wdtT_ref,
               q_ref, kcT_ref, ksT_ref, kwT_ref, g_ref, z_ref, xbc_ref, dt_ref, dtT_ref):
    x = x_ref[0]
    ms = jnp.mean(x * x, axis=-1, keepdims=True)
    h = x * lax.rsqrt(ms + RMS_EPS) * nw_ref[...]
    h = h * (1.0 + sc_ref[0]) + sh_ref[0]
    hb = h.astype(BF16)
    q_ref[0] = (jnp.dot(hb, wq_ref[...], preferred_element_type=F32) * (HEAD_DIM ** -0.5)).astype(BF16)
    kvT = _nt_dot(wkvT_ref[...], hb)
    kcT_ref[0] = kvT[0:256]
    ksT_ref[0] = kvT[256:512]
    kwT_ref[0] = kvT[512:768]
    g_ref[0] = jax.nn.sigmoid(jnp.dot(hb, wg_ref[...], preferred_element_type=F32))
    z_ref[0] = jnp.dot(hb, wz_ref[...], preferred_element_type=F32)
    xbc_ref[0] = jnp.dot(hb, wx_ref[...], preferred_element_type=F32)
    dt_ref[0] = jnp.dot(hb, wdt_ref[...], preferred_element_type=F32)
    dtT_ref[0] = _nt_dot(wdtT_ref[...], hb)


def _prep_w_in(w_in):
    w = w_in
    o = 0
    wq = w[:, o:o + NSA_DIM]; o += NSA_DIM
    wkv = w[:, o:o + 6 * KV_DIM]; o += 6 * KV_DIM
    wg = w[:, o:o + 3 * NSA_HEADS]; o += 3 * NSA_HEADS
    wz = w[:, o:o + D_INNER]; o += D_INNER
    wx = w[:, o:o + CONV_DIM]; o += CONV_DIM
    wdt = w[:, o:o + SSD_HEADS]; o += SSD_HEADS
    pad = lambda a: jnp.pad(a, ((0, 0), (0, LANES - a.shape[1])))
    per = 3 * GQA
    wg = jnp.concatenate([pad(wg[:, k * per:(k + 1) * per]) for k in range(KV_HEADS)], axis=1)
    return dict(wq=wq.astype(BF16), wkvT=wkv.T.astype(BF16), wg=wg.astype(BF16), wz=wz.astype(BF16),
                wx=wx.astype(BF16), wdt=pad(wdt).astype(BF16), wdtT=wdt.T.astype(BF16))


def _in_proj(x, sc, sh, norm_w, W, tm):
    nb, t, d = x.shape
    mt = sc.shape[1]
    assert t % tm == 0 and (mt == 1 or mt == t)
    if mt == 1:
        mod_spec = pl.BlockSpec((1, 1, d), lambda b, i: (b, 0, 0))
    else:
        mod_spec = pl.BlockSpec((1, tm, d), lambda b, i: (b, i, 0))
    full = lambda a: pl.BlockSpec(a.shape, lambda b, i: (0,) * a.ndim)
    row = lambda n: pl.BlockSpec((1, tm, n), lambda b, i: (b, i, 0))
    col = lambda n: pl.BlockSpec((1, n, tm), lambda b, i: (b, 0, i))
    ws = [W["wq"], W["wkvT"], W["wg"], W["wz"], W["wx"], W["wdt"], W["wdtT"]]
    outs = pl.pallas_call(
        _proj_body,
        grid=(nb, t // tm),
        in_specs=[row(d), mod_spec, mod_spec, pl.BlockSpec((1, d), lambda b, i: (0, 0))] + [full(a) for a in ws],
        out_specs=[row(NSA_DIM), col(256), col(256), col(256), row(KV_HEADS * LANES), row(D_INNER), row(CONV_DIM),
                   row(LANES), col(SSD_HEADS)],
        out_shape=[jax.ShapeDtypeStruct((nb, t, NSA_DIM), BF16),
                   jax.ShapeDtypeStruct((nb, 256, t), F32),
                   jax.ShapeDtypeStruct((nb, 256, t), F32),
                   jax.ShapeDtypeStruct((nb, 256, t), F32),
                   jax.ShapeDtypeStruct((nb, t, KV_HEADS * LANES), F32),
                   jax.ShapeDtypeStruct((nb, t, D_INNER), F32),
                   jax.ShapeDtypeStruct((nb, t, CONV_DIM), F32),
                   jax.ShapeDtypeStruct((nb, t, LANES), F32),
                   jax.ShapeDtypeStruct((nb, SSD_HEADS, t), F32)],
        compiler_params=_cparams(("arbitrary", "arbitrary")),
        name="in_proj",
    )(x, sc, sh, norm_w.reshape(1, d), *ws)
    names = ("q", "kcT", "ksT", "kwT", "g", "z", "xbc", "dt", "dtT")
    return dict(zip(names, outs))


def _prep_compress(cmp_pe_k, cmp_w1_k, cmp_w2_k, cmp_pe_v, cmp_w1_v, cmp_w2_v):
    w1s = jnp.stack([cmp_w1_k, cmp_w1_v]).reshape(2, 2, CMP_STRIDE, HEAD_DIM, CMP_HID)
    eye = jnp.eye(2, dtype=F32)
    wbd = jnp.einsum("ktldh,kK,vV->lkvdtKVh", w1s, eye, eye).reshape(CMP_STRIDE, 256, 512)
    w2s = jnp.stack([cmp_w2_k, cmp_w2_v])
    w2bd = jnp.einsum("khd,kK,vV->kvhKVd", w2s, eye, eye).reshape(256, 256)
    pes = jnp.stack([cmp_pe_k, cmp_pe_v]).reshape(2, 2, CMP_STRIDE, HEAD_DIM)
    pe_rows = jnp.broadcast_to(jnp.transpose(pes, (1, 2, 0, 3))[:, :, :, None, :],
                               (2, CMP_STRIDE, 2, KV_HEADS, HEAD_DIM)).reshape(2, CMP_STRIDE * 256)
    pecat = jnp.pad(pe_rows, ((0, SUBLANES - 2), (0, 0)))
    perm = np.zeros((LANES, LANES), np.float32)
    for l in range(CMP_STRIDE):
        for n in range(LANES // CMP_STRIDE):
            perm[(LANES // CMP_STRIDE) * l + n, CMP_STRIDE * n + l] = 1.0
    bias = pl.pallas_call(
        _cmp_bias_body,
        out_shape=jax.ShapeDtypeStruct((SUBLANES, 512), F32),
        name="cmp_bias",
    )(pecat, wbd.reshape(CMP_STRIDE * 256, 512))
    return dict(wbd=wbd.astype(BF16), w2bd=w2bd.astype(BF16), bias=bias, perm=jnp.asarray(perm, BF16))


def _cmp_bias_body(pe_ref, w_ref, o_ref):
    o_ref[...] = jnp.dot(pe_ref[...], w_ref[...], preferred_element_type=F32, precision=HIGHEST)


def _compress_body(n_pref, n_slab_refs, slabs_per_ref, *refs):
    refs = refs[n_pref:]
    slab_refs = refs[:n_slab_refs]
    perm_ref, wbd_ref, bias_ref, w2bd_ref, o_ref, z_scr = refs[n_slab_refs:]
    j = pl.program_id(1)
    nh = z_scr.shape[1]
    per_slab = LANES // CMP_STRIDE
    g_tot = n_slab_refs * slabs_per_ref
    base = pl.multiple_of(j * (per_slab * g_tot), per_slab)
    perm = perm_ref[...]
    for ri in range(n_slab_refs):
        for si in range(slabs_per_ref):
            slab = slab_refs[ri][0][:, si * LANES:(si + 1) * LANES].astype(BF16)
            xp = _nt_dot(perm, slab)
            g = ri * slabs_per_ref + si
            for l in range(CMP_STRIDE):
                z_scr[l, pl.ds(base + per_slab * g, per_slab), :] = xp[per_slab * l:per_slab * (l + 1), :]

    @pl.when(j == pl.num_programs(1) - 1)
    def _():
        acc = jnp.zeros((nh, 512), F32)
        for l in range(CMP_STRIDE):
            acc = acc + jnp.dot(z_scr[l].astype(BF16), wbd_ref[l], preferred_element_type=F32)
        lead = acc[:, :256] + bias_ref[0:1, :256]
        tail = acc[:, 256:] + bias_ref[1:2, 256:]
        hid = _silu(lead + pltpu.roll(tail, nh - 1, 0))
        out = jnp.dot(hid.astype(BF16), w2bd_ref[...], preferred_element_type=F32)
        row = lax.broadcasted_iota(jnp.int32, out.shape, 0)
        o_ref[0] = jnp.where(row < nh - 1, out, 0.0)


def _compress_prompt(kcT, C):
    nb, _, t = kcT.shape
    nh = t // CMP_STRIDE
    g = min(8, t // LANES)
    nsteps = t // (LANES * g)
    full = lambda a: pl.BlockSpec(a.shape, lambda b, j: (0,) * a.ndim)
    return pl.pallas_call(
        functools.partial(_compress_body, 0, 1, g),
        grid=(nb, nsteps),
        in_specs=[pl.BlockSpec((1, 256, LANES * g), lambda b, j: (b, 0, j)),
                  full(C["perm"]), full(C["wbd"]), full(C["bias"]), full(C["w2bd"])],
        out_specs=pl.BlockSpec((1, nh, 256), lambda b, j: (b, 0, 0)),
        out_shape=jax.ShapeDtypeStruct((nb, nh, 256), F32),
        scratch_shapes=[pltpu.VMEM((CMP_STRIDE, nh, 256), F32)],
        compiler_params=_cparams(("arbitrary", "arbitrary")),
        name="compress_prompt",
    )(kcT, C["perm"], C["wbd"], C["bias"], C["w2bd"])


def _compress_paged(pool, page_table, C):
    nb, n_pages = page_table.shape
    nh = n_pages * (PAGE_SIZE // CMP_STRIDE)
    g = min(16, n_pages)
    nsteps = n_pages // g
    full = lambda a: pl.BlockSpec(a.shape, lambda b, j, pt: (0,) * a.ndim)
    page_spec = lambda k: pl.BlockSpec((1, 256, LANES), lambda b, j, pt: (pt[b, j * g + k], 0, 0))
    return pl.pallas_call(
        functools.partial(_compress_body, 1, g, 1),
        grid_spec=pltpu.PrefetchScalarGridSpec(
            num_scalar_prefetch=1,
            grid=(nb, nsteps),
            in_specs=[page_spec(k) for k in range(g)] + [full(C["perm"]), full(C["wbd"]), full(C["bias"]), full(C["w2bd"])],
            out_specs=pl.BlockSpec((1, nh, 256), lambda b, j, pt: (b, 0, 0)),
            scratch_shapes=[pltpu.VMEM((CMP_STRIDE, nh, 256), F32)],
        ),
        out_shape=jax.ShapeDtypeStruct((nb, nh, 256), F32),
        compiler_params=_cparams(("arbitrary", "arbitrary")),
        name="compress_paged",
    )(page_table, *([pool] * g), C["perm"], C["wbd"], C["bias"], C["w2bd"])


def _alibi_slope(head):
    return float(2.0 ** (-8.0 * (head + 1) / NSA_HEADS))


def _overlap_T(nc, ns):
    cst = np.arange(nc)[None, :] * CMP_STRIDE
    sst = np.arange(ns)[:, None] * SEL_BLK
    ov = np.clip(np.minimum(cst + CMP_BLK, sst + SEL_BLK) - np.maximum(cst, sst), 0, None).astype(np.float32) / CMP_BLK
    return jnp.asarray(ov, F32)


def _masked_softmax(s, mask):
    s = jnp.where(mask, s, NEG_INF)
    p = jnp.exp(s - jnp.max(s, axis=-1, keepdims=True)) * mask.astype(F32)
    return p / jnp.maximum(jnp.sum(p, axis=-1, keepdims=True), 1e-30)


def _topk_mask(grp, k):
    ngrp = len(grp)
    cnt = [jnp.zeros(grp[0].shape, F32) for _ in range(ngrp)]
    sub = lax.broadcasted_iota(jnp.int32, grp[0].shape, 0)
    one, zero = jnp.float32(1.0), jnp.float32(0.0)
    for j in range(ngrp * SUBLANES):
        a, r = divmod(j, SUBLANES)
        row = grp[a][r:r + 1, :]
        for c in range(ngrp):
            if c < a:
                beats = jnp.where(row > grp[c], one, zero)
            elif c > a:
                beats = jnp.where(row >= grp[c], one, zero)
            else:
                beats = jnp.where(sub > r, jnp.where(row >= grp[c], one, zero), jnp.where(row > grp[c], one, zero))
            cnt[c] = cnt[c] + beats
    return [jnp.where(c < float(k), one, zero) for c in cnt]


def _select_blocks(imp, cur):
    ns = imp.shape[0]
    jrow = lax.broadcasted_iota(jnp.int32, imp.shape, 0)
    imp = jnp.where(jrow == cur, FORCE_CUR, jnp.where(jrow == 0, FORCE_SINK, imp))
    imp = jnp.where(jrow <= cur, imp, NEG_INF)
    assert ns % SUBLANES == 0
    grp = [imp[SUBLANES * a:SUBLANES * (a + 1)] for a in range(ns // SUBLANES)]
    return jnp.concatenate(_topk_mask(grp, N_SEL), axis=0)


def _cmp_attn_body(q_ref, kcvc_ref, ovT_ref, eye_ref, ocmp_ref, sel_ref):
    i = pl.program_id(1)
    tq = q_ref.shape[1]
    nh = kcvc_ref.shape[1]
    t0 = i * tq
    qpos_col = t0 + lax.broadcasted_iota(jnp.int32, (tq, 1), 0)
    cend = lax.broadcasted_iota(jnp.int32, (1, nh), 1) * CMP_STRIDE + (CMP_BLK - 1)
    mask = cend <= qpos_col
    dist = (qpos_col - cend).astype(F32)
    qpos_row = t0 + lax.broadcasted_iota(jnp.int32, (1, tq), 1)
    cur = qpos_row // SEL_BLK
    kcvc = kcvc_ref[0]
    for kvh in range(KV_HEADS):
        kc = kcvc[:, kvh * HEAD_DIM:(kvh + 1) * HEAD_DIM].astype(BF16)
        vc = kcvc[:, KV_DIM + kvh * HEAD_DIM:KV_DIM + (kvh + 1) * HEAD_DIM].astype(BF16)
        psum = jnp.zeros((tq, nh), F32)
        for g in range(GQA):
            head = kvh * GQA + g
            qg = q_ref[0, :, head * HEAD_DIM:(head + 1) * HEAD_DIM]
            s = _nt_dot(qg, kc) - _alibi_slope(head) * dist
            p = _masked_softmax(s, mask)
            ocmp_ref[0, :, head * HEAD_DIM:(head + 1) * HEAD_DIM] = jnp.dot(p.astype(BF16), vc, preferred_element_type=F32)
            psum = psum + p
        impT = lax.dot_general(ovT_ref[...], psum, (((1,), (1,)), ((), ())), preferred_element_type=F32,
                               precision=HIGHEST)
        selT = _select_blocks(impT, cur)
        sel = _nt_dot(eye_ref[...], selT.astype(BF16))
        sel_ref[0, kvh] = sel.astype(BF16)


def _cmp_attn_prompt(q, kcvc, tq):
    nb, t, _ = q.shape
    nh = kcvc.shape[1]
    ns = t // SEL_BLK
    ovT = _overlap_T(nh, ns)
    eye = jnp.eye(tq, dtype=BF16)
    return pl.pallas_call(
        _cmp_attn_body,
        grid=(nb, t // tq),
        in_specs=[pl.BlockSpec((1, tq, NSA_DIM), lambda b, i: (b, i, 0)),
                  pl.BlockSpec((1, nh, 256), lambda b, i: (b, 0, 0)),
                  pl.BlockSpec(ovT.shape, lambda b, i: (0, 0)),
                  pl.BlockSpec(eye.shape, lambda b, i: (0, 0))],
        out_specs=[pl.BlockSpec((1, tq, NSA_DIM), lambda b, i: (b, i, 0)),
                   pl.BlockSpec((1, KV_HEADS, tq, ns), lambda b, i: (b, 0, i, 0))],
        out_shape=[jax.ShapeDtypeStruct((nb, t, NSA_DIM), F32),
                   jax.ShapeDtypeStruct((nb, KV_HEADS, t, ns), BF16)],
        compiler_params=_cparams(("arbitrary", "arbitrary")),
        name="cmp_attn",
    )(q, kcvc, ovT, eye)


def _head_rows(qrow, kvh):
    rows = [qrow[:, (kvh * GQA + g) * HEAD_DIM:(kvh * GQA + g + 1) * HEAD_DIM] for g in range(GQA)]
    return jnp.concatenate(rows + [jnp.zeros((SUBLANES - GQA, HEAD_DIM), qrow.dtype)], axis=0)


def _slope_col(kvh):
    r = lax.broadcasted_iota(jnp.int32, (SUBLANES, 1), 0)
    col = jnp.zeros((SUBLANES, 1), F32)
    for g in range(GQA):
        col = jnp.where(r == g, _alibi_slope(kvh * GQA + g), col)
    return col


def _cmp_attn_sample_body(past, q_ref, kcvc_ref, ovT_ref, eye_ref, ocmp_ref, sel_ref):
    nh = kcvc_ref.shape[1]
    cend = lax.broadcasted_iota(jnp.int32, (1, nh), 1) * CMP_STRIDE + (CMP_BLK - 1)
    mask = cend <= past
    dist = (past - cend).astype(F32)
    cur = jnp.full((1, SUBLANES), past // SEL_BLK, jnp.int32)
    kcvc = kcvc_ref[0]
    qrow = q_ref[0]
    for kvh in range(KV_HEADS):
        kc = kcvc[:, kvh * HEAD_DIM:(kvh + 1) * HEAD_DIM].astype(BF16)
        vc = kcvc[:, KV_DIM + kvh * HEAD_DIM:KV_DIM + (kvh + 1) * HEAD_DIM].astype(BF16)
        s = _nt_dot(_head_rows(qrow, kvh), kc) - _slope_col(kvh) * dist
        p = _masked_softmax(s, mask)
        o = jnp.dot(p.astype(BF16), vc, preferred_element_type=F32)
        for g in range(GQA):
            head = kvh * GQA + g
            ocmp_ref[0, :, head * HEAD_DIM:(head + 1) * HEAD_DIM] = o[g:g + 1, :]
        psum = jnp.sum(p[0:GQA], axis=0, keepdims=True)
        impT = lax.dot_general(ovT_ref[...], jnp.broadcast_to(psum, (SUBLANES, nh)), (((1,), (1,)), ((), ())),
                               preferred_element_type=F32, precision=HIGHEST)
        selT = _select_blocks(impT, cur)
        sel_ref[0, kvh] = _nt_dot(eye_ref[...], selT.astype(BF16)).astype(jnp.int32)


def _cmp_attn_sample(q3, kcvc, past):
    nb = q3.shape[0]
    nh = kcvc.shape[1]
    ns = past // SEL_BLK + 1
    ns_pad = -(-ns // SUBLANES) * SUBLANES
    ovT = jnp.pad(_overlap_T(nh, ns), ((0, ns_pad - ns), (0, 0)))
    eye = jnp.eye(SUBLANES, dtype=BF16)
    ocmp, sel = pl.pallas_call(
        functools.partial(_cmp_attn_sample_body, past),
        grid=(nb,),
        in_specs=[pl.BlockSpec((1, 1, NSA_DIM), lambda b: (b, 0, 0)),
                  pl.BlockSpec((1, nh, 256), lambda b: (b, 0, 0)),
                  pl.BlockSpec(ovT.shape, lambda b: (0, 0)),
                  pl.BlockSpec(eye.shape, lambda b: (0, 0))],
        out_specs=[pl.BlockSpec((1, 1, NSA_DIM), lambda b: (b, 0, 0)),
                   pl.BlockSpec((1, KV_HEADS, SUBLANES, ns_pad), lambda b: (b, 0, 0, 0))],
        out_shape=[jax.ShapeDtypeStruct((nb, 1, NSA_DIM), F32),
                   jax.ShapeDtypeStruct((nb, KV_HEADS, SUBLANES, ns_pad), jnp.int32)],
        compiler_params=_cparams(("arbitrary",)),
        name="cmp_attn_sample",
    )(q3, kcvc, ovT, eye)
    return ocmp, sel[:, :, 0, :]


def _pos_rows(n, start=0):
    tab = np.zeros((HEAD_DIM, n), np.float32)
    k = start + np.arange(n)
    tab[0] = k // SEL_BLK
    tab[1] = k % SEL_BLK
    return jnp.asarray(tab, BF16)


def _slope_rows():
    tab = np.zeros((KV_HEADS, SUBLANES, HEAD_DIM), np.float32)
    for k in range(KV_HEADS):
        for g in range(GQA):
            s = _alibi_slope(k * GQA + g)
            tab[k, g, 0] = SEL_BLK * s
            tab[k, g, 1] = s
    return jnp.asarray(tab, F32)


def _block_expand(ns, n):
    return jnp.asarray((np.arange(n)[None, :] // SEL_BLK == np.arange(ns)[:, None]).astype(np.float32), BF16)


def _flash_step(q4, kT_aug, vT, bias, m_scr, l_scr, acc_scr):
    n, tk = q4.shape[0], kT_aug.shape[1]
    s = jnp.dot(q4, kT_aug, preferred_element_type=F32)
    rb = bias.shape[0]
    if rb in (1, n):
        s = s + bias
    else:
        s = (s.reshape(n // rb, rb, tk) + bias[None]).reshape(n, tk)
    m_old = m_scr[...]
    m_new = jnp.maximum(m_old, jnp.max(s, axis=-1, keepdims=True))
    alpha = jnp.exp(m_old - m_new)
    p = jnp.exp(s - jnp.concatenate([m_new] * (tk // LANES), axis=1))
    l_scr[...] = alpha * l_scr[...] + jnp.sum(p, axis=-1, keepdims=True)
    acc_scr[...] = alpha[:, :HEAD_DIM] * acc_scr[...] + _nt_dot(p.astype(BF16), vT)
    m_scr[...] = m_new


def _flash_result(l_scr, acc_scr):
    return acc_scr[...] / jnp.maximum(l_scr[...][:, :HEAD_DIM], 1e-30)


def _sel_win_body(tk, q_ref, sel_ref, g_ref, ocmp_ref, ksT_ref, vsT_ref, kwT_ref, vwT_ref, pos_ref, exp_ref, slope_ref,
                  o_ref, m_scr, l_scr, acc_scr):
    i = pl.program_id(2)
    tq = q_ref.shape[1]
    t0 = i * tq
    qpos = t0 + lax.broadcasted_iota(jnp.int32, (tq, 1), 0)
    q4 = jnp.concatenate(
        [jnp.concatenate([q_ref[0, :, g * HEAD_DIM:(g + 1) * HEAD_DIM],
                          jnp.broadcast_to(slope_ref[0, g:g + 1, :], (tq, HEAD_DIM)).astype(BF16)], axis=1)
         for g in range(GQA)], axis=0)

    def reset():
        m_scr[...] = jnp.full(m_scr.shape, NEG_INF, F32)
        l_scr[...] = jnp.zeros(l_scr.shape, F32)
        acc_scr[...] = jnp.zeros(acc_scr.shape, F32)

    def finish():
        return _flash_result(l_scr, acc_scr)

    reset()
    sel = sel_ref[0, 0]

    def sel_step(kt, carry):
        k0 = pl.multiple_of(kt * tk, tk)
        kpos = k0 + lax.broadcasted_iota(jnp.int32, (1, tk), 1)
        chosen = jnp.dot(sel, exp_ref[:, pl.ds(k0, tk)], preferred_element_type=F32)
        bias = (jnp.where(kpos <= qpos, chosen, 0.0) - 1.0) * (-NEG_INF)
        kT_aug = jnp.concatenate([ksT_ref[0, :, pl.ds(k0, tk)].astype(BF16), pos_ref[:, pl.ds(k0, tk)]], axis=0)
        _flash_step(q4, kT_aug, vsT_ref[0, :, pl.ds(k0, tk)].astype(BF16), bias, m_scr, l_scr, acc_scr)
        return carry

    lax.fori_loop(0, (t0 + tq + tk - 1) // tk, sel_step, 0)
    o_sel = finish()

    reset()
    wk = WINDOW + tq
    k0 = pl.multiple_of(jnp.maximum(t0 - WINDOW, 0), tq)
    dist = qpos - (k0 + lax.broadcasted_iota(jnp.int32, (1, wk), 1))
    bias = jnp.where(lax.bitcast_convert_type(dist, jnp.uint32) < jnp.uint32(WINDOW), 0.0, NEG_INF)
    kT_aug = jnp.concatenate([kwT_ref[0, :, pl.ds(k0, wk)].astype(BF16), pos_ref[:, pl.ds(k0, wk)]], axis=0)
    _flash_step(q4, kT_aug, vwT_ref[0, :, pl.ds(k0, wk)].astype(BF16), bias, m_scr, l_scr, acc_scr)
    o_win = finish()
    gates = g_ref[0]
    for g in range(GQA):
        rows = slice(g * tq, (g + 1) * tq)
        cols = slice(g * HEAD_DIM, (g + 1) * HEAD_DIM)
        o = (gates[:, 3 * g:3 * g + 1] * ocmp_ref[0, :, cols] + gates[:, 3 * g + 1:3 * g + 2] * o_sel[rows]
             + gates[:, 3 * g + 2:3 * g + 3] * o_win[rows])
        o_ref[0, :, cols] = o.astype(o_ref.dtype)


def _sel_win_prompt(q, sel, gates, ocmp, ksT, kwT, tq, tk):
    nb, t, _ = q.shape
    ns = sel.shape[-1]
    assert t % tk == 0 and t % tq == 0 and WINDOW % tq == 0
    pos, expand, slopes = _pos_rows(t), _block_expand(ns, t), _slope_rows()
    grp = GQA * HEAD_DIM
    kv_spec = lambda which: pl.BlockSpec((1, HEAD_DIM, t), lambda b, k, i: (b, which * KV_HEADS + k, 0))
    return pl.pallas_call(
        functools.partial(_sel_win_body, tk),
        grid=(nb, KV_HEADS, t // tq),
        in_specs=[pl.BlockSpec((1, tq, grp), lambda b, k, i: (b, i, k)),
                  pl.BlockSpec((1, 1, tq, ns), lambda b, k, i: (b, k, i, 0)),
                  pl.BlockSpec((1, tq, LANES), lambda b, k, i: (b, i, k)),
                  pl.BlockSpec((1, tq, grp), lambda b, k, i: (b, i, k)),
                  kv_spec(0), kv_spec(1), kv_spec(0), kv_spec(1),
                  pl.BlockSpec(pos.shape, lambda b, k, i: (0, 0)),
                  pl.BlockSpec(expand.shape, lambda b, k, i: (0, 0)),
                  pl.BlockSpec((1, SUBLANES, HEAD_DIM), lambda b, k, i: (k, 0, 0))],
        out_specs=pl.BlockSpec((1, tq, grp), lambda b, k, i: (b, i, k)),
        out_shape=jax.ShapeDtypeStruct((nb, t, NSA_DIM), BF16),
        scratch_shapes=[pltpu.VMEM((GQA * tq, LANES), F32), pltpu.VMEM((GQA * tq, LANES), F32),
                        pltpu.VMEM((GQA * tq, HEAD_DIM), F32)],
        compiler_params=_cparams(("arbitrary", "arbitrary", "arbitrary")),
        name="sel_win_attn",
    )(q, sel, gates, ocmp, ksT, ksT, kwT, kwT, pos, expand, slopes)


def _sel_win_sample_body(past, g_pages, pt_ref, sel_ref, q_ref, g_ref, ocmp_ref, *refs):
    page_refs = refs[:g_pages]
    (win_ref, ksn_ref, kwn_ref, pos_ref, wpos_ref, slope_ref, o_ref, wout_ref, m_scr, l_scr, acc_scr) = refs[g_pages:]
    b = pl.program_id(0)
    j = pl.program_id(1)
    ns_pad = sel_ref.shape[1] // KV_HEADS
    qrow = q_ref[0]
    q4 = [jnp.concatenate([_head_rows(qrow, k), slope_ref[k].astype(BF16)], axis=1) for k in range(KV_HEADS)]

    @pl.when(j == 0)
    def _():
        m_scr[...] = jnp.full(m_scr.shape, NEG_INF, F32)
        l_scr[...] = jnp.zeros(l_scr.shape, F32)
        acc_scr[...] = jnp.zeros(acc_scr.shape, F32)

    width = g_pages * PAGE_SIZE
    lane_blk = lax.broadcasted_iota(jnp.int32, (1, width), 1) // SEL_BLK
    row0 = lax.broadcasted_iota(jnp.int32, (HEAD_DIM, 1), 0) == 0
    blk0 = j * (width // SEL_BLK)
    pos = (pos_ref[...].astype(F32) + jnp.where(row0, blk0.astype(F32), 0.0)).astype(BF16)
    for k in range(KV_HEADS):
        kT = jnp.concatenate([r[0, k * HEAD_DIM:(k + 1) * HEAD_DIM, :] for r in page_refs], axis=1).astype(BF16)
        vT = jnp.concatenate([r[0, KV_DIM + k * HEAD_DIM:KV_DIM + (k + 1) * HEAD_DIM, :] for r in page_refs],
                             axis=1).astype(BF16)
        bias = jnp.full((1, width), NEG_INF, F32)
        for blk in range(width // SEL_BLK):
            chosen = sel_ref[b, k * ns_pad + blk0 + blk] > 0
            bias = jnp.where(lane_blk == blk, jnp.where(chosen, 0.0, NEG_INF), bias)
        _flash_step(q4[k], jnp.concatenate([kT, pos], axis=0), vT, bias, m_scr.at[k], l_scr.at[k], acc_scr.at[k])

    @pl.when(j == pl.num_programs(1) - 1)
    def _():
        nb = ksn_ref.shape[2]
        pick = lax.broadcasted_iota(jnp.int32, (1, nb), 1) == b
        ks_new = jnp.sum(jnp.where(pick, ksn_ref[0], 0.0), axis=1, keepdims=True)
        kw_new = jnp.sum(jnp.where(pick, kwn_ref[0], 0.0), axis=1, keepdims=True)
        lane = lax.broadcasted_iota(jnp.int32, (1, LANES), 1)
        tile_new = jnp.where(lane == 0, ks_new, 0.0).astype(BF16)
        pos_new = jnp.where(row0 & (lane == 0), float(past // SEL_BLK), 0.0).astype(BF16)
        bias_new = jnp.where(lane == 0, 0.0, NEG_INF)
        wlane = lax.broadcasted_iota(jnp.int32, (1, win_ref.shape[2]), 1)
        wout = jnp.where(wlane == win_ref.shape[2] - 1, kw_new, pltpu.roll(win_ref[0], win_ref.shape[2] - 1, 1))
        wout_ref[0] = wout
        woutb = wout.astype(BF16)
        gates = g_ref[0]
        for k in range(KV_HEADS):
            ksl = slice(k * HEAD_DIM, (k + 1) * HEAD_DIM)
            vsl = slice(KV_DIM + k * HEAD_DIM, KV_DIM + (k + 1) * HEAD_DIM)
            _flash_step(q4[k], jnp.concatenate([tile_new[ksl], pos_new], axis=0), tile_new[vsl], bias_new,
                        m_scr.at[k], l_scr.at[k], acc_scr.at[k])
            o_sel = _flash_result(l_scr.at[k], acc_scr.at[k])
            m_scr[k] = jnp.full(m_scr.shape[1:], NEG_INF, F32)
            l_scr[k] = jnp.zeros(l_scr.shape[1:], F32)
            acc_scr[k] = jnp.zeros(acc_scr.shape[1:], F32)
            _flash_step(q4[k], jnp.concatenate([woutb[ksl], wpos_ref[...]], axis=0), woutb[vsl],
                        jnp.zeros((1, win_ref.shape[2]), F32), m_scr.at[k], l_scr.at[k], acc_scr.at[k])
            o_win = _flash_result(l_scr.at[k], acc_scr.at[k])
            for g in range(GQA):
                head = k * GQA + g
                cols = slice(head * HEAD_DIM, (head + 1) * HEAD_DIM)
                c0 = k * LANES + 3 * g
                o = (gates[:, c0:c0 + 1] * ocmp_ref[0, :, cols] + gates[:, c0 + 1:c0 + 2] * o_sel[g:g + 1, :]
                     + gates[:, c0 + 2:c0 + 3] * o_win[g:g + 1, :])
                o_ref[0, :, cols] = o.astype(o_ref.dtype)


def _sel_win_sample(q3, g3, ocmp, sel, pool_sel, page_table, win, ksT_new, kwT_new, past):
    nb, n_pages = page_table.shape
    wlen = win.shape[2]
    assert wlen == WINDOW and past >= WINDOW
    g = min(16, n_pages)
    ns_pad = sel.shape[2]
    pos, wpos, slopes = _pos_rows(g * PAGE_SIZE), _pos_rows(wlen, past - wlen + 1), _slope_rows()
    full = lambda a: pl.BlockSpec(a.shape, lambda b, j, pt, sl: (0,) * a.ndim)
    row = lambda n: pl.BlockSpec((1, 1, n), lambda b, j, pt, sl: (b, 0, 0))
    page_spec = lambda k: pl.BlockSpec((1, 256, PAGE_SIZE), lambda b, j, pt, sl: (pt[b, j * g + k], 0, 0))
    wspec = pl.BlockSpec((1, 256, wlen), lambda b, j, pt, sl: (b, 0, 0))
    return pl.pallas_call(
        functools.partial(_sel_win_sample_body, past, g),
        grid_spec=pltpu.PrefetchScalarGridSpec(
            num_scalar_prefetch=2,
            grid=(nb, n_pages // g),
            in_specs=[row(NSA_DIM), row(KV_HEADS * LANES), row(NSA_DIM)] + [page_spec(k) for k in range(g)]
                     + [wspec, full(ksT_new), full(kwT_new), full(pos), full(wpos), full(slopes)],
            out_specs=[row(NSA_DIM), wspec],
            scratch_shapes=[pltpu.VMEM((KV_HEADS, SUBLANES, LANES), F32), pltpu.VMEM((KV_HEADS, SUBLANES, LANES), F32),
                            pltpu.VMEM((KV_HEADS, SUBLANES, HEAD_DIM), F32)],
        ),
        out_shape=[jax.ShapeDtypeStruct((nb, 1, NSA_DIM), BF16), jax.ShapeDtypeStruct((nb, 256, wlen), F32)],
        compiler_params=_cparams(("arbitrary", "arbitrary")),
        name="sel_win_sample",
    )(page_table, sel.reshape(nb, KV_HEADS * ns_pad), q3, g3, ocmp, *([pool_sel] * g), win, ksT_new, kwT_new, pos,
      wpos, slopes)


def _softplus(v):
    return jnp.maximum(v, 0.0) + jnp.log1p(jnp.exp(-jnp.abs(v)))


def _tn_dot(a, b):
    return lax.dot_general(a, b, (((0,), (0,)), ((), ())), preferred_element_type=F32)


def _prep_ssd(conv_w, conv_b, dt_bias, a_log, d_skip, ssd_norm):
    padl = lambda v: jnp.pad(v.reshape(1, -1), ((0, 0), (0, LANES - v.shape[0])))
    L = SSD_CHUNK
    tril = jnp.asarray(np.tril(np.ones((L, L), np.float32)))
    return dict(conv_w=conv_w, conv_b=conv_b.reshape(1, -1), dtb_row=padl(dt_bias), dtb_col=dt_bias.reshape(-1, 1),
                alog_row=padl(a_log), alog_col=a_log.reshape(-1, 1), dskip=padl(d_skip), norm=ssd_norm.reshape(1, -1),
                tril=tril, triu=tril.T)


def _ssd_chunk(u, z, dt, dtT, h_prev, P):
    a_row = -jnp.exp(P["alog_row"][...])
    a_col = -jnp.exp(P["alog_col"][...])
    acum = jnp.dot(P["tril"][...], dt * a_row, preferred_element_type=F32, precision=HIGHEST)
    acumT = jnp.dot(dtT * a_col, P["triu"][...], preferred_element_type=F32, precision=HIGHEST)
    L = u.shape[0]
    li = lax.broadcasted_iota(jnp.int32, (L, L), 0)
    si = lax.broadcasted_iota(jnp.int32, (L, L), 1)
    causal = li >= si
    gn = SSD_GROUPS * D_STATE
    ys, hs = [], []
    per = SSD_HEADS // SSD_GROUPS
    for g in range(SSD_GROUPS):
        bm = u[:, D_INNER + g * D_STATE:D_INNER + (g + 1) * D_STATE]
        cm = u[:, D_INNER + gn + g * D_STATE:D_INNER + gn + (g + 1) * D_STATE]
        bmb = bm.astype(BF16)
        cb = _nt_dot(cm.astype(BF16), bmb)
        for e in range(per):
            h = g * per + e
            ac = acum[:, h:h + 1]
            seg = ac - acumT[h:h + 1, :]
            decay = jnp.where(causal, jnp.exp(jnp.where(causal, seg, 0.0)), 0.0)
            xs = u[:, h * SSD_HEAD_DIM:(h + 1) * SSD_HEAD_DIM]
            xdt = xs * dt[:, h:h + 1]
            y = jnp.dot((cb * decay).astype(BF16), xdt.astype(BF16), preferred_element_type=F32)
            a_last = acum[L - 1:L, h:h + 1]
            st = _tn_dot((xdt * jnp.exp(a_last - ac)).astype(BF16), bmb)
            y = y + _nt_dot((cm * jnp.exp(ac)).astype(BF16), h_prev[h].astype(BF16))
            hs.append(jnp.exp(a_last) * h_prev[h] + st)
            ys.append(y + P["dskip"][:, h:h + 1] * xs)
    return ys, hs


def _ssd_finish(ys, z, norm_w):
    y = jnp.concatenate(ys, axis=1) * _silu(z)
    ms = jnp.mean(y * y, axis=-1, keepdims=True)
    return y * lax.rsqrt(ms + RMS_EPS) * norm_w


def _ssd_prompt_body(xbc_ref, z_ref, dt_ref, dtT_ref, cw_ref, cb_ref, dtbr_ref, dtbc_ref, alr_ref, alc_ref, dsk_ref,
                     nrm_ref, tril_ref, triu_ref, y_ref, hout_ref, xpad_scr, h_scr):
    c = pl.program_id(1)
    L = xbc_ref.shape[1]

    @pl.when(c == 0)
    def _():
        xpad_scr[0:SUBLANES, :] = jnp.zeros((SUBLANES, xpad_scr.shape[1]), F32)
        h_scr[...] = jnp.zeros(h_scr.shape, F32)

    xt = xbc_ref[0]
    xpad_scr[SUBLANES:SUBLANES + L, :] = xt
    conv = cb_ref[...] + xpad_scr[SUBLANES - (CONV_W - 1):SUBLANES - (CONV_W - 1) + L, :] * cw_ref[0:1, :]
    for k in range(1, CONV_W):
        o = SUBLANES - (CONV_W - 1) + k
        conv = conv + xpad_scr[o:o + L, :] * cw_ref[k:k + 1, :]
    xpad_scr[0:SUBLANES, :] = xt[L - SUBLANES:L, :]
    u = _silu(conv)
    dt = _softplus(dt_ref[0] + dtbr_ref[...])
    dtT = _softplus(dtT_ref[0] + dtbc_ref[...])
    P = dict(alog_row=alr_ref, alog_col=alc_ref, tril=tril_ref, triu=triu_ref, dskip=dsk_ref[...])
    ys, hs = _ssd_chunk(u, z_ref[0], dt, dtT, [h_scr[h] for h in range(SSD_HEADS)], P)
    for h in range(SSD_HEADS):
        h_scr[h] = hs[h]
    y_ref[0] = _ssd_finish(ys, z_ref[0], nrm_ref[...]).astype(y_ref.dtype)

    @pl.when(c == pl.num_programs(1) - 1)
    def _():
        hout_ref[0] = h_scr[...]


def _ssd_prompt(xbc, z, dt, dtT, SP):
    nb, t, cd = xbc.shape
    L = SSD_CHUNK
    assert t % L == 0
    full = lambda a: pl.BlockSpec(a.shape, lambda b, c: (0,) * a.ndim)
    names = ("conv_w", "conv_b", "dtb_row", "dtb_col", "alog_row", "alog_col", "dskip", "norm", "tril", "triu")
    ps = [SP[n] for n in names]
    return pl.pallas_call(
        _ssd_prompt_body,
        grid=(nb, t // L),
        in_specs=[pl.BlockSpec((1, L, cd), lambda b, c: (b, c, 0)),
                  pl.BlockSpec((1, L, D_INNER), lambda b, c: (b, c, 0)),
                  pl.BlockSpec((1, L, LANES), lambda b, c: (b, c, 0)),
                  pl.BlockSpec((1, SSD_HEADS, L), lambda b, c: (b, 0, c))] + [full(a) for a in ps],
        out_specs=[pl.BlockSpec((1, L, D_INNER), lambda b, c: (b, c, 0)),
                   pl.BlockSpec((1, SSD_HEADS, SSD_HEAD_DIM, D_STATE), lambda b, c: (b, 0, 0, 0))],
        out_shape=[jax.ShapeDtypeStruct((nb, t, D_INNER), BF16),
                   jax.ShapeDtypeStruct((nb, SSD_HEADS, SSD_HEAD_DIM, D_STATE), F32)],
        scratch_shapes=[pltpu.VMEM((SUBLANES + L, cd), F32), pltpu.VMEM((SSD_HEADS, SSD_HEAD_DIM, D_STATE), F32)],
        compiler_params=_cparams(("arbitrary", "arbitrary")),
        name="ssd_prompt",
    )(xbc, z, dt, dtT, *ps)


def _ssd_sample_body(cs_ref, xbc_ref, z_ref, dt_ref, h0_ref, cw_ref, cb_ref, dtb_ref, al_ref, dsk_ref, nrm_ref, eye_ref,
                     y_ref, cso_ref, h_ref):
    nseq = xbc_ref.shape[0]
    xn = xbc_ref[...]
    conv = cb_ref[...] + xn * cw_ref[CONV_W - 1:CONV_W, :]
    for k in range(CONV_W - 1):
        conv = conv + cs_ref[k] * cw_ref[k:k + 1, :]
        if k > 0:
            cso_ref[k - 1] = cs_ref[k]
    cso_ref[CONV_W - 2] = xn
    u = _silu(conv)
    dt = _softplus(dt_ref[...] + dtb_ref[...])
    decay = jnp.exp(dt * (-jnp.exp(al_ref[...])))
    eye = eye_ref[...]
    gn = SSD_GROUPS * D_STATE
    per = SSD_HEADS // SSD_GROUPS
    rows = []
    for s in range(nseq):
        ys = []
        for h in range(SSD_HEADS):
            g = h // per
            xs = u[s:s + 1, h * SSD_HEAD_DIM:(h + 1) * SSD_HEAD_DIM]
            bm = u[s:s + 1, D_INNER + g * D_STATE:D_INNER + (g + 1) * D_STATE]
            cm = u[s:s + 1, D_INNER + gn + g * D_STATE:D_INNER + gn + (g + 1) * D_STATE]
            xcol = jnp.sum(eye * xs, axis=1, keepdims=True)
            hn = decay[s:s + 1, h:h + 1] * h0_ref[s, h] + (dt[s:s + 1, h:h + 1] * xcol) * bm
            h_ref[s, h] = hn
            ycol = jnp.sum(hn * cm, axis=1, keepdims=True)
            ys.append(jnp.sum(eye * ycol, axis=0, keepdims=True) + dsk_ref[:, h:h + 1] * xs)
        rows.append(jnp.concatenate(ys, axis=1))
    y = jnp.concatenate(rows, axis=0) * _silu(z_ref[...])
    ms = jnp.mean(y * y, axis=-1, keepdims=True)
    y_ref[...] = (y * lax.rsqrt(ms + RMS_EPS) * nrm_ref[...]).astype(y_ref.dtype)


def _ssd_sample(conv_state, xbc, z, dt, h0, SP):
    nb, cd = xbc.shape
    ts = SUBLANES
    assert nb % ts == 0
    eye = jnp.eye(SSD_HEAD_DIM, dtype=F32)
    names = ("conv_w", "conv_b", "dtb_row", "alog_row", "dskip", "norm")
    ps = [SP[n] for n in names] + [eye]
    full = lambda a: pl.BlockSpec(a.shape, lambda i: (0,) * a.ndim)
    st = pl.BlockSpec((ts, SSD_HEADS, SSD_HEAD_DIM, D_STATE), lambda i: (i, 0, 0, 0))
    cs = pl.BlockSpec((CONV_W - 1, ts, cd), lambda i: (0, i, 0))
    row = lambda n: pl.BlockSpec((ts, n), lambda i: (i, 0))
    return pl.pallas_call(
        _ssd_sample_body,
        grid=(nb // ts,),
        in_specs=[cs, row(cd), row(D_INNER), row(LANES), st] + [full(a) for a in ps],
        out_specs=[row(D_INNER), cs, st],
        out_shape=[jax.ShapeDtypeStruct((nb, D_INNER), BF16),
                   jax.ShapeDtypeStruct((CONV_W - 1, nb, cd), F32),
                   jax.ShapeDtypeStruct(h0.shape, F32)],
        compiler_params=_cparams(("arbitrary",)),
        name="ssd_sample",
    )(conv_state, xbc, z, dt, h0, *ps)


def _pack_bf16_pairs(v):
    m = v.shape[1] // 2
    hi = pltpu.bitcast(v[:, :m].astype(BF16).astype(F32), jnp.uint32)
    lo = pltpu.bitcast(v[:, m:].astype(BF16).astype(F32), jnp.uint32)
    return hi | (lo >> 16)


def _unpack_bf16_pairs(w):
    hi = pltpu.bitcast(w & jnp.uint32(0xFFFF0000), F32)
    lo = pltpu.bitcast(w << 16, F32)
    return hi.astype(BF16), lo.astype(BF16)


def _route(logitsT, bias_col):
    s = jax.nn.sigmoid(logitsT)
    sb = s + bias_col
    per = N_EXPERTS // N_EXPERT_GROUPS
    assert per == SUBLANES
    grp = [sb[per * a:per * (a + 1)] for a in range(N_EXPERT_GROUPS)]
    sub = lax.broadcasted_iota(jnp.int32, grp[0].shape, 0)
    gs = []
    for ga in grp:
        m1 = jnp.max(ga, axis=0, keepdims=True)
        first = jnp.min(jnp.where(ga == m1, sub, per), axis=0, keepdims=True)
        m2 = jnp.max(jnp.where(sub == first, NEG_INF, ga), axis=0, keepdims=True)
        gs.append(m1 + m2)
    gmask = _topk_mask([jnp.concatenate(gs, axis=0)], TOPK_GROUPS)[0]
    masked = [jnp.where(gmask[a:a + 1, :] > 0.5, grp[a], NEG_INF) for a in range(N_EXPERT_GROUPS)]
    sel = jnp.concatenate(_topk_mask(masked, TOP_K), axis=0)
    w = s * sel
    w = w / jnp.sum(w, axis=0, keepdims=True) * ROUTED_SCALE
    return sel, w


def _post_mix_body(x_ref, on_ref, ys_ref, g1_ref, sc_ref, sh_ref, nw_ref, wo_ref, rw_ref, rb_ref,
                   x1_ref, hp_ref, selT_ref, wT_ref, cnt_ref):
    first = (pl.program_id(0) == 0) & (pl.program_id(1) == 0)
    half = wo_ref.shape[0] // 2
    mix = (jnp.dot(on_ref[0], wo_ref[0:half, :], preferred_element_type=F32)
           + jnp.dot(ys_ref[0], wo_ref[half:, :], preferred_element_type=F32))
    x1 = x_ref[0] + g1_ref[0] * mix
    x1_ref[0] = x1
    ms = jnp.mean(x1 * x1, axis=-1, keepdims=True)
    h = x1 * lax.rsqrt(ms + RMS_EPS) * nw_ref[...]
    h = h * (1.0 + sc_ref[0]) + sh_ref[0]
    hp_ref[0] = _pack_bf16_pairs(h)
    logitsT = lax.dot_general(rw_ref[...], h, (((1,), (1,)), ((), ())), preferred_element_type=F32,
                              precision=HIGHEST)
    sel, w = _route(logitsT, rb_ref[...])
    selT_ref[...] = sel.astype(selT_ref.dtype)
    wT_ref[...] = w

    @pl.when(first)
    def _():
        cnt_ref[...] = jnp.zeros(cnt_ref.shape, F32)

    cnt_ref[...] += jnp.broadcast_to(jnp.sum(sel, axis=1, keepdims=True), cnt_ref.shape)


def _post_mix(x, o_nsa, y_ssd, g1, sc2, sh2, norm_w, w_out_b, router_wT, router_bias, tm):
    nb, t, d = x.shape
    mt = g1.shape[1]
    nt = t // tm
    assert t % tm == 0 and (mt == 1 or mt == t)
    if mt == 1:
        mod_spec = pl.BlockSpec((1, 1, d), lambda b, i: (b, 0, 0))
    else:
        mod_spec = pl.BlockSpec((1, tm, d), lambda b, i: (b, i, 0))
    row = lambda n: pl.BlockSpec((1, tm, n), lambda b, i: (b, i, 0))
    full = lambda a: pl.BlockSpec(a.shape, lambda b, i: (0,) * a.ndim)
    tok = lambda: pl.BlockSpec((N_EXPERTS, tm), lambda b, i: (0, b * nt + i))
    rb = router_bias.reshape(N_EXPERTS, 1)
    nw = norm_w.reshape(1, d)
    return pl.pallas_call(
        _post_mix_body,
        grid=(nb, nt),
        in_specs=[row(d), row(NSA_DIM), row(D_INNER), mod_spec, mod_spec, mod_spec, full(nw), full(w_out_b),
                  full(router_wT), full(rb)],
        out_specs=[row(d), row(d // 2), tok(), tok(), pl.BlockSpec((N_EXPERTS, LANES), lambda b, i: (0, 0))],
        out_shape=[jax.ShapeDtypeStruct((nb, t, d), F32),
                   jax.ShapeDtypeStruct((nb, t, d // 2), jnp.uint32),
                   jax.ShapeDtypeStruct((N_EXPERTS, nb * t), BF16),
                   jax.ShapeDtypeStruct((N_EXPERTS, nb * t), F32),
                   jax.ShapeDtypeStruct((N_EXPERTS, LANES), F32)],
        compiler_params=_cparams(("arbitrary", "arbitrary")),
        name="post_mix",
    )(x, o_nsa, y_ssd, g1, sc2, sh2, nw, w_out_b, router_wT, rb)


MOE_BLOCK_SHIFT = 9
MOE_BLOCK = 1 << MOE_BLOCK_SHIFT


def _moe_rows(n_tok):
    n_blocks = n_tok * TOP_K // MOE_BLOCK + N_EXPERTS
    n_blocks_pad = -(-n_blocks // LANES) * LANES
    return n_blocks, n_blocks_pad


def _plan_body(selT_ref, wT_ref, cnt_ref, triu_ref, tril_ref, eye_ref, dest_ref, w8_ref, be_ref, fill_ref,
               carry_scr, pstart_scr):
    step = pl.program_id(0)
    ne = N_EXPERTS

    @pl.when(step == 0)
    def _():
        cnt = cnt_ref[...]
        cnt_i = cnt.astype(jnp.int32)
        padded = (((cnt_i + (MOE_BLOCK - 1)) >> MOE_BLOCK_SHIFT) << MOE_BLOCK_SHIFT).astype(F32)
        pstart = jnp.dot(tril_ref[...].astype(F32), padded, preferred_element_type=F32, precision=HIGHEST)
        pstart_scr[...] = pstart
        carry_scr[...] = jnp.zeros(carry_scr.shape, F32)
        pend = pstart + padded
        nbp = be_ref.shape[1]
        starts = (lax.broadcasted_iota(jnp.int32, (1, nbp), 1) * MOE_BLOCK).astype(F32)
        below = jnp.where(pend[:, 0:1] <= starts, 1.0, 0.0)
        be_ref[...] = jnp.minimum(jnp.sum(below, axis=0, keepdims=True), float(ne - 1)).astype(jnp.int32)
        eye = eye_ref[...]
        to_row = lambda col: jnp.sum(col * eye, axis=0, keepdims=True)
        n_used = jnp.max(pend, axis=0, keepdims=True) * (1.0 / MOE_BLOCK)
        rows = jnp.concatenate([to_row(pstart + cnt), to_row(padded - cnt), n_used,
                                jnp.zeros((SUBLANES - 3, LANES), F32)], axis=0)
        fill_ref[...] = rows.astype(jnp.int32)

    sel = selT_ref[...]
    self32 = sel.astype(F32)
    rank = jnp.dot(sel, triu_ref[...], preferred_element_type=F32) + carry_scr[:, 0:1]
    carry_scr[...] += jnp.broadcast_to(jnp.sum(self32, axis=1, keepdims=True), carry_scr.shape)
    dest = pstart_scr[:, 0:1] + rank
    slot = jnp.dot(tril_ref[...], sel, preferred_element_type=F32)
    w = wT_ref[...]
    drows, wrows = [], []
    for k in range(TOP_K):
        pick = jnp.where(slot == float(k), self32, 0.0)
        drows.append(jnp.sum(pick * dest, axis=0, keepdims=True))
        wrows.append(jnp.sum(pick * w, axis=0, keepdims=True))
    dest_ref[...] = jnp.concatenate(drows, axis=0).astype(jnp.int32)
    w8_ref[...] = jnp.concatenate(wrows, axis=0)


def _moe_plan(selT, wT, cnt, tile):
    ne, n = selT.shape
    assert n % tile == 0
    _, nbp = _moe_rows(n)
    triu = jnp.asarray(np.triu(np.ones((tile, tile), np.float32), 1), BF16)
    tril = jnp.asarray(np.tril(np.ones((ne, ne), np.float32), -1), BF16)
    eye = jnp.asarray(np.eye(ne, LANES, dtype=np.float32))
    full = lambda a: pl.BlockSpec(a.shape, lambda i: (0,) * a.ndim)
    return pl.pallas_call(
        _plan_body,
        grid=(n // tile,),
        in_specs=[pl.BlockSpec((ne, tile), lambda i: (0, i)), pl.BlockSpec((ne, tile), lambda i: (0, i)),
                  full(cnt), full(triu), full(tril), full(eye)],
        out_specs=[pl.BlockSpec((TOP_K, tile), lambda i: (0, i)), pl.BlockSpec((TOP_K, tile), lambda i: (0, i)),
                   pl.BlockSpec((1, nbp), lambda i: (0, 0)), pl.BlockSpec((SUBLANES, LANES), lambda i: (0, 0))],
        out_shape=[jax.ShapeDtypeStruct((TOP_K, n), jnp.int32), jax.ShapeDtypeStruct((TOP_K, n), F32),
                   jax.ShapeDtypeStruct((1, nbp), jnp.int32), jax.ShapeDtypeStruct((SUBLANES, LANES), jnp.int32)],
        scratch_shapes=[pltpu.VMEM((ne, LANES), F32), pltpu.VMEM((ne, LANES), F32)],
        compiler_params=_cparams(("arbitrary",)),
        name="moe_plan",
    )(selT, wT, cnt, triu, tril, eye)


_FILL_PIECES = tuple(1 << s for s in reversed(range(MOE_BLOCK_SHIFT)))


def _fill_padding(fill_ref, xd_ref, zero_scr, zsem, wait):
    def per_expert(e, carry):
        start = fill_ref[0, e]
        n = fill_ref[1, e]
        head = n & (SUBLANES - 1)
        for r in range(SUBLANES - 1):
            @pl.when(r < head)
            def _():
                cp = pltpu.make_async_copy(zero_scr.at[pl.ds(0, 1)], xd_ref.at[pl.ds(start + r, 1)], zsem)
                cp.wait() if wait else cp.start()

        cur = start + head
        for p in _FILL_PIECES:
            if p < SUBLANES:
                continue
            hit = (n & p) != 0

            @pl.when(hit)
            def _():
                off = pl.multiple_of(cur, SUBLANES)
                cp = pltpu.make_async_copy(zero_scr.at[pl.ds(0, p)], xd_ref.at[pl.ds(off, p)], zsem)
                cp.wait() if wait else cp.start()

            cur = cur + jnp.where(hit, p, 0)
        return carry

    lax.fori_loop(0, N_EXPERTS, per_expert, 0)


def _dispatch_body(dest_ref, fill_ref, hp_ref, xd_ref, zero_scr, sem, zsem):
    step = pl.program_id(0)
    tile = hp_ref.shape[0]

    def row_copy(t, k):
        return pltpu.make_async_copy(hp_ref.at[pl.ds(t, 1)], xd_ref.at[pl.ds(dest_ref[k, t], 1)], sem)

    @pl.when(step == 0)
    def _():
        zero_scr[...] = jnp.zeros(zero_scr.shape, zero_scr.dtype)
        _fill_padding(fill_ref, xd_ref, zero_scr, zsem, False)
        _fill_padding(fill_ref, xd_ref, zero_scr, zsem, True)

    def issue(t, carry):
        for k in range(TOP_K):
            row_copy(t, k).start(priority=k % 2)
        return carry

    def drain(t, carry):
        for k in range(TOP_K):
            row_copy(t, k).wait()
        return carry

    lax.fori_loop(0, tile, issue, 0)
    lax.fori_loop(0, tile, drain, 0)


def _moe_dispatch(hp, dest8, fill, tile):
    n, m = hp.shape
    n_blocks, _ = _moe_rows(n)
    nr = n_blocks * MOE_BLOCK
    return pl.pallas_call(
        _dispatch_body,
        grid=(n // tile,),
        in_specs=[pl.BlockSpec((TOP_K, tile), lambda i: (0, i), memory_space=pltpu.SMEM),
                  pl.BlockSpec(memory_space=pltpu.SMEM),
                  pl.BlockSpec((tile, m), lambda i: (i, 0))],
        out_specs=pl.BlockSpec(memory_space=pl.ANY),
        out_shape=jax.ShapeDtypeStruct((nr, m), jnp.uint32),
        scratch_shapes=[pltpu.VMEM((_FILL_PIECES[0], m), jnp.uint32), pltpu.SemaphoreType.DMA(()),
                        pltpu.SemaphoreType.DMA(())],
        compiler_params=_cparams(("arbitrary",)),
        name="moe_dispatch",
    )(dest8, fill, hp)


def _swiglu_packed(xw, w1, w3, w2):
    xa, xb = _unpack_bf16_pairs(xw)
    half = xa.shape[1]
    mm = lambda w: (jnp.dot(xa, w[0:half, :], preferred_element_type=F32)
                    + jnp.dot(xb, w[half:, :], preferred_element_type=F32))
    hid = _silu(mm(w1)) * mm(w3)
    return jnp.dot(hid.astype(BF16), w2, preferred_element_type=F32)


def _experts_body(be_ref, nu_ref, xd_ref, w1_ref, w3_ref, w2_ref, yd_ref):
    i = pl.program_id(0)

    @pl.when(i < nu_ref[0])
    def _():
        yd_ref[...] = _pack_bf16_pairs(_swiglu_packed(xd_ref[...], w1_ref[0], w3_ref[0], w2_ref[0]))

    @pl.when(i >= nu_ref[0])
    def _():
        yd_ref[...] = jnp.zeros(yd_ref.shape, yd_ref.dtype)


def _moe_experts(xd, block_e, n_used, w1b, w3b, w2b):
    nr, m = xd.shape
    n_blocks = nr // MOE_BLOCK
    d, f = w1b.shape[1:]
    clamp = lambda i, nu: jnp.minimum(i, nu[0] - 1)
    return pl.pallas_call(
        _experts_body,
        grid_spec=pltpu.PrefetchScalarGridSpec(
            num_scalar_prefetch=2,
            grid=(n_blocks,),
            in_specs=[pl.BlockSpec((MOE_BLOCK, m), lambda i, be, nu: (clamp(i, nu), 0)),
                      pl.BlockSpec((1, d, f), lambda i, be, nu: (be[clamp(i, nu)], 0, 0)),
                      pl.BlockSpec((1, d, f), lambda i, be, nu: (be[clamp(i, nu)], 0, 0)),
                      pl.BlockSpec((1, f, d), lambda i, be, nu: (be[clamp(i, nu)], 0, 0))],
            out_specs=pl.BlockSpec((MOE_BLOCK, m), lambda i, be, nu: (i, 0)),
        ),
        out_shape=jax.ShapeDtypeStruct((nr, m), jnp.uint32),
        compiler_params=_cparams(("arbitrary",)),
        name="moe_experts",
    )(block_e, n_used, xd, w1b, w3b, w2b)


def _combine_body(dest_ref, w8_ref, hp_ref, x1_ref, g2_ref, eye_ref, sw1_ref, sw3_ref, sw2_ref, nf_ref, yd_ref,
                  o_ref, ybuf, sem):
    tile = hp_ref.shape[1]

    def row_copy(t, k):
        return pltpu.make_async_copy(yd_ref.at[pl.ds(dest_ref[k, t], 1)], ybuf.at[k, pl.ds(t, 1)], sem)

    def issue(t, carry):
        for k in range(TOP_K):
            row_copy(t, k).start(priority=k % 2)
        return carry

    def drain(t, carry):
        for k in range(TOP_K):
            row_copy(t, k).wait()
        return carry

    lax.fori_loop(0, tile, issue, 0)
    shared = _swiglu_packed(hp_ref[0], sw1_ref[...], sw3_ref[...], sw2_ref[...])
    w_rows = lax.dot_general(eye_ref[...], w8_ref[...], (((1,), (1,)), ((), ())), preferred_element_type=F32,
                             precision=HIGHEST)
    lax.fori_loop(0, tile, drain, 0)
    half = ybuf.shape[2]
    acc_a = jnp.zeros((tile, half), F32)
    acc_b = jnp.zeros((tile, half), F32)
    for k in range(TOP_K):
        ya, yb = _unpack_bf16_pairs(ybuf[k])
        wk = w_rows[:, k:k + 1]
        acc_a = acc_a + wk * ya.astype(F32)
        acc_b = acc_b + wk * yb.astype(F32)
    routed = jnp.concatenate([acc_a, acc_b], axis=1)
    x2 = x1_ref[0] + g2_ref[0] * (routed + shared)
    ms = jnp.mean(x2 * x2, axis=-1, keepdims=True)
    o_ref[0] = x2 * lax.rsqrt(ms + RMS_EPS) * nf_ref[...]


def _moe_combine(dest8, w8, hp, x1, g2, yd, sw1b, sw3b, sw2b, norm_f, tile):
    nb, t, d = x1.shape
    nt = t // tile
    mt = g2.shape[1]
    assert t % tile == 0 and (mt == 1 or mt == t)
    if mt == 1:
        mod_spec = pl.BlockSpec((1, 1, d), lambda b, i: (b, 0, 0))
    else:
        mod_spec = pl.BlockSpec((1, tile, d), lambda b, i: (b, i, 0))
    eye = jnp.eye(tile, dtype=F32)
    nf = norm_f.reshape(1, d)
    full = lambda a: pl.BlockSpec(a.shape, lambda b, i: (0,) * a.ndim)
    return pl.pallas_call(
        _combine_body,
        grid=(nb, nt),
        in_specs=[pl.BlockSpec((TOP_K, tile), lambda b, i: (0, b * nt + i), memory_space=pltpu.SMEM),
                  pl.BlockSpec((TOP_K, tile), lambda b, i: (0, b * nt + i)),
                  pl.BlockSpec((1, tile, d // 2), lambda b, i: (b, i, 0)),
                  pl.BlockSpec((1, tile, d), lambda b, i: (b, i, 0)),
                  mod_spec, full(eye), full(sw1b), full(sw3b), full(sw2b), full(nf),
                  pl.BlockSpec(memory_space=pl.ANY)],
        out_specs=pl.BlockSpec((1, tile, d), lambda b, i: (b, i, 0)),
        out_shape=jax.ShapeDtypeStruct((nb, t, d), F32),
        scratch_shapes=[pltpu.VMEM((TOP_K, tile, d // 2), jnp.uint32), pltpu.SemaphoreType.DMA(())],
        compiler_params=_cparams(("arbitrary", "arbitrary")),
        name="moe_combine",
    )(dest8, w8, hp, x1, g2, eye, sw1b, sw3b, sw2b, nf, yd)


SC_CHUNK = 128


def _sc_workers():
    info = plsc.get_sparse_core_info()
    return info.num_cores, info.num_subcores


def _sc_dispatch(hp, dest8, nr):
    n, m = hp.shape
    nc, nsub = _sc_workers()
    per_w = n // (nc * nsub)
    assert n % (nc * nsub * SC_CHUNK) == 0
    mesh = plsc.VectorSubcoreMesh(core_axis_name="c", subcore_axis_name="s")

    @functools.partial(
        pl.kernel, mesh=mesh, out_type=jax.ShapeDtypeStruct((nr, m), hp.dtype),
        scratch_types=[pltpu.VMEM((TOP_K, SC_CHUNK), jnp.int32), pltpu.VMEM((SC_CHUNK, m), hp.dtype),
                       pltpu.SemaphoreType.DMA])
    def scatter_rows(hp_hbm, dest_hbm, xd_hbm, idx_v, rows_v, sem):
        wid = lax.axis_index("s") * nc + lax.axis_index("c")

        @pl.loop(0, per_w // SC_CHUNK)
        def _(c):
            base = pl.multiple_of(wid * per_w + c * SC_CHUNK, SC_CHUNK)
            pltpu.sync_copy(hp_hbm.at[pl.ds(base, SC_CHUNK)], rows_v)
            pltpu.sync_copy(dest_hbm.at[:, pl.ds(base, SC_CHUNK)], idx_v)
            copies = [pltpu.async_copy(rows_v, xd_hbm.at[idx_v.at[k]], sem) for k in range(TOP_K)]
            for cp in copies:
                cp.wait()

    return scatter_rows(hp, dest8)


def _sc_gather(yd, dest8):
    _, m = yd.shape
    n = dest8.shape[1]
    nc, nsub = _sc_workers()
    per_w = n // (nc * nsub)
    assert n % (nc * nsub * SC_CHUNK) == 0
    mesh = plsc.VectorSubcoreMesh(core_axis_name="c", subcore_axis_name="s")

    @functools.partial(
        pl.kernel, mesh=mesh, out_type=jax.ShapeDtypeStruct((TOP_K, n, m), yd.dtype),
        scratch_types=[pltpu.VMEM((TOP_K, SC_CHUNK), jnp.int32), pltpu.VMEM((SC_CHUNK, m), yd.dtype),
                       pltpu.SemaphoreType.DMA])
    def gather_rows(yd_hbm, dest_hbm, out_hbm, idx_v, rows_v, sem):
        wid = lax.axis_index("s") * nc + lax.axis_index("c")

        @pl.loop(0, per_w // SC_CHUNK)
        def _(c):
            base = pl.multiple_of(wid * per_w + c * SC_CHUNK, SC_CHUNK)
            pltpu.sync_copy(dest_hbm.at[:, pl.ds(base, SC_CHUNK)], idx_v)
            for k in range(TOP_K):
                pltpu.async_copy(yd_hbm.at[idx_v.at[k]], rows_v, sem).wait()
                pltpu.sync_copy(rows_v, out_hbm.at[k, pl.ds(base, SC_CHUNK)])

    return gather_rows(yd, dest8)


def _fill_body(fill_ref, xd_in_ref, xd_ref, zero_scr, zsem):
    del xd_in_ref
    zero_scr[...] = jnp.zeros(zero_scr.shape, zero_scr.dtype)
    for wait in (False, True):
        _fill_padding(fill_ref, xd_ref, zero_scr, zsem, wait)


def _moe_fill(xd, fill):
    return pl.pallas_call(
        _fill_body,
        in_specs=[pl.BlockSpec(memory_space=pltpu.SMEM), pl.BlockSpec(memory_space=pl.ANY)],
        out_specs=pl.BlockSpec(memory_space=pl.ANY),
        out_shape=jax.ShapeDtypeStruct(xd.shape, xd.dtype),
        scratch_shapes=[pltpu.VMEM((_FILL_PIECES[0], xd.shape[1]), xd.dtype), pltpu.SemaphoreType.DMA(())],
        input_output_aliases={1: 0},
        name="moe_fill",
    )(fill, xd)


def _combine_dense_body(w8_ref, hp_ref, x1_ref, g2_ref, eye_ref, sw1_ref, sw3_ref, sw2_ref, nf_ref, ybuf_ref, o_ref):
    tile = hp_ref.shape[1]
    shared = _swiglu_packed(hp_ref[0], sw1_ref[...], sw3_ref[...], sw2_ref[...])
    w_rows = lax.dot_general(eye_ref[...], w8_ref[...], (((1,), (1,)), ((), ())), preferred_element_type=F32,
                             precision=HIGHEST)
    half = ybuf_ref.shape[2]
    acc_a = jnp.zeros((tile, half), F32)
    acc_b = jnp.zeros((tile, half), F32)
    for k in range(TOP_K):
        ya, yb = _unpack_bf16_pairs(ybuf_ref[k])
        wk = w_rows[:, k:k + 1]
        acc_a = acc_a + wk * ya.astype(F32)
        acc_b = acc_b + wk * yb.astype(F32)
    routed = jnp.concatenate([acc_a, acc_b], axis=1)
    x2 = x1_ref[0] + g2_ref[0] * (routed + shared)
    ms = jnp.mean(x2 * x2, axis=-1, keepdims=True)
    o_ref[0] = x2 * lax.rsqrt(ms + RMS_EPS) * nf_ref[...]


def _moe_combine_dense(w8, hp, x1, g2, ybuf, sw1b, sw3b, sw2b, norm_f, tile):
    nb, t, d = x1.shape
    nt = t // tile
    assert t % tile == 0 and g2.shape[1] == 1
    eye = jnp.eye(tile, dtype=F32)
    nf = norm_f.reshape(1, d)
    full = lambda a: pl.BlockSpec(a.shape, lambda b, i: (0,) * a.ndim)
    return pl.pallas_call(
        _combine_dense_body,
        grid=(nb, nt),
        in_specs=[pl.BlockSpec((TOP_K, tile), lambda b, i: (0, b * nt + i)),
                  pl.BlockSpec((1, tile, d // 2), lambda b, i: (b, i, 0)),
                  pl.BlockSpec((1, tile, d), lambda b, i: (b, i, 0)),
                  pl.BlockSpec((1, 1, d), lambda b, i: (b, 0, 0)),
                  full(eye), full(sw1b), full(sw3b), full(sw2b), full(nf),
                  pl.BlockSpec((TOP_K, tile, d // 2), lambda b, i: (0, b * nt + i, 0))],
        out_specs=pl.BlockSpec((1, tile, d), lambda b, i: (b, i, 0)),
        out_shape=jax.ShapeDtypeStruct((nb, t, d), F32),
        compiler_params=_cparams(("arbitrary", "arbitrary")),
        name="moe_combine_dense",
    )(w8, hp, x1, g2, eye, sw1b, sw3b, sw2b, nf, ybuf)


def _moe(x1, hp, selT, wT, cnt, g2, EW, norm_f, tile, on_sparsecore):
    nb, t, d = x1.shape
    n = nb * t
    dest8, w8, block_e, fill = _moe_plan(selT, wT, cnt, tile)
    n_blocks, _ = _moe_rows(n)
    hp2 = hp.reshape(n, d // 2)
    if on_sparsecore:
        xd = _moe_fill(_sc_dispatch(hp2, dest8, n_blocks * MOE_BLOCK), fill)
    else:
        xd = _moe_dispatch(hp2, dest8, fill, tile)
    yd = _moe_experts(xd, block_e[0, :n_blocks], fill[2, 0:1], EW["w1"], EW["w3"], EW["w2"])
    if on_sparsecore:
        return _moe_combine_dense(w8, hp, x1, g2, _sc_gather(yd, dest8), EW["sw1"], EW["sw3"], EW["sw2"], norm_f, tile)
    return _moe_combine(dest8, w8, hp, x1, g2, yd, EW["sw1"], EW["sw3"], EW["sw2"], norm_f, tile)


def kernel(x_prompt, x_sample, c_prompt, c_sample, cache_kv_cmp, cache_kv_sel, cache_kv_win, state_conv, state_ssm, page_table, w_ada, b_ada, norm_mix, norm_ffn, w_in, cmp_pe_k, cmp_w1_k, cmp_w2_k, cmp_pe_v, cmp_w1_v, cmp_w2_v, conv_w, conv_b, dt_bias, a_log, d_skip, ssd_norm, w_out, router_w, router_bias, exp_w1, exp_w3, exp_w2, sh_w1, sh_w3, sh_w2, norm_f):
    nb, t, d = x_prompt.shape
    ndb = x_sample.shape[0]
    c_all = jnp.concatenate([c_prompt, c_sample], axis=0)
    mod = _modulation(c_all, w_ada[0], b_ada[0]).reshape(nb + ndb, 6, d)
    mod_p = [mod[:nb, k][:, None, :] for k in range(6)]
    mod_s = [mod[nb:, k][None, :, :] for k in range(6)]
    W = _prep_w_in(w_in[0])
    P = _in_proj(x_prompt, mod_p[1], mod_p[0], norm_mix[0], W, 512)
    S = _in_proj(x_sample.reshape(1, ndb, d), mod_s[1], mod_s[0], norm_mix[0], W, ndb)
    C = _prep_compress(cmp_pe_k[0], cmp_w1_k[0], cmp_w2_k[0], cmp_pe_v[0], cmp_w1_v[0], cmp_w2_v[0])
    kcvc_p = _compress_prompt(P["kcT"], C)
    ocmp_p, sel_p = _cmp_attn_prompt(P["q"], kcvc_p, 256)
    o_nsa_p = _sel_win_prompt(P["q"], sel_p, P["g"], ocmp_p, P["ksT"], P["kwT"], 256, 512)
    SP = _prep_ssd(conv_w[0], conv_b[0], dt_bias[0], a_log[0], d_skip[0], ssd_norm[0])
    y_ssd_p, ssm_p = _ssd_prompt(P["xbc"], P["z"], P["dt"], P["dtT"], SP)
    w_out_b = w_out[0].astype(BF16)
    router_wT = router_w[0].T
    EW = dict(w1=exp_w1[0].astype(BF16), w3=exp_w3[0].astype(BF16), w2=exp_w2[0].astype(BF16),
              sw1=sh_w1[0].astype(BF16), sw3=sh_w3[0].astype(BF16), sw2=sh_w2[0].astype(BF16))
    x1_p, hp_p, selT_p, wT_p, cnt_p = _post_mix(x_prompt, o_nsa_p, y_ssd_p, mod_p[2], mod_p[4], mod_p[3], norm_ffn[0],
                                                 w_out_b, router_wT, router_bias[0], 512)
    big = (nb * t) % (32 * SC_CHUNK) == 0
    y_prompt = _moe(x1_p, hp_p, selT_p, wT_p, cnt_p, mod_p[5], EW, norm_f, 512, big)

    n_pool = cache_kv_cmp.shape[1]
    past = page_table.shape[1] * PAGE_SIZE
    to_pages = lambda c: jnp.transpose(c, (0, 2, 3, 4, 1)).reshape(c.shape[0], 256, c.shape[1])
    pool_cmp, pool_sel, win = to_pages(cache_kv_cmp[0]), to_pages(cache_kv_sel[0]), to_pages(cache_kv_win[0])
    kcvc_s = _compress_paged(pool_cmp, page_table, C)
    q3 = S["q"].reshape(ndb, 1, NSA_DIM)
    ocmp_s, sel_s = _cmp_attn_sample(q3, kcvc_s, past)
    o_nsa_s, win_s = _sel_win_sample(q3, S["g"].reshape(ndb, 1, KV_HEADS * LANES), ocmp_s, sel_s, pool_sel, page_table,
                                     win, S["ksT"], S["kwT"], past)
    y_ssd_s, conv_s, ssm_s = _ssd_sample(jnp.transpose(state_conv[0], (1, 0, 2)), S["xbc"][0], S["z"][0], S["dt"][0],
                                         state_ssm[0], SP)
    x1_s, hp_s, selT_s, wT_s, cnt_s = _post_mix(x_sample.reshape(1, ndb, d), o_nsa_s.reshape(1, ndb, NSA_DIM),
                                                 y_ssd_s.reshape(1, ndb, D_INNER), mod_s[2], mod_s[4], mod_s[3],
                                                 norm_ffn[0], w_out_b, router_wT, router_bias[0], ndb)
    y_sample = _moe(x1_s, hp_s, selT_s, wT_s, cnt_s, mod_s[5], EW, norm_f, ndb, False).reshape(ndb, 1, d)

    from_cm = lambda a: jnp.transpose(a.reshape(a.shape[0], 2, KV_HEADS, HEAD_DIM, a.shape[2]), (0, 4, 1, 2, 3))[None]
    tw = min(WINDOW, t)
    return (y_prompt, y_sample,
            from_cm(P["kcT"]), from_cm(P["ksT"]), from_cm(P["kwT"][:, :, t - tw:]),
            P["xbc"][:, t - (CONV_W - 1):, :][None], ssm_p[None],
            from_cm(jnp.transpose(S["kcT"], (2, 1, 0))), from_cm(jnp.transpose(S["ksT"], (2, 1, 0))), from_cm(win_s),
            jnp.transpose(conv_s, (1, 0, 2))[None], ssm_s[None])
```

```python
import functools
import math

import jax
import jax.numpy as jnp
import numpy as np
from jax import lax
from jax.experimental import pallas as pl
from jax.experimental.pallas import tpu as pltpu
from jax.experimental.pallas import tpu_sc as plsc

F32 = jnp.float32
BF16 = jnp.bfloat16
HIGHEST = lax.Precision.HIGHEST

D_MODEL = 1024
NSA_HEADS = 8
KV_HEADS = 2
HEAD_DIM = 64
GQA = NSA_HEADS // KV_HEADS
CMP_BLK = 32
CMP_STRIDE = 16
CMP_HID = 64
SEL_BLK = 64
N_SEL = 16
WINDOW = 512
FORCE_CUR = 2.0e4
FORCE_SINK = 1.0e4
SSD_HEADS = 8
SSD_HEAD_DIM = 64
D_INNER = SSD_HEADS * SSD_HEAD_DIM
SSD_GROUPS = 2
D_STATE = 128
CONV_W = 4
CONV_DIM = D_INNER + 2 * SSD_GROUPS * D_STATE
SSD_CHUNK = 128
NSA_DIM = NSA_HEADS * HEAD_DIM
KV_DIM = KV_HEADS * HEAD_DIM
N_EXPERTS = 64
TOP_K = 8
N_EXPERT_GROUPS = 8
TOPK_GROUPS = 4
D_EXPERT = 256
D_SHARED = 256
ROUTED_SCALE = 2.5
RMS_EPS = 1e-6
NEG_INF = -1e30
LOG2E = 1.4426950408889634
PAGE_SIZE = 128

LANES = 128
SUBLANES = 8
VMEM_LIMIT = 56 * 1024 * 1024


def _cparams(sem):
    return pltpu.CompilerParams(dimension_semantics=sem, vmem_limit_bytes=VMEM_LIMIT)


def _silu(v):
    return v * jax.nn.sigmoid(v)


def _mod_body(c_ref, w_ref, b_ref, o_ref):
    s = _silu(c_ref[...])
    o_ref[...] = jnp.dot(s, w_ref[...], preferred_element_type=F32, precision=HIGHEST) + b_ref[...]


def _modulation(c_all, w_ada, b_ada):
    n, d = c_all.shape
    nout = w_ada.shape[1]
    tn = 512
    return pl.pallas_call(
        _mod_body,
        grid=(nout // tn,),
        in_specs=[pl.BlockSpec((n, d), lambda j: (0, 0)),
                  pl.BlockSpec((d, tn), lambda j: (0, j)),
                  pl.BlockSpec((1, tn), lambda j: (0, j))],
        out_specs=pl.BlockSpec((n, tn), lambda j: (0, j)),
        out_shape=jax.ShapeDtypeStruct((n, nout), F32),
        compiler_params=_cparams(("arbitrary",)),
        name="modulation",
    )(c_all, w_ada, b_ada.reshape(1, nout))


def _nt_dot(a, b):
    return lax.dot_general(a, b, (((1,), (1,)), ((), ())), preferred_element_type=F32)


def _proj_body(x_ref, sc_ref, sh_ref, nw_ref, wq_ref, wkvT_ref, wg_ref, wz_ref, wx_ref, wdt_ref, wdtT_ref,
               q_ref, kcT_ref, ksT_ref, kwT_ref, g_ref, z_ref, xbc_ref, dt_ref, dtT_ref):
    x = x_ref[0]
    ms = jnp.mean(x * x, axis=-1, keepdims=True)
    h = x * lax.rsqrt(ms + RMS_EPS) * nw_ref[...]
    h = h * (1.0 + sc_ref[0]) + sh_ref[0]
    hb = h.astype(BF16)
    q_ref[0] = (jnp.dot(hb, wq_ref[...], preferred_element_type=F32) * (HEAD_DIM ** -0.5 * LOG2E)).astype(BF16)
    kvT = _nt_dot(wkvT_ref[...], hb)
    kcT_ref[0] = kvT[0:256]
    ksT_ref[0] = kvT[256:512]
    kwT_ref[0] = kvT[512:768]
    g_ref[0] = jax.nn.sigmoid(jnp.dot(hb, wg_ref[...], preferred_element_type=F32))
    z_ref[0] = jnp.dot(hb, wz_ref[...], preferred_element_type=F32)
    xbc_ref[0] = jnp.dot(hb, wx_ref[...], preferred_element_type=F32)
    dt_ref[0] = jnp.dot(hb, wdt_ref[...], preferred_element_type=F32)
    dtT_ref[0] = _nt_dot(wdtT_ref[...], hb)


def _prep_w_in(w_in):
    w = w_in
    o = 0
    wq = w[:, o:o + NSA_DIM]; o += NSA_DIM
    wkv = w[:, o:o + 6 * KV_DIM]; o += 6 * KV_DIM
    wg = w[:, o:o + 3 * NSA_HEADS]; o += 3 * NSA_HEADS
    wz = w[:, o:o + D_INNER]; o += D_INNER
    wx = w[:, o:o + CONV_DIM]; o += CONV_DIM
    wdt = w[:, o:o + SSD_HEADS]; o += SSD_HEADS
    pad = lambda a: jnp.pad(a, ((0, 0), (0, LANES - a.shape[1])))
    per = 3 * GQA
    wg = jnp.concatenate([pad(wg[:, k * per:(k + 1) * per]) for k in range(KV_HEADS)], axis=1)
    return dict(wq=wq.astype(BF16), wkvT=wkv.T.astype(BF16), wg=wg.astype(BF16), wz=wz.astype(BF16),
                wx=wx.astype(BF16), wdt=pad(wdt).astype(BF16), wdtT=wdt.T.astype(BF16))


def _in_proj(x, sc, sh, norm_w, W, tm):
    nb, t, d = x.shape
    mt = sc.shape[1]
    assert t % tm == 0 and (mt == 1 or mt == t)
    if mt == 1:
        mod_spec = pl.BlockSpec((1, 1, d), lambda b, i: (b, 0, 0))
    else:
        mod_spec = pl.BlockSpec((1, tm, d), lambda b, i: (b, i, 0))
    full = lambda a: pl.BlockSpec(a.shape, lambda b, i: (0,) * a.ndim)
    row = lambda n: pl.BlockSpec((1, tm, n), lambda b, i: (b, i, 0))
    col = lambda n: pl.BlockSpec((1, n, tm), lambda b, i: (b, 0, i))
    ws = [W["wq"], W["wkvT"], W["wg"], W["wz"], W["wx"], W["wdt"], W["wdtT"]]
    outs = pl.pallas_call(
        _proj_body,
        grid=(nb, t // tm),
        in_specs=[row(d), mod_spec, mod_spec, pl.BlockSpec((1, d), lambda b, i: (0, 0))] + [full(a) for a in ws],
        out_specs=[row(NSA_DIM), col(256), col(256), col(256), row(KV_HEADS * LANES), row(D_INNER), row(CONV_DIM),
                   row(LANES), col(SSD_HEADS)],
        out_shape=[jax.ShapeDtypeStruct((nb, t, NSA_DIM), BF16),
                   jax.ShapeDtypeStruct((nb, 256, t), F32),
                   jax.ShapeDtypeStruct((nb, 256, t), F32),
                   jax.ShapeDtypeStruct((nb, 256, t), F32),
                   jax.ShapeDtypeStruct((nb, t, KV_HEADS * LANES), F32),
                   jax.ShapeDtypeStruct((nb, t, D_INNER), F32),
                   jax.ShapeDtypeStruct((nb, t, CONV_DIM), F32),
                   jax.ShapeDtypeStruct((nb, t, LANES), F32),
                   jax.ShapeDtypeStruct((nb, SSD_HEADS, t), F32)],
        compiler_params=_cparams(("arbitrary", "arbitrary")),
        name="in_proj",
    )(x, sc, sh, norm_w.reshape(1, d), *ws)
    names = ("q", "kcT", "ksT", "kwT", "g", "z", "xbc", "dt", "dtT")
    return dict(zip(names, outs))


def _prep_compress(cmp_pe_k, cmp_w1_k, cmp_w2_k, cmp_pe_v, cmp_w1_v, cmp_w2_v):
    w1s = jnp.stack([cmp_w1_k, cmp_w1_v]).reshape(2, 2, CMP_STRIDE, HEAD_DIM, CMP_HID)
    eye = jnp.eye(2, dtype=F32)
    wbd = jnp.einsum("ktldh,kK,vV->lkvdtKVh", w1s, eye, eye).reshape(CMP_STRIDE, 256, 512)
    w2s = jnp.stack([cmp_w2_k, cmp_w2_v])
    w2bd = jnp.einsum("khd,kK,vV->kvhKVd", w2s, eye, eye).reshape(256, 256)
    pes = jnp.stack([cmp_pe_k, cmp_pe_v]).reshape(2, 2, CMP_STRIDE, HEAD_DIM)
    pe_rows = jnp.broadcast_to(jnp.transpose(pes, (1, 2, 0, 3))[:, :, :, None, :],
                               (2, CMP_STRIDE, 2, KV_HEADS, HEAD_DIM)).reshape(2, CMP_STRIDE * 256)
    pecat = jnp.pad(pe_rows, ((0, SUBLANES - 2), (0, 0)))
    perm = np.zeros((LANES, LANES), np.float32)
    for l in range(CMP_STRIDE):
        for n in range(LANES // CMP_STRIDE):
            perm[(LANES // CMP_STRIDE) * l + n, CMP_STRIDE * n + l] = 1.0
    bias = pl.pallas_call(
        _cmp_bias_body,
        out_shape=jax.ShapeDtypeStruct((SUBLANES, 512), F32),
        name="cmp_bias",
    )(pecat, wbd.reshape(CMP_STRIDE * 256, 512))
    return dict(wbd=wbd.astype(BF16), w2bd=w2bd.astype(BF16), bias=bias, perm=jnp.asarray(perm, BF16))


def _cmp_bias_body(pe_ref, w_ref, o_ref):
    o_ref[...] = jnp.dot(pe_ref[...], w_ref[...], preferred_element_type=F32, precision=HIGHEST)


def _compress_body(n_pref, n_slab_refs, slabs_per_ref, *refs):
    refs = refs[n_pref:]
    slab_refs = refs[:n_slab_refs]
    perm_ref, wbd_ref, bias_ref, w2bd_ref, o_ref, z_scr = refs[n_slab_refs:]
    j = pl.program_id(1)
    nh = z_scr.shape[1]
    per_slab = LANES // CMP_STRIDE
    g_tot = n_slab_refs * slabs_per_ref
    base = pl.multiple_of(j * (per_slab * g_tot), per_slab)
    perm = perm_ref[...]
    for ri in range(n_slab_refs):
        for si in range(slabs_per_ref):
            slab = slab_refs[ri][0][:, si * LANES:(si + 1) * LANES].astype(BF16)
            xp = _nt_dot(perm, slab)
            g = ri * slabs_per_ref + si
            for l in range(CMP_STRIDE):
                z_scr[l, pl.ds(base + per_slab * g, per_slab), :] = xp[per_slab * l:per_slab * (l + 1), :]

    @pl.when(j == pl.num_programs(1) - 1)
    def _():
        acc = jnp.zeros((nh, 512), F32)
        for l in range(CMP_STRIDE):
            acc = acc + jnp.dot(z_scr[l].astype(BF16), wbd_ref[l], preferred_element_type=F32)
        lead = acc[:, :256] + bias_ref[0:1, :256]
        tail = acc[:, 256:] + bias_ref[1:2, 256:]
        hid = _silu(lead + pltpu.roll(tail, nh - 1, 0))
        out = jnp.dot(hid.astype(BF16), w2bd_ref[...], preferred_element_type=F32)
        row = lax.broadcasted_iota(jnp.int32, out.shape, 0)
        o_ref[0] = jnp.where(row < nh - 1, out, 0.0)


def _compress_prompt(kcT, C):
    nb, _, t = kcT.shape
    nh = t // CMP_STRIDE
    g = min(8, t // LANES)
    nsteps = t // (LANES * g)
    full = lambda a: pl.BlockSpec(a.shape, lambda b, j: (0,) * a.ndim)
    return pl.pallas_call(
        functools.partial(_compress_body, 0, 1, g),
        grid=(nb, nsteps),
        in_specs=[pl.BlockSpec((1, 256, LANES * g), lambda b, j: (b, 0, j)),
                  full(C["perm"]), full(C["wbd"]), full(C["bias"]), full(C["w2bd"])],
        out_specs=pl.BlockSpec((1, nh, 256), lambda b, j: (b, 0, 0)),
        out_shape=jax.ShapeDtypeStruct((nb, nh, 256), F32),
        scratch_shapes=[pltpu.VMEM((CMP_STRIDE, nh, 256), F32)],
        compiler_params=_cparams(("arbitrary", "arbitrary")),
        name="compress_prompt",
    )(kcT, C["perm"], C["wbd"], C["bias"], C["w2bd"])


def _compress_paged(pool, page_table, C):
    nb, n_pages = page_table.shape
    nh = n_pages * (PAGE_SIZE // CMP_STRIDE)
    g = min(16, n_pages)
    nsteps = n_pages // g
    full = lambda a: pl.BlockSpec(a.shape, lambda b, j, pt: (0,) * a.ndim)
    page_spec = lambda k: pl.BlockSpec((1, 256, LANES), lambda b, j, pt: (pt[b, j * g + k], 0, 0))
    return pl.pallas_call(
        functools.partial(_compress_body, 1, g, 1),
        grid_spec=pltpu.PrefetchScalarGridSpec(
            num_scalar_prefetch=1,
            grid=(nb, nsteps),
            in_specs=[page_spec(k) for k in range(g)] + [full(C["perm"]), full(C["wbd"]), full(C["bias"]), full(C["w2bd"])],
            out_specs=pl.BlockSpec((1, nh, 256), lambda b, j, pt: (b, 0, 0)),
            scratch_shapes=[pltpu.VMEM((CMP_STRIDE, nh, 256), F32)],
        ),
        out_shape=jax.ShapeDtypeStruct((nb, nh, 256), F32),
        compiler_params=_cparams(("arbitrary", "arbitrary")),
        name="compress_paged",
    )(page_table, *([pool] * g), C["perm"], C["wbd"], C["bias"], C["w2bd"])


def _alibi_slope(head):
    return float(2.0 ** (-8.0 * (head + 1) / NSA_HEADS)) * LOG2E


def _overlap_T(nc, ns):
    cst = np.arange(nc)[None, :] * CMP_STRIDE
    sst = np.arange(ns)[:, None] * SEL_BLK
    ov = np.clip(np.minimum(cst + CMP_BLK, sst + SEL_BLK) - np.maximum(cst, sst), 0, None).astype(np.float32) / CMP_BLK
    return jnp.asarray(ov, F32)


def _masked_softmax(s, mask):
    s = jnp.where(mask, s, NEG_INF)
    p = jnp.exp2(s - jnp.max(s, axis=-1, keepdims=True)) * mask.astype(F32)
    return p / jnp.maximum(jnp.sum(p, axis=-1, keepdims=True), 1e-30)


def _topk_mask(grp, k):
    ngrp = len(grp)
    cnt = [jnp.zeros(grp[0].shape, F32) for _ in range(ngrp)]
    sub = lax.broadcasted_iota(jnp.int32, grp[0].shape, 0)
    one, zero = jnp.float32(1.0), jnp.float32(0.0)
    for j in range(ngrp * SUBLANES):
        a, r = divmod(j, SUBLANES)
        row = grp[a][r:r + 1, :]
        for c in range(ngrp):
            if c < a:
                beats = jnp.where(row > grp[c], one, zero)
            elif c > a:
                beats = jnp.where(row >= grp[c], one, zero)
            else:
                beats = jnp.where(sub > r, jnp.where(row >= grp[c], one, zero), jnp.where(row > grp[c], one, zero))
            cnt[c] = cnt[c] + beats
    return [jnp.where(c < float(k), one, zero) for c in cnt]


def _select_blocks(imp, cur):
    ns = imp.shape[0]
    jrow = lax.broadcasted_iota(jnp.int32, imp.shape, 0)
    imp = jnp.where(jrow == cur, FORCE_CUR, jnp.where(jrow == 0, FORCE_SINK, imp))
    imp = jnp.where(jrow <= cur, imp, NEG_INF)
    assert ns % SUBLANES == 0
    grp = [imp[SUBLANES * a:SUBLANES * (a + 1)] for a in range(ns // SUBLANES)]
    return jnp.concatenate(_topk_mask(grp, N_SEL), axis=0)


def _cmp_attn_body(q_ref, kcvc_ref, ovT_ref, eye_ref, ocmp_ref, sel_ref):
    i = pl.program_id(1)
    tq = q_ref.shape[1]
    nh = kcvc_ref.shape[1]
    t0 = i * tq
    qpos_col = t0 + lax.broadcasted_iota(jnp.int32, (tq, 1), 0)
    cend = lax.broadcasted_iota(jnp.int32, (1, nh), 1) * CMP_STRIDE + (CMP_BLK - 1)
    mask = cend <= qpos_col
    dist = (qpos_col - cend).astype(F32)
    qpos_row = t0 + lax.broadcasted_iota(jnp.int32, (1, tq), 1)
    cur = qpos_row // SEL_BLK
    kcvc = kcvc_ref[0]
    for kvh in range(KV_HEADS):
        kc = kcvc[:, kvh * HEAD_DIM:(kvh + 1) * HEAD_DIM].astype(BF16)
        vc = kcvc[:, KV_DIM + kvh * HEAD_DIM:KV_DIM + (kvh + 1) * HEAD_DIM].astype(BF16)
        psum = jnp.zeros((tq, nh), F32)
        for g in range(GQA):
            head = kvh * GQA + g
            qg = q_ref[0, :, head * HEAD_DIM:(head + 1) * HEAD_DIM]
            s = _nt_dot(qg, kc) - _alibi_slope(head) * dist
            p = _masked_softmax(s, mask)
            ocmp_ref[0, :, head * HEAD_DIM:(head + 1) * HEAD_DIM] = jnp.dot(p.astype(BF16), vc, preferred_element_type=F32)
            psum = psum + p
        impT = lax.dot_general(ovT_ref[...], psum, (((1,), (1,)), ((), ())), preferred_element_type=F32,
                               precision=HIGHEST)
        selT = _select_blocks(impT, cur)
        sel = _nt_dot(eye_ref[...], selT.astype(BF16))
        sel_ref[0, kvh] = sel.astype(BF16)


def _cmp_attn_prompt(q, kcvc, tq):
    nb, t, _ = q.shape
    nh = kcvc.shape[1]
    ns = t // SEL_BLK
    ovT = _overlap_T(nh, ns)
    eye = jnp.eye(tq, dtype=BF16)
    return pl.pallas_call(
        _cmp_attn_body,
        grid=(nb, t // tq),
        in_specs=[pl.BlockSpec((1, tq, NSA_DIM), lambda b, i: (b, i, 0)),
                  pl.BlockSpec((1, nh, 256), lambda b, i: (b, 0, 0)),
                  pl.BlockSpec(ovT.shape, lambda b, i: (0, 0)),
                  pl.BlockSpec(eye.shape, lambda b, i: (0, 0))],
        out_specs=[pl.BlockSpec((1, tq, NSA_DIM), lambda b, i: (b, i, 0)),
                   pl.BlockSpec((1, KV_HEADS, tq, ns), lambda b, i: (b, 0, i, 0))],
        out_shape=[jax.ShapeDtypeStruct((nb, t, NSA_DIM), F32),
                   jax.ShapeDtypeStruct((nb, KV_HEADS, t, ns), BF16)],
        compiler_params=_cparams(("arbitrary", "arbitrary")),
        name="cmp_attn",
    )(q, kcvc, ovT, eye)


def _head_rows(qrow, kvh):
    rows = [qrow[:, (kvh * GQA + g) * HEAD_DIM:(kvh * GQA + g + 1) * HEAD_DIM] for g in range(GQA)]
    return jnp.concatenate(rows + [jnp.zeros((SUBLANES - GQA, HEAD_DIM), qrow.dtype)], axis=0)


def _slope_col(kvh):
    r = lax.broadcasted_iota(jnp.int32, (SUBLANES, 1), 0)
    col = jnp.zeros((SUBLANES, 1), F32)
    for g in range(GQA):
        col = jnp.where(r == g, _alibi_slope(kvh * GQA + g), col)
    return col


def _cmp_attn_sample_body(past, q_ref, kcvc_ref, ov_ref, ocmp_ref, imp_ref):
    nh = kcvc_ref.shape[1]
    nsl = ov_ref.shape[1]
    cend = lax.broadcasted_iota(jnp.int32, (1, nh), 1) * CMP_STRIDE + (CMP_BLK - 1)
    mask = cend <= past
    dist = (past - cend).astype(F32)
    kcvc = kcvc_ref[0]
    qrow = q_ref[0]
    for kvh in range(KV_HEADS):
        kc = kcvc[:, kvh * HEAD_DIM:(kvh + 1) * HEAD_DIM].astype(BF16)
        vc = kcvc[:, KV_DIM + kvh * HEAD_DIM:KV_DIM + (kvh + 1) * HEAD_DIM].astype(BF16)
        s = _nt_dot(_head_rows(qrow, kvh), kc) - _slope_col(kvh) * dist
        p = _masked_softmax(s, mask)
        o = jnp.dot(p.astype(BF16), vc, preferred_element_type=F32)
        for g in range(GQA):
            head = kvh * GQA + g
            ocmp_ref[0, :, head * HEAD_DIM:(head + 1) * HEAD_DIM] = o[g:g + 1, :]
        psum = jnp.broadcast_to(jnp.sum(p[0:GQA], axis=0, keepdims=True), (SUBLANES, nh))
        imp = jnp.dot(psum, ov_ref[...], preferred_element_type=F32, precision=HIGHEST)
        imp_ref[0, :, kvh * nsl:(kvh + 1) * nsl] = imp[0:1, :]


def _select_sample_body(past, ns_pad, imp_ref, sel_ref):
    nb = imp_ref.shape[0]
    nsl = imp_ref.shape[1] // KV_HEADS
    cur = jnp.full((1, nb), past // SEL_BLK, jnp.int32)
    for kvh in range(KV_HEADS):
        impT = jnp.transpose(imp_ref[:, kvh * nsl:(kvh + 1) * nsl])
        selT = _select_blocks(impT[0:ns_pad], cur)
        selT = jnp.concatenate([selT, jnp.zeros((nsl - ns_pad, nb), F32)], axis=0)
        sel_ref[:, kvh * nsl:(kvh + 1) * nsl] = jnp.transpose(selT).astype(jnp.int32)


def _cmp_attn_sample(q3, kcvc, past):
    nb = q3.shape[0]
    nh = kcvc.shape[1]
    ns = past // SEL_BLK + 1
    ns_pad = -(-ns // SUBLANES) * SUBLANES
    nsl = -(-ns // LANES) * LANES
    ov = jnp.pad(_overlap_T(nh, ns), ((0, nsl - ns), (0, 0))).T
    ocmp, imp = pl.pallas_call(
        functools.partial(_cmp_attn_sample_body, past),
        grid=(nb,),
        in_specs=[pl.BlockSpec((1, 1, NSA_DIM), lambda b: (b, 0, 0)),
                  pl.BlockSpec((1, nh, 256), lambda b: (b, 0, 0)),
                  pl.BlockSpec(ov.shape, lambda b: (0, 0))],
        out_specs=[pl.BlockSpec((1, 1, NSA_DIM), lambda b: (b, 0, 0)),
                   pl.BlockSpec((1, 1, KV_HEADS * nsl), lambda b: (b, 0, 0))],
        out_shape=[jax.ShapeDtypeStruct((nb, 1, NSA_DIM), F32),
                   jax.ShapeDtypeStruct((nb, 1, KV_HEADS * nsl), F32)],
        compiler_params=_cparams(("arbitrary",)),
        name="cmp_attn_sample",
    )(q3, kcvc, ov)
    sel = pl.pallas_call(
        functools.partial(_select_sample_body, past, ns_pad),
        out_shape=jax.ShapeDtypeStruct((nb, KV_HEADS * nsl), jnp.int32),
        name="select_sample",
    )(imp.reshape(nb, KV_HEADS * nsl))
    return ocmp, sel


N_SPLIT = 3


def _pos_rows(n, start=0):
    tab = np.zeros((HEAD_DIM, n), np.float32)
    k = start + np.arange(n)
    tab[0:N_SPLIT] = k // SEL_BLK
    tab[N_SPLIT:2 * N_SPLIT] = k % SEL_BLK
    return jnp.asarray(tab, BF16)


def _slope_rows():
    bf = lambda v: np.asarray(v, dtype=BF16).astype(np.float32)
    tab = np.zeros((KV_HEADS, SUBLANES, HEAD_DIM), np.float32)
    for k in range(KV_HEADS):
        for g in range(GQA):
            for c, val in enumerate((SEL_BLK * _alibi_slope(k * GQA + g), _alibi_slope(k * GQA + g))):
                rest = np.float32(val)
                for j in range(N_SPLIT):
                    piece = bf(rest)
                    tab[k, g, c * N_SPLIT + j] = piece
                    rest = np.float32(rest - piece)
    return jnp.asarray(tab, F32)


def _block_expand(ns, n):
    return jnp.asarray((np.arange(n)[None, :] // SEL_BLK == np.arange(ns)[:, None]).astype(np.float32), BF16)


def _flash_step(q4, kT_aug, vT, bias, m_scr, acc_scr):
    n, tk = q4.shape[0], kT_aug.shape[1]
    s = jnp.dot(q4, kT_aug, preferred_element_type=F32)
    rb = bias.shape[0]
    if rb in (1, n):
        s = s + bias
    else:
        s = (s.reshape(n // rb, rb, tk) + bias[None]).reshape(n, tk)
    m_old = m_scr[...]
    m_new = jnp.maximum(m_old, jnp.max(s, axis=-1, keepdims=True))
    alpha = jnp.exp2(m_old - m_new)
    p = jnp.exp2(s - jnp.concatenate([m_new] * (tk // LANES), axis=1))
    v_ones = jnp.concatenate([vT, jnp.ones((LANES - HEAD_DIM, tk), BF16)], axis=0)
    acc_scr[...] = alpha * acc_scr[...] + _nt_dot(p.astype(BF16), v_ones)
    m_scr[...] = m_new


def _flash_result(acc_scr):
    acc = acc_scr[...]
    return acc[:, :HEAD_DIM] / jnp.maximum(acc[:, HEAD_DIM:], 1e-30)


def _sel_win_body(tk, q_ref, sel_ref, g_ref, ocmp_ref, ksT_ref, vsT_ref, kwT_ref, vwT_ref, pos_ref, exp_ref, slope_ref,
                  o_ref, m_scr, acc_scr):
    i = pl.program_id(2)
    tq = q_ref.shape[1]
    t0 = i * tq
    qpos = t0 + lax.broadcasted_iota(jnp.int32, (tq, 1), 0)
    q4 = jnp.concatenate(
        [jnp.concatenate([q_ref[0, :, g * HEAD_DIM:(g + 1) * HEAD_DIM],
                          jnp.broadcast_to(slope_ref[0, g:g + 1, :], (tq, HEAD_DIM)).astype(BF16)], axis=1)
         for g in range(GQA)], axis=0)

    def reset():
        m_scr[...] = jnp.full(m_scr.shape, NEG_INF, F32)
        acc_scr[...] = jnp.zeros(acc_scr.shape, F32)

    def finish():
        return _flash_result(acc_scr)

    reset()
    sel = sel_ref[0, 0]

    def sel_step(kt, carry):
        k0 = pl.multiple_of(kt * tk, tk)
        kpos = k0 + lax.broadcasted_iota(jnp.int32, (1, tk), 1)
        chosen = jnp.dot(sel, exp_ref[:, pl.ds(k0, tk)], preferred_element_type=F32)
        bias = (jnp.where(kpos <= qpos, chosen, 0.0) - 1.0) * (-NEG_INF)
        kT_aug = jnp.concatenate([ksT_ref[0, :, pl.ds(k0, tk)].astype(BF16), pos_ref[:, pl.ds(k0, tk)]], axis=0)
        _flash_step(q4, kT_aug, vsT_ref[0, :, pl.ds(k0, tk)].astype(BF16), bias, m_scr, acc_scr)
        return carry

    lax.fori_loop(0, (t0 + tq + tk - 1) // tk, sel_step, 0)
    o_sel = finish()

    reset()
    wk = WINDOW + tq
    k0 = pl.multiple_of(jnp.maximum(t0 - WINDOW, 0), tq)
    dist = qpos - (k0 + lax.broadcasted_iota(jnp.int32, (1, wk), 1))
    bias = jnp.where(lax.bitcast_convert_type(dist, jnp.uint32) < jnp.uint32(WINDOW), 0.0, NEG_INF)
    kT_aug = jnp.concatenate([kwT_ref[0, :, pl.ds(k0, wk)].astype(BF16), pos_ref[:, pl.ds(k0, wk)]], axis=0)
    _flash_step(q4, kT_aug, vwT_ref[0, :, pl.ds(k0, wk)].astype(BF16), bias, m_scr, acc_scr)
    o_win = finish()
    gates = g_ref[0]
    for g in range(GQA):
        rows = slice(g * tq, (g + 1) * tq)
        cols = slice(g * HEAD_DIM, (g + 1) * HEAD_DIM)
        o = (gates[:, 3 * g:3 * g + 1] * ocmp_ref[0, :, cols] + gates[:, 3 * g + 1:3 * g + 2] * o_sel[rows]
             + gates[:, 3 * g + 2:3 * g + 3] * o_win[rows])
        o_ref[0, :, cols] = o.astype(o_ref.dtype)


def _sel_win_prompt(q, sel, gates, ocmp, ksT, kwT, tq, tk):
    nb, t, _ = q.shape
    ns = sel.shape[-1]
    assert t % tk == 0 and t % tq == 0 and WINDOW % tq == 0
    pos, expand, slopes = _pos_rows(t), _block_expand(ns, t), _slope_rows()
    grp = GQA * HEAD_DIM
    kv_spec = lambda which: pl.BlockSpec((1, HEAD_DIM, t), lambda b, k, i: (b, which * KV_HEADS + k, 0))
    return pl.pallas_call(
        functools.partial(_sel_win_body, tk),
        grid=(nb, KV_HEADS, t // tq),
        in_specs=[pl.BlockSpec((1, tq, grp), lambda b, k, i: (b, i, k)),
                  pl.BlockSpec((1, 1, tq, ns), lambda b, k, i: (b, k, i, 0)),
                  pl.BlockSpec((1, tq, LANES), lambda b, k, i: (b, i, k)),
                  pl.BlockSpec((1, tq, grp), lambda b, k, i: (b, i, k)),
                  kv_spec(0), kv_spec(1), kv_spec(0), kv_spec(1),
                  pl.BlockSpec(pos.shape, lambda b, k, i: (0, 0)),
                  pl.BlockSpec(expand.shape, lambda b, k, i: (0, 0)),
                  pl.BlockSpec((1, SUBLANES, HEAD_DIM), lambda b, k, i: (k, 0, 0))],
        out_specs=pl.BlockSpec((1, tq, grp), lambda b, k, i: (b, i, k)),
        out_shape=jax.ShapeDtypeStruct((nb, t, NSA_DIM), BF16),
        scratch_shapes=[pltpu.VMEM((GQA * tq, LANES), F32), pltpu.VMEM((GQA * tq, LANES), F32)],
        compiler_params=_cparams(("arbitrary", "arbitrary", "arbitrary")),
        name="sel_win_attn",
    )(q, sel, gates, ocmp, ksT, ksT, kwT, kwT, pos, expand, slopes)


def _sel_win_sample_body(past, g_pages, pt_ref, sel_ref, q_ref, g_ref, ocmp_ref, *refs):
    page_refs = refs[:g_pages]
    (win_ref, ksn_ref, kwn_ref, pos_ref, wpos_ref, slope_ref, o_ref, wout_ref, m_scr, acc_scr) = refs[g_pages:]
    b = pl.program_id(0)
    j = pl.program_id(1)
    ns_pad = sel_ref.shape[1] // KV_HEADS
    qrow = q_ref[0]
    q4 = [jnp.concatenate([_head_rows(qrow, k), slope_ref[k].astype(BF16)], axis=1) for k in range(KV_HEADS)]

    @pl.when(j == 0)
    def _():
        m_scr[...] = jnp.full(m_scr.shape, NEG_INF, F32)
        acc_scr[...] = jnp.zeros(acc_scr.shape, F32)

    width = g_pages * PAGE_SIZE
    lane_blk = lax.broadcasted_iota(jnp.int32, (1, width), 1) // SEL_BLK
    row0 = lax.broadcasted_iota(jnp.int32, (HEAD_DIM, 1), 0) < N_SPLIT
    blk0 = j * (width // SEL_BLK)
    pos = (pos_ref[...].astype(F32) + jnp.where(row0, blk0.astype(F32), 0.0)).astype(BF16)
    for k in range(KV_HEADS):
        kT = jnp.concatenate([r[0, k * HEAD_DIM:(k + 1) * HEAD_DIM, :] for r in page_refs], axis=1).astype(BF16)
        vT = jnp.concatenate([r[0, KV_DIM + k * HEAD_DIM:KV_DIM + (k + 1) * HEAD_DIM, :] for r in page_refs],
                             axis=1).astype(BF16)
        bias = jnp.full((1, width), NEG_INF, F32)
        for blk in range(width // SEL_BLK):
            chosen = sel_ref[b, k * ns_pad + blk0 + blk] > 0
            bias = jnp.where(lane_blk == blk, jnp.where(chosen, 0.0, NEG_INF), bias)
        _flash_step(q4[k], jnp.concatenate([kT, pos], axis=0), vT, bias, m_scr.at[k], acc_scr.at[k])

    @pl.when(j == pl.num_programs(1) - 1)
    def _():
        nb = ksn_ref.shape[2]
        pick = lax.broadcasted_iota(jnp.int32, (1, nb), 1) == b
        ks_new = jnp.sum(jnp.where(pick, ksn_ref[0], 0.0), axis=1, keepdims=True)
        kw_new = jnp.sum(jnp.where(pick, kwn_ref[0], 0.0), axis=1, keepdims=True)
        lane = lax.broadcasted_iota(jnp.int32, (1, LANES), 1)
        tile_new = jnp.where(lane == 0, ks_new, 0.0).astype(BF16)
        pos_new = jnp.where(row0 & (lane == 0), float(past // SEL_BLK), 0.0).astype(BF16)
        bias_new = jnp.where(lane == 0, 0.0, NEG_INF)
        wlane = lax.broadcasted_iota(jnp.int32, (1, win_ref.shape[2]), 1)
        wout = jnp.where(wlane == win_ref.shape[2] - 1, kw_new, pltpu.roll(win_ref[0], win_ref.shape[2] - 1, 1))
        wout_ref[0] = wout
        woutb = wout.astype(BF16)
        gates = g_ref[0]
        for k in range(KV_HEADS):
            ksl = slice(k * HEAD_DIM, (k + 1) * HEAD_DIM)
            vsl = slice(KV_DIM + k * HEAD_DIM, KV_DIM + (k + 1) * HEAD_DIM)
            _flash_step(q4[k], jnp.concatenate([tile_new[ksl], pos_new], axis=0), tile_new[vsl], bias_new,
                        m_scr.at[k], acc_scr.at[k])
            o_sel = _flash_result(acc_scr.at[k])
            m_scr[k] = jnp.full(m_scr.shape[1:], NEG_INF, F32)
            acc_scr[k] = jnp.zeros(acc_scr.shape[1:], F32)
            _flash_step(q4[k], jnp.concatenate([woutb[ksl], wpos_ref[...]], axis=0), woutb[vsl],
                        jnp.zeros((1, win_ref.shape[2]), F32), m_scr.at[k], acc_scr.at[k])
            o_win = _flash_result(acc_scr.at[k])
            for g in range(GQA):
                head = k * GQA + g
                cols = slice(head * HEAD_DIM, (head + 1) * HEAD_DIM)
                c0 = k * LANES + 3 * g
                o = (gates[:, c0:c0 + 1] * ocmp_ref[0, :, cols] + gates[:, c0 + 1:c0 + 2] * o_sel[g:g + 1, :]
                     + gates[:, c0 + 2:c0 + 3] * o_win[g:g + 1, :])
                o_ref[0, :, cols] = o.astype(o_ref.dtype)


def _sel_win_sample(q3, g3, ocmp, sel, pool_sel, page_table, win, ksT_new, kwT_new, past):
    nb, n_pages = page_table.shape
    wlen = win.shape[2]
    assert wlen == WINDOW and past >= WINDOW
    g = min(32, n_pages)
    pos, wpos, slopes = _pos_rows(g * PAGE_SIZE), _pos_rows(wlen, past - wlen + 1), _slope_rows()
    full = lambda a: pl.BlockSpec(a.shape, lambda b, j, pt, sl: (0,) * a.ndim)
    row = lambda n: pl.BlockSpec((1, 1, n), lambda b, j, pt, sl: (b, 0, 0))
    page_spec = lambda k: pl.BlockSpec((1, 256, PAGE_SIZE), lambda b, j, pt, sl: (pt[b, j * g + k], 0, 0))
    wspec = pl.BlockSpec((1, 256, wlen), lambda b, j, pt, sl: (b, 0, 0))
    return pl.pallas_call(
        functools.partial(_sel_win_sample_body, past, g),
        grid_spec=pltpu.PrefetchScalarGridSpec(
            num_scalar_prefetch=2,
            grid=(nb, n_pages // g),
            in_specs=[row(NSA_DIM), row(KV_HEADS * LANES), row(NSA_DIM)] + [page_spec(k) for k in range(g)]
                     + [wspec, full(ksT_new), full(kwT_new), full(pos), full(wpos), full(slopes)],
            out_specs=[row(NSA_DIM), wspec],
            scratch_shapes=[pltpu.VMEM((KV_HEADS, SUBLANES, LANES), F32), pltpu.VMEM((KV_HEADS, SUBLANES, LANES), F32)],
        ),
        out_shape=[jax.ShapeDtypeStruct((nb, 1, NSA_DIM), BF16), jax.ShapeDtypeStruct((nb, 256, wlen), F32)],
        compiler_params=_cparams(("arbitrary", "arbitrary")),
        name="sel_win_sample",
    )(page_table, sel, q3, g3, ocmp, *([pool_sel] * g), win, ksT_new, kwT_new, pos,
      wpos, slopes)


def _softplus(v):
    return jnp.maximum(v, 0.0) + jnp.log1p(jnp.exp(-jnp.abs(v)))


def _tn_dot(a, b):
    return lax.dot_general(a, b, (((0,), (0,)), ((), ())), preferred_element_type=F32)


def _prep_ssd(conv_w, conv_b, dt_bias, a_log, d_skip, ssd_norm):
    padl = lambda v: jnp.pad(v.reshape(1, -1), ((0, 0), (0, LANES - v.shape[0])))
    L = SSD_CHUNK
    tril = jnp.asarray(np.tril(np.ones((L, L), np.float32)))
    return dict(conv_w=conv_w, conv_b=conv_b.reshape(1, -1), dtb_row=padl(dt_bias), dtb_col=dt_bias.reshape(-1, 1),
                alog_row=padl(a_log), alog_col=a_log.reshape(-1, 1), dskip=padl(d_skip), norm=ssd_norm.reshape(1, -1),
                tril=tril, triu=tril.T)


def _ssd_chunk(u, z, dt, dtT, h_prev, P):
    a_row = -jnp.exp(P["alog_row"][...])
    a_col = -jnp.exp(P["alog_col"][...])
    acum = jnp.dot(P["tril"][...], dt * a_row, preferred_element_type=F32, precision=HIGHEST)
    acumT = jnp.dot(dtT * a_col, P["triu"][...], preferred_element_type=F32, precision=HIGHEST)
    L = u.shape[0]
    li = lax.broadcasted_iota(jnp.int32, (L, L), 0)
    si = lax.broadcasted_iota(jnp.int32, (L, L), 1)
    causal = li >= si
    gn = SSD_GROUPS * D_STATE
    ys, hs = [], []
    per = SSD_HEADS // SSD_GROUPS
    for g in range(SSD_GROUPS):
        bm = u[:, D_INNER + g * D_STATE:D_INNER + (g + 1) * D_STATE]
        cm = u[:, D_INNER + gn + g * D_STATE:D_INNER + gn + (g + 1) * D_STATE]
        bmb = bm.astype(BF16)
        cb = _nt_dot(cm.astype(BF16), bmb)
        for e in range(per):
            h = g * per + e
            ac = acum[:, h:h + 1]
            seg = ac - acumT[h:h + 1, :]
            decay = jnp.where(causal, jnp.exp(jnp.where(causal, seg, 0.0)), 0.0)
            xs = u[:, h * SSD_HEAD_DIM:(h + 1) * SSD_HEAD_DIM]
            xdt = xs * dt[:, h:h + 1]
            y = jnp.dot((cb * decay).astype(BF16), xdt.astype(BF16), preferred_element_type=F32)
            a_last = acum[L - 1:L, h:h + 1]
            st = _tn_dot((xdt * jnp.exp(a_last - ac)).astype(BF16), bmb)
            y = y + _nt_dot((cm * jnp.exp(ac)).astype(BF16), h_prev[h].astype(BF16))
            hs.append(jnp.exp(a_last) * h_prev[h] + st)
            ys.append(y + P["dskip"][:, h:h + 1] * xs)
    return ys, hs


def _ssd_finish(ys, z, norm_w):
    y = jnp.concatenate(ys, axis=1) * _silu(z)
    ms = jnp.mean(y * y, axis=-1, keepdims=True)
    return y * lax.rsqrt(ms + RMS_EPS) * norm_w


def _ssd_prompt_body(xbc_ref, z_ref, dt_ref, dtT_ref, cw_ref, cb_ref, dtbr_ref, dtbc_ref, alr_ref, alc_ref, dsk_ref,
                     nrm_ref, tril_ref, triu_ref, y_ref, hout_ref, xpad_scr, h_scr):
    c = pl.program_id(1)
    L = xbc_ref.shape[1]

    @pl.when(c == 0)
    def _():
        xpad_scr[0:SUBLANES, :] = jnp.zeros((SUBLANES, xpad_scr.shape[1]), F32)
        h_scr[...] = jnp.zeros(h_scr.shape, F32)

    xt = xbc_ref[0]
    xpad_scr[SUBLANES:SUBLANES + L, :] = xt
    conv = cb_ref[...] + xpad_scr[SUBLANES - (CONV_W - 1):SUBLANES - (CONV_W - 1) + L, :] * cw_ref[0:1, :]
    for k in range(1, CONV_W):
        o = SUBLANES - (CONV_W - 1) + k
        conv = conv + xpad_scr[o:o + L, :] * cw_ref[k:k + 1, :]
    xpad_scr[0:SUBLANES, :] = xt[L - SUBLANES:L, :]
    u = _silu(conv)
    dt = _softplus(dt_ref[0] + dtbr_ref[...])
    dtT = _softplus(dtT_ref[0] + dtbc_ref[...])
    P = dict(alog_row=alr_ref, alog_col=alc_ref, tril=tril_ref, triu=triu_ref, dskip=dsk_ref[...])
    ys, hs = _ssd_chunk(u, z_ref[0], dt, dtT, [h_scr[h] for h in range(SSD_HEADS)], P)
    for h in range(SSD_HEADS):
        h_scr[h] = hs[h]
    y_ref[0] = _ssd_finish(ys, z_ref[0], nrm_ref[...]).astype(y_ref.dtype)

    @pl.when(c == pl.num_programs(1) - 1)
    def _():
        hout_ref[0] = h_scr[...]


def _ssd_prompt(xbc, z, dt, dtT, SP):
    nb, t, cd = xbc.shape
    L = SSD_CHUNK
    assert t % L == 0
    full = lambda a: pl.BlockSpec(a.shape, lambda b, c: (0,) * a.ndim)
    names = ("conv_w", "conv_b", "dtb_row", "dtb_col", "alog_row", "alog_col", "dskip", "norm", "tril", "triu")
    ps = [SP[n] for n in names]
    return pl.pallas_call(
        _ssd_prompt_body,
        grid=(nb, t // L),
        in_specs=[pl.BlockSpec((1, L, cd), lambda b, c: (b, c, 0)),
                  pl.BlockSpec((1, L, D_INNER), lambda b, c: (b, c, 0)),
                  pl.BlockSpec((1, L, LANES), lambda b, c: (b, c, 0)),
                  pl.BlockSpec((1, SSD_HEADS, L), lambda b, c: (b, 0, c))] + [full(a) for a in ps],
        out_specs=[pl.BlockSpec((1, L, D_INNER), lambda b, c: (b, c, 0)),
                   pl.BlockSpec((1, SSD_HEADS, SSD_HEAD_DIM, D_STATE), lambda b, c: (b, 0, 0, 0))],
        out_shape=[jax.ShapeDtypeStruct((nb, t, D_INNER), BF16),
                   jax.ShapeDtypeStruct((nb, SSD_HEADS, SSD_HEAD_DIM, D_STATE), F32)],
        scratch_shapes=[pltpu.VMEM((SUBLANES + L, cd), F32), pltpu.VMEM((SSD_HEADS, SSD_HEAD_DIM, D_STATE), F32)],
        compiler_params=_cparams(("arbitrary", "arbitrary")),
        name="ssd_prompt",
    )(xbc, z, dt, dtT, *ps)


def _ssd_sample_body(cs_ref, xbc_ref, z_ref, dt_ref, h0_ref, cw_ref, cb_ref, dtb_ref, al_ref, dsk_ref, nrm_ref, eye_ref,
                     y_ref, cso_ref, h_ref):
    nseq = xbc_ref.shape[0]
    xn = xbc_ref[...]
    conv = cb_ref[...] + xn * cw_ref[CONV_W - 1:CONV_W, :]
    for k in range(CONV_W - 1):
        conv = conv + cs_ref[k] * cw_ref[k:k + 1, :]
        if k > 0:
            cso_ref[k - 1] = cs_ref[k]
    cso_ref[CONV_W - 2] = xn
    u = _silu(conv)
    dt = _softplus(dt_ref[...] + dtb_ref[...])
    decay = jnp.exp(dt * (-jnp.exp(al_ref[...])))
    eye = eye_ref[...]
    gn = SSD_GROUPS * D_STATE
    per = SSD_HEADS // SSD_GROUPS
    rows = []
    for s in range(nseq):
        ys = []
        for h in range(SSD_HEADS):
            g = h // per
            xs = u[s:s + 1, h * SSD_HEAD_DIM:(h + 1) * SSD_HEAD_DIM]
            bm = u[s:s + 1, D_INNER + g * D_STATE:D_INNER + (g + 1) * D_STATE]
            cm = u[s:s + 1, D_INNER + gn + g * D_STATE:D_INNER + gn + (g + 1) * D_STATE]
            xcol = jnp.sum(eye * xs, axis=1, keepdims=True)
            hn = decay[s:s + 1, h:h + 1] * h0_ref[s, h] + (dt[s:s + 1, h:h + 1] * xcol) * bm
            h_ref[s, h] = hn
            ycol = jnp.sum(hn * cm, axis=1, keepdims=True)
            ys.append(jnp.sum(eye * ycol, axis=0, keepdims=True) + dsk_ref[:, h:h + 1] * xs)
        rows.append(jnp.concatenate(ys, axis=1))
    y = jnp.concatenate(rows, axis=0) * _silu(z_ref[...])
    ms = jnp.mean(y * y, axis=-1, keepdims=True)
    y_ref[...] = (y * lax.rsqrt(ms + RMS_EPS) * nrm_ref[...]).astype(y_ref.dtype)


def _ssd_sample(conv_state, xbc, z, dt, h0, SP):
    nb, cd = xbc.shape
    ts = SUBLANES
    assert nb % ts == 0
    eye = jnp.eye(SSD_HEAD_DIM, dtype=F32)
    names = ("conv_w", "conv_b", "dtb_row", "alog_row", "dskip", "norm")
    ps = [SP[n] for n in names] + [eye]
    full = lambda a: pl.BlockSpec(a.shape, lambda i: (0,) * a.ndim)
    st = pl.BlockSpec((ts, SSD_HEADS, SSD_HEAD_DIM, D_STATE), lambda i: (i, 0, 0, 0))
    cs = pl.BlockSpec((CONV_W - 1, ts, cd), lambda i: (0, i, 0))
    row = lambda n: pl.BlockSpec((ts, n), lambda i: (i, 0))
    return pl.pallas_call(
        _ssd_sample_body,
        grid=(nb // ts,),
        in_specs=[cs, row(cd), row(D_INNER), row(LANES), st] + [full(a) for a in ps],
        out_specs=[row(D_INNER), cs, st],
        out_shape=[jax.ShapeDtypeStruct((nb, D_INNER), BF16),
                   jax.ShapeDtypeStruct((CONV_W - 1, nb, cd), F32),
                   jax.ShapeDtypeStruct(h0.shape, F32)],
        compiler_params=_cparams(("arbitrary",)),
        name="ssd_sample",
    )(conv_state, xbc, z, dt, h0, *ps)


def _pack_bf16_pairs(v):
    m = v.shape[1] // 2
    hi = pltpu.bitcast(v[:, :m].astype(BF16).astype(F32), jnp.uint32)
    lo = pltpu.bitcast(v[:, m:].astype(BF16).astype(F32), jnp.uint32)
    return hi | (lo >> 16)


def _unpack_pairs_f32(w):
    return pltpu.bitcast(w & jnp.uint32(0xFFFF0000), F32), pltpu.bitcast(w << 16, F32)


def _unpack_bf16_pairs(w):
    hi, lo = _unpack_pairs_f32(w)
    return hi.astype(BF16), lo.astype(BF16)


def _route(logitsT, bias_col):
    s = jax.nn.sigmoid(logitsT)
    sb = s + bias_col
    per = N_EXPERTS // N_EXPERT_GROUPS
    assert per == SUBLANES
    grp = [sb[per * a:per * (a + 1)] for a in range(N_EXPERT_GROUPS)]
    sub = lax.broadcasted_iota(jnp.int32, grp[0].shape, 0)
    gs = []
    for ga in grp:
        m1 = jnp.max(ga, axis=0, keepdims=True)
        first = jnp.min(jnp.where(ga == m1, sub, per), axis=0, keepdims=True)
        m2 = jnp.max(jnp.where(sub == first, NEG_INF, ga), axis=0, keepdims=True)
        gs.append(m1 + m2)
    gmask = _topk_mask([jnp.concatenate(gs, axis=0)], TOPK_GROUPS)[0]
    masked = [jnp.where(gmask[a:a + 1, :] > 0.5, grp[a], NEG_INF) for a in range(N_EXPERT_GROUPS)]
    sel = jnp.concatenate(_topk_mask(masked, TOP_K), axis=0)
    w = s * sel
    w = w / jnp.sum(w, axis=0, keepdims=True) * ROUTED_SCALE
    return sel, w


def _post_mix_body(x_ref, on_ref, ys_ref, g1_ref, sc_ref, sh_ref, nw_ref, wo_ref, rw_ref, rb_ref,
                   x1_ref, hp_ref, selT_ref, wT_ref, cnt_ref):
    first = (pl.program_id(0) == 0) & (pl.program_id(1) == 0)
    half = wo_ref.shape[0] // 2
    mix = (jnp.dot(on_ref[0], wo_ref[0:half, :], preferred_element_type=F32)
           + jnp.dot(ys_ref[0], wo_ref[half:, :], preferred_element_type=F32))
    x1 = x_ref[0] + g1_ref[0] * mix
    x1_ref[0] = x1
    ms = jnp.mean(x1 * x1, axis=-1, keepdims=True)
    h = x1 * lax.rsqrt(ms + RMS_EPS) * nw_ref[...]
    h = h * (1.0 + sc_ref[0]) + sh_ref[0]
    hp_ref[0] = _pack_bf16_pairs(h)
    logitsT = lax.dot_general(rw_ref[...], h, (((1,), (1,)), ((), ())), preferred_element_type=F32,
                              precision=HIGHEST)
    sel, w = _route(logitsT, rb_ref[...])
    selT_ref[...] = sel.astype(selT_ref.dtype)
    wT_ref[...] = w

    @pl.when(first)
    def _():
        cnt_ref[...] = jnp.zeros(cnt_ref.shape, F32)

    cnt_ref[...] += jnp.broadcast_to(jnp.sum(sel, axis=1, keepdims=True), cnt_ref.shape)


def _post_mix(x, o_nsa, y_ssd, g1, sc2, sh2, norm_w, w_out_b, router_wT, router_bias, tm):
    nb, t, d = x.shape
    mt = g1.shape[1]
    nt = t // tm
    assert t % tm == 0 and (mt == 1 or mt == t)
    if mt == 1:
        mod_spec = pl.BlockSpec((1, 1, d), lambda b, i: (b, 0, 0))
    else:
        mod_spec = pl.BlockSpec((1, tm, d), lambda b, i: (b, i, 0))
    row = lambda n: pl.BlockSpec((1, tm, n), lambda b, i: (b, i, 0))
    full = lambda a: pl.BlockSpec(a.shape, lambda b, i: (0,) * a.ndim)
    tok = lambda: pl.BlockSpec((N_EXPERTS, tm), lambda b, i: (0, b * nt + i))
    rb = router_bias.reshape(N_EXPERTS, 1)
    nw = norm_w.reshape(1, d)
    return pl.pallas_call(
        _post_mix_body,
        grid=(nb, nt),
        in_specs=[row(d), row(NSA_DIM), row(D_INNER), mod_spec, mod_spec, mod_spec, full(nw), full(w_out_b),
                  full(router_wT), full(rb)],
        out_specs=[row(d), row(d // 2), tok(), tok(), pl.BlockSpec((N_EXPERTS, LANES), lambda b, i: (0, 0))],
        out_shape=[jax.ShapeDtypeStruct((nb, t, d), F32),
                   jax.ShapeDtypeStruct((nb, t, d // 2), jnp.uint32),
                   jax.ShapeDtypeStruct((N_EXPERTS, nb * t), BF16),
                   jax.ShapeDtypeStruct((N_EXPERTS, nb * t), F32),
                   jax.ShapeDtypeStruct((N_EXPERTS, LANES), F32)],
        compiler_params=_cparams(("arbitrary", "arbitrary")),
        name="post_mix",
    )(x, o_nsa, y_ssd, g1, sc2, sh2, nw, w_out_b, router_wT, rb)


MOE_BLOCK_SHIFT = 9
MOE_BLOCK = 1 << MOE_BLOCK_SHIFT


def _moe_rows(n_tok):
    n_blocks = n_tok * TOP_K // MOE_BLOCK + N_EXPERTS
    n_blocks_pad = -(-n_blocks // LANES) * LANES
    return n_blocks, n_blocks_pad


def _plan_body(selT_ref, wT_ref, cnt_ref, triu_ref, tril_ref, eye_ref, dest_ref, w8_ref, be_ref, fill_ref,
               carry_scr, pstart_scr):
    step = pl.program_id(0)
    ne = N_EXPERTS

    @pl.when(step == 0)
    def _():
        cnt = cnt_ref[...]
        cnt_i = cnt.astype(jnp.int32)
        padded = (((cnt_i + (MOE_BLOCK - 1)) >> MOE_BLOCK_SHIFT) << MOE_BLOCK_SHIFT).astype(F32)
        pstart = jnp.dot(tril_ref[...].astype(F32), padded, preferred_element_type=F32, precision=HIGHEST)
        pstart_scr[...] = pstart
        carry_scr[...] = jnp.zeros(carry_scr.shape, F32)
        pend = pstart + padded
        nbp = be_ref.shape[1]
        starts = (lax.broadcasted_iota(jnp.int32, (1, nbp), 1) * MOE_BLOCK).astype(F32)
        below = jnp.where(pend[:, 0:1] <= starts, 1.0, 0.0)
        be_ref[...] = jnp.minimum(jnp.sum(below, axis=0, keepdims=True), float(ne - 1)).astype(jnp.int32)
        eye = eye_ref[...]
        to_row = lambda col: jnp.sum(col * eye, axis=0, keepdims=True)
        n_used = jnp.max(pend, axis=0, keepdims=True) * (1.0 / MOE_BLOCK)
        rows = jnp.concatenate([to_row(pstart + cnt), to_row(padded - cnt), n_used,
                                jnp.zeros((SUBLANES - 3, LANES), F32)], axis=0)
        fill_ref[...] = rows.astype(jnp.int32)

    sel = selT_ref[...]
    self32 = sel.astype(F32)
    rank = jnp.dot(sel, triu_ref[...], preferred_element_type=F32) + carry_scr[:, 0:1]
    carry_scr[...] += jnp.broadcast_to(jnp.sum(self32, axis=1, keepdims=True), carry_scr.shape)
    dest = pstart_scr[:, 0:1] + rank
    slot = jnp.dot(tril_ref[...], sel, preferred_element_type=F32)
    w = wT_ref[...]
    drows, wrows = [], []
    for k in range(TOP_K):
        pick = jnp.where(slot == float(k), self32, 0.0)
        drows.append(jnp.sum(pick * dest, axis=0, keepdims=True))
        wrows.append(jnp.sum(pick * w, axis=0, keepdims=True))
    dest_ref[...] = jnp.concatenate(drows, axis=0).astype(jnp.int32)
    w8_ref[...] = jnp.concatenate(wrows, axis=0)


def _moe_plan(selT, wT, cnt, tile):
    ne, n = selT.shape
    assert n % tile == 0
    _, nbp = _moe_rows(n)
    triu = jnp.asarray(np.triu(np.ones((tile, tile), np.float32), 1), BF16)
    tril = jnp.asarray(np.tril(np.ones((ne, ne), np.float32), -1), BF16)
    eye = jnp.asarray(np.eye(ne, LANES, dtype=np.float32))
    full = lambda a: pl.BlockSpec(a.shape, lambda i: (0,) * a.ndim)
    return pl.pallas_call(
        _plan_body,
        grid=(n // tile,),
        in_specs=[pl.BlockSpec((ne, tile), lambda i: (0, i)), pl.BlockSpec((ne, tile), lambda i: (0, i)),
                  full(cnt), full(triu), full(tril), full(eye)],
        out_specs=[pl.BlockSpec((TOP_K, tile), lambda i: (0, i)), pl.BlockSpec((TOP_K, tile), lambda i: (0, i)),
                   pl.BlockSpec((1, nbp), lambda i: (0, 0)), pl.BlockSpec((SUBLANES, LANES), lambda i: (0, 0))],
        out_shape=[jax.ShapeDtypeStruct((TOP_K, n), jnp.int32), jax.ShapeDtypeStruct((TOP_K, n), F32),
                   jax.ShapeDtypeStruct((1, nbp), jnp.int32), jax.ShapeDtypeStruct((SUBLANES, LANES), jnp.int32)],
        scratch_shapes=[pltpu.VMEM((ne, LANES), F32), pltpu.VMEM((ne, LANES), F32)],
        compiler_params=_cparams(("arbitrary",)),
        name="moe_plan",
    )(selT, wT, cnt, triu, tril, eye)


_FILL_PIECES = tuple(1 << s for s in reversed(range(MOE_BLOCK_SHIFT)))


def _fill_padding(fill_ref, xd_ref, zero_scr, zsem, wait):
    def per_expert(e, carry):
        start = fill_ref[0, e]
        n = fill_ref[1, e]
        head = n & (SUBLANES - 1)
        for r in range(SUBLANES - 1):
            @pl.when(r < head)
            def _():
                cp = pltpu.make_async_copy(zero_scr.at[pl.ds(0, 1)], xd_ref.at[pl.ds(start + r, 1)], zsem)
                cp.wait() if wait else cp.start()

        cur = start + head
        for p in _FILL_PIECES:
            if p < SUBLANES:
                continue
            hit = (n & p) != 0

            @pl.when(hit)
            def _():
                off = pl.multiple_of(cur, SUBLANES)
                cp = pltpu.make_async_copy(zero_scr.at[pl.ds(0, p)], xd_ref.at[pl.ds(off, p)], zsem)
                cp.wait() if wait else cp.start()

            cur = cur + jnp.where(hit, p, 0)
        return carry

    lax.fori_loop(0, N_EXPERTS, per_expert, 0)


def _dispatch_body(dest_ref, fill_ref, hp_ref, xd_ref, zero_scr, sem, zsem):
    step = pl.program_id(0)
    tile = hp_ref.shape[0]

    def row_copy(t, k):
        return pltpu.make_async_copy(hp_ref.at[pl.ds(t, 1)], xd_ref.at[pl.ds(dest_ref[k, t], 1)], sem)

    @pl.when(step == 0)
    def _():
        zero_scr[...] = jnp.zeros(zero_scr.shape, zero_scr.dtype)
        _fill_padding(fill_ref, xd_ref, zero_scr, zsem, False)
        _fill_padding(fill_ref, xd_ref, zero_scr, zsem, True)

    def issue(t, carry):
        for k in range(TOP_K):
            row_copy(t, k).start(priority=k % 2)
        return carry

    def drain(t, carry):
        for k in range(TOP_K):
            row_copy(t, k).wait()
        return carry

    lax.fori_loop(0, tile, issue, 0)
    lax.fori_loop(0, tile, drain, 0)


def _moe_dispatch(hp, dest8, fill, tile):
    n, m = hp.shape
    n_blocks, _ = _moe_rows(n)
    nr = n_blocks * MOE_BLOCK
    return pl.pallas_call(
        _dispatch_body,
        grid=(n // tile,),
        in_specs=[pl.BlockSpec((TOP_K, tile), lambda i: (0, i), memory_space=pltpu.SMEM),
                  pl.BlockSpec(memory_space=pltpu.SMEM),
                  pl.BlockSpec((tile, m), lambda i: (i, 0))],
        out_specs=pl.BlockSpec(memory_space=pl.ANY),
        out_shape=jax.ShapeDtypeStruct((nr, m), jnp.uint32),
        scratch_shapes=[pltpu.VMEM((_FILL_PIECES[0], m), jnp.uint32), pltpu.SemaphoreType.DMA(()),
                        pltpu.SemaphoreType.DMA(())],
        compiler_params=_cparams(("arbitrary",)),
        name="moe_dispatch",
    )(dest8, fill, hp)


def _swiglu_packed(xw, w1, w3, w2):
    xa, xb = _unpack_bf16_pairs(xw)
    half = xa.shape[1]
    mm = lambda w: (jnp.dot(xa, w[0:half, :], preferred_element_type=F32)
                    + jnp.dot(xb, w[half:, :], preferred_element_type=F32))
    hid = _silu(mm(w1)) * mm(w3)
    return jnp.dot(hid.astype(BF16), w2, preferred_element_type=F32)


def _experts_body(be_ref, nu_ref, xd_ref, w1_ref, w3_ref, w2_ref, yd_ref):
    i = pl.program_id(0)

    @pl.when(i < nu_ref[0])
    def _():
        yd_ref[...] = _pack_bf16_pairs(_swiglu_packed(xd_ref[...], w1_ref[0], w3_ref[0], w2_ref[0]))

    @pl.when(i >= nu_ref[0])
    def _():
        yd_ref[...] = jnp.zeros(yd_ref.shape, yd_ref.dtype)


def _moe_experts(xd, block_e, n_used, w1b, w3b, w2b):
    nr, m = xd.shape
    n_blocks = nr // MOE_BLOCK
    d, f = w1b.shape[1:]
    clamp = lambda i, nu: jnp.minimum(i, nu[0] - 1)
    return pl.pallas_call(
        _experts_body,
        grid_spec=pltpu.PrefetchScalarGridSpec(
            num_scalar_prefetch=2,
            grid=(n_blocks,),
            in_specs=[pl.BlockSpec((MOE_BLOCK, m), lambda i, be, nu: (clamp(i, nu), 0)),
                      pl.BlockSpec((1, d, f), lambda i, be, nu: (be[clamp(i, nu)], 0, 0)),
                      pl.BlockSpec((1, d, f), lambda i, be, nu: (be[clamp(i, nu)], 0, 0)),
                      pl.BlockSpec((1, f, d), lambda i, be, nu: (be[clamp(i, nu)], 0, 0))],
            out_specs=pl.BlockSpec((MOE_BLOCK, m), lambda i, be, nu: (i, 0)),
        ),
        out_shape=jax.ShapeDtypeStruct((nr, m), jnp.uint32),
        compiler_params=_cparams(("arbitrary",)),
        name="moe_experts",
    )(block_e, n_used, xd, w1b, w3b, w2b)


def _combine_body(dest_ref, w8_ref, hp_ref, x1_ref, g2_ref, eye_ref, sw1_ref, sw3_ref, sw2_ref, nf_ref, yd_ref,
                  o_ref, ybuf, sem):
    tile = hp_ref.shape[1]

    def row_copy(t, k):
        return pltpu.make_async_copy(yd_ref.at[pl.ds(dest_ref[k, t], 1)], ybuf.at[k, pl.ds(t, 1)], sem)

    def issue(t, carry):
        for k in range(TOP_K):
            row_copy(t, k).start(priority=k % 2)
        return carry

    def drain(t, carry):
        for k in range(TOP_K):
            row_copy(t, k).wait()
        return carry

    lax.fori_loop(0, tile, issue, 0)
    shared = _swiglu_packed(hp_ref[0], sw1_ref[...], sw3_ref[...], sw2_ref[...])
    w_rows = lax.dot_general(eye_ref[...], w8_ref[...], (((1,), (1,)), ((), ())), preferred_element_type=F32,
                             precision=HIGHEST)
    lax.fori_loop(0, tile, drain, 0)
    half = ybuf.shape[2]
    acc_a = jnp.zeros((tile, half), F32)
    acc_b = jnp.zeros((tile, half), F32)
    for k in range(TOP_K):
        ya, yb = _unpack_pairs_f32(ybuf[k])
        wk = w_rows[:, k:k + 1]
        acc_a = acc_a + wk * ya
        acc_b = acc_b + wk * yb
    routed = jnp.concatenate([acc_a, acc_b], axis=1)
    x2 = x1_ref[0] + g2_ref[0] * (routed + shared)
    ms = jnp.mean(x2 * x2, axis=-1, keepdims=True)
    o_ref[0] = x2 * lax.rsqrt(ms + RMS_EPS) * nf_ref[...]


def _moe_combine(dest8, w8, hp, x1, g2, yd, sw1b, sw3b, sw2b, norm_f, tile):
    nb, t, d = x1.shape
    nt = t // tile
    mt = g2.shape[1]
    assert t % tile == 0 and (mt == 1 or mt == t)
    if mt == 1:
        mod_spec = pl.BlockSpec((1, 1, d), lambda b, i: (b, 0, 0))
    else:
        mod_spec = pl.BlockSpec((1, tile, d), lambda b, i: (b, i, 0))
    eye = jnp.eye(tile, dtype=F32)
    nf = norm_f.reshape(1, d)
    full = lambda a: pl.BlockSpec(a.shape, lambda b, i: (0,) * a.ndim)
    return pl.pallas_call(
        _combine_body,
        grid=(nb, nt),
        in_specs=[pl.BlockSpec((TOP_K, tile), lambda b, i: (0, b * nt + i), memory_space=pltpu.SMEM),
                  pl.BlockSpec((TOP_K, tile), lambda b, i: (0, b * nt + i)),
                  pl.BlockSpec((1, tile, d // 2), lambda b, i: (b, i, 0)),
                  pl.BlockSpec((1, tile, d), lambda b, i: (b, i, 0)),
                  mod_spec, full(eye), full(sw1b), full(sw3b), full(sw2b), full(nf),
                  pl.BlockSpec(memory_space=pl.ANY)],
        out_specs=pl.BlockSpec((1, tile, d), lambda b, i: (b, i, 0)),
        out_shape=jax.ShapeDtypeStruct((nb, t, d), F32),
        scratch_shapes=[pltpu.VMEM((TOP_K, tile, d // 2), jnp.uint32), pltpu.SemaphoreType.DMA(())],
        compiler_params=_cparams(("arbitrary", "arbitrary")),
        name="moe_combine",
    )(dest8, w8, hp, x1, g2, eye, sw1b, sw3b, sw2b, nf, yd)


SC_CHUNK = 128


def _sc_workers():
    info = plsc.get_sparse_core_info()
    return info.num_cores, info.num_subcores


def _sc_dispatch(hp, dest8, nr):
    n, m = hp.shape
    nc, nsub = _sc_workers()
    per_w = n // (nc * nsub)
    assert n % (nc * nsub * SC_CHUNK) == 0
    mesh = plsc.VectorSubcoreMesh(core_axis_name="c", subcore_axis_name="s")

    @functools.partial(
        pl.kernel, mesh=mesh, out_type=jax.ShapeDtypeStruct((nr, m), hp.dtype),
        scratch_types=[pltpu.VMEM((TOP_K, SC_CHUNK), jnp.int32), pltpu.VMEM((SC_CHUNK, m), hp.dtype),
                       pltpu.SemaphoreType.DMA])
    def scatter_rows(hp_hbm, dest_hbm, xd_hbm, idx_v, rows_v, sem):
        wid = lax.axis_index("s") * nc + lax.axis_index("c")

        @pl.loop(0, per_w // SC_CHUNK)
        def _(c):
            base = pl.multiple_of(wid * per_w + c * SC_CHUNK, SC_CHUNK)
            pltpu.sync_copy(hp_hbm.at[pl.ds(base, SC_CHUNK)], rows_v)
            pltpu.sync_copy(dest_hbm.at[:, pl.ds(base, SC_CHUNK)], idx_v)
            copies = [pltpu.async_copy(rows_v, xd_hbm.at[idx_v.at[k]], sem) for k in range(TOP_K)]
            for cp in copies:
                cp.wait()

    return scatter_rows(hp, dest8)


def _sc_gather(yd, dest8):
    _, m = yd.shape
    n = dest8.shape[1]
    nc, nsub = _sc_workers()
    per_w = n // (nc * nsub)
    assert n % (nc * nsub * SC_CHUNK) == 0
    mesh = plsc.VectorSubcoreMesh(core_axis_name="c", subcore_axis_name="s")

    @functools.partial(
        pl.kernel, mesh=mesh, out_type=jax.ShapeDtypeStruct((TOP_K, n, m), yd.dtype),
        scratch_types=[pltpu.VMEM((TOP_K, SC_CHUNK), jnp.int32), pltpu.VMEM((SC_CHUNK, m), yd.dtype),
                       pltpu.SemaphoreType.DMA])
    def gather_rows(yd_hbm, dest_hbm, out_hbm, idx_v, rows_v, sem):
        wid = lax.axis_index("s") * nc + lax.axis_index("c")

        @pl.loop(0, per_w // SC_CHUNK)
        def _(c):
            base = pl.multiple_of(wid * per_w + c * SC_CHUNK, SC_CHUNK)
            pltpu.sync_copy(dest_hbm.at[:, pl.ds(base, SC_CHUNK)], idx_v)
            for k in range(TOP_K):
                pltpu.async_copy(yd_hbm.at[idx_v.at[k]], rows_v, sem).wait()
                pltpu.sync_copy(rows_v, out_hbm.at[k, pl.ds(base, SC_CHUNK)])

    return gather_rows(yd, dest8)


def _fill_body(fill_ref, xd_in_ref, xd_ref, zero_scr, zsem):
    del xd_in_ref
    zero_scr[...] = jnp.zeros(zero_scr.shape, zero_scr.dtype)
    for wait in (False, True):
        _fill_padding(fill_ref, xd_ref, zero_scr, zsem, wait)


def _moe_fill(xd, fill):
    return pl.pallas_call(
        _fill_body,
        in_specs=[pl.BlockSpec(memory_space=pltpu.SMEM), pl.BlockSpec(memory_space=pl.ANY)],
        out_specs=pl.BlockSpec(memory_space=pl.ANY),
        out_shape=jax.ShapeDtypeStruct(xd.shape, xd.dtype),
        scratch_shapes=[pltpu.VMEM((_FILL_PIECES[0], xd.shape[1]), xd.dtype), pltpu.SemaphoreType.DMA(())],
        input_output_aliases={1: 0},
        name="moe_fill",
    )(fill, xd)


def _combine_dense_body(w8_ref, hp_ref, x1_ref, g2_ref, eye_ref, sw1_ref, sw3_ref, sw2_ref, nf_ref, ybuf_ref, o_ref):
    tile = hp_ref.shape[1]
    shared = _swiglu_packed(hp_ref[0], sw1_ref[...], sw3_ref[...], sw2_ref[...])
    w_rows = lax.dot_general(eye_ref[...], w8_ref[...], (((1,), (1,)), ((), ())), preferred_element_type=F32,
                             precision=HIGHEST)
    half = ybuf_ref.shape[2]
    acc_a = jnp.zeros((tile, half), F32)
    acc_b = jnp.zeros((tile, half), F32)
    for k in range(TOP_K):
        ya, yb = _unpack_pairs_f32(ybuf_ref[k])
        wk = w_rows[:, k:k + 1]
        acc_a = acc_a + wk * ya
        acc_b = acc_b + wk * yb
    routed = jnp.concatenate([acc_a, acc_b], axis=1)
    x2 = x1_ref[0] + g2_ref[0] * (routed + shared)
    ms = jnp.mean(x2 * x2, axis=-1, keepdims=True)
    o_ref[0] = x2 * lax.rsqrt(ms + RMS_EPS) * nf_ref[...]


def _moe_combine_dense(w8, hp, x1, g2, ybuf, sw1b, sw3b, sw2b, norm_f, tile):
    nb, t, d = x1.shape
    nt = t // tile
    assert t % tile == 0 and g2.shape[1] == 1
    eye = jnp.eye(tile, dtype=F32)
    nf = norm_f.reshape(1, d)
    full = lambda a: pl.BlockSpec(a.shape, lambda b, i: (0,) * a.ndim)
    return pl.pallas_call(
        _combine_dense_body,
        grid=(nb, nt),
        in_specs=[pl.BlockSpec((TOP_K, tile), lambda b, i: (0, b * nt + i)),
                  pl.BlockSpec((1, tile, d // 2), lambda b, i: (b, i, 0)),
                  pl.BlockSpec((1, tile, d), lambda b, i: (b, i, 0)),
                  pl.BlockSpec((1, 1, d), lambda b, i: (b, 0, 0)),
                  full(eye), full(sw1b), full(sw3b), full(sw2b), full(nf),
                  pl.BlockSpec((TOP_K, tile, d // 2), lambda b, i: (0, b * nt + i, 0))],
        out_specs=pl.BlockSpec((1, tile, d), lambda b, i: (b, i, 0)),
        out_shape=jax.ShapeDtypeStruct((nb, t, d), F32),
        compiler_params=_cparams(("arbitrary", "arbitrary")),
        name="moe_combine_dense",
    )(w8, hp, x1, g2, eye, sw1b, sw3b, sw2b, nf, ybuf)


def _moe(x1, hp, selT, wT, cnt, g2, EW, norm_f, tile, on_sparsecore):
    nb, t, d = x1.shape
    n = nb * t
    dest8, w8, block_e, fill = _moe_plan(selT, wT, cnt, tile)
    n_blocks, _ = _moe_rows(n)
    hp2 = hp.reshape(n, d // 2)
    if on_sparsecore:
        xd = _moe_fill(_sc_dispatch(hp2, dest8, n_blocks * MOE_BLOCK), fill)
    else:
        xd = _moe_dispatch(hp2, dest8, fill, tile)
    yd = _moe_experts(xd, block_e[0, :n_blocks], fill[2, 0:1], EW["w1"], EW["w3"], EW["w2"])
    if on_sparsecore:
        return _moe_combine_dense(w8, hp, x1, g2, _sc_gather(yd, dest8), EW["sw1"], EW["sw3"], EW["sw2"], norm_f, tile)
    return _moe_combine(dest8, w8, hp, x1, g2, yd, EW["sw1"], EW["sw3"], EW["sw2"], norm_f, tile)


def kernel(x_prompt, x_sample, c_prompt, c_sample, cache_kv_cmp, cache_kv_sel, cache_kv_win, state_conv, state_ssm, page_table, w_ada, b_ada, norm_mix, norm_ffn, w_in, cmp_pe_k, cmp_w1_k, cmp_w2_k, cmp_pe_v, cmp_w1_v, cmp_w2_v, conv_w, conv_b, dt_bias, a_log, d_skip, ssd_norm, w_out, router_w, router_bias, exp_w1, exp_w3, exp_w2, sh_w1, sh_w3, sh_w2, norm_f):
    nb, t, d = x_prompt.shape
    ndb = x_sample.shape[0]
    c_all = jnp.concatenate([c_prompt, c_sample], axis=0)
    mod = _modulation(c_all, w_ada[0], b_ada[0]).reshape(nb + ndb, 6, d)
    mod_p = [mod[:nb, k][:, None, :] for k in range(6)]
    mod_s = [mod[nb:, k][None, :, :] for k in range(6)]
    W = _prep_w_in(w_in[0])
    P = _in_proj(x_prompt, mod_p[1], mod_p[0], norm_mix[0], W, 512)
    S = _in_proj(x_sample.reshape(1, ndb, d), mod_s[1], mod_s[0], norm_mix[0], W, ndb)
    C = _prep_compress(cmp_pe_k[0], cmp_w1_k[0], cmp_w2_k[0], cmp_pe_v[0], cmp_w1_v[0], cmp_w2_v[0])
    kcvc_p = _compress_prompt(P["kcT"], C)
    ocmp_p, sel_p = _cmp_attn_prompt(P["q"], kcvc_p, 256)
    o_nsa_p = _sel_win_prompt(P["q"], sel_p, P["g"], ocmp_p, P["ksT"], P["kwT"], 256, 512)
    SP = _prep_ssd(conv_w[0], conv_b[0], dt_bias[0], a_log[0], d_skip[0], ssd_norm[0])
    y_ssd_p, ssm_p = _ssd_prompt(P["xbc"], P["z"], P["dt"], P["dtT"], SP)
    w_out_b = w_out[0].astype(BF16)
    router_wT = router_w[0].T
    EW = dict(w1=exp_w1[0].astype(BF16), w3=exp_w3[0].astype(BF16), w2=exp_w2[0].astype(BF16),
              sw1=sh_w1[0].astype(BF16), sw3=sh_w3[0].astype(BF16), sw2=sh_w2[0].astype(BF16))
    x1_p, hp_p, selT_p, wT_p, cnt_p = _post_mix(x_prompt, o_nsa_p, y_ssd_p, mod_p[2], mod_p[4], mod_p[3], norm_ffn[0],
                                                 w_out_b, router_wT, router_bias[0], 512)
    big = (nb * t) % (32 * SC_CHUNK) == 0
    y_prompt = _moe(x1_p, hp_p, selT_p, wT_p, cnt_p, mod_p[5], EW, norm_f, 512, big)

    n_pool = cache_kv_cmp.shape[1]
    past = page_table.shape[1] * PAGE_SIZE
    to_pages = lambda c: jnp.transpose(c, (0, 2, 3, 4, 1)).reshape(c.shape[0], 256, c.shape[1])
    pool_cmp, pool_sel, win = to_pages(cache_kv_cmp[0]), to_pages(cache_kv_sel[0]), to_pages(cache_kv_win[0])
    kcvc_s = _compress_paged(pool_cmp, page_table, C)
    q3 = S["q"].reshape(ndb, 1, NSA_DIM)
    ocmp_s, sel_s = _cmp_attn_sample(q3, kcvc_s, past)
    o_nsa_s, win_s = _sel_win_sample(q3, S["g"].reshape(ndb, 1, KV_HEADS * LANES), ocmp_s, sel_s, pool_sel, page_table,
                                     win, S["ksT"], S["kwT"], past)
    y_ssd_s, conv_s, ssm_s = _ssd_sample(jnp.transpose(state_conv[0], (1, 0, 2)), S["xbc"][0], S["z"][0], S["dt"][0],
                                         state_ssm[0], SP)
    x1_s, hp_s, selT_s, wT_s, cnt_s = _post_mix(x_sample.reshape(1, ndb, d), o_nsa_s.reshape(1, ndb, NSA_DIM),
                                                 y_ssd_s.reshape(1, ndb, D_INNER), mod_s[2], mod_s[4], mod_s[3],
                                                 norm_ffn[0], w_out_b, router_wT, router_bias[0], ndb)
    y_sample = _moe(x1_s, hp_s, selT_s, wT_s, cnt_s, mod_s[5], EW, norm_f, ndb, False).reshape(ndb, 1, d)

    from_cm = lambda a: jnp.transpose(a.reshape(a.shape[0], 2, KV_HEADS, HEAD_DIM, a.shape[2]), (0, 4, 1, 2, 3))[None]
    tw = min(WINDOW, t)
    return (y_prompt, y_sample,
            from_cm(P["kcT"]), from_cm(P["ksT"]), from_cm(P["kwT"][:, :, t - tw:]),
            P["xbc"][:, t - (CONV_W - 1):, :][None], ssm_p[None],
            from_cm(jnp.transpose(S["kcT"], (2, 1, 0))), from_cm(jnp.transpose(S["ksT"], (2, 1, 0))), from_cm(win_s),
            jnp.transpose(conv_s, (1, 0, 2))[None], ssm_s[None])
```

```python
import functools
import math

import jax
import jax.numpy as jnp
import numpy as np
from jax import lax
from jax.experimental import pallas as pl
from jax.experimental.pallas import tpu as pltpu
from jax.experimental.pallas import tpu_sc as plsc

F32 = jnp.float32
BF16 = jnp.bfloat16
HIGHEST = lax.Precision.HIGHEST

D_MODEL = 1024
NSA_HEADS = 8
KV_HEADS = 2
HEAD_DIM = 64
GQA = NSA_HEADS // KV_HEADS
CMP_BLK = 32
CMP_STRIDE = 16
CMP_HID = 64
SEL_BLK = 64
N_SEL = 16
WINDOW = 512
FORCE_CUR = 2.0e4
FORCE_SINK = 1.0e4
SSD_HEADS = 8
SSD_HEAD_DIM = 64
D_INNER = SSD_HEADS * SSD_HEAD_DIM
SSD_GROUPS = 2
D_STATE = 128
CONV_W = 4
CONV_DIM = D_INNER + 2 * SSD_GROUPS * D_STATE
SSD_CHUNK = 128
NSA_DIM = NSA_HEADS * HEAD_DIM
KV_DIM = KV_HEADS * HEAD_DIM
N_EXPERTS = 64
TOP_K = 8
N_EXPERT_GROUPS = 8
TOPK_GROUPS = 4
D_EXPERT = 256
D_SHARED = 256
ROUTED_SCALE = 2.5
RMS_EPS = 1e-6
NEG_INF = -1e30
LOG2E = 1.4426950408889634
PAGE_SIZE = 128

LANES = 128
SUBLANES = 8
VMEM_LIMIT = 56 * 1024 * 1024


def _cparams(sem):
    return pltpu.CompilerParams(dimension_semantics=sem, vmem_limit_bytes=VMEM_LIMIT)


def _silu(v):
    return v * jax.nn.sigmoid(v)


def _mod_body(c_ref, w_ref, b_ref, o_ref):
    s = _silu(c_ref[...])
    o_ref[...] = jnp.dot(s, w_ref[...], preferred_element_type=F32, precision=HIGHEST) + b_ref[...]


def _modulation(c_all, w_ada, b_ada):
    n, d = c_all.shape
    nout = w_ada.shape[1]
    tn = 512
    return pl.pallas_call(
        _mod_body,
        grid=(nout // tn,),
        in_specs=[pl.BlockSpec((n, d), lambda j: (0, 0)),
                  pl.BlockSpec((d, tn), lambda j: (0, j)),
                  pl.BlockSpec((1, tn), lambda j: (0, j))],
        out_specs=pl.BlockSpec((n, tn), lambda j: (0, j)),
        out_shape=jax.ShapeDtypeStruct((n, nout), F32),
        compiler_params=_cparams(("arbitrary",)),
        name="modulation",
    )(c_all, w_ada, b_ada.reshape(1, nout))


def _nt_dot(a, b):
    return lax.dot_general(a, b, (((1,), (1,)), ((), ())), preferred_element_type=F32)


def _proj_body(x_ref, sc_ref, sh_ref, nw_ref, wq_ref, wkvT_ref, wg_ref, wz_ref, wx_ref, wdt_ref, wdtT_ref,
               q_ref, kcT_ref, ksT_ref, kwT_ref, g_ref, z_ref, xbc_ref, dt_ref, dtT_ref):
    x = x_ref[0]
    ms = jnp.mean(x * x, axis=-1, keepdims=True)
    h = x * lax.rsqrt(ms + RMS_EPS) * nw_ref[...]
    h = h * (1.0 + sc_ref[0]) + sh_ref[0]
    hb = h.astype(BF16)
    q_ref[0] = (jnp.dot(hb, wq_ref[...], preferred_element_type=F32) * (HEAD_DIM ** -0.5 * LOG2E)).astype(BF16)
    kvT = _nt_dot(wkvT_ref[...], hb)
    kcT_ref[0] = kvT[0:256]
    ksT_ref[0] = kvT[256:512]
    kwT_ref[0] = kvT[512:768]
    g_ref[0] = jax.nn.sigmoid(jnp.dot(hb, wg_ref[...], preferred_element_type=F32))
    z_ref[0] = jnp.dot(hb, wz_ref[...], preferred_element_type=F32)
    xbc_ref[0] = jnp.dot(hb, wx_ref[...], preferred_element_type=F32)
    dt_ref[0] = jnp.dot(hb, wdt_ref[...], preferred_element_type=F32)
    dtT_ref[0] = _nt_dot(wdtT_ref[...], hb)


def _prep_w_in(w_in):
    w = w_in
    o = 0
    wq = w[:, o:o + NSA_DIM]; o += NSA_DIM
    wkv = w[:, o:o + 6 * KV_DIM]; o += 6 * KV_DIM
    wg = w[:, o:o + 3 * NSA_HEADS]; o += 3 * NSA_HEADS
    wz = w[:, o:o + D_INNER]; o += D_INNER
    wx = w[:, o:o + CONV_DIM]; o += CONV_DIM
    wdt = w[:, o:o + SSD_HEADS]; o += SSD_HEADS
    pad = lambda a: jnp.pad(a, ((0, 0), (0, LANES - a.shape[1])))
    per = 3 * GQA
    wg = jnp.concatenate([pad(wg[:, k * per:(k + 1) * per]) for k in range(KV_HEADS)], axis=1)
    return dict(wq=wq.astype(BF16), wkvT=wkv.T.astype(BF16), wg=wg.astype(BF16), wz=wz.astype(BF16),
                wx=wx.astype(BF16), wdt=pad(wdt).astype(BF16), wdtT=wdt.T.astype(BF16))


def _in_proj(x, sc, sh, norm_w, W, tm):
    nb, t, d = x.shape
    mt = sc.shape[1]
    assert t % tm == 0 and (mt == 1 or mt == t)
    if mt == 1:
        mod_spec = pl.BlockSpec((1, 1, d), lambda b, i: (b, 0, 0))
    else:
        mod_spec = pl.BlockSpec((1, tm, d), lambda b, i: (b, i, 0))
    full = lambda a: pl.BlockSpec(a.shape, lambda b, i: (0,) * a.ndim)
    row = lambda n: pl.BlockSpec((1, tm, n), lambda b, i: (b, i, 0))
    col = lambda n: pl.BlockSpec((1, n, tm), lambda b, i: (b, 0, i))
    ws = [W["wq"], W["wkvT"], W["wg"], W["wz"], W["wx"], W["wdt"], W["wdtT"]]
    outs = pl.pallas_call(
        _proj_body,
        grid=(nb, t // tm),
        in_specs=[row(d), mod_spec, mod_spec, pl.BlockSpec((1, d), lambda b, i: (0, 0))] + [full(a) for a in ws],
        out_specs=[row(NSA_DIM), col(256), col(256), col(256), row(KV_HEADS * LANES), row(D_INNER), row(CONV_DIM),
                   row(LANES), col(SSD_HEADS)],
        out_shape=[jax.ShapeDtypeStruct((nb, t, NSA_DIM), BF16),
                   jax.ShapeDtypeStruct((nb, 256, t), F32),
                   jax.ShapeDtypeStruct((nb, 256, t), F32),
                   jax.ShapeDtypeStruct((nb, 256, t), F32),
                   jax.ShapeDtypeStruct((nb, t, KV_HEADS * LANES), F32),
                   jax.ShapeDtypeStruct((nb, t, D_INNER), F32),
                   jax.ShapeDtypeStruct((nb, t, CONV_DIM), F32),
                   jax.ShapeDtypeStruct((nb, t, LANES), F32),
                   jax.ShapeDtypeStruct((nb, SSD_HEADS, t), F32)],
        compiler_params=_cparams(("arbitrary", "arbitrary")),
        name="in_proj",
    )(x, sc, sh, norm_w.reshape(1, d), *ws)
    names = ("q", "kcT", "ksT", "kwT", "g", "z", "xbc", "dt", "dtT")
    return dict(zip(names, outs))


def _prep_compress(cmp_pe_k, cmp_w1_k, cmp_w2_k, cmp_pe_v, cmp_w1_v, cmp_w2_v):
    w1s = jnp.stack([cmp_w1_k, cmp_w1_v]).reshape(2, 2, CMP_STRIDE, HEAD_DIM, CMP_HID)
    eye = jnp.eye(2, dtype=F32)
    wbd = jnp.einsum("ktldh,kK,vV->lkvdtKVh", w1s, eye, eye).reshape(CMP_STRIDE, 256, 512)
    w2s = jnp.stack([cmp_w2_k, cmp_w2_v])
    w2bd = jnp.einsum("khd,kK,vV->kvhKVd", w2s, eye, eye).reshape(256, 256)
    pes = jnp.stack([cmp_pe_k, cmp_pe_v]).reshape(2, 2, CMP_STRIDE, HEAD_DIM)
    pe_rows = jnp.broadcast_to(jnp.transpose(pes, (1, 2, 0, 3))[:, :, :, None, :],
                               (2, CMP_STRIDE, 2, KV_HEADS, HEAD_DIM)).reshape(2, CMP_STRIDE * 256)
    pecat = jnp.pad(pe_rows, ((0, SUBLANES - 2), (0, 0)))
    perm = np.zeros((LANES, LANES), np.float32)
    for l in range(CMP_STRIDE):
        for n in range(LANES // CMP_STRIDE):
            perm[(LANES // CMP_STRIDE) * l + n, CMP_STRIDE * n + l] = 1.0
    bias = pl.pallas_call(
        _cmp_bias_body,
        out_shape=jax.ShapeDtypeStruct((SUBLANES, 512), F32),
        name="cmp_bias",
    )(pecat, wbd.reshape(CMP_STRIDE * 256, 512))
    return dict(wbd=wbd.astype(BF16), w2bd=w2bd.astype(BF16), bias=bias, perm=jnp.asarray(perm, BF16))


def _cmp_bias_body(pe_ref, w_ref, o_ref):
    o_ref[...] = jnp.dot(pe_ref[...], w_ref[...], preferred_element_type=F32, precision=HIGHEST)


def _compress_body(n_pref, n_slab_refs, slabs_per_ref, *refs):
    refs = refs[n_pref:]
    slab_refs = refs[:n_slab_refs]
    perm_ref, wbd_ref, bias_ref, w2bd_ref, o_ref, z_scr = refs[n_slab_refs:]
    j = pl.program_id(1)
    nh = z_scr.shape[1]
    per_slab = LANES // CMP_STRIDE
    g_tot = n_slab_refs * slabs_per_ref
    base = pl.multiple_of(j * (per_slab * g_tot), per_slab)
    perm = perm_ref[...]
    for ri in range(n_slab_refs):
        for si in range(slabs_per_ref):
            slab = slab_refs[ri][0][:, si * LANES:(si + 1) * LANES].astype(BF16)
            xp = _nt_dot(perm, slab)
            g = ri * slabs_per_ref + si
            for l in range(CMP_STRIDE):
                z_scr[l, pl.ds(base + per_slab * g, per_slab), :] = xp[per_slab * l:per_slab * (l + 1), :]

    @pl.when(j == pl.num_programs(1) - 1)
    def _():
        acc = jnp.zeros((nh, 512), F32)
        for l in range(CMP_STRIDE):
            acc = acc + jnp.dot(z_scr[l].astype(BF16), wbd_ref[l], preferred_element_type=F32)
        lead = acc[:, :256] + bias_ref[0:1, :256]
        tail = acc[:, 256:] + bias_ref[1:2, 256:]
        hid = _silu(lead + pltpu.roll(tail, nh - 1, 0))
        out = jnp.dot(hid.astype(BF16), w2bd_ref[...], preferred_element_type=F32)
        row = lax.broadcasted_iota(jnp.int32, out.shape, 0)
        o_ref[0] = jnp.where(row < nh - 1, out, 0.0)


def _compress_prompt(kcT, C):
    nb, _, t = kcT.shape
    nh = t // CMP_STRIDE
    g = min(8, t // LANES)
    nsteps = t // (LANES * g)
    full = lambda a: pl.BlockSpec(a.shape, lambda b, j: (0,) * a.ndim)
    return pl.pallas_call(
        functools.partial(_compress_body, 0, 1, g),
        grid=(nb, nsteps),
        in_specs=[pl.BlockSpec((1, 256, LANES * g), lambda b, j: (b, 0, j)),
                  full(C["perm"]), full(C["wbd"]), full(C["bias"]), full(C["w2bd"])],
        out_specs=pl.BlockSpec((1, nh, 256), lambda b, j: (b, 0, 0)),
        out_shape=jax.ShapeDtypeStruct((nb, nh, 256), F32),
        scratch_shapes=[pltpu.VMEM((CMP_STRIDE, nh, 256), F32)],
        compiler_params=_cparams(("arbitrary", "arbitrary")),
        name="compress_prompt",
    )(kcT, C["perm"], C["wbd"], C["bias"], C["w2bd"])


def _compress_paged(pool, page_table, C):
    nb, n_pages = page_table.shape
    nh = n_pages * (PAGE_SIZE // CMP_STRIDE)
    g = min(16, n_pages)
    nsteps = n_pages // g
    full = lambda a: pl.BlockSpec(a.shape, lambda b, j, pt: (0,) * a.ndim)
    page_spec = lambda k: pl.BlockSpec((1, 256, LANES), lambda b, j, pt: (pt[b, j * g + k], 0, 0))
    return pl.pallas_call(
        functools.partial(_compress_body, 1, g, 1),
        grid_spec=pltpu.PrefetchScalarGridSpec(
            num_scalar_prefetch=1,
            grid=(nb, nsteps),
            in_specs=[page_spec(k) for k in range(g)] + [full(C["perm"]), full(C["wbd"]), full(C["bias"]), full(C["w2bd"])],
            out_specs=pl.BlockSpec((1, nh, 256), lambda b, j, pt: (b, 0, 0)),
            scratch_shapes=[pltpu.VMEM((CMP_STRIDE, nh, 256), F32)],
        ),
        out_shape=jax.ShapeDtypeStruct((nb, nh, 256), F32),
        compiler_params=_cparams(("arbitrary", "arbitrary")),
        name="compress_paged",
    )(page_table, *([pool] * g), C["perm"], C["wbd"], C["bias"], C["w2bd"])


def _alibi_slope(head):
    return float(2.0 ** (-8.0 * (head + 1) / NSA_HEADS)) * LOG2E


def _overlap_T(nc, ns):
    cst = np.arange(nc)[None, :] * CMP_STRIDE
    sst = np.arange(ns)[:, None] * SEL_BLK
    ov = np.clip(np.minimum(cst + CMP_BLK, sst + SEL_BLK) - np.maximum(cst, sst), 0, None).astype(np.float32) / CMP_BLK
    return jnp.asarray(ov, F32)


def _masked_softmax(s, mask):
    s = jnp.where(mask, s, NEG_INF)
    p = jnp.exp2(s - jnp.max(s, axis=-1, keepdims=True)) * mask.astype(F32)
    return p / jnp.maximum(jnp.sum(p, axis=-1, keepdims=True), 1e-30)


def _topk_mask(grp, k):
    ngrp = len(grp)
    cnt = [jnp.zeros(grp[0].shape, F32) for _ in range(ngrp)]
    sub = lax.broadcasted_iota(jnp.int32, grp[0].shape, 0)
    one, zero = jnp.float32(1.0), jnp.float32(0.0)
    for j in range(ngrp * SUBLANES):
        a, r = divmod(j, SUBLANES)
        row = grp[a][r:r + 1, :]
        for c in range(ngrp):
            if c < a:
                beats = jnp.where(row > grp[c], one, zero)
            elif c > a:
                beats = jnp.where(row >= grp[c], one, zero)
            else:
                beats = jnp.where(sub > r, jnp.where(row >= grp[c], one, zero), jnp.where(row > grp[c], one, zero))
            cnt[c] = cnt[c] + beats
    return [jnp.where(c < float(k), one, zero) for c in cnt]


def _select_blocks(imp, cur):
    ns = imp.shape[0]
    jrow = lax.broadcasted_iota(jnp.int32, imp.shape, 0)
    imp = jnp.where(jrow == cur, FORCE_CUR, jnp.where(jrow == 0, FORCE_SINK, imp))
    imp = jnp.where(jrow <= cur, imp, NEG_INF)
    assert ns % SUBLANES == 0
    grp = [imp[SUBLANES * a:SUBLANES * (a + 1)] for a in range(ns // SUBLANES)]
    return jnp.concatenate(_topk_mask(grp, N_SEL), axis=0)


def _cmp_attn_body(q_ref, kcvc_ref, ovT_ref, eye_ref, ocmp_ref, sel_ref):
    i = pl.program_id(1)
    tq = q_ref.shape[1]
    nh = kcvc_ref.shape[1]
    t0 = i * tq
    qpos_col = t0 + lax.broadcasted_iota(jnp.int32, (tq, 1), 0)
    cend = lax.broadcasted_iota(jnp.int32, (1, nh), 1) * CMP_STRIDE + (CMP_BLK - 1)
    mask = cend <= qpos_col
    dist = (qpos_col - cend).astype(F32)
    qpos_row = t0 + lax.broadcasted_iota(jnp.int32, (1, tq), 1)
    cur = qpos_row // SEL_BLK
    kcvc = kcvc_ref[0]
    for kvh in range(KV_HEADS):
        kc = kcvc[:, kvh * HEAD_DIM:(kvh + 1) * HEAD_DIM].astype(BF16)
        vc = kcvc[:, KV_DIM + kvh * HEAD_DIM:KV_DIM + (kvh + 1) * HEAD_DIM].astype(BF16)
        psum = jnp.zeros((tq, nh), F32)
        for g in range(GQA):
            head = kvh * GQA + g
            qg = q_ref[0, :, head * HEAD_DIM:(head + 1) * HEAD_DIM]
            s = _nt_dot(qg, kc) - _alibi_slope(head) * dist
            p = _masked_softmax(s, mask)
            ocmp_ref[0, :, head * HEAD_DIM:(head + 1) * HEAD_DIM] = jnp.dot(p.astype(BF16), vc, preferred_element_type=F32)
            psum = psum + p
        impT = lax.dot_general(ovT_ref[...], psum, (((1,), (1,)), ((), ())), preferred_element_type=F32,
                               precision=HIGHEST)
        selT = _select_blocks(impT, cur)
        sel = _nt_dot(eye_ref[...], selT.astype(BF16))
        sel_ref[0, kvh] = sel.astype(BF16)


def _cmp_attn_prompt(q, kcvc, tq):
    nb, t, _ = q.shape
    nh = kcvc.shape[1]
    ns = t // SEL_BLK
    ovT = _overlap_T(nh, ns)
    eye = jnp.eye(tq, dtype=BF16)
    return pl.pallas_call(
        _cmp_attn_body,
        grid=(nb, t // tq),
        in_specs=[pl.BlockSpec((1, tq, NSA_DIM), lambda b, i: (b, i, 0)),
                  pl.BlockSpec((1, nh, 256), lambda b, i: (b, 0, 0)),
                  pl.BlockSpec(ovT.shape, lambda b, i: (0, 0)),
                  pl.BlockSpec(eye.shape, lambda b, i: (0, 0))],
        out_specs=[pl.BlockSpec((1, tq, NSA_DIM), lambda b, i: (b, i, 0)),
                   pl.BlockSpec((1, KV_HEADS, tq, ns), lambda b, i: (b, 0, i, 0))],
        out_shape=[jax.ShapeDtypeStruct((nb, t, NSA_DIM), F32),
                   jax.ShapeDtypeStruct((nb, KV_HEADS, t, ns), BF16)],
        compiler_params=_cparams(("arbitrary", "arbitrary")),
        name="cmp_attn",
    )(q, kcvc, ovT, eye)


def _head_rows(qrow, kvh):
    rows = [qrow[:, (kvh * GQA + g) * HEAD_DIM:(kvh * GQA + g + 1) * HEAD_DIM] for g in range(GQA)]
    return jnp.concatenate(rows + [jnp.zeros((SUBLANES - GQA, HEAD_DIM), qrow.dtype)], axis=0)


def _slope_col(kvh):
    r = lax.broadcasted_iota(jnp.int32, (SUBLANES, 1), 0)
    col = jnp.zeros((SUBLANES, 1), F32)
    for g in range(GQA):
        col = jnp.where(r == g, _alibi_slope(kvh * GQA + g), col)
    return col


def _cmp_attn_sample_body(past, q_ref, kcvc_ref, ov_ref, ocmp_ref, imp_ref):
    nh = kcvc_ref.shape[1]
    nsl = ov_ref.shape[1]
    cend = lax.broadcasted_iota(jnp.int32, (1, nh), 1) * CMP_STRIDE + (CMP_BLK - 1)
    mask = cend <= past
    dist = (past - cend).astype(F32)
    kcvc = kcvc_ref[0]
    qrow = q_ref[0]
    for kvh in range(KV_HEADS):
        kc = kcvc[:, kvh * HEAD_DIM:(kvh + 1) * HEAD_DIM].astype(BF16)
        vc = kcvc[:, KV_DIM + kvh * HEAD_DIM:KV_DIM + (kvh + 1) * HEAD_DIM].astype(BF16)
        s = _nt_dot(_head_rows(qrow, kvh), kc) - _slope_col(kvh) * dist
        p = _masked_softmax(s, mask)
        o = jnp.dot(p.astype(BF16), vc, preferred_element_type=F32)
        for g in range(GQA):
            head = kvh * GQA + g
            ocmp_ref[0, :, head * HEAD_DIM:(head + 1) * HEAD_DIM] = o[g:g + 1, :]
        psum = jnp.broadcast_to(jnp.sum(p[0:GQA], axis=0, keepdims=True), (SUBLANES, nh))
        imp = jnp.dot(psum, ov_ref[...], preferred_element_type=F32, precision=HIGHEST)
        imp_ref[0, :, kvh * nsl:(kvh + 1) * nsl] = imp[0:1, :]


def _select_sample_body(past, ns_pad, imp_ref, sel_ref):
    nb = imp_ref.shape[0]
    nsl = imp_ref.shape[1] // KV_HEADS
    cur = jnp.full((1, nb), past // SEL_BLK, jnp.int32)
    for kvh in range(KV_HEADS):
        impT = jnp.transpose(imp_ref[:, kvh * nsl:(kvh + 1) * nsl])
        selT = _select_blocks(impT[0:ns_pad], cur)
        selT = jnp.concatenate([selT, jnp.zeros((nsl - ns_pad, nb), F32)], axis=0)
        sel_ref[:, kvh * nsl:(kvh + 1) * nsl] = jnp.transpose(selT).astype(jnp.int32)


def _cmp_attn_sample(q3, kcvc, past):
    nb = q3.shape[0]
    nh = kcvc.shape[1]
    ns = past // SEL_BLK + 1
    ns_pad = -(-ns // SUBLANES) * SUBLANES
    nsl = -(-ns // LANES) * LANES
    ov = jnp.pad(_overlap_T(nh, ns), ((0, nsl - ns), (0, 0))).T
    ocmp, imp = pl.pallas_call(
        functools.partial(_cmp_attn_sample_body, past),
        grid=(nb,),
        in_specs=[pl.BlockSpec((1, 1, NSA_DIM), lambda b: (b, 0, 0)),
                  pl.BlockSpec((1, nh, 256), lambda b: (b, 0, 0)),
                  pl.BlockSpec(ov.shape, lambda b: (0, 0))],
        out_specs=[pl.BlockSpec((1, 1, NSA_DIM), lambda b: (b, 0, 0)),
                   pl.BlockSpec((1, 1, KV_HEADS * nsl), lambda b: (b, 0, 0))],
        out_shape=[jax.ShapeDtypeStruct((nb, 1, NSA_DIM), F32),
                   jax.ShapeDtypeStruct((nb, 1, KV_HEADS * nsl), F32)],
        compiler_params=_cparams(("arbitrary",)),
        name="cmp_attn_sample",
    )(q3, kcvc, ov)
    sel = pl.pallas_call(
        functools.partial(_select_sample_body, past, ns_pad),
        out_shape=jax.ShapeDtypeStruct((nb, KV_HEADS * nsl), jnp.int32),
        name="select_sample",
    )(imp.reshape(nb, KV_HEADS * nsl))
    return ocmp, sel


N_SPLIT = 3


def _pos_rows(n, start=0):
    tab = np.zeros((HEAD_DIM, n), np.float32)
    k = start + np.arange(n)
    tab[0:N_SPLIT] = k // SEL_BLK
    tab[N_SPLIT:2 * N_SPLIT] = k % SEL_BLK
    return jnp.asarray(tab, BF16)


def _slope_rows():
    bf = lambda v: np.asarray(v, dtype=BF16).astype(np.float32)
    tab = np.zeros((KV_HEADS, SUBLANES, HEAD_DIM), np.float32)
    for k in range(KV_HEADS):
        for g in range(GQA):
            for c, val in enumerate((SEL_BLK * _alibi_slope(k * GQA + g), _alibi_slope(k * GQA + g))):
                rest = np.float32(val)
                for j in range(N_SPLIT):
                    piece = bf(rest)
                    tab[k, g, c * N_SPLIT + j] = piece
                    rest = np.float32(rest - piece)
    return jnp.asarray(tab, F32)


def _block_expand(ns, n):
    return jnp.asarray((np.arange(n)[None, :] // SEL_BLK == np.arange(ns)[:, None]).astype(np.float32), BF16)


def _flash_step(q4, kT_aug, vT, bias, m_scr, acc_scr):
    n, tk = q4.shape[0], kT_aug.shape[1]
    s = jnp.dot(q4, kT_aug, preferred_element_type=F32)
    rb = bias.shape[0]
    if rb in (1, n):
        s = s + bias
    else:
        s = (s.reshape(n // rb, rb, tk) + bias[None]).reshape(n, tk)
    m_old = m_scr[...]
    m_new = jnp.maximum(m_old, jnp.max(s, axis=-1, keepdims=True))
    alpha = jnp.exp2(m_old - m_new)
    p = jnp.exp2(s - jnp.concatenate([m_new] * (tk // LANES), axis=1))
    v_ones = jnp.concatenate([vT, jnp.ones((LANES - HEAD_DIM, tk), BF16)], axis=0)
    acc_scr[...] = alpha * acc_scr[...] + _nt_dot(p.astype(BF16), v_ones)
    m_scr[...] = m_new


def _flash_result(acc_scr):
    acc = acc_scr[...]
    return acc[:, :HEAD_DIM] / jnp.maximum(acc[:, HEAD_DIM:], 1e-30)


def _sel_win_body(tk, q_ref, sel_ref, g_ref, ocmp_ref, ksT_ref, vsT_ref, kwT_ref, vwT_ref, pos_ref, exp_ref, slope_ref,
                  o_ref, m_scr, acc_scr):
    i = pl.program_id(2)
    tq = q_ref.shape[1]
    t0 = i * tq
    qpos = t0 + lax.broadcasted_iota(jnp.int32, (tq, 1), 0)
    q4 = jnp.concatenate(
        [jnp.concatenate([q_ref[0, :, g * HEAD_DIM:(g + 1) * HEAD_DIM],
                          jnp.broadcast_to(slope_ref[0, g:g + 1, :], (tq, HEAD_DIM)).astype(BF16)], axis=1)
         for g in range(GQA)], axis=0)

    def reset():
        m_scr[...] = jnp.full(m_scr.shape, NEG_INF, F32)
        acc_scr[...] = jnp.zeros(acc_scr.shape, F32)

    def finish():
        return _flash_result(acc_scr)

    reset()
    sel = sel_ref[0, 0]

    def sel_step(kt, carry):
        k0 = pl.multiple_of(kt * tk, tk)
        kpos = k0 + lax.broadcasted_iota(jnp.int32, (1, tk), 1)
        chosen = jnp.dot(sel, exp_ref[:, pl.ds(k0, tk)], preferred_element_type=F32)
        bias = (jnp.where(kpos <= qpos, chosen, 0.0) - 1.0) * (-NEG_INF)
        kT_aug = jnp.concatenate([ksT_ref[0, :, pl.ds(k0, tk)].astype(BF16), pos_ref[:, pl.ds(k0, tk)]], axis=0)
        _flash_step(q4, kT_aug, vsT_ref[0, :, pl.ds(k0, tk)].astype(BF16), bias, m_scr, acc_scr)
        return carry

    lax.fori_loop(0, (t0 + tq + tk - 1) // tk, sel_step, 0)
    o_sel = finish()

    reset()
    wk = WINDOW + tq
    k0 = pl.multiple_of(jnp.maximum(t0 - WINDOW, 0), tq)
    dist = qpos - (k0 + lax.broadcasted_iota(jnp.int32, (1, wk), 1))
    bias = jnp.where(lax.bitcast_convert_type(dist, jnp.uint32) < jnp.uint32(WINDOW), 0.0, NEG_INF)
    kT_aug = jnp.concatenate([kwT_ref[0, :, pl.ds(k0, wk)].astype(BF16), pos_ref[:, pl.ds(k0, wk)]], axis=0)
    _flash_step(q4, kT_aug, vwT_ref[0, :, pl.ds(k0, wk)].astype(BF16), bias, m_scr, acc_scr)
    o_win = finish()
    gates = g_ref[0]
    for g in range(GQA):
        rows = slice(g * tq, (g + 1) * tq)
        cols = slice(g * HEAD_DIM, (g + 1) * HEAD_DIM)
        o = (gates[:, 3 * g:3 * g + 1] * ocmp_ref[0, :, cols] + gates[:, 3 * g + 1:3 * g + 2] * o_sel[rows]
             + gates[:, 3 * g + 2:3 * g + 3] * o_win[rows])
        o_ref[0, :, cols] = o.astype(o_ref.dtype)


def _sel_win_prompt(q, sel, gates, ocmp, ksT, kwT, tq, tk):
    nb, t, _ = q.shape
    ns = sel.shape[-1]
    assert t % tk == 0 and t % tq == 0 and WINDOW % tq == 0
    pos, expand, slopes = _pos_rows(t), _block_expand(ns, t), _slope_rows()
    grp = GQA * HEAD_DIM
    kv_spec = lambda which: pl.BlockSpec((1, HEAD_DIM, t), lambda b, k, i: (b, which * KV_HEADS + k, 0))
    return pl.pallas_call(
        functools.partial(_sel_win_body, tk),
        grid=(nb, KV_HEADS, t // tq),
        in_specs=[pl.BlockSpec((1, tq, grp), lambda b, k, i: (b, i, k)),
                  pl.BlockSpec((1, 1, tq, ns), lambda b, k, i: (b, k, i, 0)),
                  pl.BlockSpec((1, tq, LANES), lambda b, k, i: (b, i, k)),
                  pl.BlockSpec((1, tq, grp), lambda b, k, i: (b, i, k)),
                  kv_spec(0), kv_spec(1), kv_spec(0), kv_spec(1),
                  pl.BlockSpec(pos.shape, lambda b, k, i: (0, 0)),
                  pl.BlockSpec(expand.shape, lambda b, k, i: (0, 0)),
                  pl.BlockSpec((1, SUBLANES, HEAD_DIM), lambda b, k, i: (k, 0, 0))],
        out_specs=pl.BlockSpec((1, tq, grp), lambda b, k, i: (b, i, k)),
        out_shape=jax.ShapeDtypeStruct((nb, t, NSA_DIM), BF16),
        scratch_shapes=[pltpu.VMEM((GQA * tq, LANES), F32), pltpu.VMEM((GQA * tq, LANES), F32)],
        compiler_params=_cparams(("arbitrary", "arbitrary", "arbitrary")),
        name="sel_win_attn",
    )(q, sel, gates, ocmp, ksT, ksT, kwT, kwT, pos, expand, slopes)


def _sel_win_sample_body(past, g_pages, pt_ref, sel_ref, q_ref, g_ref, ocmp_ref, *refs):
    page_refs = refs[:g_pages]
    (win_ref, ksn_ref, kwn_ref, pos_ref, wpos_ref, slope_ref, o_ref, wout_ref, m_scr, acc_scr) = refs[g_pages:]
    b = pl.program_id(0)
    j = pl.program_id(1)
    ns_pad = sel_ref.shape[1] // KV_HEADS
    qrow = q_ref[0]
    q4 = [jnp.concatenate([_head_rows(qrow, k), slope_ref[k].astype(BF16)], axis=1) for k in range(KV_HEADS)]

    @pl.when(j == 0)
    def _():
        m_scr[...] = jnp.full(m_scr.shape, NEG_INF, F32)
        acc_scr[...] = jnp.zeros(acc_scr.shape, F32)

    width = g_pages * PAGE_SIZE
    lane_blk = lax.broadcasted_iota(jnp.int32, (1, width), 1) // SEL_BLK
    row0 = lax.broadcasted_iota(jnp.int32, (HEAD_DIM, 1), 0) < N_SPLIT
    blk0 = j * (width // SEL_BLK)
    pos = (pos_ref[...].astype(F32) + jnp.where(row0, blk0.astype(F32), 0.0)).astype(BF16)
    for k in range(KV_HEADS):
        kT = jnp.concatenate([r[0, k * HEAD_DIM:(k + 1) * HEAD_DIM, :] for r in page_refs], axis=1).astype(BF16)
        vT = jnp.concatenate([r[0, KV_DIM + k * HEAD_DIM:KV_DIM + (k + 1) * HEAD_DIM, :] for r in page_refs],
                             axis=1).astype(BF16)
        bias = jnp.full((1, width), NEG_INF, F32)
        for blk in range(width // SEL_BLK):
            chosen = sel_ref[b, k * ns_pad + blk0 + blk] > 0
            bias = jnp.where(lane_blk == blk, jnp.where(chosen, 0.0, NEG_INF), bias)
        _flash_step(q4[k], jnp.concatenate([kT, pos], axis=0), vT, bias, m_scr.at[k], acc_scr.at[k])

    @pl.when(j == pl.num_programs(1) - 1)
    def _():
        nb = ksn_ref.shape[2]
        pick = lax.broadcasted_iota(jnp.int32, (1, nb), 1) == b
        ks_new = jnp.sum(jnp.where(pick, ksn_ref[0], 0.0), axis=1, keepdims=True)
        kw_new = jnp.sum(jnp.where(pick, kwn_ref[0], 0.0), axis=1, keepdims=True)
        lane = lax.broadcasted_iota(jnp.int32, (1, LANES), 1)
        tile_new = jnp.where(lane == 0, ks_new, 0.0).astype(BF16)
        pos_new = jnp.where(row0 & (lane == 0), float(past // SEL_BLK), 0.0).astype(BF16)
        bias_new = jnp.where(lane == 0, 0.0, NEG_INF)
        wlane = lax.broadcasted_iota(jnp.int32, (1, win_ref.shape[2]), 1)
        wout = jnp.where(wlane == win_ref.shape[2] - 1, kw_new, pltpu.roll(win_ref[0], win_ref.shape[2] - 1, 1))
        wout_ref[0] = wout
        woutb = wout.astype(BF16)
        gates = g_ref[0]
        for k in range(KV_HEADS):
            ksl = slice(k * HEAD_DIM, (k + 1) * HEAD_DIM)
            vsl = slice(KV_DIM + k * HEAD_DIM, KV_DIM + (k + 1) * HEAD_DIM)
            _flash_step(q4[k], jnp.concatenate([tile_new[ksl], pos_new], axis=0), tile_new[vsl], bias_new,
                        m_scr.at[k], acc_scr.at[k])
            o_sel = _flash_result(acc_scr.at[k])
            m_scr[k] = jnp.full(m_scr.shape[1:], NEG_INF, F32)
            acc_scr[k] = jnp.zeros(acc_scr.shape[1:], F32)
            _flash_step(q4[k], jnp.concatenate([woutb[ksl], wpos_ref[...]], axis=0), woutb[vsl],
                        jnp.zeros((1, win_ref.shape[2]), F32), m_scr.at[k], acc_scr.at[k])
            o_win = _flash_result(acc_scr.at[k])
            for g in range(GQA):
                head = k * GQA + g
                cols = slice(head * HEAD_DIM, (head + 1) * HEAD_DIM)
                c0 = k * LANES + 3 * g
                o = (gates[:, c0:c0 + 1] * ocmp_ref[0, :, cols] + gates[:, c0 + 1:c0 + 2] * o_sel[g:g + 1, :]
                     + gates[:, c0 + 2:c0 + 3] * o_win[g:g + 1, :])
                o_ref[0, :, cols] = o.astype(o_ref.dtype)


def _sel_win_sample(q3, g3, ocmp, sel, pool_sel, page_table, win, ksT_new, kwT_new, past):
    nb, n_pages = page_table.shape
    wlen = win.shape[2]
    assert wlen == WINDOW and past >= WINDOW
    g = min(32, n_pages)
    pos, wpos, slopes = _pos_rows(g * PAGE_SIZE), _pos_rows(wlen, past - wlen + 1), _slope_rows()
    full = lambda a: pl.BlockSpec(a.shape, lambda b, j, pt, sl: (0,) * a.ndim)
    row = lambda n: pl.BlockSpec((1, 1, n), lambda b, j, pt, sl: (b, 0, 0))
    page_spec = lambda k: pl.BlockSpec((1, 256, PAGE_SIZE), lambda b, j, pt, sl: (pt[b, j * g + k], 0, 0))
    wspec = pl.BlockSpec((1, 256, wlen), lambda b, j, pt, sl: (b, 0, 0))
    return pl.pallas_call(
        functools.partial(_sel_win_sample_body, past, g),
        grid_spec=pltpu.PrefetchScalarGridSpec(
            num_scalar_prefetch=2,
            grid=(nb, n_pages // g),
            in_specs=[row(NSA_DIM), row(KV_HEADS * LANES), row(NSA_DIM)] + [page_spec(k) for k in range(g)]
                     + [wspec, full(ksT_new), full(kwT_new), full(pos), full(wpos), full(slopes)],
            out_specs=[row(NSA_DIM), wspec],
            scratch_shapes=[pltpu.VMEM((KV_HEADS, SUBLANES, LANES), F32), pltpu.VMEM((KV_HEADS, SUBLANES, LANES), F32)],
        ),
        out_shape=[jax.ShapeDtypeStruct((nb, 1, NSA_DIM), BF16), jax.ShapeDtypeStruct((nb, 256, wlen), F32)],
        compiler_params=_cparams(("arbitrary", "arbitrary")),
        name="sel_win_sample",
    )(page_table, sel, q3, g3, ocmp, *([pool_sel] * g), win, ksT_new, kwT_new, pos,
      wpos, slopes)


def _softplus(v):
    return jnp.maximum(v, 0.0) + jnp.log1p(jnp.exp(-jnp.abs(v)))


def _tn_dot(a, b):
    return lax.dot_general(a, b, (((0,), (0,)), ((), ())), preferred_element_type=F32)


def _prep_ssd(conv_w, conv_b, dt_bias, a_log, d_skip, ssd_norm):
    padl = lambda v: jnp.pad(v.reshape(1, -1), ((0, 0), (0, LANES - v.shape[0])))
    L = SSD_CHUNK
    tril = jnp.asarray(np.tril(np.ones((L, L), np.float32)))
    return dict(conv_w=conv_w, conv_b=conv_b.reshape(1, -1), dtb_row=padl(dt_bias), dtb_col=dt_bias.reshape(-1, 1),
                alog_row=padl(a_log), alog_col=a_log.reshape(-1, 1), dskip=padl(d_skip), norm=ssd_norm.reshape(1, -1),
                tril=tril, triu=tril.T)


def _ssd_chunk(u, z, dt, dtT, h_prev, P):
    a_row = -jnp.exp(P["alog_row"][...])
    a_col = -jnp.exp(P["alog_col"][...])
    acum = jnp.dot(P["tril"][...], dt * a_row, preferred_element_type=F32, precision=HIGHEST)
    acumT = jnp.dot(dtT * a_col, P["triu"][...], preferred_element_type=F32, precision=HIGHEST)
    L = u.shape[0]
    li = lax.broadcasted_iota(jnp.int32, (L, L), 0)
    si = lax.broadcasted_iota(jnp.int32, (L, L), 1)
    causal = li >= si
    gn = SSD_GROUPS * D_STATE
    ys, hs = [], []
    per = SSD_HEADS // SSD_GROUPS
    for g in range(SSD_GROUPS):
        bm = u[:, D_INNER + g * D_STATE:D_INNER + (g + 1) * D_STATE]
        cm = u[:, D_INNER + gn + g * D_STATE:D_INNER + gn + (g + 1) * D_STATE]
        bmb = bm.astype(BF16)
        cb = _nt_dot(cm.astype(BF16), bmb)
        for e in range(per):
            h = g * per + e
            ac = acum[:, h:h + 1]
            seg = ac - acumT[h:h + 1, :]
            decay = jnp.where(causal, jnp.exp(jnp.where(causal, seg, 0.0)), 0.0)
            xs = u[:, h * SSD_HEAD_DIM:(h + 1) * SSD_HEAD_DIM]
            xdt = xs * dt[:, h:h + 1]
            y = jnp.dot((cb * decay).astype(BF16), xdt.astype(BF16), preferred_element_type=F32)
            a_last = acum[L - 1:L, h:h + 1]
            st = _tn_dot((xdt * jnp.exp(a_last - ac)).astype(BF16), bmb)
            y = y + _nt_dot((cm * jnp.exp(ac)).astype(BF16), h_prev[h].astype(BF16))
            hs.append(jnp.exp(a_last) * h_prev[h] + st)
            ys.append(y + P["dskip"][:, h:h + 1] * xs)
    return ys, hs


def _ssd_finish(ys, z, norm_w):
    y = jnp.concatenate(ys, axis=1) * _silu(z)
    ms = jnp.mean(y * y, axis=-1, keepdims=True)
    return y * lax.rsqrt(ms + RMS_EPS) * norm_w


def _ssd_prompt_body(xbc_ref, z_ref, dt_ref, dtT_ref, cw_ref, cb_ref, dtbr_ref, dtbc_ref, alr_ref, alc_ref, dsk_ref,
                     nrm_ref, tril_ref, triu_ref, y_ref, hout_ref, xpad_scr, h_scr):
    c = pl.program_id(1)
    L = xbc_ref.shape[1]

    @pl.when(c == 0)
    def _():
        xpad_scr[0:SUBLANES, :] = jnp.zeros((SUBLANES, xpad_scr.shape[1]), F32)
        h_scr[...] = jnp.zeros(h_scr.shape, F32)

    xt = xbc_ref[0]
    xpad_scr[SUBLANES:SUBLANES + L, :] = xt
    conv = cb_ref[...] + xpad_scr[SUBLANES - (CONV_W - 1):SUBLANES - (CONV_W - 1) + L, :] * cw_ref[0:1, :]
    for k in range(1, CONV_W):
        o = SUBLANES - (CONV_W - 1) + k
        conv = conv + xpad_scr[o:o + L, :] * cw_ref[k:k + 1, :]
    xpad_scr[0:SUBLANES, :] = xt[L - SUBLANES:L, :]
    u = _silu(conv)
    dt = _softplus(dt_ref[0] + dtbr_ref[...])
    dtT = _softplus(dtT_ref[0] + dtbc_ref[...])
    P = dict(alog_row=alr_ref, alog_col=alc_ref, tril=tril_ref, triu=triu_ref, dskip=dsk_ref[...])
    ys, hs = _ssd_chunk(u, z_ref[0], dt, dtT, [h_scr[h] for h in range(SSD_HEADS)], P)
    for h in range(SSD_HEADS):
        h_scr[h] = hs[h]
    y_ref[0] = _ssd_finish(ys, z_ref[0], nrm_ref[...]).astype(y_ref.dtype)

    @pl.when(c == pl.num_programs(1) - 1)
    def _():
        hout_ref[0] = h_scr[...]


def _ssd_prompt(xbc, z, dt, dtT, SP):
    nb, t, cd = xbc.shape
    L = SSD_CHUNK
    assert t % L == 0
    full = lambda a: pl.BlockSpec(a.shape, lambda b, c: (0,) * a.ndim)
    names = ("conv_w", "conv_b", "dtb_row", "dtb_col", "alog_row", "alog_col", "dskip", "norm", "tril", "triu")
    ps = [SP[n] for n in names]
    return pl.pallas_call(
        _ssd_prompt_body,
        grid=(nb, t // L),
        in_specs=[pl.BlockSpec((1, L, cd), lambda b, c: (b, c, 0)),
                  pl.BlockSpec((1, L, D_INNER), lambda b, c: (b, c, 0)),
                  pl.BlockSpec((1, L, LANES), lambda b, c: (b, c, 0)),
                  pl.BlockSpec((1, SSD_HEADS, L), lambda b, c: (b, 0, c))] + [full(a) for a in ps],
        out_specs=[pl.BlockSpec((1, L, D_INNER), lambda b, c: (b, c, 0)),
                   pl.BlockSpec((1, SSD_HEADS, SSD_HEAD_DIM, D_STATE), lambda b, c: (b, 0, 0, 0))],
        out_shape=[jax.ShapeDtypeStruct((nb, t, D_INNER), BF16),
                   jax.ShapeDtypeStruct((nb, SSD_HEADS, SSD_HEAD_DIM, D_STATE), F32)],
        scratch_shapes=[pltpu.VMEM((SUBLANES + L, cd), F32), pltpu.VMEM((SSD_HEADS, SSD_HEAD_DIM, D_STATE), F32)],
        compiler_params=_cparams(("arbitrary", "arbitrary")),
        name="ssd_prompt",
    )(xbc, z, dt, dtT, *ps)


def _ssd_sample_body(cs_ref, xbc_ref, z_ref, dt_ref, h0_ref, cw_ref, cb_ref, dtb_ref, al_ref, dsk_ref, nrm_ref, eye_ref,
                     y_ref, cso_ref, h_ref):
    nseq = xbc_ref.shape[0]
    xn = xbc_ref[...]
    conv = cb_ref[...] + xn * cw_ref[CONV_W - 1:CONV_W, :]
    for k in range(CONV_W - 1):
        conv = conv + cs_ref[k] * cw_ref[k:k + 1, :]
        if k > 0:
            cso_ref[k - 1] = cs_ref[k]
    cso_ref[CONV_W - 2] = xn
    u = _silu(conv)
    dt = _softplus(dt_ref[...] + dtb_ref[...])
    decay = jnp.exp(dt * (-jnp.exp(al_ref[...])))
    eye = eye_ref[...]
    gn = SSD_GROUPS * D_STATE
    per = SSD_HEADS // SSD_GROUPS
    rows = []
    for s in range(nseq):
        ys = []
        for h in range(SSD_HEADS):
            g = h // per
            xs = u[s:s + 1, h * SSD_HEAD_DIM:(h + 1) * SSD_HEAD_DIM]
            bm = u[s:s + 1, D_INNER + g * D_STATE:D_INNER + (g + 1) * D_STATE]
            cm = u[s:s + 1, D_INNER + gn + g * D_STATE:D_INNER + gn + (g + 1) * D_STATE]
            xcol = jnp.sum(eye * xs, axis=1, keepdims=True)
            hn = decay[s:s + 1, h:h + 1] * h0_ref[s, h] + (dt[s:s + 1, h:h + 1] * xcol) * bm
            h_ref[s, h] = hn
            ycol = jnp.sum(hn * cm, axis=1, keepdims=True)
            ys.append(jnp.sum(eye * ycol, axis=0, keepdims=True) + dsk_ref[:, h:h + 1] * xs)
        rows.append(jnp.concatenate(ys, axis=1))
    y = jnp.concatenate(rows, axis=0) * _silu(z_ref[...])
    ms = jnp.mean(y * y, axis=-1, keepdims=True)
    y_ref[...] = (y * lax.rsqrt(ms + RMS_EPS) * nrm_ref[...]).astype(y_ref.dtype)


def _ssd_sample(conv_state, xbc, z, dt, h0, SP):
    nb, cd = xbc.shape
    ts = SUBLANES
    assert nb % ts == 0
    eye = jnp.eye(SSD_HEAD_DIM, dtype=F32)
    names = ("conv_w", "conv_b", "dtb_row", "alog_row", "dskip", "norm")
    ps = [SP[n] for n in names] + [eye]
    full = lambda a: pl.BlockSpec(a.shape, lambda i: (0,) * a.ndim)
    st = pl.BlockSpec((ts, SSD_HEADS, SSD_HEAD_DIM, D_STATE), lambda i: (i, 0, 0, 0))
    cs = pl.BlockSpec((CONV_W - 1, ts, cd), lambda i: (0, i, 0))
    row = lambda n: pl.BlockSpec((ts, n), lambda i: (i, 0))
    return pl.pallas_call(
        _ssd_sample_body,
        grid=(nb // ts,),
        in_specs=[cs, row(cd), row(D_INNER), row(LANES), st] + [full(a) for a in ps],
        out_specs=[row(D_INNER), cs, st],
        out_shape=[jax.ShapeDtypeStruct((nb, D_INNER), BF16),
                   jax.ShapeDtypeStruct((CONV_W - 1, nb, cd), F32),
                   jax.ShapeDtypeStruct(h0.shape, F32)],
        compiler_params=_cparams(("arbitrary",)),
        name="ssd_sample",
    )(conv_state, xbc, z, dt, h0, *ps)


def _pack_bf16_pairs(v):
    m = v.shape[1] // 2
    hi = pltpu.bitcast(v[:, :m].astype(BF16).astype(F32), jnp.uint32)
    lo = pltpu.bitcast(v[:, m:].astype(BF16).astype(F32), jnp.uint32)
    return hi | (lo >> 16)


def _unpack_pairs_f32(w):
    return pltpu.bitcast(w & jnp.uint32(0xFFFF0000), F32), pltpu.bitcast(w << 16, F32)


def _unpack_bf16_pairs(w):
    hi, lo = _unpack_pairs_f32(w)
    return hi.astype(BF16), lo.astype(BF16)


def _route(logitsT, bias_col):
    s = jax.nn.sigmoid(logitsT)
    sb = s + bias_col
    per = N_EXPERTS // N_EXPERT_GROUPS
    assert per == SUBLANES
    grp = [sb[per * a:per * (a + 1)] for a in range(N_EXPERT_GROUPS)]
    sub = lax.broadcasted_iota(jnp.int32, grp[0].shape, 0)
    gs = []
    for ga in grp:
        m1 = jnp.max(ga, axis=0, keepdims=True)
        first = jnp.min(jnp.where(ga == m1, sub, per), axis=0, keepdims=True)
        m2 = jnp.max(jnp.where(sub == first, NEG_INF, ga), axis=0, keepdims=True)
        gs.append(m1 + m2)
    gmask = _topk_mask([jnp.concatenate(gs, axis=0)], TOPK_GROUPS)[0]
    masked = [jnp.where(gmask[a:a + 1, :] > 0.5, grp[a], NEG_INF) for a in range(N_EXPERT_GROUPS)]
    sel = jnp.concatenate(_topk_mask(masked, TOP_K), axis=0)
    w = s * sel
    w = w / jnp.sum(w, axis=0, keepdims=True) * ROUTED_SCALE
    return sel, w


def _post_mix_body(x_ref, on_ref, ys_ref, g1_ref, sc_ref, sh_ref, nw_ref, wo_ref, rw_ref, rb_ref,
                   x1_ref, hp_ref, selT_ref, wT_ref, cnt_ref):
    first = (pl.program_id(0) == 0) & (pl.program_id(1) == 0)
    half = wo_ref.shape[0] // 2
    mix = (jnp.dot(on_ref[0], wo_ref[0:half, :], preferred_element_type=F32)
           + jnp.dot(ys_ref[0], wo_ref[half:, :], preferred_element_type=F32))
    x1 = x_ref[0] + g1_ref[0] * mix
    x1_ref[0] = x1
    ms = jnp.mean(x1 * x1, axis=-1, keepdims=True)
    h = x1 * lax.rsqrt(ms + RMS_EPS) * nw_ref[...]
    h = h * (1.0 + sc_ref[0]) + sh_ref[0]
    hp_ref[0] = _pack_bf16_pairs(h)
    logitsT = lax.dot_general(rw_ref[...], h, (((1,), (1,)), ((), ())), preferred_element_type=F32,
                              precision=HIGHEST)
    sel, w = _route(logitsT, rb_ref[...])
    selT_ref[...] = sel.astype(selT_ref.dtype)
    wT_ref[...] = w

    @pl.when(first)
    def _():
        cnt_ref[...] = jnp.zeros(cnt_ref.shape, F32)

    cnt_ref[...] += jnp.broadcast_to(jnp.sum(sel, axis=1, keepdims=True), cnt_ref.shape)


def _post_mix(x, o_nsa, y_ssd, g1, sc2, sh2, norm_w, w_out_b, router_wT, router_bias, tm):
    nb, t, d = x.shape
    mt = g1.shape[1]
    nt = t // tm
    assert t % tm == 0 and (mt == 1 or mt == t)
    if mt == 1:
        mod_spec = pl.BlockSpec((1, 1, d), lambda b, i: (b, 0, 0))
    else:
        mod_spec = pl.BlockSpec((1, tm, d), lambda b, i: (b, i, 0))
    row = lambda n: pl.BlockSpec((1, tm, n), lambda b, i: (b, i, 0))
    full = lambda a: pl.BlockSpec(a.shape, lambda b, i: (0,) * a.ndim)
    tok = lambda: pl.BlockSpec((N_EXPERTS, tm), lambda b, i: (0, b * nt + i))
    rb = router_bias.reshape(N_EXPERTS, 1)
    nw = norm_w.reshape(1, d)
    return pl.pallas_call(
        _post_mix_body,
        grid=(nb, nt),
        in_specs=[row(d), row(NSA_DIM), row(D_INNER), mod_spec, mod_spec, mod_spec, full(nw), full(w_out_b),
                  full(router_wT), full(rb)],
        out_specs=[row(d), row(d // 2), tok(), tok(), pl.BlockSpec((N_EXPERTS, LANES), lambda b, i: (0, 0))],
        out_shape=[jax.ShapeDtypeStruct((nb, t, d), F32),
                   jax.ShapeDtypeStruct((nb, t, d // 2), jnp.uint32),
                   jax.ShapeDtypeStruct((N_EXPERTS, nb * t), BF16),
                   jax.ShapeDtypeStruct((N_EXPERTS, nb * t), F32),
                   jax.ShapeDtypeStruct((N_EXPERTS, LANES), F32)],
        compiler_params=_cparams(("arbitrary", "arbitrary")),
        name="post_mix",
    )(x, o_nsa, y_ssd, g1, sc2, sh2, nw, w_out_b, router_wT, rb)


MOE_BLOCK_SHIFT = 9
MOE_BLOCK = 1 << MOE_BLOCK_SHIFT


def _moe_rows(n_tok):
    n_blocks = n_tok * TOP_K // MOE_BLOCK + N_EXPERTS
    n_blocks_pad = -(-n_blocks // LANES) * LANES
    return n_blocks, n_blocks_pad


def _plan_body(selT_ref, wT_ref, cnt_ref, triu_ref, tril_ref, eye_ref, dest_ref, w8_ref, be_ref, fill_ref,
               carry_scr, pstart_scr):
    step = pl.program_id(0)
    ne = N_EXPERTS

    @pl.when(step == 0)
    def _():
        cnt = cnt_ref[...]
        cnt_i = cnt.astype(jnp.int32)
        padded = (((cnt_i + (MOE_BLOCK - 1)) >> MOE_BLOCK_SHIFT) << MOE_BLOCK_SHIFT).astype(F32)
        pstart = jnp.dot(tril_ref[...].astype(F32), padded, preferred_element_type=F32, precision=HIGHEST)
        pstart_scr[...] = pstart
        carry_scr[...] = jnp.zeros(carry_scr.shape, F32)
        pend = pstart + padded
        nbp = be_ref.shape[1]
        starts = (lax.broadcasted_iota(jnp.int32, (1, nbp), 1) * MOE_BLOCK).astype(F32)
        below = jnp.where(pend[:, 0:1] <= starts, 1.0, 0.0)
        be_ref[...] = jnp.minimum(jnp.sum(below, axis=0, keepdims=True), float(ne - 1)).astype(jnp.int32)
        eye = eye_ref[...]
        to_row = lambda col: jnp.sum(col * eye, axis=0, keepdims=True)
        n_used = jnp.max(pend, axis=0, keepdims=True) * (1.0 / MOE_BLOCK)
        rows = jnp.concatenate([to_row(pstart + cnt), to_row(padded - cnt), n_used,
                                jnp.zeros((SUBLANES - 3, LANES), F32)], axis=0)
        fill_ref[...] = rows.astype(jnp.int32)

    sel = selT_ref[...]
    self32 = sel.astype(F32)
    rank = jnp.dot(sel, triu_ref[...], preferred_element_type=F32) + carry_scr[:, 0:1]
    carry_scr[...] += jnp.broadcast_to(jnp.sum(self32, axis=1, keepdims=True), carry_scr.shape)
    dest = pstart_scr[:, 0:1] + rank
    slot = jnp.dot(tril_ref[...], sel, preferred_element_type=F32)
    w = wT_ref[...]
    drows, wrows = [], []
    for k in range(TOP_K):
        pick = jnp.where(slot == float(k), self32, 0.0)
        drows.append(jnp.sum(pick * dest, axis=0, keepdims=True))
        wrows.append(jnp.sum(pick * w, axis=0, keepdims=True))
    dest_ref[...] = jnp.concatenate(drows, axis=0).astype(jnp.int32)
    w8_ref[...] = jnp.concatenate(wrows, axis=0)


def _moe_plan(selT, wT, cnt, tile):
    ne, n = selT.shape
    assert n % tile == 0
    _, nbp = _moe_rows(n)
    triu = jnp.asarray(np.triu(np.ones((tile, tile), np.float32), 1), BF16)
    tril = jnp.asarray(np.tril(np.ones((ne, ne), np.float32), -1), BF16)
    eye = jnp.asarray(np.eye(ne, LANES, dtype=np.float32))
    full = lambda a: pl.BlockSpec(a.shape, lambda i: (0,) * a.ndim)
    return pl.pallas_call(
        _plan_body,
        grid=(n // tile,),
        in_specs=[pl.BlockSpec((ne, tile), lambda i: (0, i)), pl.BlockSpec((ne, tile), lambda i: (0, i)),
                  full(cnt), full(triu), full(tril), full(eye)],
        out_specs=[pl.BlockSpec((TOP_K, tile), lambda i: (0, i)), pl.BlockSpec((TOP_K, tile), lambda i: (0, i)),
                   pl.BlockSpec((1, nbp), lambda i: (0, 0)), pl.BlockSpec((SUBLANES, LANES), lambda i: (0, 0))],
        out_shape=[jax.ShapeDtypeStruct((TOP_K, n), jnp.int32), jax.ShapeDtypeStruct((TOP_K, n), F32),
                   jax.ShapeDtypeStruct((1, nbp), jnp.int32), jax.ShapeDtypeStruct((SUBLANES, LANES), jnp.int32)],
        scratch_shapes=[pltpu.VMEM((ne, LANES), F32), pltpu.VMEM((ne, LANES), F32)],
        compiler_params=_cparams(("arbitrary",)),
        name="moe_plan",
    )(selT, wT, cnt, triu, tril, eye)


_FILL_PIECES = tuple(1 << s for s in reversed(range(MOE_BLOCK_SHIFT)))


def _fill_padding(fill_ref, xd_ref, zero_scr, zsem, wait):
    def per_expert(e, carry):
        start = fill_ref[0, e]
        n = fill_ref[1, e]
        head = n & (SUBLANES - 1)
        for r in range(SUBLANES - 1):
            @pl.when(r < head)
            def _():
                cp = pltpu.make_async_copy(zero_scr.at[pl.ds(0, 1)], xd_ref.at[pl.ds(start + r, 1)], zsem)
                cp.wait() if wait else cp.start()

        cur = start + head
        for p in _FILL_PIECES:
            if p < SUBLANES:
                continue
            hit = (n & p) != 0

            @pl.when(hit)
            def _():
                off = pl.multiple_of(cur, SUBLANES)
                cp = pltpu.make_async_copy(zero_scr.at[pl.ds(0, p)], xd_ref.at[pl.ds(off, p)], zsem)
                cp.wait() if wait else cp.start()

            cur = cur + jnp.where(hit, p, 0)
        return carry

    lax.fori_loop(0, N_EXPERTS, per_expert, 0)


def _dispatch_body(dest_ref, fill_ref, hp_ref, xd_ref, zero_scr, sem, zsem):
    step = pl.program_id(0)
    tile = hp_ref.shape[0]

    def row_copy(t, k):
        return pltpu.make_async_copy(hp_ref.at[pl.ds(t, 1)], xd_ref.at[pl.ds(dest_ref[k, t], 1)], sem)

    @pl.when(step == 0)
    def _():
        zero_scr[...] = jnp.zeros(zero_scr.shape, zero_scr.dtype)
        _fill_padding(fill_ref, xd_ref, zero_scr, zsem, False)
        _fill_padding(fill_ref, xd_ref, zero_scr, zsem, True)

    def issue(t, carry):
        for k in range(TOP_K):
            row_copy(t, k).start(priority=k % 2)
        return carry

    def drain(t, carry):
        for k in range(TOP_K):
            row_copy(t, k).wait()
        return carry

    lax.fori_loop(0, tile, issue, 0)
    lax.fori_loop(0, tile, drain, 0)


def _moe_dispatch(hp, dest8, fill, tile):
    n, m = hp.shape
    n_blocks, _ = _moe_rows(n)
    nr = n_blocks * MOE_BLOCK
    return pl.pallas_call(
        _dispatch_body,
        grid=(n // tile,),
        in_specs=[pl.BlockSpec((TOP_K, tile), lambda i: (0, i), memory_space=pltpu.SMEM),
                  pl.BlockSpec(memory_space=pltpu.SMEM),
                  pl.BlockSpec((tile, m), lambda i: (i, 0))],
        out_specs=pl.BlockSpec(memory_space=pl.ANY),
        out_shape=jax.ShapeDtypeStruct((nr, m), jnp.uint32),
        scratch_shapes=[pltpu.VMEM((_FILL_PIECES[0], m), jnp.uint32), pltpu.SemaphoreType.DMA(()),
                        pltpu.SemaphoreType.DMA(())],
        compiler_params=_cparams(("arbitrary",)),
        name="moe_dispatch",
    )(dest8, fill, hp)


def _swiglu_packed(xw, w1, w3, w2):
    xa, xb = _unpack_bf16_pairs(xw)
    half = xa.shape[1]
    mm = lambda w: (jnp.dot(xa, w[0:half, :], preferred_element_type=F32)
                    + jnp.dot(xb, w[half:, :], preferred_element_type=F32))
    hid = _silu(mm(w1)) * mm(w3)
    return jnp.dot(hid.astype(BF16), w2, preferred_element_type=F32)


def _experts_body(be_ref, nu_ref, xd_ref, w1_ref, w3_ref, w2_ref, yd_ref, w1_scr, w3_scr, w2_scr):
    i = pl.program_id(0)
    last = jnp.minimum(i, nu_ref[0] - 1)
    fresh = (i == 0) | (be_ref[last] != be_ref[jnp.maximum(last - 1, 0)])

    @pl.when(fresh)
    def _():
        w1_scr[...] = w1_ref[0].astype(BF16)
        w3_scr[...] = w3_ref[0].astype(BF16)
        w2_scr[...] = w2_ref[0].astype(BF16)

    @pl.when(i < nu_ref[0])
    def _():
        yd_ref[...] = _pack_bf16_pairs(_swiglu_packed(xd_ref[...], w1_scr[...], w3_scr[...], w2_scr[...]))

    @pl.when(i >= nu_ref[0])
    def _():
        yd_ref[...] = jnp.zeros(yd_ref.shape, yd_ref.dtype)


def _moe_experts(xd, block_e, n_used, w1b, w3b, w2b):
    nr, m = xd.shape
    n_blocks = nr // MOE_BLOCK
    d, f = w1b.shape[1:]
    clamp = lambda i, nu: jnp.minimum(i, nu[0] - 1)
    return pl.pallas_call(
        _experts_body,
        grid_spec=pltpu.PrefetchScalarGridSpec(
            num_scalar_prefetch=2,
            grid=(n_blocks,),
            in_specs=[pl.BlockSpec((MOE_BLOCK, m), lambda i, be, nu: (clamp(i, nu), 0)),
                      pl.BlockSpec((1, d, f), lambda i, be, nu: (be[clamp(i, nu)], 0, 0)),
                      pl.BlockSpec((1, d, f), lambda i, be, nu: (be[clamp(i, nu)], 0, 0)),
                      pl.BlockSpec((1, f, d), lambda i, be, nu: (be[clamp(i, nu)], 0, 0))],
            out_specs=pl.BlockSpec((MOE_BLOCK, m), lambda i, be, nu: (i, 0)),
            scratch_shapes=[pltpu.VMEM((d, f), BF16), pltpu.VMEM((d, f), BF16), pltpu.VMEM((f, d), BF16)],
        ),
        out_shape=jax.ShapeDtypeStruct((nr, m), jnp.uint32),
        compiler_params=_cparams(("arbitrary",)),
        name="moe_experts",
    )(block_e, n_used, xd, w1b, w3b, w2b)


def _combine_body(dest_ref, w8_ref, hp_ref, x1_ref, g2_ref, eye_ref, sw1_ref, sw3_ref, sw2_ref, nf_ref, yd_ref,
                  o_ref, ybuf, sem):
    tile = hp_ref.shape[1]

    def row_copy(t, k):
        return pltpu.make_async_copy(yd_ref.at[pl.ds(dest_ref[k, t], 1)], ybuf.at[k, pl.ds(t, 1)], sem)

    def issue(t, carry):
        for k in range(TOP_K):
            row_copy(t, k).start(priority=k % 2)
        return carry

    def drain(t, carry):
        for k in range(TOP_K):
            row_copy(t, k).wait()
        return carry

    lax.fori_loop(0, tile, issue, 0)
    shared = _swiglu_packed(hp_ref[0], sw1_ref[...], sw3_ref[...], sw2_ref[...])
    w_rows = lax.dot_general(eye_ref[...], w8_ref[...], (((1,), (1,)), ((), ())), preferred_element_type=F32,
                             precision=HIGHEST)
    lax.fori_loop(0, tile, drain, 0)
    half = ybuf.shape[2]
    acc_a = jnp.zeros((tile, half), F32)
    acc_b = jnp.zeros((tile, half), F32)
    for k in range(TOP_K):
        ya, yb = _unpack_pairs_f32(ybuf[k])
        wk = w_rows[:, k:k + 1]
        acc_a = acc_a + wk * ya
        acc_b = acc_b + wk * yb
    routed = jnp.concatenate([acc_a, acc_b], axis=1)
    x2 = x1_ref[0] + g2_ref[0] * (routed + shared)
    ms = jnp.mean(x2 * x2, axis=-1, keepdims=True)
    o_ref[0] = x2 * lax.rsqrt(ms + RMS_EPS) * nf_ref[...]


def _moe_combine(dest8, w8, hp, x1, g2, yd, sw1b, sw3b, sw2b, norm_f, tile):
    nb, t, d = x1.shape
    nt = t // tile
    mt = g2.shape[1]
    assert t % tile == 0 and (mt == 1 or mt == t)
    if mt == 1:
        mod_spec = pl.BlockSpec((1, 1, d), lambda b, i: (b, 0, 0))
    else:
        mod_spec = pl.BlockSpec((1, tile, d), lambda b, i: (b, i, 0))
    eye = jnp.eye(tile, dtype=F32)
    nf = norm_f.reshape(1, d)
    full = lambda a: pl.BlockSpec(a.shape, lambda b, i: (0,) * a.ndim)
    return pl.pallas_call(
        _combine_body,
        grid=(nb, nt),
        in_specs=[pl.BlockSpec((TOP_K, tile), lambda b, i: (0, b * nt + i), memory_space=pltpu.SMEM),
                  pl.BlockSpec((TOP_K, tile), lambda b, i: (0, b * nt + i)),
                  pl.BlockSpec((1, tile, d // 2), lambda b, i: (b, i, 0)),
                  pl.BlockSpec((1, tile, d), lambda b, i: (b, i, 0)),
                  mod_spec, full(eye), full(sw1b), full(sw3b), full(sw2b), full(nf),
                  pl.BlockSpec(memory_space=pl.ANY)],
        out_specs=pl.BlockSpec((1, tile, d), lambda b, i: (b, i, 0)),
        out_shape=jax.ShapeDtypeStruct((nb, t, d), F32),
        scratch_shapes=[pltpu.VMEM((TOP_K, tile, d // 2), jnp.uint32), pltpu.SemaphoreType.DMA(())],
        compiler_params=_cparams(("arbitrary", "arbitrary")),
        name="moe_combine",
    )(dest8, w8, hp, x1, g2, eye, sw1b, sw3b, sw2b, nf, yd)


SC_CHUNK = 128


def _sc_workers():
    info = plsc.get_sparse_core_info()
    return info.num_cores, info.num_subcores


def _sc_dispatch(hp, dest8, nr):
    n, m = hp.shape
    nc, nsub = _sc_workers()
    per_w = n // (nc * nsub)
    assert n % (nc * nsub * SC_CHUNK) == 0
    mesh = plsc.VectorSubcoreMesh(core_axis_name="c", subcore_axis_name="s")

    @functools.partial(
        pl.kernel, mesh=mesh, out_type=jax.ShapeDtypeStruct((nr, m), hp.dtype),
        scratch_types=[pltpu.VMEM((TOP_K, SC_CHUNK), jnp.int32), pltpu.VMEM((SC_CHUNK, m), hp.dtype),
                       pltpu.SemaphoreType.DMA])
    def scatter_rows(hp_hbm, dest_hbm, xd_hbm, idx_v, rows_v, sem):
        wid = lax.axis_index("s") * nc + lax.axis_index("c")

        @pl.loop(0, per_w // SC_CHUNK)
        def _(c):
            base = pl.multiple_of(wid * per_w + c * SC_CHUNK, SC_CHUNK)
            pltpu.sync_copy(hp_hbm.at[pl.ds(base, SC_CHUNK)], rows_v)
            pltpu.sync_copy(dest_hbm.at[:, pl.ds(base, SC_CHUNK)], idx_v)
            copies = [pltpu.async_copy(rows_v, xd_hbm.at[idx_v.at[k]], sem) for k in range(TOP_K)]
            for cp in copies:
                cp.wait()

    return scatter_rows(hp, dest8)


def _sc_gather(yd, dest8):
    _, m = yd.shape
    n = dest8.shape[1]
    nc, nsub = _sc_workers()
    per_w = n // (nc * nsub)
    assert n % (nc * nsub * SC_CHUNK) == 0
    mesh = plsc.VectorSubcoreMesh(core_axis_name="c", subcore_axis_name="s")

    @functools.partial(
        pl.kernel, mesh=mesh, out_type=jax.ShapeDtypeStruct((TOP_K, n, m), yd.dtype),
        scratch_types=[pltpu.VMEM((TOP_K, SC_CHUNK), jnp.int32), pltpu.VMEM((SC_CHUNK, m), yd.dtype),
                       pltpu.SemaphoreType.DMA])
    def gather_rows(yd_hbm, dest_hbm, out_hbm, idx_v, rows_v, sem):
        wid = lax.axis_index("s") * nc + lax.axis_index("c")

        @pl.loop(0, per_w // SC_CHUNK)
        def _(c):
            base = pl.multiple_of(wid * per_w + c * SC_CHUNK, SC_CHUNK)
            pltpu.sync_copy(dest_hbm.at[:, pl.ds(base, SC_CHUNK)], idx_v)
            for k in range(TOP_K):
                pltpu.async_copy(yd_hbm.at[idx_v.at[k]], rows_v, sem).wait()
                pltpu.sync_copy(rows_v, out_hbm.at[k, pl.ds(base, SC_CHUNK)])

    return gather_rows(yd, dest8)


def _fill_body(fill_ref, xd_in_ref, after_ref, xd_ref, zero_scr, zsem):
    del xd_in_ref, after_ref
    zero_scr[...] = jnp.zeros(zero_scr.shape, zero_scr.dtype)
    for wait in (False, True):
        _fill_padding(fill_ref, xd_ref, zero_scr, zsem, wait)


def _moe_fill(xd, fill, after):
    return pl.pallas_call(
        _fill_body,
        in_specs=[pl.BlockSpec(memory_space=pltpu.SMEM), pl.BlockSpec(memory_space=pl.ANY),
                  pl.BlockSpec(memory_space=pl.ANY)],
        out_specs=pl.BlockSpec(memory_space=pl.ANY),
        out_shape=jax.ShapeDtypeStruct(xd.shape, xd.dtype),
        scratch_shapes=[pltpu.VMEM((_FILL_PIECES[0], xd.shape[1]), xd.dtype), pltpu.SemaphoreType.DMA(())],
        input_output_aliases={1: 0},
        name="moe_fill",
    )(fill, xd, after)


def _combine_dense_body(w8_ref, hp_ref, x1_ref, g2_ref, eye_ref, sw1_ref, sw3_ref, sw2_ref, nf_ref, ybuf_ref, o_ref):
    tile = hp_ref.shape[1]
    shared = _swiglu_packed(hp_ref[0], sw1_ref[...], sw3_ref[...], sw2_ref[...])
    w_rows = lax.dot_general(eye_ref[...], w8_ref[...], (((1,), (1,)), ((), ())), preferred_element_type=F32,
                             precision=HIGHEST)
    half = ybuf_ref.shape[2]
    acc_a = jnp.zeros((tile, half), F32)
    acc_b = jnp.zeros((tile, half), F32)
    for k in range(TOP_K):
        ya, yb = _unpack_pairs_f32(ybuf_ref[k])
        wk = w_rows[:, k:k + 1]
        acc_a = acc_a + wk * ya
        acc_b = acc_b + wk * yb
    routed = jnp.concatenate([acc_a, acc_b], axis=1)
    x2 = x1_ref[0] + g2_ref[0] * (routed + shared)
    ms = jnp.mean(x2 * x2, axis=-1, keepdims=True)
    o_ref[0] = x2 * lax.rsqrt(ms + RMS_EPS) * nf_ref[...]


def _moe_combine_dense(w8, hp, x1, g2, ybuf, sw1b, sw3b, sw2b, norm_f, tile):
    nb, t, d = x1.shape
    nt = t // tile
    assert t % tile == 0 and g2.shape[1] == 1
    eye = jnp.eye(tile, dtype=F32)
    nf = norm_f.reshape(1, d)
    full = lambda a: pl.BlockSpec(a.shape, lambda b, i: (0,) * a.ndim)
    return pl.pallas_call(
        _combine_dense_body,
        grid=(nb, nt),
        in_specs=[pl.BlockSpec((TOP_K, tile), lambda b, i: (0, b * nt + i)),
                  pl.BlockSpec((1, tile, d // 2), lambda b, i: (b, i, 0)),
                  pl.BlockSpec((1, tile, d), lambda b, i: (b, i, 0)),
                  pl.BlockSpec((1, 1, d), lambda b, i: (b, 0, 0)),
                  full(eye), full(sw1b), full(sw3b), full(sw2b), full(nf),
                  pl.BlockSpec((TOP_K, tile, d // 2), lambda b, i: (0, b * nt + i, 0))],
        out_specs=pl.BlockSpec((1, tile, d), lambda b, i: (b, i, 0)),
        out_shape=jax.ShapeDtypeStruct((nb, t, d), F32),
        compiler_params=_cparams(("arbitrary", "arbitrary")),
        name="moe_combine_dense",
    )(w8, hp, x1, g2, eye, sw1b, sw3b, sw2b, nf, ybuf)


def _moe(x1, hp, selT, wT, cnt, g2, EW, norm_f, tile, on_sparsecore, after=None):
    nb, t, d = x1.shape
    n = nb * t
    dest8, w8, block_e, fill = _moe_plan(selT, wT, cnt, tile)
    n_blocks, _ = _moe_rows(n)
    hp2 = hp.reshape(n, d // 2)
    if on_sparsecore:
        xd = _moe_fill(_sc_dispatch(hp2, dest8, n_blocks * MOE_BLOCK), fill, after)
    else:
        xd = _moe_dispatch(hp2, dest8, fill, tile)
    yd = _moe_experts(xd, block_e[0, :n_blocks], fill[2, 0:1], EW["w1"], EW["w3"], EW["w2"])
    if on_sparsecore:
        return _moe_combine_dense(w8, hp, x1, g2, _sc_gather(yd, dest8), EW["sw1"], EW["sw3"], EW["sw2"], norm_f, tile)
    return _moe_combine(dest8, w8, hp, x1, g2, yd, EW["sw1"], EW["sw3"], EW["sw2"], norm_f, tile)


def kernel(x_prompt, x_sample, c_prompt, c_sample, cache_kv_cmp, cache_kv_sel, cache_kv_win, state_conv, state_ssm, page_table, w_ada, b_ada, norm_mix, norm_ffn, w_in, cmp_pe_k, cmp_w1_k, cmp_w2_k, cmp_pe_v, cmp_w1_v, cmp_w2_v, conv_w, conv_b, dt_bias, a_log, d_skip, ssd_norm, w_out, router_w, router_bias, exp_w1, exp_w3, exp_w2, sh_w1, sh_w3, sh_w2, norm_f):
    nb, t, d = x_prompt.shape
    ndb = x_sample.shape[0]
    c_all = jnp.concatenate([c_prompt, c_sample], axis=0)
    mod = _modulation(c_all, w_ada[0], b_ada[0]).reshape(nb + ndb, 6, d)
    mod_p = [mod[:nb, k][:, None, :] for k in range(6)]
    mod_s = [mod[nb:, k][None, :, :] for k in range(6)]
    W = _prep_w_in(w_in[0])
    P = _in_proj(x_prompt, mod_p[1], mod_p[0], norm_mix[0], W, 512)
    S = _in_proj(x_sample.reshape(1, ndb, d), mod_s[1], mod_s[0], norm_mix[0], W, ndb)
    C = _prep_compress(cmp_pe_k[0], cmp_w1_k[0], cmp_w2_k[0], cmp_pe_v[0], cmp_w1_v[0], cmp_w2_v[0])
    kcvc_p = _compress_prompt(P["kcT"], C)
    ocmp_p, sel_p = _cmp_attn_prompt(P["q"], kcvc_p, 256)
    o_nsa_p = _sel_win_prompt(P["q"], sel_p, P["g"], ocmp_p, P["ksT"], P["kwT"], 256, 512)
    SP = _prep_ssd(conv_w[0], conv_b[0], dt_bias[0], a_log[0], d_skip[0], ssd_norm[0])
    y_ssd_p, ssm_p = _ssd_prompt(P["xbc"], P["z"], P["dt"], P["dtT"], SP)
    w_out_b = w_out[0].astype(BF16)
    router_wT = router_w[0].T
    EW = dict(w1=exp_w1[0], w3=exp_w3[0], w2=exp_w2[0],
              sw1=sh_w1[0].astype(BF16), sw3=sh_w3[0].astype(BF16), sw2=sh_w2[0].astype(BF16))
    x1_p, hp_p, selT_p, wT_p, cnt_p = _post_mix(x_prompt, o_nsa_p, y_ssd_p, mod_p[2], mod_p[4], mod_p[3], norm_ffn[0],
                                                 w_out_b, router_wT, router_bias[0], 512)
    past = page_table.shape[1] * PAGE_SIZE
    to_pages = lambda c: jnp.transpose(c, (0, 2, 3, 4, 1)).reshape(c.shape[0], 256, c.shape[1])
    pool_cmp, pool_sel, win = to_pages(cache_kv_cmp[0]), to_pages(cache_kv_sel[0]), to_pages(cache_kv_win[0])
    kcvc_s = _compress_paged(pool_cmp, page_table, C)

    big = (nb * t) % (32 * SC_CHUNK) == 0
    y_prompt = _moe(x1_p, hp_p, selT_p, wT_p, cnt_p, mod_p[5], EW, norm_f, 512, big, after=kcvc_s)

    q3 = S["q"].reshape(ndb, 1, NSA_DIM)
    ocmp_s, sel_s = _cmp_attn_sample(q3, kcvc_s, past)
    o_nsa_s, win_s = _sel_win_sample(q3, S["g"].reshape(ndb, 1, KV_HEADS * LANES), ocmp_s, sel_s, pool_sel, page_table,
                                     win, S["ksT"], S["kwT"], past)
    y_ssd_s, conv_s, ssm_s = _ssd_sample(jnp.transpose(state_conv[0], (1, 0, 2)), S["xbc"][0], S["z"][0], S["dt"][0],
                                         state_ssm[0], SP)
    x1_s, hp_s, selT_s, wT_s, cnt_s = _post_mix(x_sample.reshape(1, ndb, d), o_nsa_s.reshape(1, ndb, NSA_DIM),
                                                 y_ssd_s.reshape(1, ndb, D_INNER), mod_s[2], mod_s[4], mod_s[3],
                                                 norm_ffn[0], w_out_b, router_wT, router_bias[0], ndb)
    y_sample = _moe(x1_s, hp_s, selT_s, wT_s, cnt_s, mod_s[5], EW, norm_f, ndb, False).reshape(ndb, 1, d)

    from_cm = lambda a: jnp.transpose(a.reshape(a.shape[0], 2, KV_HEADS, HEAD_DIM, a.shape[2]), (0, 4, 1, 2, 3))[None]
    tw = min(WINDOW, t)
    return (y_prompt, y_sample,
            from_cm(P["kcT"]), from_cm(P["ksT"]), from_cm(P["kwT"][:, :, t - tw:]),
            P["xbc"][:, t - (CONV_W - 1):, :][None], ssm_p[None],
            from_cm(jnp.transpose(S["kcT"], (2, 1, 0))), from_cm(jnp.transpose(S["ksT"], (2, 1, 0))), from_cm(win_s),
            jnp.transpose(conv_s, (1, 0, 2))[None], ssm_s[None])
```

```python
import functools

import jax
import jax.numpy as jnp
import numpy as np
from jax import lax
from jax.experimental import pallas as pl
from jax.experimental.pallas import tpu as pltpu
from jax.experimental.pallas import tpu_sc as plsc

F32 = jnp.float32
BF16 = jnp.bfloat16
HIGHEST = lax.Precision.HIGHEST

D_MODEL = 1024
NSA_HEADS = 8
KV_HEADS = 2
HEAD_DIM = 64
GQA = NSA_HEADS // KV_HEADS
CMP_BLK = 32
CMP_STRIDE = 16
CMP_HID = 64
SEL_BLK = 64
N_SEL = 16
WINDOW = 512
FORCE_CUR = 2.0e4
FORCE_SINK = 1.0e4
SSD_HEADS = 8
SSD_HEAD_DIM = 64
D_INNER = SSD_HEADS * SSD_HEAD_DIM
SSD_GROUPS = 2
D_STATE = 128
CONV_W = 4
CONV_DIM = D_INNER + 2 * SSD_GROUPS * D_STATE
SSD_CHUNK = 128
NSA_DIM = NSA_HEADS * HEAD_DIM
KV_DIM = KV_HEADS * HEAD_DIM
N_EXPERTS = 64
TOP_K = 8
N_EXPERT_GROUPS = 8
TOPK_GROUPS = 4
D_EXPERT = 256
D_SHARED = 256
ROUTED_SCALE = 2.5
RMS_EPS = 1e-6
NEG_INF = -1e30
LOG2E = 1.4426950408889634
PAGE_SIZE = 128

LANES = 128
SUBLANES = 8
VMEM_LIMIT = 56 * 1024 * 1024
ROW_TILE = 512
ATTN_Q_TILE = 256
ATTN_K_TILE = 512


def _cparams(sem):
    return pltpu.CompilerParams(dimension_semantics=sem, vmem_limit_bytes=VMEM_LIMIT)


def _silu(v):
    return v * jax.nn.sigmoid(v)


def _mod_body(c_ref, w_ref, b_ref, o_ref):
    s = _silu(c_ref[...])
    o_ref[...] = jnp.dot(s, w_ref[...], preferred_element_type=F32, precision=HIGHEST) + b_ref[...]


def _modulation(c_all, w_ada, b_ada):
    n, d = c_all.shape
    nout = w_ada.shape[1]
    tn = 512
    return pl.pallas_call(
        _mod_body,
        grid=(nout // tn,),
        in_specs=[pl.BlockSpec((n, d), lambda j: (0, 0)),
                  pl.BlockSpec((d, tn), lambda j: (0, j)),
                  pl.BlockSpec((1, tn), lambda j: (0, j))],
        out_specs=pl.BlockSpec((n, tn), lambda j: (0, j)),
        out_shape=jax.ShapeDtypeStruct((n, nout), F32),
        compiler_params=_cparams(("arbitrary",)),
        name="modulation",
    )(c_all, w_ada, b_ada.reshape(1, nout))


def _nt_dot(a, b):
    return lax.dot_general(a, b, (((1,), (1,)), ((), ())), preferred_element_type=F32)


def _proj_body(x_ref, sc_ref, sh_ref, nw_ref, wq_ref, wkvT_ref, wg_ref, wz_ref, wx_ref, wdt_ref, wdtT_ref,
               q_ref, kcT_ref, ksT_ref, kwT_ref, g_ref, z_ref, xbc_ref, dt_ref, dtT_ref):
    x = x_ref[0]
    ms = jnp.mean(x * x, axis=-1, keepdims=True)
    h = x * lax.rsqrt(ms + RMS_EPS) * nw_ref[...]
    h = h * (1.0 + sc_ref[0]) + sh_ref[0]
    hb = h.astype(BF16)
    q_ref[0] = (jnp.dot(hb, wq_ref[...], preferred_element_type=F32) * (HEAD_DIM ** -0.5 * LOG2E)).astype(BF16)
    kvT = _nt_dot(wkvT_ref[...], hb)
    kcT_ref[0] = kvT[0:256]
    ksT_ref[0] = kvT[256:512]
    kwT_ref[0] = kvT[512:768]
    g_ref[0] = jax.nn.sigmoid(jnp.dot(hb, wg_ref[...], preferred_element_type=F32))
    z_ref[0] = jnp.dot(hb, wz_ref[...], preferred_element_type=F32)
    xbc_ref[0] = jnp.dot(hb, wx_ref[...], preferred_element_type=F32)
    dt_ref[0] = jnp.dot(hb, wdt_ref[...], preferred_element_type=F32)
    dtT_ref[0] = _nt_dot(wdtT_ref[...], hb)


def _prep_w_in(w_in):
    w = w_in
    o = 0
    wq = w[:, o:o + NSA_DIM]; o += NSA_DIM
    wkv = w[:, o:o + 6 * KV_DIM]; o += 6 * KV_DIM
    wg = w[:, o:o + 3 * NSA_HEADS]; o += 3 * NSA_HEADS
    wz = w[:, o:o + D_INNER]; o += D_INNER
    wx = w[:, o:o + CONV_DIM]; o += CONV_DIM
    wdt = w[:, o:o + SSD_HEADS]; o += SSD_HEADS
    pad = lambda a: jnp.pad(a, ((0, 0), (0, LANES - a.shape[1])))
    per = 3 * GQA
    wg = jnp.concatenate([pad(wg[:, k * per:(k + 1) * per]) for k in range(KV_HEADS)], axis=1)
    return dict(wq=wq.astype(BF16), wkvT=wkv.T.astype(BF16), wg=wg.astype(BF16), wz=wz.astype(BF16),
                wx=wx.astype(BF16), wdt=pad(wdt).astype(BF16), wdtT=wdt.T.astype(BF16))


def _in_proj(x, sc, sh, norm_w, W, tm):
    nb, t, d = x.shape
    mt = sc.shape[1]
    assert t % tm == 0 and (mt == 1 or mt == t)
    if mt == 1:
        mod_spec = pl.BlockSpec((1, 1, d), lambda b, i: (b, 0, 0))
    else:
        mod_spec = pl.BlockSpec((1, tm, d), lambda b, i: (b, i, 0))
    full = lambda a: pl.BlockSpec(a.shape, lambda b, i: (0,) * a.ndim)
    row = lambda n: pl.BlockSpec((1, tm, n), lambda b, i: (b, i, 0))
    col = lambda n: pl.BlockSpec((1, n, tm), lambda b, i: (b, 0, i))
    ws = [W["wq"], W["wkvT"], W["wg"], W["wz"], W["wx"], W["wdt"], W["wdtT"]]
    outs = pl.pallas_call(
        _proj_body,
        grid=(nb, t // tm),
        in_specs=[row(d), mod_spec, mod_spec, pl.BlockSpec((1, d), lambda b, i: (0, 0))] + [full(a) for a in ws],
        out_specs=[row(NSA_DIM), col(256), col(256), col(256), row(KV_HEADS * LANES), row(D_INNER), row(CONV_DIM),
                   row(LANES), col(SSD_HEADS)],
        out_shape=[jax.ShapeDtypeStruct((nb, t, NSA_DIM), BF16),
                   jax.ShapeDtypeStruct((nb, 256, t), F32),
                   jax.ShapeDtypeStruct((nb, 256, t), F32),
                   jax.ShapeDtypeStruct((nb, 256, t), F32),
                   jax.ShapeDtypeStruct((nb, t, KV_HEADS * LANES), F32),
                   jax.ShapeDtypeStruct((nb, t, D_INNER), F32),
                   jax.ShapeDtypeStruct((nb, t, CONV_DIM), F32),
                   jax.ShapeDtypeStruct((nb, t, LANES), F32),
                   jax.ShapeDtypeStruct((nb, SSD_HEADS, t), F32)],
        compiler_params=_cparams(("arbitrary", "arbitrary")),
        name="in_proj",
    )(x, sc, sh, norm_w.reshape(1, d), *ws)
    names = ("q", "kcT", "ksT", "kwT", "g", "z", "xbc", "dt", "dtT")
    return dict(zip(names, outs))


def _prep_compress(cmp_pe_k, cmp_w1_k, cmp_w2_k, cmp_pe_v, cmp_w1_v, cmp_w2_v):
    w1s = jnp.stack([cmp_w1_k, cmp_w1_v]).reshape(2, 2, CMP_STRIDE, HEAD_DIM, CMP_HID)
    eye = jnp.eye(2, dtype=F32)
    wbd = jnp.einsum("ktldh,kK,vV->lkvdtKVh", w1s, eye, eye).reshape(CMP_STRIDE, 256, 512)
    w2s = jnp.stack([cmp_w2_k, cmp_w2_v])
    w2bd = jnp.einsum("khd,kK,vV->kvhKVd", w2s, eye, eye).reshape(256, 256)
    pes = jnp.stack([cmp_pe_k, cmp_pe_v]).reshape(2, 2, CMP_STRIDE, HEAD_DIM)
    pe_rows = jnp.broadcast_to(jnp.transpose(pes, (1, 2, 0, 3))[:, :, :, None, :],
                               (2, CMP_STRIDE, 2, KV_HEADS, HEAD_DIM)).reshape(2, CMP_STRIDE * 256)
    pecat = jnp.pad(pe_rows, ((0, SUBLANES - 2), (0, 0)))
    perm = np.zeros((LANES, LANES), np.float32)
    for l in range(CMP_STRIDE):
        for n in range(LANES // CMP_STRIDE):
            perm[(LANES // CMP_STRIDE) * l + n, CMP_STRIDE * n + l] = 1.0
    bias = pl.pallas_call(
        _cmp_bias_body,
        out_shape=jax.ShapeDtypeStruct((SUBLANES, 512), F32),
        name="cmp_bias",
    )(pecat, wbd.reshape(CMP_STRIDE * 256, 512))
    return dict(wbd=wbd.astype(BF16), w2bd=w2bd.astype(BF16), bias=bias, perm=jnp.asarray(perm, BF16))


def _cmp_bias_body(pe_ref, w_ref, o_ref):
    o_ref[...] = jnp.dot(pe_ref[...], w_ref[...], preferred_element_type=F32, precision=HIGHEST)


def _compress_body(n_pref, n_slab_refs, slabs_per_ref, *refs):
    refs = refs[n_pref:]
    slab_refs = refs[:n_slab_refs]
    perm_ref, wbd_ref, bias_ref, w2bd_ref, o_ref, z_scr = refs[n_slab_refs:]
    j = pl.program_id(1)
    nh = z_scr.shape[1]
    per_slab = LANES // CMP_STRIDE
    g_tot = n_slab_refs * slabs_per_ref
    base = pl.multiple_of(j * (per_slab * g_tot), per_slab)
    perm = perm_ref[...]
    for ri in range(n_slab_refs):
        for si in range(slabs_per_ref):
            slab = slab_refs[ri][0][:, si * LANES:(si + 1) * LANES].astype(BF16)
            xp = _nt_dot(perm, slab)
            g = ri * slabs_per_ref + si
            for l in range(CMP_STRIDE):
                z_scr[l, pl.ds(base + per_slab * g, per_slab), :] = xp[per_slab * l:per_slab * (l + 1), :]

    @pl.when(j == pl.num_programs(1) - 1)
    def _():
        acc = jnp.zeros((nh, 512), F32)
        for l in range(CMP_STRIDE):
            acc = acc + jnp.dot(z_scr[l].astype(BF16), wbd_ref[l], preferred_element_type=F32)
        lead = acc[:, :256] + bias_ref[0:1, :256]
        tail = acc[:, 256:] + bias_ref[1:2, 256:]
        hid = _silu(lead + pltpu.roll(tail, nh - 1, 0))
        out = jnp.dot(hid.astype(BF16), w2bd_ref[...], preferred_element_type=F32)
        row = lax.broadcasted_iota(jnp.int32, out.shape, 0)
        o_ref[0] = jnp.where(row < nh - 1, out, 0.0)


def _compress_prompt(kcT, C):
    nb, _, t = kcT.shape
    nh = t // CMP_STRIDE
    g = min(8, t // LANES)
    nsteps = t // (LANES * g)
    full = lambda a: pl.BlockSpec(a.shape, lambda b, j: (0,) * a.ndim)
    return pl.pallas_call(
        functools.partial(_compress_body, 0, 1, g),
        grid=(nb, nsteps),
        in_specs=[pl.BlockSpec((1, 256, LANES * g), lambda b, j: (b, 0, j)),
                  full(C["perm"]), full(C["wbd"]), full(C["bias"]), full(C["w2bd"])],
        out_specs=pl.BlockSpec((1, nh, 256), lambda b, j: (b, 0, 0)),
        out_shape=jax.ShapeDtypeStruct((nb, nh, 256), F32),
        scratch_shapes=[pltpu.VMEM((CMP_STRIDE, nh, 256), F32)],
        compiler_params=_cparams(("arbitrary", "arbitrary")),
        name="compress_prompt",
    )(kcT, C["perm"], C["wbd"], C["bias"], C["w2bd"])


def _compress_paged(pool, page_table, C):
    nb, n_pages = page_table.shape
    nh = n_pages * (PAGE_SIZE // CMP_STRIDE)
    g = min(16, n_pages)
    nsteps = n_pages // g
    full = lambda a: pl.BlockSpec(a.shape, lambda b, j, pt: (0,) * a.ndim)
    page_spec = lambda k: pl.BlockSpec((1, 256, LANES), lambda b, j, pt: (pt[b, j * g + k], 0, 0))
    return pl.pallas_call(
        functools.partial(_compress_body, 1, g, 1),
        grid_spec=pltpu.PrefetchScalarGridSpec(
            num_scalar_prefetch=1,
            grid=(nb, nsteps),
            in_specs=[page_spec(k) for k in range(g)] + [full(C["perm"]), full(C["wbd"]), full(C["bias"]), full(C["w2bd"])],
            out_specs=pl.BlockSpec((1, nh, 256), lambda b, j, pt: (b, 0, 0)),
            scratch_shapes=[pltpu.VMEM((CMP_STRIDE, nh, 256), F32)],
        ),
        out_shape=jax.ShapeDtypeStruct((nb, nh, 256), F32),
        compiler_params=_cparams(("arbitrary", "arbitrary")),
        name="compress_paged",
    )(page_table, *([pool] * g), C["perm"], C["wbd"], C["bias"], C["w2bd"])


def _alibi_slope(head):
    return float(2.0 ** (-8.0 * (head + 1) / NSA_HEADS)) * LOG2E


def _overlap_T(nc, ns):
    cst = np.arange(nc)[None, :] * CMP_STRIDE
    sst = np.arange(ns)[:, None] * SEL_BLK
    ov = np.clip(np.minimum(cst + CMP_BLK, sst + SEL_BLK) - np.maximum(cst, sst), 0, None).astype(np.float32) / CMP_BLK
    return jnp.asarray(ov, F32)


def _masked_softmax(s, mask):
    s = jnp.where(mask, s, NEG_INF)
    p = jnp.exp2(s - jnp.max(s, axis=-1, keepdims=True)) * mask.astype(F32)
    return p / jnp.maximum(jnp.sum(p, axis=-1, keepdims=True), 1e-30)


def _topk_mask(grp, k, by_rounds=False):
    ngrp = len(grp)
    sub = lax.broadcasted_iota(jnp.int32, grp[0].shape, 0)
    one, zero = jnp.float32(1.0), jnp.float32(0.0)
    if by_rounds:
        rowidx = [sub + SUBLANES * c for c in range(ngrp)]
        vals = list(grp)
        keep = [jnp.zeros(grp[0].shape, F32) for _ in range(ngrp)]
        for _ in range(k):
            top = functools.reduce(jnp.maximum, vals)
            top = jnp.max(top, axis=0, keepdims=True)
            first = functools.reduce(jnp.minimum, [jnp.where(v == top, r, ngrp * SUBLANES) for v, r in zip(vals, rowidx)])
            first = jnp.min(first, axis=0, keepdims=True)
            for c in range(ngrp):
                hit = rowidx[c] == first
                keep[c] = jnp.where(hit, one, keep[c])
                vals[c] = jnp.where(hit, -jnp.inf, vals[c])
        return keep
    cnt =[jnp.zeros(grp[0].shape, F32) for _ in range(ngrp)]
    for j in range(ngrp * SUBLANES):
        a, r = divmod(j, SUBLANES)
        row = grp[a][r:r + 1, :]
        for c in range(ngrp):
            if c < a:
                beats = jnp.where(row > grp[c], one, zero)
            elif c > a:
                beats = jnp.where(row >= grp[c], one, zero)
            else:
                beats = jnp.where(sub > r, jnp.where(row >= grp[c], one, zero), jnp.where(row > grp[c], one, zero))
            cnt[c] = cnt[c] + beats
    return [jnp.where(c < float(k), one, zero) for c in cnt]


def _select_blocks(imp, cur):
    ns = imp.shape[0]
    jrow = lax.broadcasted_iota(jnp.int32, imp.shape, 0)
    imp = jnp.where(jrow == cur, FORCE_CUR, jnp.where(jrow == 0, FORCE_SINK, imp))
    imp = jnp.where(jrow <= cur, imp, NEG_INF)
    assert ns % SUBLANES == 0
    grp = [imp[SUBLANES * a:SUBLANES * (a + 1)] for a in range(ns // SUBLANES)]
    return jnp.concatenate(_topk_mask(grp, N_SEL), axis=0)


def _cmp_attn_body(q_ref, kcvc_ref, ovT_ref, eye_ref, ocmp_ref, sel_ref):
    i = pl.program_id(1)
    tq = q_ref.shape[1]
    nh = kcvc_ref.shape[1]
    t0 = i * tq
    qpos_col = t0 + lax.broadcasted_iota(jnp.int32, (tq, 1), 0)
    cend = lax.broadcasted_iota(jnp.int32, (1, nh), 1) * CMP_STRIDE + (CMP_BLK - 1)
    mask = cend <= qpos_col
    dist = (qpos_col - cend).astype(F32)
    qpos_row = t0 + lax.broadcasted_iota(jnp.int32, (1, tq), 1)
    cur = qpos_row // SEL_BLK
    kcvc = kcvc_ref[0]
    for kvh in range(KV_HEADS):
        kc = kcvc[:, kvh * HEAD_DIM:(kvh + 1) * HEAD_DIM].astype(BF16)
        vc = kcvc[:, KV_DIM + kvh * HEAD_DIM:KV_DIM + (kvh + 1) * HEAD_DIM].astype(BF16)
        psum = jnp.zeros((tq, nh), F32)
        for g in range(GQA):
            head = kvh * GQA + g
            qg = q_ref[0, :, head * HEAD_DIM:(head + 1) * HEAD_DIM]
            s = _nt_dot(qg, kc) - _alibi_slope(head) * dist
            p = _masked_softmax(s, mask)
            ocmp_ref[0, :, head * HEAD_DIM:(head + 1) * HEAD_DIM] = jnp.dot(p.astype(BF16), vc, preferred_element_type=F32)
            psum = psum + p
        impT = lax.dot_general(ovT_ref[...], psum, (((1,), (1,)), ((), ())), preferred_element_type=F32,
                               precision=HIGHEST)
        selT = _select_blocks(impT, cur)
        sel = _nt_dot(eye_ref[...], selT.astype(BF16))
        sel_ref[0, kvh] = sel.astype(BF16)


def _cmp_attn_prompt(q, kcvc, tq):
    nb, t, _ = q.shape
    nh = kcvc.shape[1]
    ns = t // SEL_BLK
    ovT = _overlap_T(nh, ns)
    eye = jnp.eye(tq, dtype=BF16)
    return pl.pallas_call(
        _cmp_attn_body,
        grid=(nb, t // tq),
        in_specs=[pl.BlockSpec((1, tq, NSA_DIM), lambda b, i: (b, i, 0)),
                  pl.BlockSpec((1, nh, 256), lambda b, i: (b, 0, 0)),
                  pl.BlockSpec(ovT.shape, lambda b, i: (0, 0)),
                  pl.BlockSpec(eye.shape, lambda b, i: (0, 0))],
        out_specs=[pl.BlockSpec((1, tq, NSA_DIM), lambda b, i: (b, i, 0)),
                   pl.BlockSpec((1, KV_HEADS, tq, ns), lambda b, i: (b, 0, i, 0))],
        out_shape=[jax.ShapeDtypeStruct((nb, t, NSA_DIM), F32),
                   jax.ShapeDtypeStruct((nb, KV_HEADS, t, ns), BF16)],
        compiler_params=_cparams(("arbitrary", "arbitrary")),
        name="cmp_attn",
    )(q, kcvc, ovT, eye)


def _head_rows(qrow, kvh):
    rows = [qrow[:, (kvh * GQA + g) * HEAD_DIM:(kvh * GQA + g + 1) * HEAD_DIM] for g in range(GQA)]
    return jnp.concatenate(rows + [jnp.zeros((SUBLANES - GQA, HEAD_DIM), qrow.dtype)], axis=0)


def _slope_col(kvh):
    r = lax.broadcasted_iota(jnp.int32, (SUBLANES, 1), 0)
    col = jnp.zeros((SUBLANES, 1), F32)
    for g in range(GQA):
        col = jnp.where(r == g, _alibi_slope(kvh * GQA + g), col)
    return col


def _cmp_attn_sample_body(past, q_ref, kcvc_ref, ov_ref, ocmp_ref, imp_ref):
    nh = kcvc_ref.shape[1]
    nsl = ov_ref.shape[1]
    cend = lax.broadcasted_iota(jnp.int32, (1, nh), 1) * CMP_STRIDE + (CMP_BLK - 1)
    mask = cend <= past
    dist = (past - cend).astype(F32)
    kcvc = kcvc_ref[0]
    qrow = q_ref[0]
    for kvh in range(KV_HEADS):
        kc = kcvc[:, kvh * HEAD_DIM:(kvh + 1) * HEAD_DIM].astype(BF16)
        vc = kcvc[:, KV_DIM + kvh * HEAD_DIM:KV_DIM + (kvh + 1) * HEAD_DIM].astype(BF16)
        s = _nt_dot(_head_rows(qrow, kvh), kc) - _slope_col(kvh) * dist
        p = _masked_softmax(s, mask)
        o = jnp.dot(p.astype(BF16), vc, preferred_element_type=F32)
        for g in range(GQA):
            head = kvh * GQA + g
            ocmp_ref[0, :, head * HEAD_DIM:(head + 1) * HEAD_DIM] = o[g:g + 1, :]
        psum = jnp.broadcast_to(jnp.sum(p[0:GQA], axis=0, keepdims=True), (SUBLANES, nh))
        imp = jnp.dot(psum, ov_ref[...], preferred_element_type=F32, precision=HIGHEST)
        imp_ref[0, :, kvh * nsl:(kvh + 1) * nsl] = imp[0:1, :]


def _select_sample_body(past, ns_pad, imp_ref, sel_ref):
    nb = imp_ref.shape[0]
    nsl = imp_ref.shape[1] // KV_HEADS
    cur = jnp.full((1, nb), past // SEL_BLK, jnp.int32)
    for kvh in range(KV_HEADS):
        impT = jnp.transpose(imp_ref[:, kvh * nsl:(kvh + 1) * nsl])
        selT = _select_blocks(impT[0:ns_pad], cur)
        selT = jnp.concatenate([selT, jnp.zeros((nsl - ns_pad, nb), F32)], axis=0)
        sel_ref[:, kvh * nsl:(kvh + 1) * nsl] = jnp.transpose(selT).astype(jnp.int32)


def _cmp_attn_sample(q3, kcvc, past):
    nb = q3.shape[0]
    nh = kcvc.shape[1]
    ns = past // SEL_BLK + 1
    ns_pad = -(-ns // SUBLANES) * SUBLANES
    nsl = -(-ns // LANES) * LANES
    ov = jnp.pad(_overlap_T(nh, ns), ((0, nsl - ns), (0, 0))).T
    ocmp, imp = pl.pallas_call(
        functools.partial(_cmp_attn_sample_body, past),
        grid=(nb,),
        in_specs=[pl.BlockSpec((1, 1, NSA_DIM), lambda b: (b, 0, 0)),
                  pl.BlockSpec((1, nh, 256), lambda b: (b, 0, 0)),
                  pl.BlockSpec(ov.shape, lambda b: (0, 0))],
        out_specs=[pl.BlockSpec((1, 1, NSA_DIM), lambda b: (b, 0, 0)),
                   pl.BlockSpec((1, 1, KV_HEADS * nsl), lambda b: (b, 0, 0))],
        out_shape=[jax.ShapeDtypeStruct((nb, 1, NSA_DIM), F32),
                   jax.ShapeDtypeStruct((nb, 1, KV_HEADS * nsl), F32)],
        compiler_params=_cparams(("arbitrary",)),
        name="cmp_attn_sample",
    )(q3, kcvc, ov)
    sel = pl.pallas_call(
        functools.partial(_select_sample_body, past, ns_pad),
        out_shape=jax.ShapeDtypeStruct((nb, KV_HEADS * nsl), jnp.int32),
        name="select_sample",
    )(imp.reshape(nb, KV_HEADS * nsl))
    return ocmp, sel


N_SPLIT = 3


def _pos_rows(n, start=0):
    tab = np.zeros((HEAD_DIM, n), np.float32)
    k = start + np.arange(n)
    tab[0:N_SPLIT] = k // SEL_BLK
    tab[N_SPLIT:2 * N_SPLIT] = k % SEL_BLK
    return jnp.asarray(tab, BF16)


def _slope_rows():
    bf = lambda v: np.asarray(v, dtype=BF16).astype(np.float32)
    tab = np.zeros((KV_HEADS, SUBLANES, HEAD_DIM), np.float32)
    for k in range(KV_HEADS):
        for g in range(GQA):
            for c, val in enumerate((SEL_BLK * _alibi_slope(k * GQA + g), _alibi_slope(k * GQA + g))):
                rest = np.float32(val)
                for j in range(N_SPLIT):
                    piece = bf(rest)
                    tab[k, g, c * N_SPLIT + j] = piece
                    rest = np.float32(rest - piece)
    return jnp.asarray(tab, F32)


def _block_expand(ns, n):
    return jnp.asarray((np.arange(n)[None, :] // SEL_BLK == np.arange(ns)[:, None]).astype(np.float32), BF16)


def _flash_step(q4, kT_aug, vT, bias, m_scr, acc_scr):
    n, tk = q4.shape[0], kT_aug.shape[1]
    s = jnp.dot(q4, kT_aug, preferred_element_type=F32)
    rb = bias.shape[0]
    if rb in (1, n):
        s = s + bias
    else:
        s = (s.reshape(n // rb, rb, tk) + bias[None]).reshape(n, tk)
    m_old = m_scr[...]
    m_new = jnp.maximum(m_old, jnp.max(s, axis=-1, keepdims=True))
    alpha = jnp.exp2(m_old - m_new)
    p = jnp.exp2(s - jnp.concatenate([m_new] * (tk // LANES), axis=1))
    v_ones = jnp.concatenate([vT, jnp.ones((LANES - HEAD_DIM, tk), BF16)], axis=0)
    acc_scr[...] = alpha * acc_scr[...] + _nt_dot(p.astype(BF16), v_ones)
    m_scr[...] = m_new


def _flash_result(acc_scr):
    acc = acc_scr[...]
    return acc[:, :HEAD_DIM] / jnp.maximum(acc[:, HEAD_DIM:], 1e-30)


def _sel_win_body(tk, q_ref, sel_ref, g_ref, ocmp_ref, ksT_ref, vsT_ref, kwT_ref, vwT_ref, pos_ref, exp_ref, slope_ref,
                  o_ref, m_scr, acc_scr):
    i = pl.program_id(2)
    tq = q_ref.shape[1]
    t0 = i * tq
    qpos = t0 + lax.broadcasted_iota(jnp.int32, (tq, 1), 0)
    q4 = jnp.concatenate(
        [jnp.concatenate([q_ref[0, :, g * HEAD_DIM:(g + 1) * HEAD_DIM],
                          jnp.broadcast_to(slope_ref[0, g:g + 1, :], (tq, HEAD_DIM)).astype(BF16)], axis=1)
         for g in range(GQA)], axis=0)

    def reset():
        m_scr[...] = jnp.full(m_scr.shape, NEG_INF, F32)
        acc_scr[...] = jnp.zeros(acc_scr.shape, F32)

    def finish():
        return _flash_result(acc_scr)

    reset()
    sel = sel_ref[0, 0]

    def sel_step(kt, carry):
        k0 = pl.multiple_of(kt * tk, tk)
        kpos = k0 + lax.broadcasted_iota(jnp.int32, (1, tk), 1)
        chosen = jnp.dot(sel, exp_ref[:, pl.ds(k0, tk)], preferred_element_type=F32)
        bias = (jnp.where(kpos <= qpos, chosen, 0.0) - 1.0) * (-NEG_INF)
        kT_aug = jnp.concatenate([ksT_ref[0, :, pl.ds(k0, tk)].astype(BF16), pos_ref[:, pl.ds(k0, tk)]], axis=0)
        _flash_step(q4, kT_aug, vsT_ref[0, :, pl.ds(k0, tk)].astype(BF16), bias, m_scr, acc_scr)
        return carry

    lax.fori_loop(0, (t0 + tq + tk - 1) // tk, sel_step, 0)
    o_sel = finish()

    reset()
    wk = WINDOW + tq
    k0 = pl.multiple_of(jnp.maximum(t0 - WINDOW, 0), tq)
    dist = qpos - (k0 + lax.broadcasted_iota(jnp.int32, (1, wk), 1))
    bias = jnp.where(lax.bitcast_convert_type(dist, jnp.uint32) < jnp.uint32(WINDOW), 0.0, NEG_INF)
    kT_aug = jnp.concatenate([kwT_ref[0, :, pl.ds(k0, wk)].astype(BF16), pos_ref[:, pl.ds(k0, wk)]], axis=0)
    _flash_step(q4, kT_aug, vwT_ref[0, :, pl.ds(k0, wk)].astype(BF16), bias, m_scr, acc_scr)
    o_win = finish()
    gates = g_ref[0]
    for g in range(GQA):
        rows = slice(g * tq, (g + 1) * tq)
        cols = slice(g * HEAD_DIM, (g + 1) * HEAD_DIM)
        o = (gates[:, 3 * g:3 * g + 1] * ocmp_ref[0, :, cols] + gates[:, 3 * g + 1:3 * g + 2] * o_sel[rows]
             + gates[:, 3 * g + 2:3 * g + 3] * o_win[rows])
        o_ref[0, :, cols] = o.astype(o_ref.dtype)


def _sel_win_prompt(q, sel, gates, ocmp, ksT, kwT, tq, tk):
    nb, t, _ = q.shape
    ns = sel.shape[-1]
    assert t % tk == 0 and t % tq == 0 and WINDOW % tq == 0
    pos, expand, slopes = _pos_rows(t), _block_expand(ns, t), _slope_rows()
    grp = GQA * HEAD_DIM
    kv_spec = lambda which: pl.BlockSpec((1, HEAD_DIM, t), lambda b, k, i: (b, which * KV_HEADS + k, 0))
    return pl.pallas_call(
        functools.partial(_sel_win_body, tk),
        grid=(nb, KV_HEADS, t // tq),
        in_specs=[pl.BlockSpec((1, tq, grp), lambda b, k, i: (b, i, k)),
                  pl.BlockSpec((1, 1, tq, ns), lambda b, k, i: (b, k, i, 0)),
                  pl.BlockSpec((1, tq, LANES), lambda b, k, i: (b, i, k)),
                  pl.BlockSpec((1, tq, grp), lambda b, k, i: (b, i, k)),
                  kv_spec(0), kv_spec(1), kv_spec(0), kv_spec(1),
                  pl.BlockSpec(pos.shape, lambda b, k, i: (0, 0)),
                  pl.BlockSpec(expand.shape, lambda b, k, i: (0, 0)),
                  pl.BlockSpec((1, SUBLANES, HEAD_DIM), lambda b, k, i: (k, 0, 0))],
        out_specs=pl.BlockSpec((1, tq, grp), lambda b, k, i: (b, i, k)),
        out_shape=jax.ShapeDtypeStruct((nb, t, NSA_DIM), BF16),
        scratch_shapes=[pltpu.VMEM((GQA * tq, LANES), F32), pltpu.VMEM((GQA * tq, LANES), F32)],
        compiler_params=_cparams(("arbitrary", "arbitrary", "arbitrary")),
        name="sel_win_attn",
    )(q, sel, gates, ocmp, ksT, ksT, kwT, kwT, pos, expand, slopes)


def _sel_win_sample_body(past, g_pages, pt_ref, sel_ref, q_ref, g_ref, ocmp_ref, *refs):
    page_refs = refs[:g_pages]
    (win_ref, ksn_ref, kwn_ref, pos_ref, wpos_ref, slope_ref, o_ref, wout_ref, m_scr, acc_scr) = refs[g_pages:]
    b = pl.program_id(0)
    j = pl.program_id(1)
    ns_pad = sel_ref.shape[1] // KV_HEADS
    qrow = q_ref[0]
    q4 = [jnp.concatenate([_head_rows(qrow, k), slope_ref[k].astype(BF16)], axis=1) for k in range(KV_HEADS)]

    @pl.when(j == 0)
    def _():
        m_scr[...] = jnp.full(m_scr.shape, NEG_INF, F32)
        acc_scr[...] = jnp.zeros(acc_scr.shape, F32)

    width = g_pages * PAGE_SIZE
    lane_blk = lax.broadcasted_iota(jnp.int32, (1, width), 1) // SEL_BLK
    row0 = lax.broadcasted_iota(jnp.int32, (HEAD_DIM, 1), 0) < N_SPLIT
    blk0 = j * (width // SEL_BLK)
    pos = (pos_ref[...].astype(F32) + jnp.where(row0, blk0.astype(F32), 0.0)).astype(BF16)
    for k in range(KV_HEADS):
        kT = jnp.concatenate([r[0, k * HEAD_DIM:(k + 1) * HEAD_DIM, :] for r in page_refs], axis=1).astype(BF16)
        vT = jnp.concatenate([r[0, KV_DIM + k * HEAD_DIM:KV_DIM + (k + 1) * HEAD_DIM, :] for r in page_refs],
                             axis=1).astype(BF16)
        bias = jnp.full((1, width), NEG_INF, F32)
        for blk in range(width // SEL_BLK):
            chosen = sel_ref[b, k * ns_pad + blk0 + blk] > 0
            bias = jnp.where(lane_blk == blk, jnp.where(chosen, 0.0, NEG_INF), bias)
        _flash_step(q4[k], jnp.concatenate([kT, pos], axis=0), vT, bias, m_scr.at[k], acc_scr.at[k])

    @pl.when(j == pl.num_programs(1) - 1)
    def _():
        nb = ksn_ref.shape[2]
        pick = lax.broadcasted_iota(jnp.int32, (1, nb), 1) == b
        ks_new = jnp.sum(jnp.where(pick, ksn_ref[0], 0.0), axis=1, keepdims=True)
        kw_new = jnp.sum(jnp.where(pick, kwn_ref[0], 0.0), axis=1, keepdims=True)
        lane = lax.broadcasted_iota(jnp.int32, (1, LANES), 1)
        tile_new = jnp.where(lane == 0, ks_new, 0.0).astype(BF16)
        pos_new = jnp.where(row0 & (lane == 0), float(past // SEL_BLK), 0.0).astype(BF16)
        bias_new = jnp.where(lane == 0, 0.0, NEG_INF)
        wlane = lax.broadcasted_iota(jnp.int32, (1, win_ref.shape[2]), 1)
        wout = jnp.where(wlane == win_ref.shape[2] - 1, kw_new, pltpu.roll(win_ref[0], win_ref.shape[2] - 1, 1))
        wout_ref[0] = wout
        woutb = wout.astype(BF16)
        gates = g_ref[0]
        for k in range(KV_HEADS):
            ksl = slice(k * HEAD_DIM, (k + 1) * HEAD_DIM)
            vsl = slice(KV_DIM + k * HEAD_DIM, KV_DIM + (k + 1) * HEAD_DIM)
            _flash_step(q4[k], jnp.concatenate([tile_new[ksl], pos_new], axis=0), tile_new[vsl], bias_new,
                        m_scr.at[k], acc_scr.at[k])
            o_sel = _flash_result(acc_scr.at[k])
            m_scr[k] = jnp.full(m_scr.shape[1:], NEG_INF, F32)
            acc_scr[k] = jnp.zeros(acc_scr.shape[1:], F32)
            _flash_step(q4[k], jnp.concatenate([woutb[ksl], wpos_ref[...]], axis=0), woutb[vsl],
                        jnp.zeros((1, win_ref.shape[2]), F32), m_scr.at[k], acc_scr.at[k])
            o_win = _flash_result(acc_scr.at[k])
            for g in range(GQA):
                head = k * GQA + g
                cols = slice(head * HEAD_DIM, (head + 1) * HEAD_DIM)
                c0 = k * LANES + 3 * g
                o = (gates[:, c0:c0 + 1] * ocmp_ref[0, :, cols] + gates[:, c0 + 1:c0 + 2] * o_sel[g:g + 1, :]
                     + gates[:, c0 + 2:c0 + 3] * o_win[g:g + 1, :])
                o_ref[0, :, cols] = o.astype(o_ref.dtype)


def _sel_win_sample(q3, g3, ocmp, sel, pool_sel, page_table, win, ksT_new, kwT_new, past):
    nb, n_pages = page_table.shape
    wlen = win.shape[2]
    assert wlen == WINDOW and past >= WINDOW
    g = min(32, n_pages)
    pos, wpos, slopes = _pos_rows(g * PAGE_SIZE), _pos_rows(wlen, past - wlen + 1), _slope_rows()
    full = lambda a: pl.BlockSpec(a.shape, lambda b, j, pt, sl: (0,) * a.ndim)
    row = lambda n: pl.BlockSpec((1, 1, n), lambda b, j, pt, sl: (b, 0, 0))
    page_spec = lambda k: pl.BlockSpec((1, 256, PAGE_SIZE), lambda b, j, pt, sl: (pt[b, j * g + k], 0, 0))
    wspec = pl.BlockSpec((1, 256, wlen), lambda b, j, pt, sl: (b, 0, 0))
    return pl.pallas_call(
        functools.partial(_sel_win_sample_body, past, g),
        grid_spec=pltpu.PrefetchScalarGridSpec(
            num_scalar_prefetch=2,
            grid=(nb, n_pages // g),
            in_specs=[row(NSA_DIM), row(KV_HEADS * LANES), row(NSA_DIM)] + [page_spec(k) for k in range(g)]
                     + [wspec, full(ksT_new), full(kwT_new), full(pos), full(wpos), full(slopes)],
            out_specs=[row(NSA_DIM), wspec],
            scratch_shapes=[pltpu.VMEM((KV_HEADS, SUBLANES, LANES), F32), pltpu.VMEM((KV_HEADS, SUBLANES, LANES), F32)],
        ),
        out_shape=[jax.ShapeDtypeStruct((nb, 1, NSA_DIM), BF16), jax.ShapeDtypeStruct((nb, 256, wlen), F32)],
        compiler_params=_cparams(("arbitrary", "arbitrary")),
        name="sel_win_sample",
    )(page_table, sel, q3, g3, ocmp, *([pool_sel] * g), win, ksT_new, kwT_new, pos,
      wpos, slopes)


def _softplus(v):
    return jnp.maximum(v, 0.0) + jnp.log1p(jnp.exp(-jnp.abs(v)))


def _tn_dot(a, b):
    return lax.dot_general(a, b, (((0,), (0,)), ((), ())), preferred_element_type=F32)


def _prep_ssd(conv_w, conv_b, dt_bias, a_log, d_skip, ssd_norm):
    padl = lambda v: jnp.pad(v.reshape(1, -1), ((0, 0), (0, LANES - v.shape[0])))
    L = SSD_CHUNK
    tril = jnp.asarray(np.tril(np.ones((L, L), np.float32)))
    return dict(conv_w=conv_w, conv_b=conv_b.reshape(1, -1), dtb_row=padl(dt_bias), dtb_col=dt_bias.reshape(-1, 1),
                alog_row=padl(a_log), alog_col=a_log.reshape(-1, 1), dskip=padl(d_skip), norm=ssd_norm.reshape(1, -1),
                tril=tril, triu=tril.T)


def _ssd_chunk(u, z, dt, dtT, h_prev, P):
    a_row = -jnp.exp(P["alog_row"][...])
    a_col = -jnp.exp(P["alog_col"][...])
    acum = jnp.dot(P["tril"][...], dt * a_row, preferred_element_type=F32, precision=HIGHEST)
    acumT = jnp.dot(dtT * a_col, P["triu"][...], preferred_element_type=F32, precision=HIGHEST)
    L = u.shape[0]
    li = lax.broadcasted_iota(jnp.int32, (L, L), 0)
    si = lax.broadcasted_iota(jnp.int32, (L, L), 1)
    causal = li >= si
    gn = SSD_GROUPS * D_STATE
    ys, hs = [], []
    per = SSD_HEADS // SSD_GROUPS
    for g in range(SSD_GROUPS):
        bm = u[:, D_INNER + g * D_STATE:D_INNER + (g + 1) * D_STATE]
        cm = u[:, D_INNER + gn + g * D_STATE:D_INNER + gn + (g + 1) * D_STATE]
        bmb = bm.astype(BF16)
        cb = _nt_dot(cm.astype(BF16), bmb)
        for e in range(per):
            h = g * per + e
            ac = acum[:, h:h + 1]
            seg = ac - acumT[h:h + 1, :]
            decay = jnp.where(causal, jnp.exp(jnp.where(causal, seg, 0.0)), 0.0)
            xs = u[:, h * SSD_HEAD_DIM:(h + 1) * SSD_HEAD_DIM]
            xdt = xs * dt[:, h:h + 1]
            y = jnp.dot((cb * decay).astype(BF16), xdt.astype(BF16), preferred_element_type=F32)
            a_last = acum[L - 1:L, h:h + 1]
            st = _tn_dot((xdt * jnp.exp(a_last - ac)).astype(BF16), bmb)
            y = y + _nt_dot((cm * jnp.exp(ac)).astype(BF16), h_prev[h].astype(BF16))
            hs.append(jnp.exp(a_last) * h_prev[h] + st)
            ys.append(y + P["dskip"][:, h:h + 1] * xs)
    return ys, hs


def _ssd_finish(ys, z, norm_w):
    y = jnp.concatenate(ys, axis=1) * _silu(z)
    ms = jnp.mean(y * y, axis=-1, keepdims=True)
    return y * lax.rsqrt(ms + RMS_EPS) * norm_w


def _ssd_prompt_body(xbc_ref, z_ref, dt_ref, dtT_ref, cw_ref, cb_ref, dtbr_ref, dtbc_ref, alr_ref, alc_ref, dsk_ref,
                     nrm_ref, tril_ref, triu_ref, y_ref, hout_ref, xpad_scr, h_scr):
    c = pl.program_id(1)
    L = xbc_ref.shape[1]

    @pl.when(c == 0)
    def _():
        xpad_scr[0:SUBLANES, :] = jnp.zeros((SUBLANES, xpad_scr.shape[1]), F32)
        h_scr[...] = jnp.zeros(h_scr.shape, F32)

    xt = xbc_ref[0]
    xpad_scr[SUBLANES:SUBLANES + L, :] = xt
    conv = cb_ref[...] + xpad_scr[SUBLANES - (CONV_W - 1):SUBLANES - (CONV_W - 1) + L, :] * cw_ref[0:1, :]
    for k in range(1, CONV_W):
        o = SUBLANES - (CONV_W - 1) + k
        conv = conv + xpad_scr[o:o + L, :] * cw_ref[k:k + 1, :]
    xpad_scr[0:SUBLANES, :] = xt[L - SUBLANES:L, :]
    u = _silu(conv)
    dt = _softplus(dt_ref[0] + dtbr_ref[...])
    dtT = _softplus(dtT_ref[0] + dtbc_ref[...])
    P = dict(alog_row=alr_ref, alog_col=alc_ref, tril=tril_ref, triu=triu_ref, dskip=dsk_ref[...])
    ys, hs = _ssd_chunk(u, z_ref[0], dt, dtT, [h_scr[h] for h in range(SSD_HEADS)], P)
    for h in range(SSD_HEADS):
        h_scr[h] = hs[h]
    y_ref[0] = _ssd_finish(ys, z_ref[0], nrm_ref[...]).astype(y_ref.dtype)

    @pl.when(c == pl.num_programs(1) - 1)
    def _():
        hout_ref[0] = h_scr[...]


def _ssd_prompt(xbc, z, dt, dtT, SP):
    nb, t, cd = xbc.shape
    L = SSD_CHUNK
    assert t % L == 0
    full = lambda a: pl.BlockSpec(a.shape, lambda b, c: (0,) * a.ndim)
    names = ("conv_w", "conv_b", "dtb_row", "dtb_col", "alog_row", "alog_col", "dskip", "norm", "tril", "triu")
    ps = [SP[n] for n in names]
    return pl.pallas_call(
        _ssd_prompt_body,
        grid=(nb, t // L),
        in_specs=[pl.BlockSpec((1, L, cd), lambda b, c: (b, c, 0)),
                  pl.BlockSpec((1, L, D_INNER), lambda b, c: (b, c, 0)),
                  pl.BlockSpec((1, L, LANES), lambda b, c: (b, c, 0)),
                  pl.BlockSpec((1, SSD_HEADS, L), lambda b, c: (b, 0, c))] + [full(a) for a in ps],
        out_specs=[pl.BlockSpec((1, L, D_INNER), lambda b, c: (b, c, 0)),
                   pl.BlockSpec((1, SSD_HEADS, SSD_HEAD_DIM, D_STATE), lambda b, c: (b, 0, 0, 0))],
        out_shape=[jax.ShapeDtypeStruct((nb, t, D_INNER), BF16),
                   jax.ShapeDtypeStruct((nb, SSD_HEADS, SSD_HEAD_DIM, D_STATE), F32)],
        scratch_shapes=[pltpu.VMEM((SUBLANES + L, cd), F32), pltpu.VMEM((SSD_HEADS, SSD_HEAD_DIM, D_STATE), F32)],
        compiler_params=_cparams(("arbitrary", "arbitrary")),
        name="ssd_prompt",
    )(xbc, z, dt, dtT, *ps)


def _ssd_sample_body(cs_ref, xbc_ref, z_ref, dt_ref, h0_ref, cw_ref, cb_ref, dtb_ref, al_ref, dsk_ref, nrm_ref, eye_ref,
                     y_ref, cso_ref, h_ref):
    nseq = xbc_ref.shape[0]
    xn = xbc_ref[...]
    conv = cb_ref[...] + xn * cw_ref[CONV_W - 1:CONV_W, :]
    for k in range(CONV_W - 1):
        conv = conv + cs_ref[k] * cw_ref[k:k + 1, :]
        if k > 0:
            cso_ref[k - 1] = cs_ref[k]
    cso_ref[CONV_W - 2] = xn
    u = _silu(conv)
    dt = _softplus(dt_ref[...] + dtb_ref[...])
    decay = jnp.exp(dt * (-jnp.exp(al_ref[...])))
    eye = eye_ref[...]
    gn = SSD_GROUPS * D_STATE
    per = SSD_HEADS // SSD_GROUPS
    rows = []
    for s in range(nseq):
        ys = []
        for h in range(SSD_HEADS):
            g = h // per
            xs = u[s:s + 1, h * SSD_HEAD_DIM:(h + 1) * SSD_HEAD_DIM]
            bm = u[s:s + 1, D_INNER + g * D_STATE:D_INNER + (g + 1) * D_STATE]
            cm = u[s:s + 1, D_INNER + gn + g * D_STATE:D_INNER + gn + (g + 1) * D_STATE]
            xcol = jnp.sum(eye * xs, axis=1, keepdims=True)
            hn = decay[s:s + 1, h:h + 1] * h0_ref[s, h] + (dt[s:s + 1, h:h + 1] * xcol) * bm
            h_ref[s, h] = hn
            ycol = jnp.sum(hn * cm, axis=1, keepdims=True)
            ys.append(jnp.sum(eye * ycol, axis=0, keepdims=True) + dsk_ref[:, h:h + 1] * xs)
        rows.append(jnp.concatenate(ys, axis=1))
    y = jnp.concatenate(rows, axis=0) * _silu(z_ref[...])
    ms = jnp.mean(y * y, axis=-1, keepdims=True)
    y_ref[...] = (y * lax.rsqrt(ms + RMS_EPS) * nrm_ref[...]).astype(y_ref.dtype)


def _ssd_sample(conv_state, xbc, z, dt, h0, SP):
    nb, cd = xbc.shape
    ts = SUBLANES
    assert nb % ts == 0
    eye = jnp.eye(SSD_HEAD_DIM, dtype=F32)
    names = ("conv_w", "conv_b", "dtb_row", "alog_row", "dskip", "norm")
    ps = [SP[n] for n in names] + [eye]
    full = lambda a: pl.BlockSpec(a.shape, lambda i: (0,) * a.ndim)
    st = pl.BlockSpec((ts, SSD_HEADS, SSD_HEAD_DIM, D_STATE), lambda i: (i, 0, 0, 0))
    cs = pl.BlockSpec((CONV_W - 1, ts, cd), lambda i: (0, i, 0))
    row = lambda n: pl.BlockSpec((ts, n), lambda i: (i, 0))
    return pl.pallas_call(
        _ssd_sample_body,
        grid=(nb // ts,),
        in_specs=[cs, row(cd), row(D_INNER), row(LANES), st] + [full(a) for a in ps],
        out_specs=[row(D_INNER), cs, st],
        out_shape=[jax.ShapeDtypeStruct((nb, D_INNER), BF16),
                   jax.ShapeDtypeStruct((CONV_W - 1, nb, cd), F32),
                   jax.ShapeDtypeStruct(h0.shape, F32)],
        compiler_params=_cparams(("arbitrary",)),
        name="ssd_sample",
    )(conv_state, xbc, z, dt, h0, *ps)


def _pack_bf16_pairs(v):
    m = v.shape[1] // 2
    hi = pltpu.bitcast(v[:, :m].astype(BF16).astype(F32), jnp.uint32)
    lo = pltpu.bitcast(v[:, m:].astype(BF16).astype(F32), jnp.uint32)
    return hi | (lo >> 16)


def _unpack_pairs_f32(w):
    return pltpu.bitcast(w & jnp.uint32(0xFFFF0000), F32), pltpu.bitcast(w << 16, F32)


def _unpack_bf16_pairs(w):
    hi, lo = _unpack_pairs_f32(w)
    return hi.astype(BF16), lo.astype(BF16)


def _route(logitsT, bias_col):
    s = jax.nn.sigmoid(logitsT)
    sb = s + bias_col
    per = N_EXPERTS // N_EXPERT_GROUPS
    assert per == SUBLANES
    grp = [sb[per * a:per * (a + 1)] for a in range(N_EXPERT_GROUPS)]
    sub = lax.broadcasted_iota(jnp.int32, grp[0].shape, 0)
    gs = []
    for ga in grp:
        m1 = jnp.max(ga, axis=0, keepdims=True)
        first = jnp.min(jnp.where(ga == m1, sub, per), axis=0, keepdims=True)
        m2 = jnp.max(jnp.where(sub == first, NEG_INF, ga), axis=0, keepdims=True)
        gs.append(m1 + m2)
    gmask = _topk_mask([jnp.concatenate(gs, axis=0)], TOPK_GROUPS)[0]
    masked = [jnp.where(gmask[a:a + 1, :] > 0.5, grp[a], NEG_INF) for a in range(N_EXPERT_GROUPS)]
    sel = jnp.concatenate(_topk_mask(masked, TOP_K, by_rounds=True), axis=0)
    w = s * sel
    w = w / jnp.sum(w, axis=0, keepdims=True) * ROUTED_SCALE
    return sel, w


def _post_mix_body(x_ref, on_ref, ys_ref, g1_ref, sc_ref, sh_ref, nw_ref, wo_ref, rw_ref, rb_ref,
                   x1_ref, hp_ref, selT_ref, wT_ref, cnt_ref):
    first = (pl.program_id(0) == 0) & (pl.program_id(1) == 0)
    half = wo_ref.shape[0] // 2
    mix = (jnp.dot(on_ref[0], wo_ref[0:half, :], preferred_element_type=F32)
           + jnp.dot(ys_ref[0], wo_ref[half:, :], preferred_element_type=F32))
    x1 = x_ref[0] + g1_ref[0] * mix
    x1_ref[0] = x1
    ms = jnp.mean(x1 * x1, axis=-1, keepdims=True)
    h = x1 * lax.rsqrt(ms + RMS_EPS) * nw_ref[...]
    h = h * (1.0 + sc_ref[0]) + sh_ref[0]
    hp_ref[0] = _pack_bf16_pairs(h)
    logitsT = lax.dot_general(rw_ref[...], h, (((1,), (1,)), ((), ())), preferred_element_type=F32,
                              precision=HIGHEST)
    sel, w = _route(logitsT, rb_ref[...])
    selT_ref[...] = sel.astype(selT_ref.dtype)
    wT_ref[...] = w

    @pl.when(first)
    def _():
        cnt_ref[...] = jnp.zeros(cnt_ref.shape, F32)

    cnt_ref[...] += jnp.broadcast_to(jnp.sum(sel, axis=1, keepdims=True), cnt_ref.shape)


def _post_mix(x, o_nsa, y_ssd, g1, sc2, sh2, norm_w, w_out_b, router_wT, router_bias, tm):
    nb, t, d = x.shape
    mt = g1.shape[1]
    nt = t // tm
    assert t % tm == 0 and (mt == 1 or mt == t)
    if mt == 1:
        mod_spec = pl.BlockSpec((1, 1, d), lambda b, i: (b, 0, 0))
    else:
        mod_spec = pl.BlockSpec((1, tm, d), lambda b, i: (b, i, 0))
    row = lambda n: pl.BlockSpec((1, tm, n), lambda b, i: (b, i, 0))
    full = lambda a: pl.BlockSpec(a.shape, lambda b, i: (0,) * a.ndim)
    tok = lambda: pl.BlockSpec((N_EXPERTS, tm), lambda b, i: (0, b * nt + i))
    rb = router_bias.reshape(N_EXPERTS, 1)
    nw = norm_w.reshape(1, d)
    return pl.pallas_call(
        _post_mix_body,
        grid=(nb, nt),
        in_specs=[row(d), row(NSA_DIM), row(D_INNER), mod_spec, mod_spec, mod_spec, full(nw), full(w_out_b),
                  full(router_wT), full(rb)],
        out_specs=[row(d), row(d // 2), tok(), tok(), pl.BlockSpec((N_EXPERTS, LANES), lambda b, i: (0, 0))],
        out_shape=[jax.ShapeDtypeStruct((nb, t, d), F32),
                   jax.ShapeDtypeStruct((nb, t, d // 2), jnp.uint32),
                   jax.ShapeDtypeStruct((N_EXPERTS, nb * t), BF16),
                   jax.ShapeDtypeStruct((N_EXPERTS, nb * t), F32),
                   jax.ShapeDtypeStruct((N_EXPERTS, LANES), F32)],
        compiler_params=_cparams(("arbitrary", "arbitrary")),
        name="post_mix",
    )(x, o_nsa, y_ssd, g1, sc2, sh2, nw, w_out_b, router_wT, rb)


MOE_BLOCK_SHIFT = 10
MOE_BLOCK = 1 << MOE_BLOCK_SHIFT


def _moe_rows(n_tok):
    n_blocks = n_tok * TOP_K // MOE_BLOCK + N_EXPERTS
    n_blocks_pad = -(-n_blocks // LANES) * LANES
    return n_blocks, n_blocks_pad


def _plan_body(selT_ref, wT_ref, cnt_ref, triu_ref, tril_ref, eye_ref, dest_ref, w8_ref, be_ref, fill_ref,
               carry_scr, pstart_scr):
    step = pl.program_id(0)
    ne = N_EXPERTS

    @pl.when(step == 0)
    def _():
        cnt = cnt_ref[...]
        cnt_i = cnt.astype(jnp.int32)
        padded = (((cnt_i + (MOE_BLOCK - 1)) >> MOE_BLOCK_SHIFT) << MOE_BLOCK_SHIFT).astype(F32)
        pstart = jnp.dot(tril_ref[...].astype(F32), padded, preferred_element_type=F32, precision=HIGHEST)
        pstart_scr[...] = pstart
        carry_scr[...] = jnp.zeros(carry_scr.shape, F32)
        pend = pstart + padded
        nbp = be_ref.shape[1]
        starts = (lax.broadcasted_iota(jnp.int32, (1, nbp), 1) * MOE_BLOCK).astype(F32)
        below = jnp.where(pend[:, 0:1] <= starts, 1.0, 0.0)
        be_ref[...] = jnp.minimum(jnp.sum(below, axis=0, keepdims=True), float(ne - 1)).astype(jnp.int32)
        eye = eye_ref[...]
        to_row = lambda col: jnp.sum(col * eye, axis=0, keepdims=True)
        n_used = jnp.max(pend, axis=0, keepdims=True) * (1.0 / MOE_BLOCK)
        rows = jnp.concatenate([to_row(pstart + cnt), to_row(padded - cnt), n_used,
                                jnp.zeros((SUBLANES - 3, LANES), F32)], axis=0)
        fill_ref[...] = rows.astype(jnp.int32)

    sel = selT_ref[...]
    self32 = sel.astype(F32)
    rank = jnp.dot(sel, triu_ref[...], preferred_element_type=F32) + carry_scr[:, 0:1]
    carry_scr[...] += jnp.broadcast_to(jnp.sum(self32, axis=1, keepdims=True), carry_scr.shape)
    dest = pstart_scr[:, 0:1] + rank
    slot = jnp.dot(tril_ref[...], sel, preferred_element_type=F32)
    w = wT_ref[...]
    drows, wrows = [], []
    for k in range(TOP_K):
        pick = jnp.where(slot == float(k), self32, 0.0)
        drows.append(jnp.sum(pick * dest, axis=0, keepdims=True))
        wrows.append(jnp.sum(pick * w, axis=0, keepdims=True))
    dest_ref[...] = jnp.concatenate(drows, axis=0).astype(jnp.int32)
    w8_ref[...] = jnp.concatenate(wrows, axis=0)


def _moe_plan(selT, wT, cnt, tile):
    ne, n = selT.shape
    assert n % tile == 0
    _, nbp = _moe_rows(n)
    triu = jnp.asarray(np.triu(np.ones((tile, tile), np.float32), 1), BF16)
    tril = jnp.asarray(np.tril(np.ones((ne, ne), np.float32), -1), BF16)
    eye = jnp.asarray(np.eye(ne, LANES, dtype=np.float32))
    full = lambda a: pl.BlockSpec(a.shape, lambda i: (0,) * a.ndim)
    return pl.pallas_call(
        _plan_body,
        grid=(n // tile,),
        in_specs=[pl.BlockSpec((ne, tile), lambda i: (0, i)), pl.BlockSpec((ne, tile), lambda i: (0, i)),
                  full(cnt), full(triu), full(tril), full(eye)],
        out_specs=[pl.BlockSpec((TOP_K, tile), lambda i: (0, i)), pl.BlockSpec((TOP_K, tile), lambda i: (0, i)),
                   pl.BlockSpec((1, nbp), lambda i: (0, 0)), pl.BlockSpec((SUBLANES, LANES), lambda i: (0, 0))],
        out_shape=[jax.ShapeDtypeStruct((TOP_K, n), jnp.int32), jax.ShapeDtypeStruct((TOP_K, n), F32),
                   jax.ShapeDtypeStruct((1, nbp), jnp.int32), jax.ShapeDtypeStruct((SUBLANES, LANES), jnp.int32)],
        scratch_shapes=[pltpu.VMEM((ne, LANES), F32), pltpu.VMEM((ne, LANES), F32)],
        compiler_params=_cparams(("arbitrary",)),
        name="moe_plan",
    )(selT, wT, cnt, triu, tril, eye)


_FILL_PIECES = tuple(1 << s for s in reversed(range(MOE_BLOCK_SHIFT)))


def _fill_padding(fill_ref, xd_ref, zero_scr, zsem, wait):
    def per_expert(e, carry):
        start = fill_ref[0, e]
        n = fill_ref[1, e]
        head = n & (SUBLANES - 1)
        for r in range(SUBLANES - 1):
            @pl.when(r < head)
            def _():
                cp = pltpu.make_async_copy(zero_scr.at[pl.ds(0, 1)], xd_ref.at[pl.ds(start + r, 1)], zsem)
                cp.wait() if wait else cp.start()

        cur = start + head
        for p in _FILL_PIECES:
            if p < SUBLANES:
                continue
            hit = (n & p) != 0

            @pl.when(hit)
            def _():
                off = pl.multiple_of(cur, SUBLANES)
                cp = pltpu.make_async_copy(zero_scr.at[pl.ds(0, p)], xd_ref.at[pl.ds(off, p)], zsem)
                cp.wait() if wait else cp.start()

            cur = cur + jnp.where(hit, p, 0)
        return carry

    lax.fori_loop(0, N_EXPERTS, per_expert, 0)


def _dispatch_body(dest_ref, fill_ref, hp_ref, xd_ref, zero_scr, sem, zsem):
    step = pl.program_id(0)
    tile = hp_ref.shape[0]

    def row_copy(t, k):
        return pltpu.make_async_copy(hp_ref.at[pl.ds(t, 1)], xd_ref.at[pl.ds(dest_ref[k, t], 1)], sem)

    @pl.when(step == 0)
    def _():
        zero_scr[...] = jnp.zeros(zero_scr.shape, zero_scr.dtype)
        _fill_padding(fill_ref, xd_ref, zero_scr, zsem, False)
        _fill_padding(fill_ref, xd_ref, zero_scr, zsem, True)

    def issue(t, carry):
        for k in range(TOP_K):
            row_copy(t, k).start(priority=k % 2)
        return carry

    def drain(t, carry):
        for k in range(TOP_K):
            row_copy(t, k).wait()
        return carry

    lax.fori_loop(0, tile, issue, 0)
    lax.fori_loop(0, tile, drain, 0)


def _moe_dispatch(hp, dest8, fill, tile):
    n, m = hp.shape
    n_blocks, _ = _moe_rows(n)
    nr = n_blocks * MOE_BLOCK
    return pl.pallas_call(
        _dispatch_body,
        grid=(n // tile,),
        in_specs=[pl.BlockSpec((TOP_K, tile), lambda i: (0, i), memory_space=pltpu.SMEM),
                  pl.BlockSpec(memory_space=pltpu.SMEM),
                  pl.BlockSpec((tile, m), lambda i: (i, 0))],
        out_specs=pl.BlockSpec(memory_space=pl.ANY),
        out_shape=jax.ShapeDtypeStruct((nr, m), jnp.uint32),
        scratch_shapes=[pltpu.VMEM((_FILL_PIECES[0], m), jnp.uint32), pltpu.SemaphoreType.DMA(()),
                        pltpu.SemaphoreType.DMA(())],
        compiler_params=_cparams(("arbitrary",)),
        name="moe_dispatch",
    )(dest8, fill, hp)


def _swiglu_packed(xw, w1, w3, w2):
    xa, xb = _unpack_bf16_pairs(xw)
    half = xa.shape[1]
    mm = lambda w: (jnp.dot(xa, w[0:half, :], preferred_element_type=F32)
                    + jnp.dot(xb, w[half:, :], preferred_element_type=F32))
    hid = _silu(mm(w1)) * mm(w3)
    return jnp.dot(hid.astype(BF16), w2, preferred_element_type=F32)


def _experts_body(be_ref, nu_ref, xd_ref, w1_ref, w3_ref, w2_ref, yd_ref, w1_scr, w3_scr, w2_scr):
    i = pl.program_id(0)
    last = jnp.minimum(i, nu_ref[0] - 1)
    fresh = (i == 0) | (be_ref[last] != be_ref[jnp.maximum(last - 1, 0)])

    @pl.when(fresh)
    def _():
        w1_scr[...] = w1_ref[0].astype(BF16)
        w3_scr[...] = w3_ref[0].astype(BF16)
        w2_scr[...] = w2_ref[0].astype(BF16)

    @pl.when(i < nu_ref[0])
    def _():
        yd_ref[...] = _pack_bf16_pairs(_swiglu_packed(xd_ref[...], w1_scr[...], w3_scr[...], w2_scr[...]))

    @pl.when(i >= nu_ref[0])
    def _():
        yd_ref[...] = jnp.zeros(yd_ref.shape, yd_ref.dtype)


def _moe_experts(xd, block_e, n_used, w1b, w3b, w2b):
    nr, m = xd.shape
    n_blocks = nr // MOE_BLOCK
    d, f = w1b.shape[1:]
    clamp = lambda i, nu: jnp.minimum(i, nu[0] - 1)
    return pl.pallas_call(
        _experts_body,
        grid_spec=pltpu.PrefetchScalarGridSpec(
            num_scalar_prefetch=2,
            grid=(n_blocks,),
            in_specs=[pl.BlockSpec((MOE_BLOCK, m), lambda i, be, nu: (clamp(i, nu), 0)),
                      pl.BlockSpec((1, d, f), lambda i, be, nu: (be[clamp(i, nu)], 0, 0)),
                      pl.BlockSpec((1, d, f), lambda i, be, nu: (be[clamp(i, nu)], 0, 0)),
                      pl.BlockSpec((1, f, d), lambda i, be, nu: (be[clamp(i, nu)], 0, 0))],
            out_specs=pl.BlockSpec((MOE_BLOCK, m), lambda i, be, nu: (i, 0)),
            scratch_shapes=[pltpu.VMEM((d, f), BF16), pltpu.VMEM((d, f), BF16), pltpu.VMEM((f, d), BF16)],
        ),
        out_shape=jax.ShapeDtypeStruct((nr, m), jnp.uint32),
        compiler_params=_cparams(("arbitrary",)),
        name="moe_experts",
    )(block_e, n_used, xd, w1b, w3b, w2b)


def _combine_body(dest_ref, w8_ref, hp_ref, x1_ref, g2_ref, eye_ref, sw1_ref, sw3_ref, sw2_ref, nf_ref, yd_ref,
                  o_ref, ybuf, sem):
    tile = hp_ref.shape[1]

    def row_copy(t, k):
        return pltpu.make_async_copy(yd_ref.at[pl.ds(dest_ref[k, t], 1)], ybuf.at[k, pl.ds(t, 1)], sem)

    def issue(t, carry):
        for k in range(TOP_K):
            row_copy(t, k).start(priority=k % 2)
        return carry

    def drain(t, carry):
        for k in range(TOP_K):
            row_copy(t, k).wait()
        return carry

    lax.fori_loop(0, tile, issue, 0)
    shared = _swiglu_packed(hp_ref[0], sw1_ref[...], sw3_ref[...], sw2_ref[...])
    w_rows = lax.dot_general(eye_ref[...], w8_ref[...], (((1,), (1,)), ((), ())), preferred_element_type=F32,
                             precision=HIGHEST)
    lax.fori_loop(0, tile, drain, 0)
    half = ybuf.shape[2]
    acc_a = jnp.zeros((tile, half), F32)
    acc_b = jnp.zeros((tile, half), F32)
    for k in range(TOP_K):
        ya, yb = _unpack_pairs_f32(ybuf[k])
        wk = w_rows[:, k:k + 1]
        acc_a = acc_a + wk * ya
        acc_b = acc_b + wk * yb
    routed = jnp.concatenate([acc_a, acc_b], axis=1)
    x2 = x1_ref[0] + g2_ref[0] * (routed + shared)
    ms = jnp.mean(x2 * x2, axis=-1, keepdims=True)
    o_ref[0] = x2 * lax.rsqrt(ms + RMS_EPS) * nf_ref[...]


def _moe_combine(dest8, w8, hp, x1, g2, yd, sw1b, sw3b, sw2b, norm_f, tile):
    nb, t, d = x1.shape
    nt = t // tile
    mt = g2.shape[1]
    assert t % tile == 0 and (mt == 1 or mt == t)
    if mt == 1:
        mod_spec = pl.BlockSpec((1, 1, d), lambda b, i: (b, 0, 0))
    else:
        mod_spec = pl.BlockSpec((1, tile, d), lambda b, i: (b, i, 0))
    eye = jnp.eye(tile, dtype=F32)
    nf = norm_f.reshape(1, d)
    full = lambda a: pl.BlockSpec(a.shape, lambda b, i: (0,) * a.ndim)
    return pl.pallas_call(
        _combine_body,
        grid=(nb, nt),
        in_specs=[pl.BlockSpec((TOP_K, tile), lambda b, i: (0, b * nt + i), memory_space=pltpu.SMEM),
                  pl.BlockSpec((TOP_K, tile), lambda b, i: (0, b * nt + i)),
                  pl.BlockSpec((1, tile, d // 2), lambda b, i: (b, i, 0)),
                  pl.BlockSpec((1, tile, d), lambda b, i: (b, i, 0)),
                  mod_spec, full(eye), full(sw1b), full(sw3b), full(sw2b), full(nf),
                  pl.BlockSpec(memory_space=pl.ANY)],
        out_specs=pl.BlockSpec((1, tile, d), lambda b, i: (b, i, 0)),
        out_shape=jax.ShapeDtypeStruct((nb, t, d), F32),
        scratch_shapes=[pltpu.VMEM((TOP_K, tile, d // 2), jnp.uint32), pltpu.SemaphoreType.DMA(())],
        compiler_params=_cparams(("arbitrary", "arbitrary")),
        name="moe_combine",
    )(dest8, w8, hp, x1, g2, eye, sw1b, sw3b, sw2b, nf, yd)


SC_CHUNK = 128
SC_WORKERS = 32


def _sc_workers():
    info = plsc.get_sparse_core_info()
    assert info.num_cores * info.num_subcores == SC_WORKERS
    return info.num_cores, info.num_subcores


def _sc_dispatch(hp, dest8, nr):
    n, m = hp.shape
    nc, nsub = _sc_workers()
    per_w = n // (nc * nsub)
    assert n % (nc * nsub * SC_CHUNK) == 0
    mesh = plsc.VectorSubcoreMesh(core_axis_name="c", subcore_axis_name="s")

    @functools.partial(
        pl.kernel, mesh=mesh, out_type=jax.ShapeDtypeStruct((nr, m), hp.dtype),
        scratch_types=[pltpu.VMEM((TOP_K, SC_CHUNK), jnp.int32), pltpu.VMEM((SC_CHUNK, m), hp.dtype),
                       pltpu.SemaphoreType.DMA])
    def scatter_rows(hp_hbm, dest_hbm, xd_hbm, idx_v, rows_v, sem):
        wid = lax.axis_index("s") * nc + lax.axis_index("c")

        @pl.loop(0, per_w // SC_CHUNK)
        def _(c):
            base = pl.multiple_of(wid * per_w + c * SC_CHUNK, SC_CHUNK)
            pltpu.sync_copy(hp_hbm.at[pl.ds(base, SC_CHUNK)], rows_v)
            pltpu.sync_copy(dest_hbm.at[:, pl.ds(base, SC_CHUNK)], idx_v)
            copies = [pltpu.async_copy(rows_v, xd_hbm.at[idx_v.at[k]], sem) for k in range(TOP_K)]
            for cp in copies:
                cp.wait()

    return scatter_rows(hp, dest8)


def _sc_gather(yd, dest8):
    _, m = yd.shape
    n = dest8.shape[1]
    nc, nsub = _sc_workers()
    per_w = n // (nc * nsub)
    assert n % (nc * nsub * SC_CHUNK) == 0
    mesh = plsc.VectorSubcoreMesh(core_axis_name="c", subcore_axis_name="s")

    @functools.partial(
        pl.kernel, mesh=mesh, out_type=jax.ShapeDtypeStruct((TOP_K, n, m), yd.dtype),
        scratch_types=[pltpu.VMEM((TOP_K, SC_CHUNK), jnp.int32), pltpu.VMEM((SC_CHUNK, m), yd.dtype),
                       pltpu.SemaphoreType.DMA])
    def gather_rows(yd_hbm, dest_hbm, out_hbm, idx_v, rows_v, sem):
        wid = lax.axis_index("s") * nc + lax.axis_index("c")

        @pl.loop(0, per_w // SC_CHUNK)
        def _(c):
            base = pl.multiple_of(wid * per_w + c * SC_CHUNK, SC_CHUNK)
            pltpu.sync_copy(dest_hbm.at[:, pl.ds(base, SC_CHUNK)], idx_v)
            for k in range(TOP_K):
                pltpu.async_copy(yd_hbm.at[idx_v.at[k]], rows_v, sem).wait()
                pltpu.sync_copy(rows_v, out_hbm.at[k, pl.ds(base, SC_CHUNK)])

    return gather_rows(yd, dest8)


def _fill_body(fill_ref, xd_in_ref, after_ref, xd_ref, zero_scr, zsem):
    del xd_in_ref, after_ref
    zero_scr[...] = jnp.zeros(zero_scr.shape, zero_scr.dtype)
    for wait in (False, True):
        _fill_padding(fill_ref, xd_ref, zero_scr, zsem, wait)


def _moe_fill(xd, fill, after):
    return pl.pallas_call(
        _fill_body,
        in_specs=[pl.BlockSpec(memory_space=pltpu.SMEM), pl.BlockSpec(memory_space=pl.ANY),
                  pl.BlockSpec(memory_space=pl.ANY)],
        out_specs=pl.BlockSpec(memory_space=pl.ANY),
        out_shape=jax.ShapeDtypeStruct(xd.shape, xd.dtype),
        scratch_shapes=[pltpu.VMEM((_FILL_PIECES[0], xd.shape[1]), xd.dtype), pltpu.SemaphoreType.DMA(())],
        input_output_aliases={1: 0},
        name="moe_fill",
    )(fill, xd, after)


def _combine_dense_body(w8_ref, hp_ref, x1_ref, g2_ref, eye_ref, sw1_ref, sw3_ref, sw2_ref, nf_ref, ybuf_ref, o_ref):
    tile = hp_ref.shape[1]
    shared = _swiglu_packed(hp_ref[0], sw1_ref[...], sw3_ref[...], sw2_ref[...])
    w_rows = lax.dot_general(eye_ref[...], w8_ref[...], (((1,), (1,)), ((), ())), preferred_element_type=F32,
                             precision=HIGHEST)
    half = ybuf_ref.shape[2]
    acc_a = jnp.zeros((tile, half), F32)
    acc_b = jnp.zeros((tile, half), F32)
    for k in range(TOP_K):
        ya, yb = _unpack_pairs_f32(ybuf_ref[k])
        wk = w_rows[:, k:k + 1]
        acc_a = acc_a + wk * ya
        acc_b = acc_b + wk * yb
    routed = jnp.concatenate([acc_a, acc_b], axis=1)
    x2 = x1_ref[0] + g2_ref[0] * (routed + shared)
    ms = jnp.mean(x2 * x2, axis=-1, keepdims=True)
    o_ref[0] = x2 * lax.rsqrt(ms + RMS_EPS) * nf_ref[...]


def _moe_combine_dense(w8, hp, x1, g2, ybuf, sw1b, sw3b, sw2b, norm_f, tile):
    nb, t, d = x1.shape
    nt = t // tile
    assert t % tile == 0 and g2.shape[1] == 1
    eye = jnp.eye(tile, dtype=F32)
    nf = norm_f.reshape(1, d)
    full = lambda a: pl.BlockSpec(a.shape, lambda b, i: (0,) * a.ndim)
    return pl.pallas_call(
        _combine_dense_body,
        grid=(nb, nt),
        in_specs=[pl.BlockSpec((TOP_K, tile), lambda b, i: (0, b * nt + i)),
                  pl.BlockSpec((1, tile, d // 2), lambda b, i: (b, i, 0)),
                  pl.BlockSpec((1, tile, d), lambda b, i: (b, i, 0)),
                  pl.BlockSpec((1, 1, d), lambda b, i: (b, 0, 0)),
                  full(eye), full(sw1b), full(sw3b), full(sw2b), full(nf),
                  pl.BlockSpec((TOP_K, tile, d // 2), lambda b, i: (0, b * nt + i, 0))],
        out_specs=pl.BlockSpec((1, tile, d), lambda b, i: (b, i, 0)),
        out_shape=jax.ShapeDtypeStruct((nb, t, d), F32),
        compiler_params=_cparams(("arbitrary", "arbitrary")),
        name="moe_combine_dense",
    )(w8, hp, x1, g2, eye, sw1b, sw3b, sw2b, nf, ybuf)


def _moe(x1, hp, selT, wT, cnt, g2, EW, norm_f, tile, on_sparsecore, after=None):
    nb, t, d = x1.shape
    n = nb * t
    dest8, w8, block_e, fill = _moe_plan(selT, wT, cnt, tile)
    n_blocks, _ = _moe_rows(n)
    hp2 = hp.reshape(n, d // 2)
    if on_sparsecore:
        xd = _moe_fill(_sc_dispatch(hp2, dest8, n_blocks * MOE_BLOCK), fill, after)
    else:
        xd = _moe_dispatch(hp2, dest8, fill, tile)
    yd = _moe_experts(xd, block_e[0, :n_blocks], fill[2, 0:1], EW["w1"], EW["w3"], EW["w2"])
    if on_sparsecore:
        return _moe_combine_dense(w8, hp, x1, g2, _sc_gather(yd, dest8), EW["sw1"], EW["sw3"], EW["sw2"], norm_f, tile)
    return _moe_combine(dest8, w8, hp, x1, g2, yd, EW["sw1"], EW["sw3"], EW["sw2"], norm_f, tile)


def kernel(x_prompt, x_sample, c_prompt, c_sample, cache_kv_cmp, cache_kv_sel, cache_kv_win, state_conv, state_ssm, page_table, w_ada, b_ada, norm_mix, norm_ffn, w_in, cmp_pe_k, cmp_w1_k, cmp_w2_k, cmp_pe_v, cmp_w1_v, cmp_w2_v, conv_w, conv_b, dt_bias, a_log, d_skip, ssd_norm, w_out, router_w, router_bias, exp_w1, exp_w3, exp_w2, sh_w1, sh_w3, sh_w2, norm_f):
    nb, t, d = x_prompt.shape
    ndb = x_sample.shape[0]
    c_all = jnp.concatenate([c_prompt, c_sample], axis=0)
    mod = _modulation(c_all, w_ada[0], b_ada[0]).reshape(nb + ndb, 6, d)
    mod_p = [mod[:nb, k][:, None, :] for k in range(6)]
    mod_s = [mod[nb:, k][None, :, :] for k in range(6)]
    W = _prep_w_in(w_in[0])
    P = _in_proj(x_prompt, mod_p[1], mod_p[0], norm_mix[0], W, ROW_TILE)
    S = _in_proj(x_sample.reshape(1, ndb, d), mod_s[1], mod_s[0], norm_mix[0], W, ndb)
    C = _prep_compress(cmp_pe_k[0], cmp_w1_k[0], cmp_w2_k[0], cmp_pe_v[0], cmp_w1_v[0], cmp_w2_v[0])
    kcvc_p = _compress_prompt(P["kcT"], C)
    ocmp_p, sel_p = _cmp_attn_prompt(P["q"], kcvc_p, ATTN_Q_TILE)
    o_nsa_p = _sel_win_prompt(P["q"], sel_p, P["g"], ocmp_p, P["ksT"], P["kwT"], ATTN_Q_TILE, ATTN_K_TILE)
    SP = _prep_ssd(conv_w[0], conv_b[0], dt_bias[0], a_log[0], d_skip[0], ssd_norm[0])
    y_ssd_p, ssm_p = _ssd_prompt(P["xbc"], P["z"], P["dt"], P["dtT"], SP)
    w_out_b = w_out[0].astype(BF16)
    router_wT = router_w[0].T
    EW = dict(w1=exp_w1[0], w3=exp_w3[0], w2=exp_w2[0],
              sw1=sh_w1[0].astype(BF16), sw3=sh_w3[0].astype(BF16), sw2=sh_w2[0].astype(BF16))
    x1_p, hp_p, selT_p, wT_p, cnt_p = _post_mix(x_prompt, o_nsa_p, y_ssd_p, mod_p[2], mod_p[4], mod_p[3], norm_ffn[0],
                                                 w_out_b, router_wT, router_bias[0], ROW_TILE)
    past = page_table.shape[1] * PAGE_SIZE
    to_pages = lambda c: jnp.transpose(c, (0, 2, 3, 4, 1)).reshape(c.shape[0], 256, c.shape[1])
    pool_cmp, pool_sel, win = to_pages(cache_kv_cmp[0]), to_pages(cache_kv_sel[0]), to_pages(cache_kv_win[0])
    kcvc_s = _compress_paged(pool_cmp, page_table, C)

    big = (nb * t) % (SC_WORKERS * SC_CHUNK) == 0
    y_prompt = _moe(x1_p, hp_p, selT_p, wT_p, cnt_p, mod_p[5], EW, norm_f, ROW_TILE, big, after=kcvc_s)

    q3 = S["q"].reshape(ndb, 1, NSA_DIM)
    ocmp_s, sel_s = _cmp_attn_sample(q3, kcvc_s, past)
    o_nsa_s, win_s = _sel_win_sample(q3, S["g"].reshape(ndb, 1, KV_HEADS * LANES), ocmp_s, sel_s, pool_sel, page_table,
                                     win, S["ksT"], S["kwT"], past)
    y_ssd_s, conv_s, ssm_s = _ssd_sample(jnp.transpose(state_conv[0], (1, 0, 2)), S["xbc"][0], S["z"][0], S["dt"][0],
                                         state_ssm[0], SP)
    x1_s, hp_s, selT_s, wT_s, cnt_s = _post_mix(x_sample.reshape(1, ndb, d), o_nsa_s.reshape(1, ndb, NSA_DIM),
                                                 y_ssd_s.reshape(1, ndb, D_INNER), mod_s[2], mod_s[4], mod_s[3],
                                                 norm_ffn[0], w_out_b, router_wT, router_bias[0], ndb)
    y_sample = _moe(x1_s, hp_s, selT_s, wT_s, cnt_s, mod_s[5], EW, norm_f, ndb, False).reshape(ndb, 1, d)

    from_cm = lambda a: jnp.transpose(a.reshape(a.shape[0], 2, KV_HEADS, HEAD_DIM, a.shape[2]), (0, 4, 1, 2, 3))[None]
    tw = min(WINDOW, t)
    return (y_prompt, y_sample,
            from_cm(P["kcT"]), from_cm(P["ksT"]), from_cm(P["kwT"][:, :, t - tw:]),
            P["xbc"][:, t - (CONV_W - 1):, :][None], ssm_p[None],
            from_cm(jnp.transpose(S["kcT"], (2, 1, 0))), from_cm(jnp.transpose(S["ksT"], (2, 1, 0))), from_cm(win_s),
            jnp.transpose(conv_s, (1, 0, 2))[None], ssm_s[None])
```

```python
import functools

import jax
import jax.numpy as jnp
import numpy as np
from jax import lax
from jax.experimental import pallas as pl
from jax.experimental.pallas import tpu as pltpu
from jax.experimental.pallas import tpu_sc as plsc

F32 = jnp.float32
BF16 = jnp.bfloat16
HIGHEST = lax.Precision.HIGHEST

D_MODEL = 1024
NSA_HEADS = 8
KV_HEADS = 2
HEAD_DIM = 64
GQA = NSA_HEADS // KV_HEADS
CMP_BLK = 32
CMP_STRIDE = 16
CMP_HID = 64
SEL_BLK = 64
N_SEL = 16
WINDOW = 512
FORCE_CUR = 2.0e4
FORCE_SINK = 1.0e4
SSD_HEADS = 8
SSD_HEAD_DIM = 64
D_INNER = SSD_HEADS * SSD_HEAD_DIM
SSD_GROUPS = 2
D_STATE = 128
CONV_W = 4
CONV_DIM = D_INNER + 2 * SSD_GROUPS * D_STATE
SSD_CHUNK = 128
NSA_DIM = NSA_HEADS * HEAD_DIM
KV_DIM = KV_HEADS * HEAD_DIM
N_EXPERTS = 64
TOP_K = 8
N_EXPERT_GROUPS = 8
TOPK_GROUPS = 4
D_EXPERT = 256
D_SHARED = 256
ROUTED_SCALE = 2.5
RMS_EPS = 1e-6
NEG_INF = -1e30
LOG2E = 1.4426950408889634
PAGE_SIZE = 128

LANES = 128
SUBLANES = 8
VMEM_LIMIT = 56 * 1024 * 1024
ROW_TILE = 512
ATTN_Q_TILE = 256
ATTN_K_TILE = 512


def _cparams(sem):
    return pltpu.CompilerParams(dimension_semantics=sem, vmem_limit_bytes=VMEM_LIMIT)


def _silu(v):
    return v * jax.nn.sigmoid(v)


def _mod_body(c_ref, w_ref, b_ref, o_ref):
    s = _silu(c_ref[...])
    o_ref[...] = jnp.dot(s, w_ref[...], preferred_element_type=F32, precision=HIGHEST) + b_ref[...]


def _modulation(c_all, w_ada, b_ada):
    n, d = c_all.shape
    nout = w_ada.shape[1]
    tn = 512
    return pl.pallas_call(
        _mod_body,
        grid=(nout // tn,),
        in_specs=[pl.BlockSpec((n, d), lambda j: (0, 0)),
                  pl.BlockSpec((d, tn), lambda j: (0, j)),
                  pl.BlockSpec((1, tn), lambda j: (0, j))],
        out_specs=pl.BlockSpec((n, tn), lambda j: (0, j)),
        out_shape=jax.ShapeDtypeStruct((n, nout), F32),
        compiler_params=_cparams(("arbitrary",)),
        name="modulation",
    )(c_all, w_ada, b_ada.reshape(1, nout))


def _nt_dot(a, b):
    return lax.dot_general(a, b, (((1,), (1,)), ((), ())), preferred_element_type=F32)


def _proj_body(x_ref, sc_ref, sh_ref, nw_ref, wq_ref, wkvT_ref, wg_ref, wz_ref, wx_ref, wdt_ref, wdtT_ref,
               q_ref, kcT_ref, ksT_ref, kwT_ref, g_ref, z_ref, xbc_ref, dt_ref, dtT_ref):
    x = x_ref[0]
    ms = jnp.mean(x * x, axis=-1, keepdims=True)
    h = x * lax.rsqrt(ms + RMS_EPS) * nw_ref[...]
    h = h * (1.0 + sc_ref[0]) + sh_ref[0]
    hb = h.astype(BF16)
    q_ref[0] = (jnp.dot(hb, wq_ref[...], preferred_element_type=F32) * (HEAD_DIM ** -0.5 * LOG2E)).astype(BF16)
    kvT = _nt_dot(wkvT_ref[...], hb)
    kcT_ref[0] = kvT[0:256]
    ksT_ref[0] = kvT[256:512]
    kwT_ref[0] = kvT[512:768]
    g_ref[0] = jax.nn.sigmoid(jnp.dot(hb, wg_ref[...], preferred_element_type=F32))
    z_ref[0] = jnp.dot(hb, wz_ref[...], preferred_element_type=F32)
    xbc_ref[0] = jnp.dot(hb, wx_ref[...], preferred_element_type=F32)
    dt_ref[0] = jnp.dot(hb, wdt_ref[...], preferred_element_type=F32)
    dtT_ref[0] = _nt_dot(wdtT_ref[...], hb)


def _prep_w_in(w_in):
    w = w_in
    o = 0
    wq = w[:, o:o + NSA_DIM]; o += NSA_DIM
    wkv = w[:, o:o + 6 * KV_DIM]; o += 6 * KV_DIM
    wg = w[:, o:o + 3 * NSA_HEADS]; o += 3 * NSA_HEADS
    wz = w[:, o:o + D_INNER]; o += D_INNER
    wx = w[:, o:o + CONV_DIM]; o += CONV_DIM
    wdt = w[:, o:o + SSD_HEADS]; o += SSD_HEADS
    pad = lambda a: jnp.pad(a, ((0, 0), (0, LANES - a.shape[1])))
    per = 3 * GQA
    wg = jnp.concatenate([pad(wg[:, k * per:(k + 1) * per]) for k in range(KV_HEADS)], axis=1)
    return dict(wq=wq.astype(BF16), wkvT=wkv.T.astype(BF16), wg=wg.astype(BF16), wz=wz.astype(BF16),
                wx=wx.astype(BF16), wdt=pad(wdt).astype(BF16), wdtT=wdt.T.astype(BF16))


def _in_proj(x, sc, sh, norm_w, W, tm):
    nb, t, d = x.shape
    mt = sc.shape[1]
    assert t % tm == 0 and (mt == 1 or mt == t)
    if mt == 1:
        mod_spec = pl.BlockSpec((1, 1, d), lambda b, i: (b, 0, 0))
    else:
        mod_spec = pl.BlockSpec((1, tm, d), lambda b, i: (b, i, 0))
    full = lambda a: pl.BlockSpec(a.shape, lambda b, i: (0,) * a.ndim)
    row = lambda n: pl.BlockSpec((1, tm, n), lambda b, i: (b, i, 0))
    col = lambda n: pl.BlockSpec((1, n, tm), lambda b, i: (b, 0, i))
    ws = [W["wq"], W["wkvT"], W["wg"], W["wz"], W["wx"], W["wdt"], W["wdtT"]]
    outs = pl.pallas_call(
        _proj_body,
        grid=(nb, t // tm),
        in_specs=[row(d), mod_spec, mod_spec, pl.BlockSpec((1, d), lambda b, i: (0, 0))] + [full(a) for a in ws],
        out_specs=[row(NSA_DIM), col(256), col(256), col(256), row(KV_HEADS * LANES), row(D_INNER), row(CONV_DIM),
                   row(LANES), col(SSD_HEADS)],
        out_shape=[jax.ShapeDtypeStruct((nb, t, NSA_DIM), BF16),
                   jax.ShapeDtypeStruct((nb, 256, t), F32),
                   jax.ShapeDtypeStruct((nb, 256, t), F32),
                   jax.ShapeDtypeStruct((nb, 256, t), F32),
                   jax.ShapeDtypeStruct((nb, t, KV_HEADS * LANES), F32),
                   jax.ShapeDtypeStruct((nb, t, D_INNER), F32),
                   jax.ShapeDtypeStruct((nb, t, CONV_DIM), F32),
                   jax.ShapeDtypeStruct((nb, t, LANES), F32),
                   jax.ShapeDtypeStruct((nb, SSD_HEADS, t), F32)],
        compiler_params=_cparams(("arbitrary", "arbitrary")),
        name="in_proj",
    )(x, sc, sh, norm_w.reshape(1, d), *ws)
    names = ("q", "kcT", "ksT", "kwT", "g", "z", "xbc", "dt", "dtT")
    return dict(zip(names, outs))


def _prep_compress(cmp_pe_k, cmp_w1_k, cmp_w2_k, cmp_pe_v, cmp_w1_v, cmp_w2_v):
    w1s = jnp.stack([cmp_w1_k, cmp_w1_v]).reshape(2, 2, CMP_STRIDE, HEAD_DIM, CMP_HID)
    eye = jnp.eye(2, dtype=F32)
    wbd = jnp.einsum("ktldh,kK,vV->lkvdtKVh", w1s, eye, eye).reshape(CMP_STRIDE, 256, 512)
    w2s = jnp.stack([cmp_w2_k, cmp_w2_v])
    w2bd = jnp.einsum("khd,kK,vV->kvhKVd", w2s, eye, eye).reshape(256, 256)
    pes = jnp.stack([cmp_pe_k, cmp_pe_v]).reshape(2, 2, CMP_STRIDE, HEAD_DIM)
    pe_rows = jnp.broadcast_to(jnp.transpose(pes, (1, 2, 0, 3))[:, :, :, None, :],
                               (2, CMP_STRIDE, 2, KV_HEADS, HEAD_DIM)).reshape(2, CMP_STRIDE * 256)
    pecat = jnp.pad(pe_rows, ((0, SUBLANES - 2), (0, 0)))
    perm = np.zeros((LANES, LANES), np.float32)
    for l in range(CMP_STRIDE):
        for n in range(LANES // CMP_STRIDE):
            perm[(LANES // CMP_STRIDE) * l + n, CMP_STRIDE * n + l] = 1.0
    bias = pl.pallas_call(
        _cmp_bias_body,
        out_shape=jax.ShapeDtypeStruct((SUBLANES, 512), F32),
        name="cmp_bias",
    )(pecat, wbd.reshape(CMP_STRIDE * 256, 512))
    return dict(wbd=wbd.astype(BF16), w2bd=w2bd.astype(BF16), bias=bias, perm=jnp.asarray(perm, BF16))


def _cmp_bias_body(pe_ref, w_ref, o_ref):
    o_ref[...] = jnp.dot(pe_ref[...], w_ref[...], preferred_element_type=F32, precision=HIGHEST)


def _compress_body(n_pref, n_slab_refs, slabs_per_ref, *refs):
    refs = refs[n_pref:]
    slab_refs = refs[:n_slab_refs]
    perm_ref, wbd_ref, bias_ref, w2bd_ref, o_ref, z_scr = refs[n_slab_refs:]
    j = pl.program_id(1)
    nh = z_scr.shape[1]
    per_slab = LANES // CMP_STRIDE
    g_tot = n_slab_refs * slabs_per_ref
    base = pl.multiple_of(j * (per_slab * g_tot), per_slab)
    perm = perm_ref[...]
    for ri in range(n_slab_refs):
        for si in range(slabs_per_ref):
            slab = slab_refs[ri][0][:, si * LANES:(si + 1) * LANES].astype(BF16)
            xp = _nt_dot(perm, slab)
            g = ri * slabs_per_ref + si
            for l in range(CMP_STRIDE):
                z_scr[l, pl.ds(base + per_slab * g, per_slab), :] = xp[per_slab * l:per_slab * (l + 1), :]

    @pl.when(j == pl.num_programs(1) - 1)
    def _():
        acc = jnp.zeros((nh, 512), F32)
        for l in range(CMP_STRIDE):
            acc = acc + jnp.dot(z_scr[l].astype(BF16), wbd_ref[l], preferred_element_type=F32)
        lead = acc[:, :256] + bias_ref[0:1, :256]
        tail = acc[:, 256:] + bias_ref[1:2, 256:]
        hid = _silu(lead + pltpu.roll(tail, nh - 1, 0))
        out = jnp.dot(hid.astype(BF16), w2bd_ref[...], preferred_element_type=F32)
        row = lax.broadcasted_iota(jnp.int32, out.shape, 0)
        o_ref[0] = jnp.where(row < nh - 1, out, 0.0)


def _compress_prompt(kcT, C):
    nb, _, t = kcT.shape
    nh = t // CMP_STRIDE
    g = min(8, t // LANES)
    nsteps = t // (LANES * g)
    full = lambda a: pl.BlockSpec(a.shape, lambda b, j: (0,) * a.ndim)
    return pl.pallas_call(
        functools.partial(_compress_body, 0, 1, g),
        grid=(nb, nsteps),
        in_specs=[pl.BlockSpec((1, 256, LANES * g), lambda b, j: (b, 0, j)),
                  full(C["perm"]), full(C["wbd"]), full(C["bias"]), full(C["w2bd"])],
        out_specs=pl.BlockSpec((1, nh, 256), lambda b, j: (b, 0, 0)),
        out_shape=jax.ShapeDtypeStruct((nb, nh, 256), F32),
        scratch_shapes=[pltpu.VMEM((CMP_STRIDE, nh, 256), F32)],
        compiler_params=_cparams(("arbitrary", "arbitrary")),
        name="compress_prompt",
    )(kcT, C["perm"], C["wbd"], C["bias"], C["w2bd"])


def _compress_paged(pool, page_table, C):
    nb, n_pages = page_table.shape
    nh = n_pages * (PAGE_SIZE // CMP_STRIDE)
    g = min(16, n_pages)
    nsteps = n_pages // g
    full = lambda a: pl.BlockSpec(a.shape, lambda b, j, pt: (0,) * a.ndim)
    page_spec = lambda k: pl.BlockSpec((1, 256, LANES), lambda b, j, pt: (pt[b, j * g + k], 0, 0))
    return pl.pallas_call(
        functools.partial(_compress_body, 1, g, 1),
        grid_spec=pltpu.PrefetchScalarGridSpec(
            num_scalar_prefetch=1,
            grid=(nb, nsteps),
            in_specs=[page_spec(k) for k in range(g)] + [full(C["perm"]), full(C["wbd"]), full(C["bias"]), full(C["w2bd"])],
            out_specs=pl.BlockSpec((1, nh, 256), lambda b, j, pt: (b, 0, 0)),
            scratch_shapes=[pltpu.VMEM((CMP_STRIDE, nh, 256), F32)],
        ),
        out_shape=jax.ShapeDtypeStruct((nb, nh, 256), F32),
        compiler_params=_cparams(("arbitrary", "arbitrary")),
        name="compress_paged",
    )(page_table, *([pool] * g), C["perm"], C["wbd"], C["bias"], C["w2bd"])


def _alibi_slope(head):
    return float(2.0 ** (-8.0 * (head + 1) / NSA_HEADS)) * LOG2E


def _overlap_T(nc, ns):
    cst = np.arange(nc)[None, :] * CMP_STRIDE
    sst = np.arange(ns)[:, None] * SEL_BLK
    ov = np.clip(np.minimum(cst + CMP_BLK, sst + SEL_BLK) - np.maximum(cst, sst), 0, None).astype(np.float32) / CMP_BLK
    return jnp.asarray(ov, F32)


def _masked_softmax(s, mask):
    s = jnp.where(mask, s, NEG_INF)
    p = jnp.exp2(s - jnp.max(s, axis=-1, keepdims=True)) * mask.astype(F32)
    return p / jnp.maximum(jnp.sum(p, axis=-1, keepdims=True), 1e-30)


def _topk_mask(grp, k, by_rounds=False):
    ngrp = len(grp)
    sub = lax.broadcasted_iota(jnp.int32, grp[0].shape, 0)
    one, zero = jnp.float32(1.0), jnp.float32(0.0)
    if by_rounds:
        rowidx = [sub + SUBLANES * c for c in range(ngrp)]
        vals = list(grp)
        keep = [jnp.zeros(grp[0].shape, F32) for _ in range(ngrp)]
        for _ in range(k):
            top = functools.reduce(jnp.maximum, vals)
            top = jnp.max(top, axis=0, keepdims=True)
            first = functools.reduce(jnp.minimum, [jnp.where(v == top, r, ngrp * SUBLANES) for v, r in zip(vals, rowidx)])
            first = jnp.min(first, axis=0, keepdims=True)
            for c in range(ngrp):
                hit = rowidx[c] == first
                keep[c] = jnp.where(hit, one, keep[c])
                vals[c] = jnp.where(hit, -jnp.inf, vals[c])
        return keep
    cnt =[jnp.zeros(grp[0].shape, F32) for _ in range(ngrp)]
    for j in range(ngrp * SUBLANES):
        a, r = divmod(j, SUBLANES)
        row = grp[a][r:r + 1, :]
        for c in range(ngrp):
            if c < a:
                beats = jnp.where(row > grp[c], one, zero)
            elif c > a:
                beats = jnp.where(row >= grp[c], one, zero)
            else:
                beats = jnp.where(sub > r, jnp.where(row >= grp[c], one, zero), jnp.where(row > grp[c], one, zero))
            cnt[c] = cnt[c] + beats
    return [jnp.where(c < float(k), one, zero) for c in cnt]


def _select_blocks(imp, cur):
    ns = imp.shape[0]
    jrow = lax.broadcasted_iota(jnp.int32, imp.shape, 0)
    imp = jnp.where(jrow == cur, FORCE_CUR, jnp.where(jrow == 0, FORCE_SINK, imp))
    imp = jnp.where(jrow <= cur, imp, NEG_INF)
    assert ns % SUBLANES == 0
    grp = [imp[SUBLANES * a:SUBLANES * (a + 1)] for a in range(ns // SUBLANES)]
    return jnp.concatenate(_topk_mask(grp, N_SEL), axis=0)


def _cmp_attn_body(q_ref, kcvc_ref, ovT_ref, eye_ref, ocmp_ref, sel_ref):
    i = pl.program_id(1)
    tq = q_ref.shape[1]
    nh = kcvc_ref.shape[1]
    t0 = i * tq
    qpos_col = t0 + lax.broadcasted_iota(jnp.int32, (tq, 1), 0)
    cend = lax.broadcasted_iota(jnp.int32, (1, nh), 1) * CMP_STRIDE + (CMP_BLK - 1)
    mask = cend <= qpos_col
    dist = (qpos_col - cend).astype(F32)
    qpos_row = t0 + lax.broadcasted_iota(jnp.int32, (1, tq), 1)
    cur = qpos_row // SEL_BLK
    kcvc = kcvc_ref[0]
    for kvh in range(KV_HEADS):
        kc = kcvc[:, kvh * HEAD_DIM:(kvh + 1) * HEAD_DIM].astype(BF16)
        vc = kcvc[:, KV_DIM + kvh * HEAD_DIM:KV_DIM + (kvh + 1) * HEAD_DIM].astype(BF16)
        psum = jnp.zeros((tq, nh), F32)
        for g in range(GQA):
            head = kvh * GQA + g
            qg = q_ref[0, :, head * HEAD_DIM:(head + 1) * HEAD_DIM]
            s = _nt_dot(qg, kc) - _alibi_slope(head) * dist
            p = _masked_softmax(s, mask)
            ocmp_ref[0, :, head * HEAD_DIM:(head + 1) * HEAD_DIM] = jnp.dot(p.astype(BF16), vc, preferred_element_type=F32)
            psum = psum + p
        impT = lax.dot_general(ovT_ref[...], psum, (((1,), (1,)), ((), ())), preferred_element_type=F32,
                               precision=HIGHEST)
        selT = _select_blocks(impT, cur)
        sel = _nt_dot(eye_ref[...], selT.astype(BF16))
        sel_ref[0, kvh] = sel.astype(BF16)


def _cmp_attn_prompt(q, kcvc, tq):
    nb, t, _ = q.shape
    nh = kcvc.shape[1]
    ns = t // SEL_BLK
    ovT = _overlap_T(nh, ns)
    eye = jnp.eye(tq, dtype=BF16)
    return pl.pallas_call(
        _cmp_attn_body,
        grid=(nb, t // tq),
        in_specs=[pl.BlockSpec((1, tq, NSA_DIM), lambda b, i: (b, i, 0)),
                  pl.BlockSpec((1, nh, 256), lambda b, i: (b, 0, 0)),
                  pl.BlockSpec(ovT.shape, lambda b, i: (0, 0)),
                  pl.BlockSpec(eye.shape, lambda b, i: (0, 0))],
        out_specs=[pl.BlockSpec((1, tq, NSA_DIM), lambda b, i: (b, i, 0)),
                   pl.BlockSpec((1, KV_HEADS, tq, ns), lambda b, i: (b, 0, i, 0))],
        out_shape=[jax.ShapeDtypeStruct((nb, t, NSA_DIM), F32),
                   jax.ShapeDtypeStruct((nb, KV_HEADS, t, ns), BF16)],
        compiler_params=_cparams(("arbitrary", "arbitrary")),
        name="cmp_attn",
    )(q, kcvc, ovT, eye)


def _head_rows(qrow, kvh):
    rows = [qrow[:, (kvh * GQA + g) * HEAD_DIM:(kvh * GQA + g + 1) * HEAD_DIM] for g in range(GQA)]
    return jnp.concatenate(rows + [jnp.zeros((SUBLANES - GQA, HEAD_DIM), qrow.dtype)], axis=0)


def _slope_col(kvh):
    r = lax.broadcasted_iota(jnp.int32, (SUBLANES, 1), 0)
    col = jnp.zeros((SUBLANES, 1), F32)
    for g in range(GQA):
        col = jnp.where(r == g, _alibi_slope(kvh * GQA + g), col)
    return col


def _cmp_attn_sample_body(past, q_ref, kcvc_ref, ov_ref, ocmp_ref, imp_ref):
    nh = kcvc_ref.shape[1]
    nsl = ov_ref.shape[1]
    cend = lax.broadcasted_iota(jnp.int32, (1, nh), 1) * CMP_STRIDE + (CMP_BLK - 1)
    mask = cend <= past
    dist = (past - cend).astype(F32)
    kcvc = kcvc_ref[0]
    qrow = q_ref[0]
    for kvh in range(KV_HEADS):
        kc = kcvc[:, kvh * HEAD_DIM:(kvh + 1) * HEAD_DIM].astype(BF16)
        vc = kcvc[:, KV_DIM + kvh * HEAD_DIM:KV_DIM + (kvh + 1) * HEAD_DIM].astype(BF16)
        s = _nt_dot(_head_rows(qrow, kvh), kc) - _slope_col(kvh) * dist
        p = _masked_softmax(s, mask)
        o = jnp.dot(p.astype(BF16), vc, preferred_element_type=F32)
        for g in range(GQA):
            head = kvh * GQA + g
            ocmp_ref[0, :, head * HEAD_DIM:(head + 1) * HEAD_DIM] = o[g:g + 1, :]
        psum = jnp.broadcast_to(jnp.sum(p[0:GQA], axis=0, keepdims=True), (SUBLANES, nh))
        imp = jnp.dot(psum, ov_ref[...], preferred_element_type=F32, precision=HIGHEST)
        imp_ref[0, :, kvh * nsl:(kvh + 1) * nsl] = imp[0:1, :]


def _select_sample_body(past, ns_pad, imp_ref, sel_ref):
    nb = imp_ref.shape[0]
    nsl = imp_ref.shape[1] // KV_HEADS
    cur = jnp.full((1, nb), past // SEL_BLK, jnp.int32)
    for kvh in range(KV_HEADS):
        impT = jnp.transpose(imp_ref[:, kvh * nsl:(kvh + 1) * nsl])
        selT = _select_blocks(impT[0:ns_pad], cur)
        selT = jnp.concatenate([selT, jnp.zeros((nsl - ns_pad, nb), F32)], axis=0)
        sel_ref[:, kvh * nsl:(kvh + 1) * nsl] = jnp.transpose(selT).astype(jnp.int32)


def _cmp_attn_sample(q3, kcvc, past):
    nb = q3.shape[0]
    nh = kcvc.shape[1]
    ns = past // SEL_BLK + 1
    ns_pad = -(-ns // SUBLANES) * SUBLANES
    nsl = -(-ns // LANES) * LANES
    ov = jnp.pad(_overlap_T(nh, ns), ((0, nsl - ns), (0, 0))).T
    ocmp, imp = pl.pallas_call(
        functools.partial(_cmp_attn_sample_body, past),
        grid=(nb,),
        in_specs=[pl.BlockSpec((1, 1, NSA_DIM), lambda b: (b, 0, 0)),
                  pl.BlockSpec((1, nh, 256), lambda b: (b, 0, 0)),
                  pl.BlockSpec(ov.shape, lambda b: (0, 0))],
        out_specs=[pl.BlockSpec((1, 1, NSA_DIM), lambda b: (b, 0, 0)),
                   pl.BlockSpec((1, 1, KV_HEADS * nsl), lambda b: (b, 0, 0))],
        out_shape=[jax.ShapeDtypeStruct((nb, 1, NSA_DIM), F32),
                   jax.ShapeDtypeStruct((nb, 1, KV_HEADS * nsl), F32)],
        compiler_params=_cparams(("arbitrary",)),
        name="cmp_attn_sample",
    )(q3, kcvc, ov)
    sel = pl.pallas_call(
        functools.partial(_select_sample_body, past, ns_pad),
        out_shape=jax.ShapeDtypeStruct((nb, KV_HEADS * nsl), jnp.int32),
        name="select_sample",
    )(imp.reshape(nb, KV_HEADS * nsl))
    return ocmp, sel


N_SPLIT = 3


def _pos_rows(n, start=0):
    tab = np.zeros((HEAD_DIM, n), np.float32)
    k = start + np.arange(n)
    tab[0:N_SPLIT] = k // SEL_BLK
    tab[N_SPLIT:2 * N_SPLIT] = k % SEL_BLK
    return jnp.asarray(tab, BF16)


def _slope_rows():
    bf = lambda v: np.asarray(v, dtype=BF16).astype(np.float32)
    tab = np.zeros((KV_HEADS, SUBLANES, HEAD_DIM), np.float32)
    for k in range(KV_HEADS):
        for g in range(GQA):
            for c, val in enumerate((SEL_BLK * _alibi_slope(k * GQA + g), _alibi_slope(k * GQA + g))):
                rest = np.float32(val)
                for j in range(N_SPLIT):
                    piece = bf(rest)
                    tab[k, g, c * N_SPLIT + j] = piece
                    rest = np.float32(rest - piece)
    return jnp.asarray(tab, F32)


def _block_expand(ns, n):
    return jnp.asarray((np.arange(n)[None, :] // SEL_BLK == np.arange(ns)[:, None]).astype(np.float32), BF16)


def _flash_step(q4, kT_aug, vT, bias, m_scr, acc_scr):
    n, tk = q4.shape[0], kT_aug.shape[1]
    s = jnp.dot(q4, kT_aug, preferred_element_type=F32)
    rb = bias.shape[0]
    if rb in (1, n):
        s = s + bias
    else:
        s = (s.reshape(n // rb, rb, tk) + bias[None]).reshape(n, tk)
    m_old = m_scr[...]
    m_new = jnp.maximum(m_old, jnp.max(s, axis=-1, keepdims=True))
    alpha = jnp.exp2(m_old - m_new)
    p = jnp.exp2(s - jnp.concatenate([m_new] * (tk // LANES), axis=1))
    v_ones = jnp.concatenate([vT, jnp.ones((LANES - HEAD_DIM, tk), BF16)], axis=0)
    acc_scr[...] = alpha * acc_scr[...] + _nt_dot(p.astype(BF16), v_ones)
    m_scr[...] = m_new


def _flash_result(acc_scr):
    acc = acc_scr[...]
    return acc[:, :HEAD_DIM] / jnp.maximum(acc[:, HEAD_DIM:], 1e-30)


def _sel_win_body(tk, q_ref, sel_ref, g_ref, ocmp_ref, ksT_ref, vsT_ref, kwT_ref, vwT_ref, pos_ref, exp_ref, slope_ref,
                  o_ref, m_scr, acc_scr):
    i = pl.program_id(2)
    tq = q_ref.shape[1]
    t0 = i * tq
    qpos = t0 + lax.broadcasted_iota(jnp.int32, (tq, 1), 0)
    q4 = jnp.concatenate(
        [jnp.concatenate([q_ref[0, :, g * HEAD_DIM:(g + 1) * HEAD_DIM],
                          jnp.broadcast_to(slope_ref[0, g:g + 1, :], (tq, HEAD_DIM)).astype(BF16)], axis=1)
         for g in range(GQA)], axis=0)

    def reset():
        m_scr[...] = jnp.full(m_scr.shape, NEG_INF, F32)
        acc_scr[...] = jnp.zeros(acc_scr.shape, F32)

    def finish():
        return _flash_result(acc_scr)

    reset()
    sel = sel_ref[0, 0]

    def sel_step(kt, carry):
        k0 = pl.multiple_of(kt * tk, tk)
        kpos = k0 + lax.broadcasted_iota(jnp.int32, (1, tk), 1)
        chosen = jnp.dot(sel, exp_ref[:, pl.ds(k0, tk)], preferred_element_type=F32)
        bias = (jnp.where(kpos <= qpos, chosen, 0.0) - 1.0) * (-NEG_INF)
        kT_aug = jnp.concatenate([ksT_ref[0, :, pl.ds(k0, tk)].astype(BF16), pos_ref[:, pl.ds(k0, tk)]], axis=0)
        _flash_step(q4, kT_aug, vsT_ref[0, :, pl.ds(k0, tk)].astype(BF16), bias, m_scr, acc_scr)
        return carry

    lax.fori_loop(0, (t0 + tq + tk - 1) // tk, sel_step, 0)
    o_sel = finish()

    reset()
    wk = WINDOW + tq
    k0 = pl.multiple_of(jnp.maximum(t0 - WINDOW, 0), tq)
    dist = qpos - (k0 + lax.broadcasted_iota(jnp.int32, (1, wk), 1))
    bias = jnp.where(lax.bitcast_convert_type(dist, jnp.uint32) < jnp.uint32(WINDOW), 0.0, NEG_INF)
    kT_aug = jnp.concatenate([kwT_ref[0, :, pl.ds(k0, wk)].astype(BF16), pos_ref[:, pl.ds(k0, wk)]], axis=0)
    _flash_step(q4, kT_aug, vwT_ref[0, :, pl.ds(k0, wk)].astype(BF16), bias, m_scr, acc_scr)
    o_win = finish()
    gates = g_ref[0]
    for g in range(GQA):
        rows = slice(g * tq, (g + 1) * tq)
        cols = slice(g * HEAD_DIM, (g + 1) * HEAD_DIM)
        o = (gates[:, 3 * g:3 * g + 1] * ocmp_ref[0, :, cols] + gates[:, 3 * g + 1:3 * g + 2] * o_sel[rows]
             + gates[:, 3 * g + 2:3 * g + 3] * o_win[rows])
        o_ref[0, :, cols] = o.astype(o_ref.dtype)


def _sel_win_prompt(q, sel, gates, ocmp, ksT, kwT, tq, tk):
    nb, t, _ = q.shape
    ns = sel.shape[-1]
    assert t % tk == 0 and t % tq == 0 and WINDOW % tq == 0
    pos, expand, slopes = _pos_rows(t), _block_expand(ns, t), _slope_rows()
    grp = GQA * HEAD_DIM
    kv_spec = lambda which: pl.BlockSpec((1, HEAD_DIM, t), lambda b, k, i: (b, which * KV_HEADS + k, 0))
    return pl.pallas_call(
        functools.partial(_sel_win_body, tk),
        grid=(nb, KV_HEADS, t // tq),
        in_specs=[pl.BlockSpec((1, tq, grp), lambda b, k, i: (b, i, k)),
                  pl.BlockSpec((1, 1, tq, ns), lambda b, k, i: (b, k, i, 0)),
                  pl.BlockSpec((1, tq, LANES), lambda b, k, i: (b, i, k)),
                  pl.BlockSpec((1, tq, grp), lambda b, k, i: (b, i, k)),
                  kv_spec(0), kv_spec(1), kv_spec(0), kv_spec(1),
                  pl.BlockSpec(pos.shape, lambda b, k, i: (0, 0)),
                  pl.BlockSpec(expand.shape, lambda b, k, i: (0, 0)),
                  pl.BlockSpec((1, SUBLANES, HEAD_DIM), lambda b, k, i: (k, 0, 0))],
        out_specs=pl.BlockSpec((1, tq, grp), lambda b, k, i: (b, i, k)),
        out_shape=jax.ShapeDtypeStruct((nb, t, NSA_DIM), BF16),
        scratch_shapes=[pltpu.VMEM((GQA * tq, LANES), F32), pltpu.VMEM((GQA * tq, LANES), F32)],
        compiler_params=_cparams(("arbitrary", "arbitrary", "arbitrary")),
        name="sel_win_attn",
    )(q, sel, gates, ocmp, ksT, ksT, kwT, kwT, pos, expand, slopes)


def _sel_win_sample_body(past, g_pages, pt_ref, sel_ref, q_ref, g_ref, ocmp_ref, *refs):
    page_refs = refs[:g_pages]
    (win_ref, ksn_ref, kwn_ref, pos_ref, wpos_ref, slope_ref, o_ref, wout_ref, m_scr, acc_scr) = refs[g_pages:]
    b = pl.program_id(0)
    j = pl.program_id(1)
    ns_pad = sel_ref.shape[1] // KV_HEADS
    qrow = q_ref[0]
    q4 = [jnp.concatenate([_head_rows(qrow, k), slope_ref[k].astype(BF16)], axis=1) for k in range(KV_HEADS)]

    @pl.when(j == 0)
    def _():
        m_scr[...] = jnp.full(m_scr.shape, NEG_INF, F32)
        acc_scr[...] = jnp.zeros(acc_scr.shape, F32)

    width = g_pages * PAGE_SIZE
    lane_blk = lax.broadcasted_iota(jnp.int32, (1, width), 1) // SEL_BLK
    row0 = lax.broadcasted_iota(jnp.int32, (HEAD_DIM, 1), 0) < N_SPLIT
    blk0 = j * (width // SEL_BLK)
    pos = (pos_ref[...].astype(F32) + jnp.where(row0, blk0.astype(F32), 0.0)).astype(BF16)
    for k in range(KV_HEADS):
        kT = jnp.concatenate([r[0, k * HEAD_DIM:(k + 1) * HEAD_DIM, :] for r in page_refs], axis=1).astype(BF16)
        vT = jnp.concatenate([r[0, KV_DIM + k * HEAD_DIM:KV_DIM + (k + 1) * HEAD_DIM, :] for r in page_refs],
                             axis=1).astype(BF16)
        bias = jnp.full((1, width), NEG_INF, F32)
        for blk in range(width // SEL_BLK):
            chosen = sel_ref[b, k * ns_pad + blk0 + blk] > 0
            bias = jnp.where(lane_blk == blk, jnp.where(chosen, 0.0, NEG_INF), bias)
        _flash_step(q4[k], jnp.concatenate([kT, pos], axis=0), vT, bias, m_scr.at[k], acc_scr.at[k])

    @pl.when(j == pl.num_programs(1) - 1)
    def _():
        nb = ksn_ref.shape[2]
        pick = lax.broadcasted_iota(jnp.int32, (1, nb), 1) == b
        ks_new = jnp.sum(jnp.where(pick, ksn_ref[0], 0.0), axis=1, keepdims=True)
        kw_new = jnp.sum(jnp.where(pick, kwn_ref[0], 0.0), axis=1, keepdims=True)
        lane = lax.broadcasted_iota(jnp.int32, (1, LANES), 1)
        tile_new = jnp.where(lane == 0, ks_new, 0.0).astype(BF16)
        pos_new = jnp.where(row0 & (lane == 0), float(past // SEL_BLK), 0.0).astype(BF16)
        bias_new = jnp.where(lane == 0, 0.0, NEG_INF)
        wlane = lax.broadcasted_iota(jnp.int32, (1, win_ref.shape[2]), 1)
        wout = jnp.where(wlane == win_ref.shape[2] - 1, kw_new, pltpu.roll(win_ref[0], win_ref.shape[2] - 1, 1))
        wout_ref[0] = wout
        woutb = wout.astype(BF16)
        gates = g_ref[0]
        for k in range(KV_HEADS):
            ksl = slice(k * HEAD_DIM, (k + 1) * HEAD_DIM)
            vsl = slice(KV_DIM + k * HEAD_DIM, KV_DIM + (k + 1) * HEAD_DIM)
            _flash_step(q4[k], jnp.concatenate([tile_new[ksl], pos_new], axis=0), tile_new[vsl], bias_new,
                        m_scr.at[k], acc_scr.at[k])
            o_sel = _flash_result(acc_scr.at[k])
            m_scr[k] = jnp.full(m_scr.shape[1:], NEG_INF, F32)
            acc_scr[k] = jnp.zeros(acc_scr.shape[1:], F32)
            _flash_step(q4[k], jnp.concatenate([woutb[ksl], wpos_ref[...]], axis=0), woutb[vsl],
                        jnp.zeros((1, win_ref.shape[2]), F32), m_scr.at[k], acc_scr.at[k])
            o_win = _flash_result(acc_scr.at[k])
            for g in range(GQA):
                head = k * GQA + g
                cols = slice(head * HEAD_DIM, (head + 1) * HEAD_DIM)
                c0 = k * LANES + 3 * g
                o = (gates[:, c0:c0 + 1] * ocmp_ref[0, :, cols] + gates[:, c0 + 1:c0 + 2] * o_sel[g:g + 1, :]
                     + gates[:, c0 + 2:c0 + 3] * o_win[g:g + 1, :])
                o_ref[0, :, cols] = o.astype(o_ref.dtype)


def _sel_win_sample(q3, g3, ocmp, sel, pool_sel, page_table, win, ksT_new, kwT_new, past):
    nb, n_pages = page_table.shape
    wlen = win.shape[2]
    assert wlen == WINDOW and past >= WINDOW
    g = min(32, n_pages)
    pos, wpos, slopes = _pos_rows(g * PAGE_SIZE), _pos_rows(wlen, past - wlen + 1), _slope_rows()
    full = lambda a: pl.BlockSpec(a.shape, lambda b, j, pt, sl: (0,) * a.ndim)
    row = lambda n: pl.BlockSpec((1, 1, n), lambda b, j, pt, sl: (b, 0, 0))
    page_spec = lambda k: pl.BlockSpec((1, 256, PAGE_SIZE), lambda b, j, pt, sl: (pt[b, j * g + k], 0, 0))
    wspec = pl.BlockSpec((1, 256, wlen), lambda b, j, pt, sl: (b, 0, 0))
    return pl.pallas_call(
        functools.partial(_sel_win_sample_body, past, g),
        grid_spec=pltpu.PrefetchScalarGridSpec(
            num_scalar_prefetch=2,
            grid=(nb, n_pages // g),
            in_specs=[row(NSA_DIM), row(KV_HEADS * LANES), row(NSA_DIM)] + [page_spec(k) for k in range(g)]
                     + [wspec, full(ksT_new), full(kwT_new), full(pos), full(wpos), full(slopes)],
            out_specs=[row(NSA_DIM), wspec],
            scratch_shapes=[pltpu.VMEM((KV_HEADS, SUBLANES, LANES), F32), pltpu.VMEM((KV_HEADS, SUBLANES, LANES), F32)],
        ),
        out_shape=[jax.ShapeDtypeStruct((nb, 1, NSA_DIM), BF16), jax.ShapeDtypeStruct((nb, 256, wlen), F32)],
        compiler_params=_cparams(("arbitrary", "arbitrary")),
        name="sel_win_sample",
    )(page_table, sel, q3, g3, ocmp, *([pool_sel] * g), win, ksT_new, kwT_new, pos,
      wpos, slopes)


def _softplus(v):
    return jnp.maximum(v, 0.0) + jnp.log1p(jnp.exp(-jnp.abs(v)))


def _tn_dot(a, b):
    return lax.dot_general(a, b, (((0,), (0,)), ((), ())), preferred_element_type=F32)


def _prep_ssd(conv_w, conv_b, dt_bias, a_log, d_skip, ssd_norm):
    padl = lambda v: jnp.pad(v.reshape(1, -1), ((0, 0), (0, LANES - v.shape[0])))
    L = SSD_CHUNK
    tril = jnp.asarray(np.tril(np.ones((L, L), np.float32)))
    return dict(conv_w=conv_w, conv_b=conv_b.reshape(1, -1), dtb_row=padl(dt_bias), dtb_col=dt_bias.reshape(-1, 1),
                alog_row=padl(a_log), alog_col=a_log.reshape(-1, 1), dskip=padl(d_skip), norm=ssd_norm.reshape(1, -1),
                tril=tril, triu=tril.T)


def _ssd_chunk(u, z, dt, dtT, h_prev, P):
    a_row = -jnp.exp(P["alog_row"][...])
    a_col = -jnp.exp(P["alog_col"][...])
    acum = jnp.dot(P["tril"][...], dt * a_row, preferred_element_type=F32, precision=HIGHEST)
    acumT = jnp.dot(dtT * a_col, P["triu"][...], preferred_element_type=F32, precision=HIGHEST)
    L = u.shape[0]
    li = lax.broadcasted_iota(jnp.int32, (L, L), 0)
    si = lax.broadcasted_iota(jnp.int32, (L, L), 1)
    causal = li >= si
    gn = SSD_GROUPS * D_STATE
    ys, hs = [], []
    per = SSD_HEADS // SSD_GROUPS
    for g in range(SSD_GROUPS):
        bm = u[:, D_INNER + g * D_STATE:D_INNER + (g + 1) * D_STATE]
        cm = u[:, D_INNER + gn + g * D_STATE:D_INNER + gn + (g + 1) * D_STATE]
        bmb = bm.astype(BF16)
        cb = _nt_dot(cm.astype(BF16), bmb)
        for e in range(per):
            h = g * per + e
            ac = acum[:, h:h + 1]
            seg = ac - acumT[h:h + 1, :]
            decay = jnp.where(causal, jnp.exp(jnp.where(causal, seg, 0.0)), 0.0)
            xs = u[:, h * SSD_HEAD_DIM:(h + 1) * SSD_HEAD_DIM]
            xdt = xs * dt[:, h:h + 1]
            y = jnp.dot((cb * decay).astype(BF16), xdt.astype(BF16), preferred_element_type=F32)
            a_last = acum[L - 1:L, h:h + 1]
            st = _tn_dot((xdt * jnp.exp(a_last - ac)).astype(BF16), bmb)
            y = y + _nt_dot((cm * jnp.exp(ac)).astype(BF16), h_prev[h].astype(BF16))
            hs.append(jnp.exp(a_last) * h_prev[h] + st)
            ys.append(y + P["dskip"][:, h:h + 1] * xs)
    return ys, hs


def _ssd_finish(ys, z, norm_w):
    y = jnp.concatenate(ys, axis=1) * _silu(z)
    ms = jnp.mean(y * y, axis=-1, keepdims=True)
    return y * lax.rsqrt(ms + RMS_EPS) * norm_w


def _ssd_prompt_body(xbc_ref, z_ref, dt_ref, dtT_ref, cw_ref, cb_ref, dtbr_ref, dtbc_ref, alr_ref, alc_ref, dsk_ref,
                     nrm_ref, tril_ref, triu_ref, y_ref, hout_ref, xpad_scr, h_scr):
    c = pl.program_id(1)
    L = xbc_ref.shape[1]

    @pl.when(c == 0)
    def _():
        xpad_scr[0:SUBLANES, :] = jnp.zeros((SUBLANES, xpad_scr.shape[1]), F32)
        h_scr[...] = jnp.zeros(h_scr.shape, F32)

    xt = xbc_ref[0]
    xpad_scr[SUBLANES:SUBLANES + L, :] = xt
    conv = cb_ref[...] + xpad_scr[SUBLANES - (CONV_W - 1):SUBLANES - (CONV_W - 1) + L, :] * cw_ref[0:1, :]
    for k in range(1, CONV_W):
        o = SUBLANES - (CONV_W - 1) + k
        conv = conv + xpad_scr[o:o + L, :] * cw_ref[k:k + 1, :]
    xpad_scr[0:SUBLANES, :] = xt[L - SUBLANES:L, :]
    u = _silu(conv)
    dt = _softplus(dt_ref[0] + dtbr_ref[...])
    dtT = _softplus(dtT_ref[0] + dtbc_ref[...])
    P = dict(alog_row=alr_ref, alog_col=alc_ref, tril=tril_ref, triu=triu_ref, dskip=dsk_ref[...])
    ys, hs = _ssd_chunk(u, z_ref[0], dt, dtT, [h_scr[h] for h in range(SSD_HEADS)], P)
    for h in range(SSD_HEADS):
        h_scr[h] = hs[h]
    y_ref[0] = _ssd_finish(ys, z_ref[0], nrm_ref[...]).astype(y_ref.dtype)

    @pl.when(c == pl.num_programs(1) - 1)
    def _():
        hout_ref[0] = h_scr[...]


def _ssd_prompt(xbc, z, dt, dtT, SP):
    nb, t, cd = xbc.shape
    L = SSD_CHUNK
    assert t % L == 0
    full = lambda a: pl.BlockSpec(a.shape, lambda b, c: (0,) * a.ndim)
    names = ("conv_w", "conv_b", "dtb_row", "dtb_col", "alog_row", "alog_col", "dskip", "norm", "tril", "triu")
    ps = [SP[n] for n in names]
    return pl.pallas_call(
        _ssd_prompt_body,
        grid=(nb, t // L),
        in_specs=[pl.BlockSpec((1, L, cd), lambda b, c: (b, c, 0)),
                  pl.BlockSpec((1, L, D_INNER), lambda b, c: (b, c, 0)),
                  pl.BlockSpec((1, L, LANES), lambda b, c: (b, c, 0)),
                  pl.BlockSpec((1, SSD_HEADS, L), lambda b, c: (b, 0, c))] + [full(a) for a in ps],
        out_specs=[pl.BlockSpec((1, L, D_INNER), lambda b, c: (b, c, 0)),
                   pl.BlockSpec((1, SSD_HEADS, SSD_HEAD_DIM, D_STATE), lambda b, c: (b, 0, 0, 0))],
        out_shape=[jax.ShapeDtypeStruct((nb, t, D_INNER), BF16),
                   jax.ShapeDtypeStruct((nb, SSD_HEADS, SSD_HEAD_DIM, D_STATE), F32)],
        scratch_shapes=[pltpu.VMEM((SUBLANES + L, cd), F32), pltpu.VMEM((SSD_HEADS, SSD_HEAD_DIM, D_STATE), F32)],
        compiler_params=_cparams(("arbitrary", "arbitrary")),
        name="ssd_prompt",
    )(xbc, z, dt, dtT, *ps)


def _ssd_sample_body(cs_ref, xbc_ref, z_ref, dt_ref, h0_ref, cw_ref, cb_ref, dtb_ref, al_ref, dsk_ref, nrm_ref, eye_ref,
                     y_ref, cso_ref, h_ref):
    nseq = xbc_ref.shape[0]
    xn = xbc_ref[...]
    conv = cb_ref[...] + xn * cw_ref[CONV_W - 1:CONV_W, :]
    for k in range(CONV_W - 1):
        conv = conv + cs_ref[k] * cw_ref[k:k + 1, :]
        if k > 0:
            cso_ref[k - 1] = cs_ref[k]
    cso_ref[CONV_W - 2] = xn
    u = _silu(conv)
    dt = _softplus(dt_ref[...] + dtb_ref[...])
    decay = jnp.exp(dt * (-jnp.exp(al_ref[...])))
    eye = eye_ref[...]
    gn = SSD_GROUPS * D_STATE
    per = SSD_HEADS // SSD_GROUPS
    rows = []
    for s in range(nseq):
        ys = []
        for h in range(SSD_HEADS):
            g = h // per
            xs = u[s:s + 1, h * SSD_HEAD_DIM:(h + 1) * SSD_HEAD_DIM]
            bm = u[s:s + 1, D_INNER + g * D_STATE:D_INNER + (g + 1) * D_STATE]
            cm = u[s:s + 1, D_INNER + gn + g * D_STATE:D_INNER + gn + (g + 1) * D_STATE]
            xcol = jnp.sum(eye * xs, axis=1, keepdims=True)
            hn = decay[s:s + 1, h:h + 1] * h0_ref[s, h] + (dt[s:s + 1, h:h + 1] * xcol) * bm
            h_ref[s, h] = hn
            ycol = jnp.sum(hn * cm, axis=1, keepdims=True)
            ys.append(jnp.sum(eye * ycol, axis=0, keepdims=True) + dsk_ref[:, h:h + 1] * xs)
        rows.append(jnp.concatenate(ys, axis=1))
    y = jnp.concatenate(rows, axis=0) * _silu(z_ref[...])
    ms = jnp.mean(y * y, axis=-1, keepdims=True)
    y_ref[...] = (y * lax.rsqrt(ms + RMS_EPS) * nrm_ref[...]).astype(y_ref.dtype)


def _ssd_sample(conv_state, xbc, z, dt, h0, SP):
    nb, cd = xbc.shape
    ts = SUBLANES
    assert nb % ts == 0
    eye = jnp.eye(SSD_HEAD_DIM, dtype=F32)
    names = ("conv_w", "conv_b", "dtb_row", "alog_row", "dskip", "norm")
    ps = [SP[n] for n in names] + [eye]
    full = lambda a: pl.BlockSpec(a.shape, lambda i: (0,) * a.ndim)
    st = pl.BlockSpec((ts, SSD_HEADS, SSD_HEAD_DIM, D_STATE), lambda i: (i, 0, 0, 0))
    cs = pl.BlockSpec((CONV_W - 1, ts, cd), lambda i: (0, i, 0))
    row = lambda n: pl.BlockSpec((ts, n), lambda i: (i, 0))
    return pl.pallas_call(
        _ssd_sample_body,
        grid=(nb // ts,),
        in_specs=[cs, row(cd), row(D_INNER), row(LANES), st] + [full(a) for a in ps],
        out_specs=[row(D_INNER), cs, st],
        out_shape=[jax.ShapeDtypeStruct((nb, D_INNER), BF16),
                   jax.ShapeDtypeStruct((CONV_W - 1, nb, cd), F32),
                   jax.ShapeDtypeStruct(h0.shape, F32)],
        compiler_params=_cparams(("arbitrary",)),
        name="ssd_sample",
    )(conv_state, xbc, z, dt, h0, *ps)


def _pack_bf16_pairs(v):
    m = v.shape[1] // 2
    hi = pltpu.bitcast(v[:, :m].astype(BF16).astype(F32), jnp.uint32)
    lo = pltpu.bitcast(v[:, m:].astype(BF16).astype(F32), jnp.uint32)
    return hi | (lo >> 16)


def _unpack_pairs_f32(w):
    return pltpu.bitcast(w & jnp.uint32(0xFFFF0000), F32), pltpu.bitcast(w << 16, F32)


def _unpack_bf16_pairs(w):
    hi, lo = _unpack_pairs_f32(w)
    return hi.astype(BF16), lo.astype(BF16)


def _route(logitsT, bias_col):
    s = jax.nn.sigmoid(logitsT)
    sb = s + bias_col
    per = N_EXPERTS // N_EXPERT_GROUPS
    assert per == SUBLANES
    grp = [sb[per * a:per * (a + 1)] for a in range(N_EXPERT_GROUPS)]
    sub = lax.broadcasted_iota(jnp.int32, grp[0].shape, 0)
    gs = []
    for ga in grp:
        m1 = jnp.max(ga, axis=0, keepdims=True)
        first = jnp.min(jnp.where(ga == m1, sub, per), axis=0, keepdims=True)
        m2 = jnp.max(jnp.where(sub == first, NEG_INF, ga), axis=0, keepdims=True)
        gs.append(m1 + m2)
    gmask = _topk_mask([jnp.concatenate(gs, axis=0)], TOPK_GROUPS)[0]
    masked = [jnp.where(gmask[a:a + 1, :] > 0.5, grp[a], NEG_INF) for a in range(N_EXPERT_GROUPS)]
    sel = jnp.concatenate(_topk_mask(masked, TOP_K, by_rounds=True), axis=0)
    w = s * sel
    w = w / jnp.sum(w, axis=0, keepdims=True) * ROUTED_SCALE
    return sel, w


def _post_mix_body(x_ref, on_ref, ys_ref, g1_ref, sc_ref, sh_ref, nw_ref, wo_ref, rw_ref, rb_ref,
                   x1_ref, hp_ref, selT_ref, wT_ref, cnt_ref):
    first = (pl.program_id(0) == 0) & (pl.program_id(1) == 0)
    half = wo_ref.shape[0] // 2
    mix = (jnp.dot(on_ref[0], wo_ref[0:half, :], preferred_element_type=F32)
           + jnp.dot(ys_ref[0], wo_ref[half:, :], preferred_element_type=F32))
    x1 = x_ref[0] + g1_ref[0] * mix
    x1_ref[0] = x1
    ms = jnp.mean(x1 * x1, axis=-1, keepdims=True)
    h = x1 * lax.rsqrt(ms + RMS_EPS) * nw_ref[...]
    h = h * (1.0 + sc_ref[0]) + sh_ref[0]
    hp_ref[0] = _pack_bf16_pairs(h)
    logitsT = lax.dot_general(rw_ref[...], h, (((1,), (1,)), ((), ())), preferred_element_type=F32,
                              precision=HIGHEST)
    sel, w = _route(logitsT, rb_ref[...])
    selT_ref[...] = sel.astype(selT_ref.dtype)
    wT_ref[...] = w

    @pl.when(first)
    def _():
        cnt_ref[...] = jnp.zeros(cnt_ref.shape, F32)

    cnt_ref[...] += jnp.broadcast_to(jnp.sum(sel, axis=1, keepdims=True), cnt_ref.shape)


def _post_mix(x, o_nsa, y_ssd, g1, sc2, sh2, norm_w, w_out_b, router_wT, router_bias, tm):
    nb, t, d = x.shape
    mt = g1.shape[1]
    nt = t // tm
    assert t % tm == 0 and (mt == 1 or mt == t)
    if mt == 1:
        mod_spec = pl.BlockSpec((1, 1, d), lambda b, i: (b, 0, 0))
    else:
        mod_spec = pl.BlockSpec((1, tm, d), lambda b, i: (b, i, 0))
    row = lambda n: pl.BlockSpec((1, tm, n), lambda b, i: (b, i, 0))
    full = lambda a: pl.BlockSpec(a.shape, lambda b, i: (0,) * a.ndim)
    tok = lambda: pl.BlockSpec((N_EXPERTS, tm), lambda b, i: (0, b * nt + i))
    rb = router_bias.reshape(N_EXPERTS, 1)
    nw = norm_w.reshape(1, d)
    return pl.pallas_call(
        _post_mix_body,
        grid=(nb, nt),
        in_specs=[row(d), row(NSA_DIM), row(D_INNER), mod_spec, mod_spec, mod_spec, full(nw), full(w_out_b),
                  full(router_wT), full(rb)],
        out_specs=[row(d), row(d // 2), tok(), tok(), pl.BlockSpec((N_EXPERTS, LANES), lambda b, i: (0, 0))],
        out_shape=[jax.ShapeDtypeStruct((nb, t, d), F32),
                   jax.ShapeDtypeStruct((nb, t, d // 2), jnp.uint32),
                   jax.ShapeDtypeStruct((N_EXPERTS, nb * t), BF16),
                   jax.ShapeDtypeStruct((N_EXPERTS, nb * t), F32),
                   jax.ShapeDtypeStruct((N_EXPERTS, LANES), F32)],
        compiler_params=_cparams(("arbitrary", "arbitrary")),
        name="post_mix",
    )(x, o_nsa, y_ssd, g1, sc2, sh2, nw, w_out_b, router_wT, rb)


MOE_SHIFT_LARGE = 10
MOE_SHIFT_SMALL = 8


def _moe_rows(n_tok, shift):
    n_blocks = (n_tok * TOP_K >> shift) + N_EXPERTS
    n_blocks_pad = -(-n_blocks // LANES) * LANES
    return n_blocks, n_blocks_pad


def _plan_body(shift, selT_ref, wT_ref, cnt_ref, triu_ref, tril_ref, eye_ref, dest_ref, w8_ref, be_ref, fill_ref,
               carry_scr, pstart_scr):
    step = pl.program_id(0)
    ne = N_EXPERTS
    block = 1 << shift

    @pl.when(step == 0)
    def _():
        cnt = cnt_ref[...]
        cnt_i = cnt.astype(jnp.int32)
        padded = (((cnt_i + (block - 1)) >> shift) << shift).astype(F32)
        pstart = jnp.dot(tril_ref[...].astype(F32), padded, preferred_element_type=F32, precision=HIGHEST)
        pstart_scr[...] = pstart
        carry_scr[...] = jnp.zeros(carry_scr.shape, F32)
        pend = pstart + padded
        nbp = be_ref.shape[1]
        starts = (lax.broadcasted_iota(jnp.int32, (1, nbp), 1) * block).astype(F32)
        below = jnp.where(pend[:, 0:1] <= starts, 1.0, 0.0)
        be_ref[...] = jnp.minimum(jnp.sum(below, axis=0, keepdims=True), float(ne - 1)).astype(jnp.int32)
        eye = eye_ref[...]
        to_row = lambda col: jnp.sum(col * eye, axis=0, keepdims=True)
        n_used = jnp.max(pend, axis=0, keepdims=True) * (1.0 / block)
        rows = jnp.concatenate([to_row(pstart + cnt), to_row(padded - cnt), n_used,
                                jnp.zeros((SUBLANES - 3, LANES), F32)], axis=0)
        fill_ref[...] = rows.astype(jnp.int32)

    sel = selT_ref[...]
    self32 = sel.astype(F32)
    rank = jnp.dot(sel, triu_ref[...], preferred_element_type=F32) + carry_scr[:, 0:1]
    carry_scr[...] += jnp.broadcast_to(jnp.sum(self32, axis=1, keepdims=True), carry_scr.shape)
    dest = pstart_scr[:, 0:1] + rank
    slot = jnp.dot(tril_ref[...], sel, preferred_element_type=F32)
    w = wT_ref[...]
    drows, wrows = [], []
    for k in range(TOP_K):
        pick = jnp.where(slot == float(k), self32, 0.0)
        drows.append(jnp.sum(pick * dest, axis=0, keepdims=True))
        wrows.append(jnp.sum(pick * w, axis=0, keepdims=True))
    dest_ref[...] = jnp.concatenate(drows, axis=0).astype(jnp.int32)
    w8_ref[...] = jnp.concatenate(wrows, axis=0)


def _moe_plan(selT, wT, cnt, tile, shift):
    ne, n = selT.shape
    assert n % tile == 0
    _, nbp = _moe_rows(n, shift)
    triu = jnp.asarray(np.triu(np.ones((tile, tile), np.float32), 1), BF16)
    tril = jnp.asarray(np.tril(np.ones((ne, ne), np.float32), -1), BF16)
    eye = jnp.asarray(np.eye(ne, LANES, dtype=np.float32))
    full = lambda a: pl.BlockSpec(a.shape, lambda i: (0,) * a.ndim)
    return pl.pallas_call(
        functools.partial(_plan_body, shift),
        grid=(n // tile,),
        in_specs=[pl.BlockSpec((ne, tile), lambda i: (0, i)), pl.BlockSpec((ne, tile), lambda i: (0, i)),
                  full(cnt), full(triu), full(tril), full(eye)],
        out_specs=[pl.BlockSpec((TOP_K, tile), lambda i: (0, i)), pl.BlockSpec((TOP_K, tile), lambda i: (0, i)),
                   pl.BlockSpec((1, nbp), lambda i: (0, 0)), pl.BlockSpec((SUBLANES, LANES), lambda i: (0, 0))],
        out_shape=[jax.ShapeDtypeStruct((TOP_K, n), jnp.int32), jax.ShapeDtypeStruct((TOP_K, n), F32),
                   jax.ShapeDtypeStruct((1, nbp), jnp.int32), jax.ShapeDtypeStruct((SUBLANES, LANES), jnp.int32)],
        scratch_shapes=[pltpu.VMEM((ne, LANES), F32), pltpu.VMEM((ne, LANES), F32)],
        compiler_params=_cparams(("arbitrary",)),
        name="moe_plan",
    )(selT, wT, cnt, triu, tril, eye)


def _fill_pieces(shift):
    return tuple(1 << s for s in reversed(range(shift)))


def _fill_padding(fill_ref, xd_ref, zero_scr, zsem, wait):
    def per_expert(e, carry):
        start = fill_ref[0, e]
        n = fill_ref[1, e]
        head = n & (SUBLANES - 1)
        for r in range(SUBLANES - 1):
            @pl.when(r < head)
            def _():
                cp = pltpu.make_async_copy(zero_scr.at[pl.ds(0, 1)], xd_ref.at[pl.ds(start + r, 1)], zsem)
                cp.wait() if wait else cp.start()

        cur = start + head
        for p in _fill_pieces(zero_scr.shape[0].bit_length()):
            if p < SUBLANES:
                continue
            hit = (n & p) != 0

            @pl.when(hit)
            def _():
                off = pl.multiple_of(cur, SUBLANES)
                cp = pltpu.make_async_copy(zero_scr.at[pl.ds(0, p)], xd_ref.at[pl.ds(off, p)], zsem)
                cp.wait() if wait else cp.start()

            cur = cur + jnp.where(hit, p, 0)
        return carry

    lax.fori_loop(0, N_EXPERTS, per_expert, 0)


def _dispatch_body(dest_ref, fill_ref, hp_ref, xd_ref, zero_scr, sem, zsem):
    step = pl.program_id(0)
    tile = hp_ref.shape[0]

    def row_copy(t, k):
        return pltpu.make_async_copy(hp_ref.at[pl.ds(t, 1)], xd_ref.at[pl.ds(dest_ref[k, t], 1)], sem)

    @pl.when(step == 0)
    def _():
        zero_scr[...] = jnp.zeros(zero_scr.shape, zero_scr.dtype)
        _fill_padding(fill_ref, xd_ref, zero_scr, zsem, False)
        _fill_padding(fill_ref, xd_ref, zero_scr, zsem, True)

    def issue(t, carry):
        for k in range(TOP_K):
            row_copy(t, k).start(priority=k % 2)
        return carry

    def drain(t, carry):
        for k in range(TOP_K):
            row_copy(t, k).wait()
        return carry

    lax.fori_loop(0, tile, issue, 0)
    lax.fori_loop(0, tile, drain, 0)


def _moe_dispatch(hp, dest8, fill, tile, shift):
    n, m = hp.shape
    n_blocks, _ = _moe_rows(n, shift)
    nr = n_blocks << shift
    return pl.pallas_call(
        _dispatch_body,
        grid=(n // tile,),
        in_specs=[pl.BlockSpec((TOP_K, tile), lambda i: (0, i), memory_space=pltpu.SMEM),
                  pl.BlockSpec(memory_space=pltpu.SMEM),
                  pl.BlockSpec((tile, m), lambda i: (i, 0))],
        out_specs=pl.BlockSpec(memory_space=pl.ANY),
        out_shape=jax.ShapeDtypeStruct((nr, m), jnp.uint32),
        scratch_shapes=[pltpu.VMEM((_fill_pieces(shift)[0], m), jnp.uint32), pltpu.SemaphoreType.DMA(()),
                        pltpu.SemaphoreType.DMA(())],
        compiler_params=_cparams(("arbitrary",)),
        name="moe_dispatch",
    )(dest8, fill, hp)


def _swiglu_packed(xw, w1, w3, w2):
    xa, xb = _unpack_bf16_pairs(xw)
    half = xa.shape[1]
    mm = lambda w: (jnp.dot(xa, w[0:half, :], preferred_element_type=F32)
                    + jnp.dot(xb, w[half:, :], preferred_element_type=F32))
    hid = _silu(mm(w1)) * mm(w3)
    return jnp.dot(hid.astype(BF16), w2, preferred_element_type=F32)


def _experts_body(be_ref, nu_ref, xd_ref, w1_ref, w3_ref, w2_ref, yd_ref, w1_scr, w3_scr, w2_scr):
    i = pl.program_id(0)
    last = jnp.minimum(i, nu_ref[0] - 1)
    fresh = (i == 0) | (be_ref[last] != be_ref[jnp.maximum(last - 1, 0)])

    @pl.when(fresh)
    def _():
        w1_scr[...] = w1_ref[0].astype(BF16)
        w3_scr[...] = w3_ref[0].astype(BF16)
        w2_scr[...] = w2_ref[0].astype(BF16)

    @pl.when(i < nu_ref[0])
    def _():
        yd_ref[...] = _pack_bf16_pairs(_swiglu_packed(xd_ref[...], w1_scr[...], w3_scr[...], w2_scr[...]))

    @pl.when(i >= nu_ref[0])
    def _():
        yd_ref[...] = jnp.zeros(yd_ref.shape, yd_ref.dtype)


def _moe_experts(xd, block_e, n_used, w1b, w3b, w2b, shift):
    nr, m = xd.shape
    n_blocks = nr >> shift
    block = 1 << shift
    d, f = w1b.shape[1:]
    clamp = lambda i, nu: jnp.minimum(i, nu[0] - 1)
    return pl.pallas_call(
        _experts_body,
        grid_spec=pltpu.PrefetchScalarGridSpec(
            num_scalar_prefetch=2,
            grid=(n_blocks,),
            in_specs=[pl.BlockSpec((block, m), lambda i, be, nu: (clamp(i, nu), 0)),
                      pl.BlockSpec((1, d, f), lambda i, be, nu: (be[clamp(i, nu)], 0, 0)),
                      pl.BlockSpec((1, d, f), lambda i, be, nu: (be[clamp(i, nu)], 0, 0)),
                      pl.BlockSpec((1, f, d), lambda i, be, nu: (be[clamp(i, nu)], 0, 0))],
            out_specs=pl.BlockSpec((block, m), lambda i, be, nu: (i, 0)),
            scratch_shapes=[pltpu.VMEM((d, f), BF16), pltpu.VMEM((d, f), BF16), pltpu.VMEM((f, d), BF16)],
        ),
        out_shape=jax.ShapeDtypeStruct((nr, m), jnp.uint32),
        compiler_params=_cparams(("arbitrary",)),
        name="moe_experts",
    )(block_e, n_used, xd, w1b, w3b, w2b)


def _combine_body(dest_ref, w8_ref, hp_ref, x1_ref, g2_ref, eye_ref, sw1_ref, sw3_ref, sw2_ref, nf_ref, yd_ref,
                  o_ref, ybuf, sem):
    tile = hp_ref.shape[1]

    def row_copy(t, k):
        return pltpu.make_async_copy(yd_ref.at[pl.ds(dest_ref[k, t], 1)], ybuf.at[k, pl.ds(t, 1)], sem)

    def issue(t, carry):
        for k in range(TOP_K):
            row_copy(t, k).start(priority=k % 2)
        return carry

    def drain(t, carry):
        for k in range(TOP_K):
            row_copy(t, k).wait()
        return carry

    lax.fori_loop(0, tile, issue, 0)
    shared = _swiglu_packed(hp_ref[0], sw1_ref[...], sw3_ref[...], sw2_ref[...])
    w_rows = lax.dot_general(eye_ref[...], w8_ref[...], (((1,), (1,)), ((), ())), preferred_element_type=F32,
                             precision=HIGHEST)
    lax.fori_loop(0, tile, drain, 0)
    half = ybuf.shape[2]
    acc_a = jnp.zeros((tile, half), F32)
    acc_b = jnp.zeros((tile, half), F32)
    for k in range(TOP_K):
        ya, yb = _unpack_pairs_f32(ybuf[k])
        wk = w_rows[:, k:k + 1]
        acc_a = acc_a + wk * ya
        acc_b = acc_b + wk * yb
    routed = jnp.concatenate([acc_a, acc_b], axis=1)
    x2 = x1_ref[0] + g2_ref[0] * (routed + shared)
    ms = jnp.mean(x2 * x2, axis=-1, keepdims=True)
    o_ref[0] = x2 * lax.rsqrt(ms + RMS_EPS) * nf_ref[...]


def _moe_combine(dest8, w8, hp, x1, g2, yd, sw1b, sw3b, sw2b, norm_f, tile):
    nb, t, d = x1.shape
    nt = t // tile
    mt = g2.shape[1]
    assert t % tile == 0 and (mt == 1 or mt == t)
    if mt == 1:
        mod_spec = pl.BlockSpec((1, 1, d), lambda b, i: (b, 0, 0))
    else:
        mod_spec = pl.BlockSpec((1, tile, d), lambda b, i: (b, i, 0))
    eye = jnp.eye(tile, dtype=F32)
    nf = norm_f.reshape(1, d)
    full = lambda a: pl.BlockSpec(a.shape, lambda b, i: (0,) * a.ndim)
    return pl.pallas_call(
        _combine_body,
        grid=(nb, nt),
        in_specs=[pl.BlockSpec((TOP_K, tile), lambda b, i: (0, b * nt + i), memory_space=pltpu.SMEM),
                  pl.BlockSpec((TOP_K, tile), lambda b, i: (0, b * nt + i)),
                  pl.BlockSpec((1, tile, d // 2), lambda b, i: (b, i, 0)),
                  pl.BlockSpec((1, tile, d), lambda b, i: (b, i, 0)),
                  mod_spec, full(eye), full(sw1b), full(sw3b), full(sw2b), full(nf),
                  pl.BlockSpec(memory_space=pl.ANY)],
        out_specs=pl.BlockSpec((1, tile, d), lambda b, i: (b, i, 0)),
        out_shape=jax.ShapeDtypeStruct((nb, t, d), F32),
        scratch_shapes=[pltpu.VMEM((TOP_K, tile, d // 2), jnp.uint32), pltpu.SemaphoreType.DMA(())],
        compiler_params=_cparams(("arbitrary", "arbitrary")),
        name="moe_combine",
    )(dest8, w8, hp, x1, g2, eye, sw1b, sw3b, sw2b, nf, yd)


SC_CHUNK = 128
SC_WORKERS = 32


def _sc_workers():
    info = plsc.get_sparse_core_info()
    assert info.num_cores * info.num_subcores == SC_WORKERS
    return info.num_cores, info.num_subcores


def _sc_dispatch(hp, dest8, nr):
    n, m = hp.shape
    nc, nsub = _sc_workers()
    per_w = n // (nc * nsub)
    assert n % (nc * nsub * SC_CHUNK) == 0
    mesh = plsc.VectorSubcoreMesh(core_axis_name="c", subcore_axis_name="s")

    @functools.partial(
        pl.kernel, mesh=mesh, out_type=jax.ShapeDtypeStruct((nr, m), hp.dtype),
        scratch_types=[pltpu.VMEM((TOP_K, SC_CHUNK), jnp.int32), pltpu.VMEM((SC_CHUNK, m), hp.dtype),
                       pltpu.SemaphoreType.DMA])
    def scatter_rows(hp_hbm, dest_hbm, xd_hbm, idx_v, rows_v, sem):
        wid = lax.axis_index("s") * nc + lax.axis_index("c")

        @pl.loop(0, per_w // SC_CHUNK)
        def _(c):
            base = pl.multiple_of(wid * per_w + c * SC_CHUNK, SC_CHUNK)
            pltpu.sync_copy(hp_hbm.at[pl.ds(base, SC_CHUNK)], rows_v)
            pltpu.sync_copy(dest_hbm.at[:, pl.ds(base, SC_CHUNK)], idx_v)
            copies = [pltpu.async_copy(rows_v, xd_hbm.at[idx_v.at[k]], sem) for k in range(TOP_K)]
            for cp in copies:
                cp.wait()

    return scatter_rows(hp, dest8)


def _sc_gather(yd, dest8):
    _, m = yd.shape
    n = dest8.shape[1]
    nc, nsub = _sc_workers()
    per_w = n // (nc * nsub)
    assert n % (nc * nsub * SC_CHUNK) == 0
    mesh = plsc.VectorSubcoreMesh(core_axis_name="c", subcore_axis_name="s")

    @functools.partial(
        pl.kernel, mesh=mesh, out_type=jax.ShapeDtypeStruct((TOP_K, n, m), yd.dtype),
        scratch_types=[pltpu.VMEM((TOP_K, SC_CHUNK), jnp.int32), pltpu.VMEM((SC_CHUNK, m), yd.dtype),
                       pltpu.SemaphoreType.DMA])
    def gather_rows(yd_hbm, dest_hbm, out_hbm, idx_v, rows_v, sem):
        wid = lax.axis_index("s") * nc + lax.axis_index("c")

        @pl.loop(0, per_w // SC_CHUNK)
        def _(c):
            base = pl.multiple_of(wid * per_w + c * SC_CHUNK, SC_CHUNK)
            pltpu.sync_copy(dest_hbm.at[:, pl.ds(base, SC_CHUNK)], idx_v)
            for k in range(TOP_K):
                pltpu.async_copy(yd_hbm.at[idx_v.at[k]], rows_v, sem).wait()
                pltpu.sync_copy(rows_v, out_hbm.at[k, pl.ds(base, SC_CHUNK)])

    return gather_rows(yd, dest8)


def _fill_body(fill_ref, xd_in_ref, after_ref, xd_ref, zero_scr, zsem):
    del xd_in_ref, after_ref
    zero_scr[...] = jnp.zeros(zero_scr.shape, zero_scr.dtype)
    for wait in (False, True):
        _fill_padding(fill_ref, xd_ref, zero_scr, zsem, wait)


def _moe_fill(xd, fill, after, shift):
    return pl.pallas_call(
        _fill_body,
        in_specs=[pl.BlockSpec(memory_space=pltpu.SMEM), pl.BlockSpec(memory_space=pl.ANY),
                  pl.BlockSpec(memory_space=pl.ANY)],
        out_specs=pl.BlockSpec(memory_space=pl.ANY),
        out_shape=jax.ShapeDtypeStruct(xd.shape, xd.dtype),
        scratch_shapes=[pltpu.VMEM((_fill_pieces(shift)[0], xd.shape[1]), xd.dtype), pltpu.SemaphoreType.DMA(())],
        input_output_aliases={1: 0},
        name="moe_fill",
    )(fill, xd, after)


def _combine_dense_body(w8_ref, hp_ref, x1_ref, g2_ref, eye_ref, sw1_ref, sw3_ref, sw2_ref, nf_ref, ybuf_ref, o_ref):
    tile = hp_ref.shape[1]
    shared = _swiglu_packed(hp_ref[0], sw1_ref[...], sw3_ref[...], sw2_ref[...])
    w_rows = lax.dot_general(eye_ref[...], w8_ref[...], (((1,), (1,)), ((), ())), preferred_element_type=F32,
                             precision=HIGHEST)
    half = ybuf_ref.shape[2]
    acc_a = jnp.zeros((tile, half), F32)
    acc_b = jnp.zeros((tile, half), F32)
    for k in range(TOP_K):
        ya, yb = _unpack_pairs_f32(ybuf_ref[k])
        wk = w_rows[:, k:k + 1]
        acc_a = acc_a + wk * ya
        acc_b = acc_b + wk * yb
    routed = jnp.concatenate([acc_a, acc_b], axis=1)
    x2 = x1_ref[0] + g2_ref[0] * (routed + shared)
    ms = jnp.mean(x2 * x2, axis=-1, keepdims=True)
    o_ref[0] = x2 * lax.rsqrt(ms + RMS_EPS) * nf_ref[...]


def _moe_combine_dense(w8, hp, x1, g2, ybuf, sw1b, sw3b, sw2b, norm_f, tile):
    nb, t, d = x1.shape
    nt = t // tile
    assert t % tile == 0 and g2.shape[1] == 1
    eye = jnp.eye(tile, dtype=F32)
    nf = norm_f.reshape(1, d)
    full = lambda a: pl.BlockSpec(a.shape, lambda b, i: (0,) * a.ndim)
    return pl.pallas_call(
        _combine_dense_body,
        grid=(nb, nt),
        in_specs=[pl.BlockSpec((TOP_K, tile), lambda b, i: (0, b * nt + i)),
                  pl.BlockSpec((1, tile, d // 2), lambda b, i: (b, i, 0)),
                  pl.BlockSpec((1, tile, d), lambda b, i: (b, i, 0)),
                  pl.BlockSpec((1, 1, d), lambda b, i: (b, 0, 0)),
                  full(eye), full(sw1b), full(sw3b), full(sw2b), full(nf),
                  pl.BlockSpec((TOP_K, tile, d // 2), lambda b, i: (0, b * nt + i, 0))],
        out_specs=pl.BlockSpec((1, tile, d), lambda b, i: (b, i, 0)),
        out_shape=jax.ShapeDtypeStruct((nb, t, d), F32),
        compiler_params=_cparams(("arbitrary", "arbitrary")),
        name="moe_combine_dense",
    )(w8, hp, x1, g2, eye, sw1b, sw3b, sw2b, nf, ybuf)


def _moe(x1, hp, selT, wT, cnt, g2, EW, norm_f, tile, on_sparsecore, after=None):
    nb, t, d = x1.shape
    n = nb * t
    shift = MOE_SHIFT_LARGE if n * TOP_K >= N_EXPERTS << MOE_SHIFT_LARGE else MOE_SHIFT_SMALL
    dest8, w8, block_e, fill = _moe_plan(selT, wT, cnt, tile, shift)
    n_blocks, _ = _moe_rows(n, shift)
    hp2 = hp.reshape(n, d // 2)
    if on_sparsecore:
        xd = _moe_fill(_sc_dispatch(hp2, dest8, n_blocks << shift), fill, after, shift)
    else:
        xd = _moe_dispatch(hp2, dest8, fill, tile, shift)
    yd = _moe_experts(xd, block_e[0, :n_blocks], fill[2, 0:1], EW["w1"], EW["w3"], EW["w2"], shift)
    if on_sparsecore:
        return _moe_combine_dense(w8, hp, x1, g2, _sc_gather(yd, dest8), EW["sw1"], EW["sw3"], EW["sw2"], norm_f, tile)
    return _moe_combine(dest8, w8, hp, x1, g2, yd, EW["sw1"], EW["sw3"], EW["sw2"], norm_f, tile)


def kernel(x_prompt, x_sample, c_prompt, c_sample, cache_kv_cmp, cache_kv_sel, cache_kv_win, state_conv, state_ssm, page_table, w_ada, b_ada, norm_mix, norm_ffn, w_in, cmp_pe_k, cmp_w1_k, cmp_w2_k, cmp_pe_v, cmp_w1_v, cmp_w2_v, conv_w, conv_b, dt_bias, a_log, d_skip, ssd_norm, w_out, router_w, router_bias, exp_w1, exp_w3, exp_w2, sh_w1, sh_w3, sh_w2, norm_f):
    nb, t, d = x_prompt.shape
    ndb = x_sample.shape[0]
    c_all = jnp.concatenate([c_prompt, c_sample], axis=0)
    mod = _modulation(c_all, w_ada[0], b_ada[0]).reshape(nb + ndb, 6, d)
    mod_p = [mod[:nb, k][:, None, :] for k in range(6)]
    mod_s = [mod[nb:, k][None, :, :] for k in range(6)]
    W = _prep_w_in(w_in[0])
    P = _in_proj(x_prompt, mod_p[1], mod_p[0], norm_mix[0], W, ROW_TILE)
    S = _in_proj(x_sample.reshape(1, ndb, d), mod_s[1], mod_s[0], norm_mix[0], W, ndb)
    C = _prep_compress(cmp_pe_k[0], cmp_w1_k[0], cmp_w2_k[0], cmp_pe_v[0], cmp_w1_v[0], cmp_w2_v[0])
    kcvc_p = _compress_prompt(P["kcT"], C)
    ocmp_p, sel_p = _cmp_attn_prompt(P["q"], kcvc_p, ROW_TILE)
    o_nsa_p = _sel_win_prompt(P["q"], sel_p, P["g"], ocmp_p, P["ksT"], P["kwT"], ATTN_Q_TILE, ATTN_K_TILE)
    SP = _prep_ssd(conv_w[0], conv_b[0], dt_bias[0], a_log[0], d_skip[0], ssd_norm[0])
    y_ssd_p, ssm_p = _ssd_prompt(P["xbc"], P["z"], P["dt"], P["dtT"], SP)
    w_out_b = w_out[0].astype(BF16)
    router_wT = router_w[0].T
    EW = dict(w1=exp_w1[0], w3=exp_w3[0], w2=exp_w2[0],
              sw1=sh_w1[0].astype(BF16), sw3=sh_w3[0].astype(BF16), sw2=sh_w2[0].astype(BF16))
    x1_p, hp_p, selT_p, wT_p, cnt_p = _post_mix(x_prompt, o_nsa_p, y_ssd_p, mod_p[2], mod_p[4], mod_p[3], norm_ffn[0],
                                                 w_out_b, router_wT, router_bias[0], ROW_TILE)
    past = page_table.shape[1] * PAGE_SIZE
    to_pages = lambda c: jnp.transpose(c, (0, 2, 3, 4, 1)).reshape(c.shape[0], 256, c.shape[1])
    pool_cmp, pool_sel, win = to_pages(cache_kv_cmp[0]), to_pages(cache_kv_sel[0]), to_pages(cache_kv_win[0])
    kcvc_s = _compress_paged(pool_cmp, page_table, C)

    big = (nb * t) % (SC_WORKERS * SC_CHUNK) == 0
    y_prompt = _moe(x1_p, hp_p, selT_p, wT_p, cnt_p, mod_p[5], EW, norm_f, ROW_TILE, big, after=kcvc_s)

    q3 = S["q"].reshape(ndb, 1, NSA_DIM)
    ocmp_s, sel_s = _cmp_attn_sample(q3, kcvc_s, past)
    o_nsa_s, win_s = _sel_win_sample(q3, S["g"].reshape(ndb, 1, KV_HEADS * LANES), ocmp_s, sel_s, pool_sel, page_table,
                                     win, S["ksT"], S["kwT"], past)
    y_ssd_s, conv_s, ssm_s = _ssd_sample(jnp.transpose(state_conv[0], (1, 0, 2)), S["xbc"][0], S["z"][0], S["dt"][0],
                                         state_ssm[0], SP)
    x1_s, hp_s, selT_s, wT_s, cnt_s = _post_mix(x_sample.reshape(1, ndb, d), o_nsa_s.reshape(1, ndb, NSA_DIM),
                                                 y_ssd_s.reshape(1, ndb, D_INNER), mod_s[2], mod_s[4], mod_s[3],
                                                 norm_ffn[0], w_out_b, router_wT, router_bias[0], ndb)
    y_sample = _moe(x1_s, hp_s, selT_s, wT_s, cnt_s, mod_s[5], EW, norm_f, ndb, False).reshape(ndb, 1, d)

    from_cm = lambda a: jnp.transpose(a.reshape(a.shape[0], 2, KV_HEADS, HEAD_DIM, a.shape[2]), (0, 4, 1, 2, 3))[None]
    tw = min(WINDOW, t)
    return (y_prompt, y_sample,
            from_cm(P["kcT"]), from_cm(P["ksT"]), from_cm(P["kwT"][:, :, t - tw:]),
            P["xbc"][:, t - (CONV_W - 1):, :][None], ssm_p[None],
            from_cm(jnp.transpose(S["kcT"], (2, 1, 0))), from_cm(jnp.transpose(S["ksT"], (2, 1, 0))), from_cm(win_s),
            jnp.transpose(conv_s, (1, 0, 2))[None], ssm_s[None])
```

```python
import functools

import jax
import jax.numpy as jnp
import numpy as np
from jax import lax
from jax.experimental import pallas as pl
from jax.experimental.pallas import tpu as pltpu
from jax.experimental.pallas import tpu_sc as plsc

F32 = jnp.float32
BF16 = jnp.bfloat16
HIGHEST = lax.Precision.HIGHEST

D_MODEL = 1024
NSA_HEADS = 8
KV_HEADS = 2
HEAD_DIM = 64
GQA = NSA_HEADS // KV_HEADS
CMP_BLK = 32
CMP_STRIDE = 16
CMP_HID = 64
SEL_BLK = 64
N_SEL = 16
WINDOW = 512
FORCE_CUR = 2.0e4
FORCE_SINK = 1.0e4
SSD_HEADS = 8
SSD_HEAD_DIM = 64
D_INNER = SSD_HEADS * SSD_HEAD_DIM
SSD_GROUPS = 2
D_STATE = 128
CONV_W = 4
CONV_DIM = D_INNER + 2 * SSD_GROUPS * D_STATE
SSD_CHUNK = 128
NSA_DIM = NSA_HEADS * HEAD_DIM
KV_DIM = KV_HEADS * HEAD_DIM
N_EXPERTS = 64
TOP_K = 8
N_EXPERT_GROUPS = 8
TOPK_GROUPS = 4
D_EXPERT = 256
D_SHARED = 256
ROUTED_SCALE = 2.5
RMS_EPS = 1e-6
NEG_INF = -1e30
LOG2E = 1.4426950408889634
PAGE_SIZE = 128

LANES = 128
SUBLANES = 8
VMEM_LIMIT = 56 * 1024 * 1024
ROW_TILE = 512
ATTN_Q_TILE = 256
ATTN_K_TILE = 512


def _cparams(sem):
    return pltpu.CompilerParams(dimension_semantics=sem, vmem_limit_bytes=VMEM_LIMIT)


def _silu(v):
    return v * jax.nn.sigmoid(v)


def _mod_body(c_ref, w_ref, b_ref, o_ref):
    s = _silu(c_ref[...])
    o_ref[...] = jnp.dot(s, w_ref[...], preferred_element_type=F32, precision=HIGHEST) + b_ref[...]


def _modulation(c_all, w_ada, b_ada):
    n, d = c_all.shape
    nout = w_ada.shape[1]
    tn = 512
    return pl.pallas_call(
        _mod_body,
        grid=(nout // tn,),
        in_specs=[pl.BlockSpec((n, d), lambda j: (0, 0)),
                  pl.BlockSpec((d, tn), lambda j: (0, j)),
                  pl.BlockSpec((1, tn), lambda j: (0, j))],
        out_specs=pl.BlockSpec((n, tn), lambda j: (0, j)),
        out_shape=jax.ShapeDtypeStruct((n, nout), F32),
        compiler_params=_cparams(("arbitrary",)),
        name="modulation",
    )(c_all, w_ada, b_ada.reshape(1, nout))


def _nt_dot(a, b):
    return lax.dot_general(a, b, (((1,), (1,)), ((), ())), preferred_element_type=F32)


def _proj_body(x_ref, sc_ref, sh_ref, nw_ref, wq_ref, wkvT_ref, wg_ref, wz_ref, wx_ref, wdt_ref, wdtT_ref,
               q_ref, kcT_ref, ksT_ref, kwT_ref, g_ref, z_ref, xbc_ref, dt_ref, dtT_ref):
    x = x_ref[0]
    ms = jnp.mean(x * x, axis=-1, keepdims=True)
    h = x * lax.rsqrt(ms + RMS_EPS) * nw_ref[...]
    h = h * (1.0 + sc_ref[0]) + sh_ref[0]
    hb = h.astype(BF16)
    q_ref[0] = (jnp.dot(hb, wq_ref[...], preferred_element_type=F32) * (HEAD_DIM ** -0.5 * LOG2E)).astype(BF16)
    kvT = _nt_dot(wkvT_ref[...], hb)
    kcT_ref[0] = kvT[0:256]
    ksT_ref[0] = kvT[256:512]
    kwT_ref[0] = kvT[512:768]
    g_ref[0] = jax.nn.sigmoid(jnp.dot(hb, wg_ref[...], preferred_element_type=F32))
    z_ref[0] = jnp.dot(hb, wz_ref[...], preferred_element_type=F32)
    xbc_ref[0] = jnp.dot(hb, wx_ref[...], preferred_element_type=F32)
    dt_ref[0] = jnp.dot(hb, wdt_ref[...], preferred_element_type=F32)
    dtT_ref[0] = _nt_dot(wdtT_ref[...], hb)


def _prep_w_in(w_in):
    w = w_in
    o = 0
    wq = w[:, o:o + NSA_DIM]; o += NSA_DIM
    wkv = w[:, o:o + 6 * KV_DIM]; o += 6 * KV_DIM
    wg = w[:, o:o + 3 * NSA_HEADS]; o += 3 * NSA_HEADS
    wz = w[:, o:o + D_INNER]; o += D_INNER
    wx = w[:, o:o + CONV_DIM]; o += CONV_DIM
    wdt = w[:, o:o + SSD_HEADS]; o += SSD_HEADS
    pad = lambda a: jnp.pad(a, ((0, 0), (0, LANES - a.shape[1])))
    per = 3 * GQA
    wg = jnp.concatenate([pad(wg[:, k * per:(k + 1) * per]) for k in range(KV_HEADS)], axis=1)
    return dict(wq=wq.astype(BF16), wkvT=wkv.T.astype(BF16), wg=wg.astype(BF16), wz=wz.astype(BF16),
                wx=wx.astype(BF16), wdt=pad(wdt).astype(BF16), wdtT=wdt.T.astype(BF16))


def _in_proj(x, sc, sh, norm_w, W, tm):
    nb, t, d = x.shape
    mt = sc.shape[1]
    assert t % tm == 0 and (mt == 1 or mt == t)
    if mt == 1:
        mod_spec = pl.BlockSpec((1, 1, d), lambda b, i: (b, 0, 0))
    else:
        mod_spec = pl.BlockSpec((1, tm, d), lambda b, i: (b, i, 0))
    full = lambda a: pl.BlockSpec(a.shape, lambda b, i: (0,) * a.ndim)
    row = lambda n: pl.BlockSpec((1, tm, n), lambda b, i: (b, i, 0))
    col = lambda n: pl.BlockSpec((1, n, tm), lambda b, i: (b, 0, i))
    ws = [W["wq"], W["wkvT"], W["wg"], W["wz"], W["wx"], W["wdt"], W["wdtT"]]
    outs = pl.pallas_call(
        _proj_body,
        grid=(nb, t // tm),
        in_specs=[row(d), mod_spec, mod_spec, pl.BlockSpec((1, d), lambda b, i: (0, 0))] + [full(a) for a in ws],
        out_specs=[row(NSA_DIM), col(256), col(256), col(256), row(KV_HEADS * LANES), row(D_INNER), row(CONV_DIM),
                   row(LANES), col(SSD_HEADS)],
        out_shape=[jax.ShapeDtypeStruct((nb, t, NSA_DIM), BF16),
                   jax.ShapeDtypeStruct((nb, 256, t), F32),
                   jax.ShapeDtypeStruct((nb, 256, t), F32),
                   jax.ShapeDtypeStruct((nb, 256, t), F32),
                   jax.ShapeDtypeStruct((nb, t, KV_HEADS * LANES), F32),
                   jax.ShapeDtypeStruct((nb, t, D_INNER), F32),
                   jax.ShapeDtypeStruct((nb, t, CONV_DIM), F32),
                   jax.ShapeDtypeStruct((nb, t, LANES), F32),
                   jax.ShapeDtypeStruct((nb, SSD_HEADS, t), F32)],
        compiler_params=_cparams(("arbitrary", "arbitrary")),
        name="in_proj",
    )(x, sc, sh, norm_w.reshape(1, d), *ws)
    names = ("q", "kcT", "ksT", "kwT", "g", "z", "xbc", "dt", "dtT")
    return dict(zip(names, outs))


def _prep_compress(cmp_pe_k, cmp_w1_k, cmp_w2_k, cmp_pe_v, cmp_w1_v, cmp_w2_v):
    w1s = jnp.stack([cmp_w1_k, cmp_w1_v]).reshape(2, 2, CMP_STRIDE, HEAD_DIM, CMP_HID)
    eye = jnp.eye(2, dtype=F32)
    wbd = jnp.einsum("ktldh,kK,vV->lkvdtKVh", w1s, eye, eye).reshape(CMP_STRIDE, 256, 512)
    w2s = jnp.stack([cmp_w2_k, cmp_w2_v])
    w2bd = jnp.einsum("khd,kK,vV->kvhKVd", w2s, eye, eye).reshape(256, 256)
    pes = jnp.stack([cmp_pe_k, cmp_pe_v]).reshape(2, 2, CMP_STRIDE, HEAD_DIM)
    pe_rows = jnp.broadcast_to(jnp.transpose(pes, (1, 2, 0, 3))[:, :, :, None, :],
                               (2, CMP_STRIDE, 2, KV_HEADS, HEAD_DIM)).reshape(2, CMP_STRIDE * 256)
    pecat = jnp.pad(pe_rows, ((0, SUBLANES - 2), (0, 0)))
    perm = np.zeros((LANES, LANES), np.float32)
    for l in range(CMP_STRIDE):
        for n in range(LANES // CMP_STRIDE):
            perm[(LANES // CMP_STRIDE) * l + n, CMP_STRIDE * n + l] = 1.0
    bias = pl.pallas_call(
        _cmp_bias_body,
        out_shape=jax.ShapeDtypeStruct((SUBLANES, 512), F32),
        name="cmp_bias",
    )(pecat, wbd.reshape(CMP_STRIDE * 256, 512))
    return dict(wbd=wbd.astype(BF16), w2bd=w2bd.astype(BF16), bias=bias, perm=jnp.asarray(perm, BF16))


def _cmp_bias_body(pe_ref, w_ref, o_ref):
    o_ref[...] = jnp.dot(pe_ref[...], w_ref[...], preferred_element_type=F32, precision=HIGHEST)


def _compress_body(n_pref, n_slab_refs, slabs_per_ref, *refs):
    refs = refs[n_pref:]
    slab_refs = refs[:n_slab_refs]
    perm_ref, wbd_ref, bias_ref, w2bd_ref, o_ref, z_scr = refs[n_slab_refs:]
    j = pl.program_id(1)
    nh = z_scr.shape[1]
    per_slab = LANES // CMP_STRIDE
    g_tot = n_slab_refs * slabs_per_ref
    base = pl.multiple_of(j * (per_slab * g_tot), per_slab)
    perm = perm_ref[...]
    for ri in range(n_slab_refs):
        for si in range(slabs_per_ref):
            slab = slab_refs[ri][0][:, si * LANES:(si + 1) * LANES].astype(BF16)
            xp = _nt_dot(perm, slab)
            g = ri * slabs_per_ref + si
            for l in range(CMP_STRIDE):
                z_scr[l, pl.ds(base + per_slab * g, per_slab), :] = xp[per_slab * l:per_slab * (l + 1), :]

    @pl.when(j == pl.num_programs(1) - 1)
    def _():
        acc = jnp.zeros((nh, 512), F32)
        for l in range(CMP_STRIDE):
            acc = acc + jnp.dot(z_scr[l].astype(BF16), wbd_ref[l], preferred_element_type=F32)
        lead = acc[:, :256] + bias_ref[0:1, :256]
        tail = acc[:, 256:] + bias_ref[1:2, 256:]
        hid = _silu(lead + pltpu.roll(tail, nh - 1, 0))
        out = jnp.dot(hid.astype(BF16), w2bd_ref[...], preferred_element_type=F32)
        row = lax.broadcasted_iota(jnp.int32, out.shape, 0)
        o_ref[0] = jnp.where(row < nh - 1, out, 0.0)


def _compress_prompt(kcT, C):
    nb, _, t = kcT.shape
    nh = t // CMP_STRIDE
    g = min(8, t // LANES)
    nsteps = t // (LANES * g)
    full = lambda a: pl.BlockSpec(a.shape, lambda b, j: (0,) * a.ndim)
    return pl.pallas_call(
        functools.partial(_compress_body, 0, 1, g),
        grid=(nb, nsteps),
        in_specs=[pl.BlockSpec((1, 256, LANES * g), lambda b, j: (b, 0, j)),
                  full(C["perm"]), full(C["wbd"]), full(C["bias"]), full(C["w2bd"])],
        out_specs=pl.BlockSpec((1, nh, 256), lambda b, j: (b, 0, 0)),
        out_shape=jax.ShapeDtypeStruct((nb, nh, 256), F32),
        scratch_shapes=[pltpu.VMEM((CMP_STRIDE, nh, 256), F32)],
        compiler_params=_cparams(("arbitrary", "arbitrary")),
        name="compress_prompt",
    )(kcT, C["perm"], C["wbd"], C["bias"], C["w2bd"])


def _compress_paged(pool, page_table, C):
    nb, n_pages = page_table.shape
    nh = n_pages * (PAGE_SIZE // CMP_STRIDE)
    g = min(32, n_pages)
    nsteps = n_pages // g
    full = lambda a: pl.BlockSpec(a.shape, lambda b, j, pt: (0,) * a.ndim)
    page_spec = lambda k: pl.BlockSpec((1, 256, LANES), lambda b, j, pt: (pt[b, j * g + k], 0, 0))
    return pl.pallas_call(
        functools.partial(_compress_body, 1, g, 1),
        grid_spec=pltpu.PrefetchScalarGridSpec(
            num_scalar_prefetch=1,
            grid=(nb, nsteps),
            in_specs=[page_spec(k) for k in range(g)] + [full(C["perm"]), full(C["wbd"]), full(C["bias"]), full(C["w2bd"])],
            out_specs=pl.BlockSpec((1, nh, 256), lambda b, j, pt: (b, 0, 0)),
            scratch_shapes=[pltpu.VMEM((CMP_STRIDE, nh, 256), F32)],
        ),
        out_shape=jax.ShapeDtypeStruct((nb, nh, 256), F32),
        compiler_params=_cparams(("arbitrary", "arbitrary")),
        name="compress_paged",
    )(page_table, *([pool] * g), C["perm"], C["wbd"], C["bias"], C["w2bd"])


def _alibi_slope(head):
    return float(2.0 ** (-8.0 * (head + 1) / NSA_HEADS)) * LOG2E


def _overlap_T(nc, ns):
    cst = np.arange(nc)[None, :] * CMP_STRIDE
    sst = np.arange(ns)[:, None] * SEL_BLK
    ov = np.clip(np.minimum(cst + CMP_BLK, sst + SEL_BLK) - np.maximum(cst, sst), 0, None).astype(np.float32) / CMP_BLK
    return jnp.asarray(ov, F32)


def _masked_softmax(s, mask):
    s = jnp.where(mask, s, NEG_INF)
    p = jnp.exp2(s - jnp.max(s, axis=-1, keepdims=True)) * mask.astype(F32)
    return p / jnp.maximum(jnp.sum(p, axis=-1, keepdims=True), 1e-30)


def _topk_mask(grp, k, by_rounds=False):
    ngrp = len(grp)
    sub = lax.broadcasted_iota(jnp.int32, grp[0].shape, 0)
    one, zero = jnp.float32(1.0), jnp.float32(0.0)
    if by_rounds:
        rowidx = [sub + SUBLANES * c for c in range(ngrp)]
        vals = list(grp)
        keep = [jnp.zeros(grp[0].shape, F32) for _ in range(ngrp)]
        for _ in range(k):
            top = functools.reduce(jnp.maximum, vals)
            top = jnp.max(top, axis=0, keepdims=True)
            first = functools.reduce(jnp.minimum, [jnp.where(v == top, r, ngrp * SUBLANES) for v, r in zip(vals, rowidx)])
            first = jnp.min(first, axis=0, keepdims=True)
            for c in range(ngrp):
                hit = rowidx[c] == first
                keep[c] = jnp.where(hit, one, keep[c])
                vals[c] = jnp.where(hit, -jnp.inf, vals[c])
        return keep
    cnt =[jnp.zeros(grp[0].shape, F32) for _ in range(ngrp)]
    for j in range(ngrp * SUBLANES):
        a, r = divmod(j, SUBLANES)
        row = grp[a][r:r + 1, :]
        for c in range(ngrp):
            if c < a:
                beats = jnp.where(row > grp[c], one, zero)
            elif c > a:
                beats = jnp.where(row >= grp[c], one, zero)
            else:
                beats = jnp.where(sub > r, jnp.where(row >= grp[c], one, zero), jnp.where(row > grp[c], one, zero))
            cnt[c] = cnt[c] + beats
    return [jnp.where(c < float(k), one, zero) for c in cnt]


def _select_blocks(imp, cur):
    ns = imp.shape[0]
    jrow = lax.broadcasted_iota(jnp.int32, imp.shape, 0)
    imp = jnp.where(jrow == cur, FORCE_CUR, jnp.where(jrow == 0, FORCE_SINK, imp))
    imp = jnp.where(jrow <= cur, imp, NEG_INF)
    assert ns % SUBLANES == 0
    grp = [imp[SUBLANES * a:SUBLANES * (a + 1)] for a in range(ns // SUBLANES)]
    return jnp.concatenate(_topk_mask(grp, N_SEL), axis=0)


def _cmp_attn_body(q_ref, kcvc_ref, ovT_ref, eye_ref, ocmp_ref, sel_ref):
    i = pl.program_id(1)
    tq = q_ref.shape[1]
    nh = kcvc_ref.shape[1]
    t0 = i * tq
    qpos_col = t0 + lax.broadcasted_iota(jnp.int32, (tq, 1), 0)
    cend = lax.broadcasted_iota(jnp.int32, (1, nh), 1) * CMP_STRIDE + (CMP_BLK - 1)
    mask = cend <= qpos_col
    dist = (qpos_col - cend).astype(F32)
    qpos_row = t0 + lax.broadcasted_iota(jnp.int32, (1, tq), 1)
    cur = qpos_row // SEL_BLK
    kcvc = kcvc_ref[0]
    for kvh in range(KV_HEADS):
        kc = kcvc[:, kvh * HEAD_DIM:(kvh + 1) * HEAD_DIM].astype(BF16)
        vc = kcvc[:, KV_DIM + kvh * HEAD_DIM:KV_DIM + (kvh + 1) * HEAD_DIM].astype(BF16)
        psum = jnp.zeros((tq, nh), F32)
        for g in range(GQA):
            head = kvh * GQA + g
            qg = q_ref[0, :, head * HEAD_DIM:(head + 1) * HEAD_DIM]
            s = _nt_dot(qg, kc) - _alibi_slope(head) * dist
            p = _masked_softmax(s, mask)
            ocmp_ref[0, :, head * HEAD_DIM:(head + 1) * HEAD_DIM] = jnp.dot(p.astype(BF16), vc, preferred_element_type=F32)
            psum = psum + p
        impT = lax.dot_general(ovT_ref[...], psum, (((1,), (1,)), ((), ())), preferred_element_type=F32,
                               precision=HIGHEST)
        selT = _select_blocks(impT, cur)
        sel = _nt_dot(eye_ref[...], selT.astype(BF16))
        sel_ref[0, kvh] = sel.astype(BF16)


def _cmp_attn_prompt(q, kcvc, tq):
    nb, t, _ = q.shape
    nh = kcvc.shape[1]
    ns = t // SEL_BLK
    ovT = _overlap_T(nh, ns)
    eye = jnp.eye(tq, dtype=BF16)
    return pl.pallas_call(
        _cmp_attn_body,
        grid=(nb, t // tq),
        in_specs=[pl.BlockSpec((1, tq, NSA_DIM), lambda b, i: (b, i, 0)),
                  pl.BlockSpec((1, nh, 256), lambda b, i: (b, 0, 0)),
                  pl.BlockSpec(ovT.shape, lambda b, i: (0, 0)),
                  pl.BlockSpec(eye.shape, lambda b, i: (0, 0))],
        out_specs=[pl.BlockSpec((1, tq, NSA_DIM), lambda b, i: (b, i, 0)),
                   pl.BlockSpec((1, KV_HEADS, tq, ns), lambda b, i: (b, 0, i, 0))],
        out_shape=[jax.ShapeDtypeStruct((nb, t, NSA_DIM), F32),
                   jax.ShapeDtypeStruct((nb, KV_HEADS, t, ns), BF16)],
        compiler_params=_cparams(("arbitrary", "arbitrary")),
        name="cmp_attn",
    )(q, kcvc, ovT, eye)


def _head_rows(qrow, kvh):
    rows = [qrow[:, (kvh * GQA + g) * HEAD_DIM:(kvh * GQA + g + 1) * HEAD_DIM] for g in range(GQA)]
    return jnp.concatenate(rows + [jnp.zeros((SUBLANES - GQA, HEAD_DIM), qrow.dtype)], axis=0)


def _slope_col(kvh):
    r = lax.broadcasted_iota(jnp.int32, (SUBLANES, 1), 0)
    col = jnp.zeros((SUBLANES, 1), F32)
    for g in range(GQA):
        col = jnp.where(r == g, _alibi_slope(kvh * GQA + g), col)
    return col


def _cmp_attn_sample_body(past, q_ref, kcvc_ref, ov_ref, ocmp_ref, imp_ref):
    nh = kcvc_ref.shape[1]
    nsl = ov_ref.shape[1]
    cend = lax.broadcasted_iota(jnp.int32, (1, nh), 1) * CMP_STRIDE + (CMP_BLK - 1)
    mask = cend <= past
    dist = (past - cend).astype(F32)
    kcvc = kcvc_ref[0]
    qrow = q_ref[0]
    for kvh in range(KV_HEADS):
        kc = kcvc[:, kvh * HEAD_DIM:(kvh + 1) * HEAD_DIM].astype(BF16)
        vc = kcvc[:, KV_DIM + kvh * HEAD_DIM:KV_DIM + (kvh + 1) * HEAD_DIM].astype(BF16)
        s = _nt_dot(_head_rows(qrow, kvh), kc) - _slope_col(kvh) * dist
        p = _masked_softmax(s, mask)
        o = jnp.dot(p.astype(BF16), vc, preferred_element_type=F32)
        for g in range(GQA):
            head = kvh * GQA + g
            ocmp_ref[0, :, head * HEAD_DIM:(head + 1) * HEAD_DIM] = o[g:g + 1, :]
        psum = jnp.broadcast_to(jnp.sum(p[0:GQA], axis=0, keepdims=True), (SUBLANES, nh))
        imp = jnp.dot(psum, ov_ref[...], preferred_element_type=F32, precision=HIGHEST)
        imp_ref[0, :, kvh * nsl:(kvh + 1) * nsl] = imp[0:1, :]


def _select_sample_body(past, ns_pad, imp_ref, sel_ref):
    nb = imp_ref.shape[0]
    nsl = imp_ref.shape[1] // KV_HEADS
    cur = jnp.full((1, nb), past // SEL_BLK, jnp.int32)
    for kvh in range(KV_HEADS):
        impT = jnp.transpose(imp_ref[:, kvh * nsl:(kvh + 1) * nsl])
        selT = _select_blocks(impT[0:ns_pad], cur)
        selT = jnp.concatenate([selT, jnp.zeros((nsl - ns_pad, nb), F32)], axis=0)
        sel_ref[:, kvh * nsl:(kvh + 1) * nsl] = jnp.transpose(selT).astype(jnp.int32)


def _cmp_attn_sample(q3, kcvc, past):
    nb = q3.shape[0]
    nh = kcvc.shape[1]
    ns = past // SEL_BLK + 1
    ns_pad = -(-ns // SUBLANES) * SUBLANES
    nsl = -(-ns // LANES) * LANES
    ov = jnp.pad(_overlap_T(nh, ns), ((0, nsl - ns), (0, 0))).T
    ocmp, imp = pl.pallas_call(
        functools.partial(_cmp_attn_sample_body, past),
        grid=(nb,),
        in_specs=[pl.BlockSpec((1, 1, NSA_DIM), lambda b: (b, 0, 0)),
                  pl.BlockSpec((1, nh, 256), lambda b: (b, 0, 0)),
                  pl.BlockSpec(ov.shape, lambda b: (0, 0))],
        out_specs=[pl.BlockSpec((1, 1, NSA_DIM), lambda b: (b, 0, 0)),
                   pl.BlockSpec((1, 1, KV_HEADS * nsl), lambda b: (b, 0, 0))],
        out_shape=[jax.ShapeDtypeStruct((nb, 1, NSA_DIM), F32),
                   jax.ShapeDtypeStruct((nb, 1, KV_HEADS * nsl), F32)],
        compiler_params=_cparams(("arbitrary",)),
        name="cmp_attn_sample",
    )(q3, kcvc, ov)
    sel = pl.pallas_call(
        functools.partial(_select_sample_body, past, ns_pad),
        out_shape=jax.ShapeDtypeStruct((nb, KV_HEADS * nsl), jnp.int32),
        name="select_sample",
    )(imp.reshape(nb, KV_HEADS * nsl))
    return ocmp, sel


N_SPLIT = 3


def _pos_rows(n, start=0):
    tab = np.zeros((HEAD_DIM, n), np.float32)
    k = start + np.arange(n)
    tab[0:N_SPLIT] = k // SEL_BLK
    tab[N_SPLIT:2 * N_SPLIT] = k % SEL_BLK
    return jnp.asarray(tab, BF16)


def _slope_rows():
    bf = lambda v: np.asarray(v, dtype=BF16).astype(np.float32)
    tab = np.zeros((KV_HEADS, SUBLANES, HEAD_DIM), np.float32)
    for k in range(KV_HEADS):
        for g in range(GQA):
            for c, val in enumerate((SEL_BLK * _alibi_slope(k * GQA + g), _alibi_slope(k * GQA + g))):
                rest = np.float32(val)
                for j in range(N_SPLIT):
                    piece = bf(rest)
                    tab[k, g, c * N_SPLIT + j] = piece
                    rest = np.float32(rest - piece)
    return jnp.asarray(tab, F32)


def _block_expand(ns, n):
    return jnp.asarray((np.arange(n)[None, :] // SEL_BLK == np.arange(ns)[:, None]).astype(np.float32), BF16)


def _flash_step(q4, kT_aug, vT, bias, m_scr, acc_scr):
    n, tk = q4.shape[0], kT_aug.shape[1]
    s = jnp.dot(q4, kT_aug, preferred_element_type=F32)
    rb = bias.shape[0]
    if rb in (1, n):
        s = s + bias
    else:
        s = (s.reshape(n // rb, rb, tk) + bias[None]).reshape(n, tk)
    m_old = m_scr[...]
    m_new = jnp.maximum(m_old, jnp.max(s, axis=-1, keepdims=True))
    alpha = jnp.exp2(m_old - m_new)
    p = jnp.exp2(s - jnp.concatenate([m_new] * (tk // LANES), axis=1))
    v_ones = jnp.concatenate([vT, jnp.ones((LANES - HEAD_DIM, tk), BF16)], axis=0)
    acc_scr[...] = alpha * acc_scr[...] + _nt_dot(p.astype(BF16), v_ones)
    m_scr[...] = m_new


def _flash_result(acc_scr):
    acc = acc_scr[...]
    return acc[:, :HEAD_DIM] / jnp.maximum(acc[:, HEAD_DIM:], 1e-30)


def _sel_win_body(tk, q_ref, sel_ref, g_ref, ocmp_ref, ksT_ref, vsT_ref, kwT_ref, vwT_ref, pos_ref, exp_ref, slope_ref,
                  o_ref, m_scr, acc_scr):
    i = pl.program_id(2)
    tq = q_ref.shape[1]
    t0 = i * tq
    qpos = t0 + lax.broadcasted_iota(jnp.int32, (tq, 1), 0)
    q4 = jnp.concatenate(
        [jnp.concatenate([q_ref[0, :, g * HEAD_DIM:(g + 1) * HEAD_DIM],
                          jnp.broadcast_to(slope_ref[0, g:g + 1, :], (tq, HEAD_DIM)).astype(BF16)], axis=1)
         for g in range(GQA)], axis=0)

    def reset():
        m_scr[...] = jnp.full(m_scr.shape, NEG_INF, F32)
        acc_scr[...] = jnp.zeros(acc_scr.shape, F32)

    def finish():
        return _flash_result(acc_scr)

    reset()
    sel = sel_ref[0, 0]

    def sel_step(kt, carry):
        k0 = pl.multiple_of(kt * tk, tk)
        kpos = k0 + lax.broadcasted_iota(jnp.int32, (1, tk), 1)
        chosen = jnp.dot(sel, exp_ref[:, pl.ds(k0, tk)], preferred_element_type=F32)
        bias = (jnp.where(kpos <= qpos, chosen, 0.0) - 1.0) * (-NEG_INF)
        kT_aug = jnp.concatenate([ksT_ref[0, :, pl.ds(k0, tk)].astype(BF16), pos_ref[:, pl.ds(k0, tk)]], axis=0)
        _flash_step(q4, kT_aug, vsT_ref[0, :, pl.ds(k0, tk)].astype(BF16), bias, m_scr, acc_scr)
        return carry

    lax.fori_loop(0, (t0 + tq + tk - 1) // tk, sel_step, 0)
    o_sel = finish()

    reset()
    wk = WINDOW + tq
    k0 = pl.multiple_of(jnp.maximum(t0 - WINDOW, 0), tq)
    dist = qpos - (k0 + lax.broadcasted_iota(jnp.int32, (1, wk), 1))
    bias = jnp.where(lax.bitcast_convert_type(dist, jnp.uint32) < jnp.uint32(WINDOW), 0.0, NEG_INF)
    kT_aug = jnp.concatenate([kwT_ref[0, :, pl.ds(k0, wk)].astype(BF16), pos_ref[:, pl.ds(k0, wk)]], axis=0)
    _flash_step(q4, kT_aug, vwT_ref[0, :, pl.ds(k0, wk)].astype(BF16), bias, m_scr, acc_scr)
    o_win = finish()
    gates = g_ref[0]
    for g in range(GQA):
        rows = slice(g * tq, (g + 1) * tq)
        cols = slice(g * HEAD_DIM, (g + 1) * HEAD_DIM)
        o = (gates[:, 3 * g:3 * g + 1] * ocmp_ref[0, :, cols] + gates[:, 3 * g + 1:3 * g + 2] * o_sel[rows]
             + gates[:, 3 * g + 2:3 * g + 3] * o_win[rows])
        o_ref[0, :, cols] = o.astype(o_ref.dtype)


def _sel_win_prompt(q, sel, gates, ocmp, ksT, kwT, tq, tk):
    nb, t, _ = q.shape
    ns = sel.shape[-1]
    assert t % tk == 0 and t % tq == 0 and WINDOW % tq == 0
    pos, expand, slopes = _pos_rows(t), _block_expand(ns, t), _slope_rows()
    grp = GQA * HEAD_DIM
    kv_spec = lambda which: pl.BlockSpec((1, HEAD_DIM, t), lambda b, k, i: (b, which * KV_HEADS + k, 0))
    return pl.pallas_call(
        functools.partial(_sel_win_body, tk),
        grid=(nb, KV_HEADS, t // tq),
        in_specs=[pl.BlockSpec((1, tq, grp), lambda b, k, i: (b, i, k)),
                  pl.BlockSpec((1, 1, tq, ns), lambda b, k, i: (b, k, i, 0)),
                  pl.BlockSpec((1, tq, LANES), lambda b, k, i: (b, i, k)),
                  pl.BlockSpec((1, tq, grp), lambda b, k, i: (b, i, k)),
                  kv_spec(0), kv_spec(1), kv_spec(0), kv_spec(1),
                  pl.BlockSpec(pos.shape, lambda b, k, i: (0, 0)),
                  pl.BlockSpec(expand.shape, lambda b, k, i: (0, 0)),
                  pl.BlockSpec((1, SUBLANES, HEAD_DIM), lambda b, k, i: (k, 0, 0))],
        out_specs=pl.BlockSpec((1, tq, grp), lambda b, k, i: (b, i, k)),
        out_shape=jax.ShapeDtypeStruct((nb, t, NSA_DIM), BF16),
        scratch_shapes=[pltpu.VMEM((GQA * tq, LANES), F32), pltpu.VMEM((GQA * tq, LANES), F32)],
        compiler_params=_cparams(("arbitrary", "arbitrary", "arbitrary")),
        name="sel_win_attn",
    )(q, sel, gates, ocmp, ksT, ksT, kwT, kwT, pos, expand, slopes)


def _sel_win_sample_body(past, g_pages, pt_ref, sel_ref, q_ref, g_ref, ocmp_ref, *refs):
    page_refs = refs[:g_pages]
    (win_ref, ksn_ref, kwn_ref, pos_ref, wpos_ref, slope_ref, o_ref, wout_ref, m_scr, acc_scr) = refs[g_pages:]
    b = pl.program_id(0)
    j = pl.program_id(1)
    ns_pad = sel_ref.shape[1] // KV_HEADS
    qrow = q_ref[0]
    q4 = [jnp.concatenate([_head_rows(qrow, k), slope_ref[k].astype(BF16)], axis=1) for k in range(KV_HEADS)]

    @pl.when(j == 0)
    def _():
        m_scr[...] = jnp.full(m_scr.shape, NEG_INF, F32)
        acc_scr[...] = jnp.zeros(acc_scr.shape, F32)

    width = g_pages * PAGE_SIZE
    lane_blk = lax.broadcasted_iota(jnp.int32, (1, width), 1) // SEL_BLK
    row0 = lax.broadcasted_iota(jnp.int32, (HEAD_DIM, 1), 0) < N_SPLIT
    blk0 = j * (width // SEL_BLK)
    pos = (pos_ref[...].astype(F32) + jnp.where(row0, blk0.astype(F32), 0.0)).astype(BF16)
    for k in range(KV_HEADS):
        kT = jnp.concatenate([r[0, k * HEAD_DIM:(k + 1) * HEAD_DIM, :] for r in page_refs], axis=1).astype(BF16)
        vT = jnp.concatenate([r[0, KV_DIM + k * HEAD_DIM:KV_DIM + (k + 1) * HEAD_DIM, :] for r in page_refs],
                             axis=1).astype(BF16)
        bias = jnp.full((1, width), NEG_INF, F32)
        for blk in range(width // SEL_BLK):
            chosen = sel_ref[b, k * ns_pad + blk0 + blk] > 0
            bias = jnp.where(lane_blk == blk, jnp.where(chosen, 0.0, NEG_INF), bias)
        _flash_step(q4[k], jnp.concatenate([kT, pos], axis=0), vT, bias, m_scr.at[k], acc_scr.at[k])

    @pl.when(j == pl.num_programs(1) - 1)
    def _():
        nb = ksn_ref.shape[2]
        pick = lax.broadcasted_iota(jnp.int32, (1, nb), 1) == b
        ks_new = jnp.sum(jnp.where(pick, ksn_ref[0], 0.0), axis=1, keepdims=True)
        kw_new = jnp.sum(jnp.where(pick, kwn_ref[0], 0.0), axis=1, keepdims=True)
        lane = lax.broadcasted_iota(jnp.int32, (1, LANES), 1)
        tile_new = jnp.where(lane == 0, ks_new, 0.0).astype(BF16)
        pos_new = jnp.where(row0 & (lane == 0), float(past // SEL_BLK), 0.0).astype(BF16)
        bias_new = jnp.where(lane == 0, 0.0, NEG_INF)
        wlane = lax.broadcasted_iota(jnp.int32, (1, win_ref.shape[2]), 1)
        wout = jnp.where(wlane == win_ref.shape[2] - 1, kw_new, pltpu.roll(win_ref[0], win_ref.shape[2] - 1, 1))
        wout_ref[0] = wout
        woutb = wout.astype(BF16)
        gates = g_ref[0]
        for k in range(KV_HEADS):
            ksl = slice(k * HEAD_DIM, (k + 1) * HEAD_DIM)
            vsl = slice(KV_DIM + k * HEAD_DIM, KV_DIM + (k + 1) * HEAD_DIM)
            _flash_step(q4[k], jnp.concatenate([tile_new[ksl], pos_new], axis=0), tile_new[vsl], bias_new,
                        m_scr.at[k], acc_scr.at[k])
            o_sel = _flash_result(acc_scr.at[k])
            m_scr[k] = jnp.full(m_scr.shape[1:], NEG_INF, F32)
            acc_scr[k] = jnp.zeros(acc_scr.shape[1:], F32)
            _flash_step(q4[k], jnp.concatenate([woutb[ksl], wpos_ref[...]], axis=0), woutb[vsl],
                        jnp.zeros((1, win_ref.shape[2]), F32), m_scr.at[k], acc_scr.at[k])
            o_win = _flash_result(acc_scr.at[k])
            for g in range(GQA):
                head = k * GQA + g
                cols = slice(head * HEAD_DIM, (head + 1) * HEAD_DIM)
                c0 = k * LANES + 3 * g
                o = (gates[:, c0:c0 + 1] * ocmp_ref[0, :, cols] + gates[:, c0 + 1:c0 + 2] * o_sel[g:g + 1, :]
                     + gates[:, c0 + 2:c0 + 3] * o_win[g:g + 1, :])
                o_ref[0, :, cols] = o.astype(o_ref.dtype)


def _sel_win_sample(q3, g3, ocmp, sel, pool_sel, page_table, win, ksT_new, kwT_new, past):
    nb, n_pages = page_table.shape
    wlen = win.shape[2]
    assert wlen == WINDOW and past >= WINDOW
    g = min(32, n_pages)
    pos, wpos, slopes = _pos_rows(g * PAGE_SIZE), _pos_rows(wlen, past - wlen + 1), _slope_rows()
    full = lambda a: pl.BlockSpec(a.shape, lambda b, j, pt, sl: (0,) * a.ndim)
    row = lambda n: pl.BlockSpec((1, 1, n), lambda b, j, pt, sl: (b, 0, 0))
    page_spec = lambda k: pl.BlockSpec((1, 256, PAGE_SIZE), lambda b, j, pt, sl: (pt[b, j * g + k], 0, 0))
    wspec = pl.BlockSpec((1, 256, wlen), lambda b, j, pt, sl: (b, 0, 0))
    return pl.pallas_call(
        functools.partial(_sel_win_sample_body, past, g),
        grid_spec=pltpu.PrefetchScalarGridSpec(
            num_scalar_prefetch=2,
            grid=(nb, n_pages // g),
            in_specs=[row(NSA_DIM), row(KV_HEADS * LANES), row(NSA_DIM)] + [page_spec(k) for k in range(g)]
                     + [wspec, full(ksT_new), full(kwT_new), full(pos), full(wpos), full(slopes)],
            out_specs=[row(NSA_DIM), wspec],
            scratch_shapes=[pltpu.VMEM((KV_HEADS, SUBLANES, LANES), F32), pltpu.VMEM((KV_HEADS, SUBLANES, LANES), F32)],
        ),
        out_shape=[jax.ShapeDtypeStruct((nb, 1, NSA_DIM), BF16), jax.ShapeDtypeStruct((nb, 256, wlen), F32)],
        compiler_params=_cparams(("arbitrary", "arbitrary")),
        name="sel_win_sample",
    )(page_table, sel, q3, g3, ocmp, *([pool_sel] * g), win, ksT_new, kwT_new, pos,
      wpos, slopes)


def _softplus(v):
    return jnp.maximum(v, 0.0) + jnp.log1p(jnp.exp(-jnp.abs(v)))


def _tn_dot(a, b):
    return lax.dot_general(a, b, (((0,), (0,)), ((), ())), preferred_element_type=F32)


def _prep_ssd(conv_w, conv_b, dt_bias, a_log, d_skip, ssd_norm):
    padl = lambda v: jnp.pad(v.reshape(1, -1), ((0, 0), (0, LANES - v.shape[0])))
    L = SSD_CHUNK
    tril = jnp.asarray(np.tril(np.ones((L, L), np.float32)))
    return dict(conv_w=conv_w, conv_b=conv_b.reshape(1, -1), dtb_row=padl(dt_bias), dtb_col=dt_bias.reshape(-1, 1),
                alog_row=padl(a_log), alog_col=a_log.reshape(-1, 1), dskip=padl(d_skip), norm=ssd_norm.reshape(1, -1),
                tril=tril, triu=tril.T)


def _ssd_chunk(u, z, dt, dtT, h_prev, P):
    a_row = -jnp.exp(P["alog_row"][...])
    a_col = -jnp.exp(P["alog_col"][...])
    acum = jnp.dot(P["tril"][...], dt * a_row, preferred_element_type=F32, precision=HIGHEST)
    acumT = jnp.dot(dtT * a_col, P["triu"][...], preferred_element_type=F32, precision=HIGHEST)
    L = u.shape[0]
    li = lax.broadcasted_iota(jnp.int32, (L, L), 0)
    si = lax.broadcasted_iota(jnp.int32, (L, L), 1)
    causal = li >= si
    gn = SSD_GROUPS * D_STATE
    ys, hs = [], []
    per = SSD_HEADS // SSD_GROUPS
    for g in range(SSD_GROUPS):
        bm = u[:, D_INNER + g * D_STATE:D_INNER + (g + 1) * D_STATE]
        cm = u[:, D_INNER + gn + g * D_STATE:D_INNER + gn + (g + 1) * D_STATE]
        bmb = bm.astype(BF16)
        cb = _nt_dot(cm.astype(BF16), bmb)
        for e in range(per):
            h = g * per + e
            ac = acum[:, h:h + 1]
            seg = ac - acumT[h:h + 1, :]
            decay = jnp.where(causal, jnp.exp(jnp.where(causal, seg, 0.0)), 0.0)
            xs = u[:, h * SSD_HEAD_DIM:(h + 1) * SSD_HEAD_DIM]
            xdt = xs * dt[:, h:h + 1]
            y = jnp.dot((cb * decay).astype(BF16), xdt.astype(BF16), preferred_element_type=F32)
            a_last = acum[L - 1:L, h:h + 1]
            st = _tn_dot((xdt * jnp.exp(a_last - ac)).astype(BF16), bmb)
            y = y + _nt_dot((cm * jnp.exp(ac)).astype(BF16), h_prev[h].astype(BF16))
            hs.append(jnp.exp(a_last) * h_prev[h] + st)
            ys.append(y + P["dskip"][:, h:h + 1] * xs)
    return ys, hs


def _ssd_finish(ys, z, norm_w):
    y = jnp.concatenate(ys, axis=1) * _silu(z)
    ms = jnp.mean(y * y, axis=-1, keepdims=True)
    return y * lax.rsqrt(ms + RMS_EPS) * norm_w


def _ssd_prompt_body(xbc_ref, z_ref, dt_ref, dtT_ref, cw_ref, cb_ref, dtbr_ref, dtbc_ref, alr_ref, alc_ref, dsk_ref,
                     nrm_ref, tril_ref, triu_ref, y_ref, hout_ref, xpad_scr, h_scr):
    c = pl.program_id(1)
    L = xbc_ref.shape[1]

    @pl.when(c == 0)
    def _():
        xpad_scr[0:SUBLANES, :] = jnp.zeros((SUBLANES, xpad_scr.shape[1]), F32)
        h_scr[...] = jnp.zeros(h_scr.shape, F32)

    xt = xbc_ref[0]
    xpad_scr[SUBLANES:SUBLANES + L, :] = xt
    conv = cb_ref[...] + xpad_scr[SUBLANES - (CONV_W - 1):SUBLANES - (CONV_W - 1) + L, :] * cw_ref[0:1, :]
    for k in range(1, CONV_W):
        o = SUBLANES - (CONV_W - 1) + k
        conv = conv + xpad_scr[o:o + L, :] * cw_ref[k:k + 1, :]
    xpad_scr[0:SUBLANES, :] = xt[L - SUBLANES:L, :]
    u = _silu(conv)
    dt = _softplus(dt_ref[0] + dtbr_ref[...])
    dtT = _softplus(dtT_ref[0] + dtbc_ref[...])
    P = dict(alog_row=alr_ref, alog_col=alc_ref, tril=tril_ref, triu=triu_ref, dskip=dsk_ref[...])
    ys, hs = _ssd_chunk(u, z_ref[0], dt, dtT, [h_scr[h] for h in range(SSD_HEADS)], P)
    for h in range(SSD_HEADS):
        h_scr[h] = hs[h]
    y_ref[0] = _ssd_finish(ys, z_ref[0], nrm_ref[...]).astype(y_ref.dtype)

    @pl.when(c == pl.num_programs(1) - 1)
    def _():
        hout_ref[0] = h_scr[...]


def _ssd_prompt(xbc, z, dt, dtT, SP):
    nb, t, cd = xbc.shape
    L = SSD_CHUNK
    assert t % L == 0
    full = lambda a: pl.BlockSpec(a.shape, lambda b, c: (0,) * a.ndim)
    names = ("conv_w", "conv_b", "dtb_row", "dtb_col", "alog_row", "alog_col", "dskip", "norm", "tril", "triu")
    ps = [SP[n] for n in names]
    return pl.pallas_call(
        _ssd_prompt_body,
        grid=(nb, t // L),
        in_specs=[pl.BlockSpec((1, L, cd), lambda b, c: (b, c, 0)),
                  pl.BlockSpec((1, L, D_INNER), lambda b, c: (b, c, 0)),
                  pl.BlockSpec((1, L, LANES), lambda b, c: (b, c, 0)),
                  pl.BlockSpec((1, SSD_HEADS, L), lambda b, c: (b, 0, c))] + [full(a) for a in ps],
        out_specs=[pl.BlockSpec((1, L, D_INNER), lambda b, c: (b, c, 0)),
                   pl.BlockSpec((1, SSD_HEADS, SSD_HEAD_DIM, D_STATE), lambda b, c: (b, 0, 0, 0))],
        out_shape=[jax.ShapeDtypeStruct((nb, t, D_INNER), BF16),
                   jax.ShapeDtypeStruct((nb, SSD_HEADS, SSD_HEAD_DIM, D_STATE), F32)],
        scratch_shapes=[pltpu.VMEM((SUBLANES + L, cd), F32), pltpu.VMEM((SSD_HEADS, SSD_HEAD_DIM, D_STATE), F32)],
        compiler_params=_cparams(("arbitrary", "arbitrary")),
        name="ssd_prompt",
    )(xbc, z, dt, dtT, *ps)


def _ssd_sample_body(cs_ref, xbc_ref, z_ref, dt_ref, h0_ref, cw_ref, cb_ref, dtb_ref, al_ref, dsk_ref, nrm_ref, eye_ref,
                     y_ref, cso_ref, h_ref):
    nseq = xbc_ref.shape[0]
    xn = xbc_ref[...]
    conv = cb_ref[...] + xn * cw_ref[CONV_W - 1:CONV_W, :]
    for k in range(CONV_W - 1):
        conv = conv + cs_ref[k] * cw_ref[k:k + 1, :]
        if k > 0:
            cso_ref[k - 1] = cs_ref[k]
    cso_ref[CONV_W - 2] = xn
    u = _silu(conv)
    dt = _softplus(dt_ref[...] + dtb_ref[...])
    decay = jnp.exp(dt * (-jnp.exp(al_ref[...])))
    eye = eye_ref[...]
    gn = SSD_GROUPS * D_STATE
    per = SSD_HEADS // SSD_GROUPS
    rows = []
    for s in range(nseq):
        ys = []
        for h in range(SSD_HEADS):
            g = h // per
            xs = u[s:s + 1, h * SSD_HEAD_DIM:(h + 1) * SSD_HEAD_DIM]
            bm = u[s:s + 1, D_INNER + g * D_STATE:D_INNER + (g + 1) * D_STATE]
            cm = u[s:s + 1, D_INNER + gn + g * D_STATE:D_INNER + gn + (g + 1) * D_STATE]
            xcol = jnp.sum(eye * xs, axis=1, keepdims=True)
            hn = decay[s:s + 1, h:h + 1] * h0_ref[s, h] + (dt[s:s + 1, h:h + 1] * xcol) * bm
            h_ref[s, h] = hn
            ycol = jnp.sum(hn * cm, axis=1, keepdims=True)
            ys.append(jnp.sum(eye * ycol, axis=0, keepdims=True) + dsk_ref[:, h:h + 1] * xs)
        rows.append(jnp.concatenate(ys, axis=1))
    y = jnp.concatenate(rows, axis=0) * _silu(z_ref[...])
    ms = jnp.mean(y * y, axis=-1, keepdims=True)
    y_ref[...] = (y * lax.rsqrt(ms + RMS_EPS) * nrm_ref[...]).astype(y_ref.dtype)


def _ssd_sample(conv_state, xbc, z, dt, h0, SP):
    nb, cd = xbc.shape
    ts = SUBLANES
    assert nb % ts == 0
    eye = jnp.eye(SSD_HEAD_DIM, dtype=F32)
    names = ("conv_w", "conv_b", "dtb_row", "alog_row", "dskip", "norm")
    ps = [SP[n] for n in names] + [eye]
    full = lambda a: pl.BlockSpec(a.shape, lambda i: (0,) * a.ndim)
    st = pl.BlockSpec((ts, SSD_HEADS, SSD_HEAD_DIM, D_STATE), lambda i: (i, 0, 0, 0))
    cs = pl.BlockSpec((CONV_W - 1, ts, cd), lambda i: (0, i, 0))
    row = lambda n: pl.BlockSpec((ts, n), lambda i: (i, 0))
    return pl.pallas_call(
        _ssd_sample_body,
        grid=(nb // ts,),
        in_specs=[cs, row(cd), row(D_INNER), row(LANES), st] + [full(a) for a in ps],
        out_specs=[row(D_INNER), cs, st],
        out_shape=[jax.ShapeDtypeStruct((nb, D_INNER), BF16),
                   jax.ShapeDtypeStruct((CONV_W - 1, nb, cd), F32),
                   jax.ShapeDtypeStruct(h0.shape, F32)],
        compiler_params=_cparams(("arbitrary",)),
        name="ssd_sample",
    )(conv_state, xbc, z, dt, h0, *ps)


def _pack_bf16_pairs(v):
    m = v.shape[1] // 2
    hi = pltpu.bitcast(v[:, :m].astype(BF16).astype(F32), jnp.uint32)
    lo = pltpu.bitcast(v[:, m:].astype(BF16).astype(F32), jnp.uint32)
    return hi | (lo >> 16)


def _unpack_pairs_f32(w):
    return pltpu.bitcast(w & jnp.uint32(0xFFFF0000), F32), pltpu.bitcast(w << 16, F32)


def _unpack_bf16_pairs(w):
    hi, lo = _unpack_pairs_f32(w)
    return hi.astype(BF16), lo.astype(BF16)


def _route(logitsT, bias_col):
    s = jax.nn.sigmoid(logitsT)
    sb = s + bias_col
    per = N_EXPERTS // N_EXPERT_GROUPS
    assert per == SUBLANES
    grp = [sb[per * a:per * (a + 1)] for a in range(N_EXPERT_GROUPS)]
    sub = lax.broadcasted_iota(jnp.int32, grp[0].shape, 0)
    gs = []
    for ga in grp:
        m1 = jnp.max(ga, axis=0, keepdims=True)
        first = jnp.min(jnp.where(ga == m1, sub, per), axis=0, keepdims=True)
        m2 = jnp.max(jnp.where(sub == first, NEG_INF, ga), axis=0, keepdims=True)
        gs.append(m1 + m2)
    gmask = _topk_mask([jnp.concatenate(gs, axis=0)], TOPK_GROUPS)[0]
    masked = [jnp.where(gmask[a:a + 1, :] > 0.5, grp[a], NEG_INF) for a in range(N_EXPERT_GROUPS)]
    sel = jnp.concatenate(_topk_mask(masked, TOP_K, by_rounds=True), axis=0)
    w = s * sel
    w = w / jnp.sum(w, axis=0, keepdims=True) * ROUTED_SCALE
    return sel, w


def _post_mix_body(x_ref, on_ref, ys_ref, g1_ref, sc_ref, sh_ref, nw_ref, wo_ref, rw_ref, rb_ref,
                   x1_ref, hp_ref, selT_ref, wT_ref, cnt_ref):
    first = (pl.program_id(0) == 0) & (pl.program_id(1) == 0)
    half = wo_ref.shape[0] // 2
    mix = (jnp.dot(on_ref[0], wo_ref[0:half, :], preferred_element_type=F32)
           + jnp.dot(ys_ref[0], wo_ref[half:, :], preferred_element_type=F32))
    x1 = x_ref[0] + g1_ref[0] * mix
    x1_ref[0] = x1
    ms = jnp.mean(x1 * x1, axis=-1, keepdims=True)
    h = x1 * lax.rsqrt(ms + RMS_EPS) * nw_ref[...]
    h = h * (1.0 + sc_ref[0]) + sh_ref[0]
    hp_ref[0] = _pack_bf16_pairs(h)
    logitsT = lax.dot_general(rw_ref[...], h, (((1,), (1,)), ((), ())), preferred_element_type=F32,
                              precision=HIGHEST)
    sel, w = _route(logitsT, rb_ref[...])
    selT_ref[...] = sel.astype(selT_ref.dtype)
    wT_ref[...] = w

    @pl.when(first)
    def _():
        cnt_ref[...] = jnp.zeros(cnt_ref.shape, F32)

    cnt_ref[...] += jnp.broadcast_to(jnp.sum(sel, axis=1, keepdims=True), cnt_ref.shape)


def _post_mix(x, o_nsa, y_ssd, g1, sc2, sh2, norm_w, w_out_b, router_wT, router_bias, tm):
    nb, t, d = x.shape
    mt = g1.shape[1]
    nt = t // tm
    assert t % tm == 0 and (mt == 1 or mt == t)
    if mt == 1:
        mod_spec = pl.BlockSpec((1, 1, d), lambda b, i: (b, 0, 0))
    else:
        mod_spec = pl.BlockSpec((1, tm, d), lambda b, i: (b, i, 0))
    row = lambda n: pl.BlockSpec((1, tm, n), lambda b, i: (b, i, 0))
    full = lambda a: pl.BlockSpec(a.shape, lambda b, i: (0,) * a.ndim)
    tok = lambda: pl.BlockSpec((N_EXPERTS, tm), lambda b, i: (0, b * nt + i))
    rb = router_bias.reshape(N_EXPERTS, 1)
    nw = norm_w.reshape(1, d)
    return pl.pallas_call(
        _post_mix_body,
        grid=(nb, nt),
        in_specs=[row(d), row(NSA_DIM), row(D_INNER), mod_spec, mod_spec, mod_spec, full(nw), full(w_out_b),
                  full(router_wT), full(rb)],
        out_specs=[row(d), row(d // 2), tok(), tok(), pl.BlockSpec((N_EXPERTS, LANES), lambda b, i: (0, 0))],
        out_shape=[jax.ShapeDtypeStruct((nb, t, d), F32),
                   jax.ShapeDtypeStruct((nb, t, d // 2), jnp.uint32),
                   jax.ShapeDtypeStruct((N_EXPERTS, nb * t), BF16),
                   jax.ShapeDtypeStruct((N_EXPERTS, nb * t), F32),
                   jax.ShapeDtypeStruct((N_EXPERTS, LANES), F32)],
        compiler_params=_cparams(("arbitrary", "arbitrary")),
        name="post_mix",
    )(x, o_nsa, y_ssd, g1, sc2, sh2, nw, w_out_b, router_wT, rb)


MOE_SHIFT_LARGE = 10
MOE_SHIFT_SMALL = 8


def _moe_rows(n_tok, shift):
    n_blocks = (n_tok * TOP_K >> shift) + N_EXPERTS
    n_blocks_pad = -(-n_blocks // LANES) * LANES
    return n_blocks, n_blocks_pad


def _plan_body(shift, selT_ref, wT_ref, cnt_ref, triu_ref, tril_ref, eye_ref, dest_ref, w8_ref, be_ref, fill_ref,
               carry_scr, pstart_scr):
    step = pl.program_id(0)
    ne = N_EXPERTS
    block = 1 << shift

    @pl.when(step == 0)
    def _():
        cnt = cnt_ref[...]
        cnt_i = cnt.astype(jnp.int32)
        padded = (((cnt_i + (block - 1)) >> shift) << shift).astype(F32)
        pstart = jnp.dot(tril_ref[...].astype(F32), padded, preferred_element_type=F32, precision=HIGHEST)
        pstart_scr[...] = pstart
        carry_scr[...] = jnp.zeros(carry_scr.shape, F32)
        pend = pstart + padded
        nbp = be_ref.shape[1]
        starts = (lax.broadcasted_iota(jnp.int32, (1, nbp), 1) * block).astype(F32)
        below = jnp.where(pend[:, 0:1] <= starts, 1.0, 0.0)
        be_ref[...] = jnp.minimum(jnp.sum(below, axis=0, keepdims=True), float(ne - 1)).astype(jnp.int32)
        eye = eye_ref[...]
        to_row = lambda col: jnp.sum(col * eye, axis=0, keepdims=True)
        n_used = jnp.max(pend, axis=0, keepdims=True) * (1.0 / block)
        rows = jnp.concatenate([to_row(pstart + cnt), to_row(padded - cnt), n_used,
                                jnp.zeros((SUBLANES - 3, LANES), F32)], axis=0)
        fill_ref[...] = rows.astype(jnp.int32)

    sel = selT_ref[...]
    self32 = sel.astype(F32)
    rank = jnp.dot(sel, triu_ref[...], preferred_element_type=F32) + carry_scr[:, 0:1]
    carry_scr[...] += jnp.broadcast_to(jnp.sum(self32, axis=1, keepdims=True), carry_scr.shape)
    dest = pstart_scr[:, 0:1] + rank
    slot = jnp.dot(tril_ref[...], sel, preferred_element_type=F32)
    w = wT_ref[...]
    drows, wrows = [], []
    for k in range(TOP_K):
        pick = jnp.where(slot == float(k), self32, 0.0)
        drows.append(jnp.sum(pick * dest, axis=0, keepdims=True))
        wrows.append(jnp.sum(pick * w, axis=0, keepdims=True))
    dest_ref[...] = jnp.concatenate(drows, axis=0).astype(jnp.int32)
    w8_ref[...] = jnp.concatenate(wrows, axis=0)


def _moe_plan(selT, wT, cnt, tile, shift):
    ne, n = selT.shape
    assert n % tile == 0
    _, nbp = _moe_rows(n, shift)
    triu = jnp.asarray(np.triu(np.ones((tile, tile), np.float32), 1), BF16)
    tril = jnp.asarray(np.tril(np.ones((ne, ne), np.float32), -1), BF16)
    eye = jnp.asarray(np.eye(ne, LANES, dtype=np.float32))
    full = lambda a: pl.BlockSpec(a.shape, lambda i: (0,) * a.ndim)
    return pl.pallas_call(
        functools.partial(_plan_body, shift),
        grid=(n // tile,),
        in_specs=[pl.BlockSpec((ne, tile), lambda i: (0, i)), pl.BlockSpec((ne, tile), lambda i: (0, i)),
                  full(cnt), full(triu), full(tril), full(eye)],
        out_specs=[pl.BlockSpec((TOP_K, tile), lambda i: (0, i)), pl.BlockSpec((TOP_K, tile), lambda i: (0, i)),
                   pl.BlockSpec((1, nbp), lambda i: (0, 0)), pl.BlockSpec((SUBLANES, LANES), lambda i: (0, 0))],
        out_shape=[jax.ShapeDtypeStruct((TOP_K, n), jnp.int32), jax.ShapeDtypeStruct((TOP_K, n), F32),
                   jax.ShapeDtypeStruct((1, nbp), jnp.int32), jax.ShapeDtypeStruct((SUBLANES, LANES), jnp.int32)],
        scratch_shapes=[pltpu.VMEM((ne, LANES), F32), pltpu.VMEM((ne, LANES), F32)],
        compiler_params=_cparams(("arbitrary",)),
        name="moe_plan",
    )(selT, wT, cnt, triu, tril, eye)


def _fill_pieces(shift):
    return tuple(1 << s for s in reversed(range(shift)))


def _fill_padding(fill_ref, xd_ref, zero_scr, zsem, wait):
    def per_expert(e, carry):
        start = fill_ref[0, e]
        n = fill_ref[1, e]
        head = n & (SUBLANES - 1)
        for r in range(SUBLANES - 1):
            @pl.when(r < head)
            def _():
                cp = pltpu.make_async_copy(zero_scr.at[pl.ds(0, 1)], xd_ref.at[pl.ds(start + r, 1)], zsem)
                cp.wait() if wait else cp.start()

        cur = start + head
        for p in _fill_pieces(zero_scr.shape[0].bit_length()):
            if p < SUBLANES:
                continue
            hit = (n & p) != 0

            @pl.when(hit)
            def _():
                off = pl.multiple_of(cur, SUBLANES)
                cp = pltpu.make_async_copy(zero_scr.at[pl.ds(0, p)], xd_ref.at[pl.ds(off, p)], zsem)
                cp.wait() if wait else cp.start()

            cur = cur + jnp.where(hit, p, 0)
        return carry

    lax.fori_loop(0, N_EXPERTS, per_expert, 0)


def _dispatch_body(dest_ref, fill_ref, hp_ref, xd_ref, zero_scr, sem, zsem):
    step = pl.program_id(0)
    tile = hp_ref.shape[0]

    def row_copy(t, k):
        return pltpu.make_async_copy(hp_ref.at[pl.ds(t, 1)], xd_ref.at[pl.ds(dest_ref[k, t], 1)], sem)

    @pl.when(step == 0)
    def _():
        zero_scr[...] = jnp.zeros(zero_scr.shape, zero_scr.dtype)
        _fill_padding(fill_ref, xd_ref, zero_scr, zsem, False)
        _fill_padding(fill_ref, xd_ref, zero_scr, zsem, True)

    def issue(t, carry):
        for k in range(TOP_K):
            row_copy(t, k).start(priority=k % 2)
        return carry

    def drain(t, carry):
        for k in range(TOP_K):
            row_copy(t, k).wait()
        return carry

    lax.fori_loop(0, tile, issue, 0)
    lax.fori_loop(0, tile, drain, 0)


def _moe_dispatch(hp, dest8, fill, tile, shift):
    n, m = hp.shape
    n_blocks, _ = _moe_rows(n, shift)
    nr = n_blocks << shift
    return pl.pallas_call(
        _dispatch_body,
        grid=(n // tile,),
        in_specs=[pl.BlockSpec((TOP_K, tile), lambda i: (0, i), memory_space=pltpu.SMEM),
                  pl.BlockSpec(memory_space=pltpu.SMEM),
                  pl.BlockSpec((tile, m), lambda i: (i, 0))],
        out_specs=pl.BlockSpec(memory_space=pl.ANY),
        out_shape=jax.ShapeDtypeStruct((nr, m), jnp.uint32),
        scratch_shapes=[pltpu.VMEM((_fill_pieces(shift)[0], m), jnp.uint32), pltpu.SemaphoreType.DMA(()),
                        pltpu.SemaphoreType.DMA(())],
        compiler_params=_cparams(("arbitrary",)),
        name="moe_dispatch",
    )(dest8, fill, hp)


def _swiglu_packed(xw, w1, w3, w2):
    xa, xb = _unpack_bf16_pairs(xw)
    half = xa.shape[1]
    mm = lambda w: (jnp.dot(xa, w[0:half, :], preferred_element_type=F32)
                    + jnp.dot(xb, w[half:, :], preferred_element_type=F32))
    hid = _silu(mm(w1)) * mm(w3)
    return jnp.dot(hid.astype(BF16), w2, preferred_element_type=F32)


def _experts_body(be_ref, nu_ref, xd_ref, w1_ref, w3_ref, w2_ref, yd_ref, w1_scr, w3_scr, w2_scr):
    i = pl.program_id(0)
    last = jnp.minimum(i, nu_ref[0] - 1)
    fresh = (i == 0) | (be_ref[last] != be_ref[jnp.maximum(last - 1, 0)])

    @pl.when(fresh)
    def _():
        w1_scr[...] = w1_ref[0].astype(BF16)
        w3_scr[...] = w3_ref[0].astype(BF16)
        w2_scr[...] = w2_ref[0].astype(BF16)

    @pl.when(i < nu_ref[0])
    def _():
        yd_ref[...] = _pack_bf16_pairs(_swiglu_packed(xd_ref[...], w1_scr[...], w3_scr[...], w2_scr[...]))

    @pl.when(i >= nu_ref[0])
    def _():
        yd_ref[...] = jnp.zeros(yd_ref.shape, yd_ref.dtype)


def _moe_experts(xd, block_e, n_used, w1b, w3b, w2b, shift):
    nr, m = xd.shape
    n_blocks = nr >> shift
    block = 1 << shift
    d, f = w1b.shape[1:]
    clamp = lambda i, nu: jnp.minimum(i, nu[0] - 1)
    return pl.pallas_call(
        _experts_body,
        grid_spec=pltpu.PrefetchScalarGridSpec(
            num_scalar_prefetch=2,
            grid=(n_blocks,),
            in_specs=[pl.BlockSpec((block, m), lambda i, be, nu: (clamp(i, nu), 0)),
                      pl.BlockSpec((1, d, f), lambda i, be, nu: (be[clamp(i, nu)], 0, 0)),
                      pl.BlockSpec((1, d, f), lambda i, be, nu: (be[clamp(i, nu)], 0, 0)),
                      pl.BlockSpec((1, f, d), lambda i, be, nu: (be[clamp(i, nu)], 0, 0))],
            out_specs=pl.BlockSpec((block, m), lambda i, be, nu: (i, 0)),
            scratch_shapes=[pltpu.VMEM((d, f), BF16), pltpu.VMEM((d, f), BF16), pltpu.VMEM((f, d), BF16)],
        ),
        out_shape=jax.ShapeDtypeStruct((nr, m), jnp.uint32),
        compiler_params=_cparams(("arbitrary",)),
        name="moe_experts",
    )(block_e, n_used, xd, w1b, w3b, w2b)


def _combine_body(dest_ref, w8_ref, hp_ref, x1_ref, g2_ref, eye_ref, sw1_ref, sw3_ref, sw2_ref, nf_ref, yd_ref,
                  o_ref, ybuf, sem):
    tile = hp_ref.shape[1]

    def row_copy(t, k):
        return pltpu.make_async_copy(yd_ref.at[pl.ds(dest_ref[k, t], 1)], ybuf.at[k, pl.ds(t, 1)], sem)

    def issue(t, carry):
        for k in range(TOP_K):
            row_copy(t, k).start(priority=k % 2)
        return carry

    def drain(t, carry):
        for k in range(TOP_K):
            row_copy(t, k).wait()
        return carry

    lax.fori_loop(0, tile, issue, 0)
    shared = _swiglu_packed(hp_ref[0], sw1_ref[...], sw3_ref[...], sw2_ref[...])
    w_rows = lax.dot_general(eye_ref[...], w8_ref[...], (((1,), (1,)), ((), ())), preferred_element_type=F32,
                             precision=HIGHEST)
    lax.fori_loop(0, tile, drain, 0)
    half = ybuf.shape[2]
    acc_a = jnp.zeros((tile, half), F32)
    acc_b = jnp.zeros((tile, half), F32)
    for k in range(TOP_K):
        ya, yb = _unpack_pairs_f32(ybuf[k])
        wk = w_rows[:, k:k + 1]
        acc_a = acc_a + wk * ya
        acc_b = acc_b + wk * yb
    routed = jnp.concatenate([acc_a, acc_b], axis=1)
    x2 = x1_ref[0] + g2_ref[0] * (routed + shared)
    ms = jnp.mean(x2 * x2, axis=-1, keepdims=True)
    o_ref[0] = x2 * lax.rsqrt(ms + RMS_EPS) * nf_ref[...]


def _moe_combine(dest8, w8, hp, x1, g2, yd, sw1b, sw3b, sw2b, norm_f, tile):
    nb, t, d = x1.shape
    nt = t // tile
    mt = g2.shape[1]
    assert t % tile == 0 and (mt == 1 or mt == t)
    if mt == 1:
        mod_spec = pl.BlockSpec((1, 1, d), lambda b, i: (b, 0, 0))
    else:
        mod_spec = pl.BlockSpec((1, tile, d), lambda b, i: (b, i, 0))
    eye = jnp.eye(tile, dtype=F32)
    nf = norm_f.reshape(1, d)
    full = lambda a: pl.BlockSpec(a.shape, lambda b, i: (0,) * a.ndim)
    return pl.pallas_call(
        _combine_body,
        grid=(nb, nt),
        in_specs=[pl.BlockSpec((TOP_K, tile), lambda b, i: (0, b * nt + i), memory_space=pltpu.SMEM),
                  pl.BlockSpec((TOP_K, tile), lambda b, i: (0, b * nt + i)),
                  pl.BlockSpec((1, tile, d // 2), lambda b, i: (b, i, 0)),
                  pl.BlockSpec((1, tile, d), lambda b, i: (b, i, 0)),
                  mod_spec, full(eye), full(sw1b), full(sw3b), full(sw2b), full(nf),
                  pl.BlockSpec(memory_space=pl.ANY)],
        out_specs=pl.BlockSpec((1, tile, d), lambda b, i: (b, i, 0)),
        out_shape=jax.ShapeDtypeStruct((nb, t, d), F32),
        scratch_shapes=[pltpu.VMEM((TOP_K, tile, d // 2), jnp.uint32), pltpu.SemaphoreType.DMA(())],
        compiler_params=_cparams(("arbitrary", "arbitrary")),
        name="moe_combine",
    )(dest8, w8, hp, x1, g2, eye, sw1b, sw3b, sw2b, nf, yd)


SC_CHUNK = 128
SC_WORKERS = 32


def _sc_workers():
    info = plsc.get_sparse_core_info()
    assert info.num_cores * info.num_subcores == SC_WORKERS
    return info.num_cores, info.num_subcores


def _sc_dispatch(hp, dest8, nr):
    n, m = hp.shape
    nc, nsub = _sc_workers()
    per_w = n // (nc * nsub)
    assert n % (nc * nsub * SC_CHUNK) == 0
    mesh = plsc.VectorSubcoreMesh(core_axis_name="c", subcore_axis_name="s")

    @functools.partial(
        pl.kernel, mesh=mesh, out_type=jax.ShapeDtypeStruct((nr, m), hp.dtype),
        scratch_types=[pltpu.VMEM((TOP_K, SC_CHUNK), jnp.int32), pltpu.VMEM((SC_CHUNK, m), hp.dtype),
                       pltpu.SemaphoreType.DMA])
    def scatter_rows(hp_hbm, dest_hbm, xd_hbm, idx_v, rows_v, sem):
        wid = lax.axis_index("s") * nc + lax.axis_index("c")

        @pl.loop(0, per_w // SC_CHUNK)
        def _(c):
            base = pl.multiple_of(wid * per_w + c * SC_CHUNK, SC_CHUNK)
            pltpu.sync_copy(hp_hbm.at[pl.ds(base, SC_CHUNK)], rows_v)
            pltpu.sync_copy(dest_hbm.at[:, pl.ds(base, SC_CHUNK)], idx_v)
            copies = [pltpu.async_copy(rows_v, xd_hbm.at[idx_v.at[k]], sem) for k in range(TOP_K)]
            for cp in copies:
                cp.wait()

    return scatter_rows(hp, dest8)


def _sc_gather(yd, dest8):
    _, m = yd.shape
    n = dest8.shape[1]
    nc, nsub = _sc_workers()
    per_w = n // (nc * nsub)
    assert n % (nc * nsub * SC_CHUNK) == 0
    mesh = plsc.VectorSubcoreMesh(core_axis_name="c", subcore_axis_name="s")

    @functools.partial(
        pl.kernel, mesh=mesh, out_type=jax.ShapeDtypeStruct((TOP_K, n, m), yd.dtype),
        scratch_types=[pltpu.VMEM((TOP_K, SC_CHUNK), jnp.int32), pltpu.VMEM((SC_CHUNK, m), yd.dtype),
                       pltpu.SemaphoreType.DMA])
    def gather_rows(yd_hbm, dest_hbm, out_hbm, idx_v, rows_v, sem):
        wid = lax.axis_index("s") * nc + lax.axis_index("c")

        @pl.loop(0, per_w // SC_CHUNK)
        def _(c):
            base = pl.multiple_of(wid * per_w + c * SC_CHUNK, SC_CHUNK)
            pltpu.sync_copy(dest_hbm.at[:, pl.ds(base, SC_CHUNK)], idx_v)
            for k in range(TOP_K):
                pltpu.async_copy(yd_hbm.at[idx_v.at[k]], rows_v, sem).wait()
                pltpu.sync_copy(rows_v, out_hbm.at[k, pl.ds(base, SC_CHUNK)])

    return gather_rows(yd, dest8)


def _fill_body(fill_ref, xd_in_ref, after_ref, xd_ref, zero_scr, zsem):
    del xd_in_ref, after_ref
    zero_scr[...] = jnp.zeros(zero_scr.shape, zero_scr.dtype)
    for wait in (False, True):
        _fill_padding(fill_ref, xd_ref, zero_scr, zsem, wait)


def _moe_fill(xd, fill, after, shift):
    return pl.pallas_call(
        _fill_body,
        in_specs=[pl.BlockSpec(memory_space=pltpu.SMEM), pl.BlockSpec(memory_space=pl.ANY),
                  pl.BlockSpec(memory_space=pl.ANY)],
        out_specs=pl.BlockSpec(memory_space=pl.ANY),
        out_shape=jax.ShapeDtypeStruct(xd.shape, xd.dtype),
        scratch_shapes=[pltpu.VMEM((_fill_pieces(shift)[0], xd.shape[1]), xd.dtype), pltpu.SemaphoreType.DMA(())],
        input_output_aliases={1: 0},
        name="moe_fill",
    )(fill, xd, after)


def _combine_dense_body(w8_ref, hp_ref, x1_ref, g2_ref, eye_ref, sw1_ref, sw3_ref, sw2_ref, nf_ref, ybuf_ref, o_ref):
    tile = hp_ref.shape[1]
    shared = _swiglu_packed(hp_ref[0], sw1_ref[...], sw3_ref[...], sw2_ref[...])
    w_rows = lax.dot_general(eye_ref[...], w8_ref[...], (((1,), (1,)), ((), ())), preferred_element_type=F32,
                             precision=HIGHEST)
    half = ybuf_ref.shape[2]
    acc_a = jnp.zeros((tile, half), F32)
    acc_b = jnp.zeros((tile, half), F32)
    for k in range(TOP_K):
        ya, yb = _unpack_pairs_f32(ybuf_ref[k])
        wk = w_rows[:, k:k + 1]
        acc_a = acc_a + wk * ya
        acc_b = acc_b + wk * yb
    routed = jnp.concatenate([acc_a, acc_b], axis=1)
    x2 = x1_ref[0] + g2_ref[0] * (routed + shared)
    ms = jnp.mean(x2 * x2, axis=-1, keepdims=True)
    o_ref[0] = x2 * lax.rsqrt(ms + RMS_EPS) * nf_ref[...]


def _moe_combine_dense(w8, hp, x1, g2, ybuf, sw1b, sw3b, sw2b, norm_f, tile):
    nb, t, d = x1.shape
    nt = t // tile
    assert t % tile == 0 and g2.shape[1] == 1
    eye = jnp.eye(tile, dtype=F32)
    nf = norm_f.reshape(1, d)
    full = lambda a: pl.BlockSpec(a.shape, lambda b, i: (0,) * a.ndim)
    return pl.pallas_call(
        _combine_dense_body,
        grid=(nb, nt),
        in_specs=[pl.BlockSpec((TOP_K, tile), lambda b, i: (0, b * nt + i)),
                  pl.BlockSpec((1, tile, d // 2), lambda b, i: (b, i, 0)),
                  pl.BlockSpec((1, tile, d), lambda b, i: (b, i, 0)),
                  pl.BlockSpec((1, 1, d), lambda b, i: (b, 0, 0)),
                  full(eye), full(sw1b), full(sw3b), full(sw2b), full(nf),
                  pl.BlockSpec((TOP_K, tile, d // 2), lambda b, i: (0, b * nt + i, 0))],
        out_specs=pl.BlockSpec((1, tile, d), lambda b, i: (b, i, 0)),
        out_shape=jax.ShapeDtypeStruct((nb, t, d), F32),
        compiler_params=_cparams(("arbitrary", "arbitrary")),
        name="moe_combine_dense",
    )(w8, hp, x1, g2, eye, sw1b, sw3b, sw2b, nf, ybuf)


def _moe(x1, hp, selT, wT, cnt, g2, EW, norm_f, tile, on_sparsecore, after=None):
    nb, t, d = x1.shape
    n = nb * t
    shift = MOE_SHIFT_LARGE if n * TOP_K >= N_EXPERTS << MOE_SHIFT_LARGE else MOE_SHIFT_SMALL
    dest8, w8, block_e, fill = _moe_plan(selT, wT, cnt, tile, shift)
    n_blocks, _ = _moe_rows(n, shift)
    hp2 = hp.reshape(n, d // 2)
    if on_sparsecore:
        xd = _moe_fill(_sc_dispatch(hp2, dest8, n_blocks << shift), fill, after, shift)
    else:
        xd = _moe_dispatch(hp2, dest8, fill, tile, shift)
    yd = _moe_experts(xd, block_e[0, :n_blocks], fill[2, 0:1], EW["w1"], EW["w3"], EW["w2"], shift)
    if on_sparsecore:
        return _moe_combine_dense(w8, hp, x1, g2, _sc_gather(yd, dest8), EW["sw1"], EW["sw3"], EW["sw2"], norm_f, tile)
    return _moe_combine(dest8, w8, hp, x1, g2, yd, EW["sw1"], EW["sw3"], EW["sw2"], norm_f, tile)


def kernel(x_prompt, x_sample, c_prompt, c_sample, cache_kv_cmp, cache_kv_sel, cache_kv_win, state_conv, state_ssm, page_table, w_ada, b_ada, norm_mix, norm_ffn, w_in, cmp_pe_k, cmp_w1_k, cmp_w2_k, cmp_pe_v, cmp_w1_v, cmp_w2_v, conv_w, conv_b, dt_bias, a_log, d_skip, ssd_norm, w_out, router_w, router_bias, exp_w1, exp_w3, exp_w2, sh_w1, sh_w3, sh_w2, norm_f):
    nb, t, d = x_prompt.shape
    ndb = x_sample.shape[0]
    c_all = jnp.concatenate([c_prompt, c_sample], axis=0)
    mod = _modulation(c_all, w_ada[0], b_ada[0]).reshape(nb + ndb, 6, d)
    mod_p = [mod[:nb, k][:, None, :] for k in range(6)]
    mod_s = [mod[nb:, k][None, :, :] for k in range(6)]
    W = _prep_w_in(w_in[0])
    P = _in_proj(x_prompt, mod_p[1], mod_p[0], norm_mix[0], W, ROW_TILE)
    S = _in_proj(x_sample.reshape(1, ndb, d), mod_s[1], mod_s[0], norm_mix[0], W, ndb)
    C = _prep_compress(cmp_pe_k[0], cmp_w1_k[0], cmp_w2_k[0], cmp_pe_v[0], cmp_w1_v[0], cmp_w2_v[0])
    kcvc_p = _compress_prompt(P["kcT"], C)
    ocmp_p, sel_p = _cmp_attn_prompt(P["q"], kcvc_p, ROW_TILE)
    o_nsa_p = _sel_win_prompt(P["q"], sel_p, P["g"], ocmp_p, P["ksT"], P["kwT"], ATTN_Q_TILE, ATTN_K_TILE)
    SP = _prep_ssd(conv_w[0], conv_b[0], dt_bias[0], a_log[0], d_skip[0], ssd_norm[0])
    y_ssd_p, ssm_p = _ssd_prompt(P["xbc"], P["z"], P["dt"], P["dtT"], SP)
    w_out_b = w_out[0].astype(BF16)
    router_wT = router_w[0].T
    EW = dict(w1=exp_w1[0], w3=exp_w3[0], w2=exp_w2[0],
              sw1=sh_w1[0].astype(BF16), sw3=sh_w3[0].astype(BF16), sw2=sh_w2[0].astype(BF16))
    x1_p, hp_p, selT_p, wT_p, cnt_p = _post_mix(x_prompt, o_nsa_p, y_ssd_p, mod_p[2], mod_p[4], mod_p[3], norm_ffn[0],
                                                 w_out_b, router_wT, router_bias[0], 2 * ROW_TILE)
    past = page_table.shape[1] * PAGE_SIZE
    to_pages = lambda c: jnp.transpose(c, (0, 2, 3, 4, 1)).reshape(c.shape[0], 256, c.shape[1])
    pool_cmp, pool_sel, win = to_pages(cache_kv_cmp[0]), to_pages(cache_kv_sel[0]), to_pages(cache_kv_win[0])
    kcvc_s = _compress_paged(pool_cmp, page_table, C)

    big = (nb * t) % (SC_WORKERS * SC_CHUNK) == 0
    y_prompt = _moe(x1_p, hp_p, selT_p, wT_p, cnt_p, mod_p[5], EW, norm_f, ROW_TILE, big, after=kcvc_s)

    q3 = S["q"].reshape(ndb, 1, NSA_DIM)
    ocmp_s, sel_s = _cmp_attn_sample(q3, kcvc_s, past)
    o_nsa_s, win_s = _sel_win_sample(q3, S["g"].reshape(ndb, 1, KV_HEADS * LANES), ocmp_s, sel_s, pool_sel, page_table,
                                     win, S["ksT"], S["kwT"], past)
    y_ssd_s, conv_s, ssm_s = _ssd_sample(jnp.transpose(state_conv[0], (1, 0, 2)), S["xbc"][0], S["z"][0], S["dt"][0],
                                         state_ssm[0], SP)
    x1_s, hp_s, selT_s, wT_s, cnt_s = _post_mix(x_sample.reshape(1, ndb, d), o_nsa_s.reshape(1, ndb, NSA_DIM),
                                                 y_ssd_s.reshape(1, ndb, D_INNER), mod_s[2], mod_s[4], mod_s[3],
                                                 norm_ffn[0], w_out_b, router_wT, router_bias[0], ndb)
    y_sample = _moe(x1_s, hp_s, selT_s, wT_s, cnt_s, mod_s[5], EW, norm_f, ndb, False).reshape(ndb, 1, d)

    from_cm = lambda a: jnp.transpose(a.reshape(a.shape[0], 2, KV_HEADS, HEAD_DIM, a.shape[2]), (0, 4, 1, 2, 3))[None]
    tw = min(WINDOW, t)
    return (y_prompt, y_sample,
            from_cm(P["kcT"]), from_cm(P["ksT"]), from_cm(P["kwT"][:, :, t - tw:]),
            P["xbc"][:, t - (CONV_W - 1):, :][None], ssm_p[None],
            from_cm(jnp.transpose(S["kcT"], (2, 1, 0))), from_cm(jnp.transpose(S["ksT"], (2, 1, 0))), from_cm(win_s),
            jnp.transpose(conv_s, (1, 0, 2))[None], ssm_s[None])
```

```python
import functools

import jax
import jax.numpy as jnp
import numpy as np
from jax import lax
from jax.experimental import pallas as pl
from jax.experimental.pallas import tpu as pltpu
from jax.experimental.pallas import tpu_sc as plsc

F32 = jnp.float32
BF16 = jnp.bfloat16
HIGHEST = lax.Precision.HIGHEST

D_MODEL = 1024
NSA_HEADS = 8
KV_HEADS = 2
HEAD_DIM = 64
GQA = NSA_HEADS // KV_HEADS
CMP_BLK = 32
CMP_STRIDE = 16
CMP_HID = 64
SEL_BLK = 64
N_SEL = 16
WINDOW = 512
FORCE_CUR = 2.0e4
FORCE_SINK = 1.0e4
SSD_HEADS = 8
SSD_HEAD_DIM = 64
D_INNER = SSD_HEADS * SSD_HEAD_DIM
SSD_GROUPS = 2
D_STATE = 128
CONV_W = 4
CONV_DIM = D_INNER + 2 * SSD_GROUPS * D_STATE
SSD_CHUNK = 128
NSA_DIM = NSA_HEADS * HEAD_DIM
KV_DIM = KV_HEADS * HEAD_DIM
N_EXPERTS = 64
TOP_K = 8
N_EXPERT_GROUPS = 8
TOPK_GROUPS = 4
D_EXPERT = 256
D_SHARED = 256
ROUTED_SCALE = 2.5
RMS_EPS = 1e-6
NEG_INF = -1e30
LOG2E = 1.4426950408889634
PAGE_SIZE = 128

LANES = 128
SUBLANES = 8
VMEM_LIMIT = 56 * 1024 * 1024
ROW_TILE = 512
ATTN_Q_TILE = 256
ATTN_K_TILE = 512


def _cparams(sem):
    return pltpu.CompilerParams(dimension_semantics=sem, vmem_limit_bytes=VMEM_LIMIT)


def _silu(v):
    return v * jax.nn.sigmoid(v)


def _mod_body(c_ref, w_ref, b_ref, o_ref):
    s = _silu(c_ref[...])
    o_ref[...] = jnp.dot(s, w_ref[...], preferred_element_type=F32, precision=HIGHEST) + b_ref[...]


def _modulation(c_all, w_ada, b_ada):
    n, d = c_all.shape
    nout = w_ada.shape[1]
    tn = 512
    return pl.pallas_call(
        _mod_body,
        grid=(nout // tn,),
        in_specs=[pl.BlockSpec((n, d), lambda j: (0, 0)),
                  pl.BlockSpec((d, tn), lambda j: (0, j)),
                  pl.BlockSpec((1, tn), lambda j: (0, j))],
        out_specs=pl.BlockSpec((n, tn), lambda j: (0, j)),
        out_shape=jax.ShapeDtypeStruct((n, nout), F32),
        compiler_params=_cparams(("arbitrary",)),
        name="modulation",
    )(c_all, w_ada, b_ada.reshape(1, nout))


def _nt_dot(a, b):
    return lax.dot_general(a, b, (((1,), (1,)), ((), ())), preferred_element_type=F32)


def _proj_body(x_ref, sc_ref, sh_ref, nw_ref, wq_ref, wkvT_ref, wg_ref, wz_ref, wx_ref, wdt_ref, wdtT_ref,
               q_ref, kcT_ref, ksT_ref, kwT_ref, g_ref, z_ref, xbc_ref, dt_ref, dtT_ref):
    x = x_ref[0]
    ms = jnp.mean(x * x, axis=-1, keepdims=True)
    h = x * lax.rsqrt(ms + RMS_EPS) * nw_ref[...]
    h = h * (1.0 + sc_ref[0]) + sh_ref[0]
    hb = h.astype(BF16)
    q_ref[0] = (jnp.dot(hb, wq_ref[...], preferred_element_type=F32) * (HEAD_DIM ** -0.5 * LOG2E)).astype(BF16)
    kvT = _nt_dot(wkvT_ref[...], hb)
    kcT_ref[0] = kvT[0:256]
    ksT_ref[0] = kvT[256:512]
    kwT_ref[0] = kvT[512:768]
    g_ref[0] = jax.nn.sigmoid(jnp.dot(hb, wg_ref[...], preferred_element_type=F32))
    z_ref[0] = jnp.dot(hb, wz_ref[...], preferred_element_type=F32)
    xbc_ref[0] = jnp.dot(hb, wx_ref[...], preferred_element_type=F32)
    dt_ref[0] = jnp.dot(hb, wdt_ref[...], preferred_element_type=F32)
    dtT_ref[0] = _nt_dot(wdtT_ref[...], hb)


def _prep_w_in(w_in):
    w = w_in
    o = 0
    wq = w[:, o:o + NSA_DIM]; o += NSA_DIM
    wkv = w[:, o:o + 6 * KV_DIM]; o += 6 * KV_DIM
    wg = w[:, o:o + 3 * NSA_HEADS]; o += 3 * NSA_HEADS
    wz = w[:, o:o + D_INNER]; o += D_INNER
    wx = w[:, o:o + CONV_DIM]; o += CONV_DIM
    wdt = w[:, o:o + SSD_HEADS]; o += SSD_HEADS
    pad = lambda a: jnp.pad(a, ((0, 0), (0, LANES - a.shape[1])))
    per = 3 * GQA
    wg = jnp.concatenate([pad(wg[:, k * per:(k + 1) * per]) for k in range(KV_HEADS)], axis=1)
    return dict(wq=wq.astype(BF16), wkvT=wkv.T.astype(BF16), wg=wg.astype(BF16), wz=wz.astype(BF16),
                wx=wx.astype(BF16), wdt=pad(wdt).astype(BF16), wdtT=wdt.T.astype(BF16))


def _in_proj(x, sc, sh, norm_w, W, tm):
    nb, t, d = x.shape
    mt = sc.shape[1]
    assert t % tm == 0 and (mt == 1 or mt == t)
    if mt == 1:
        mod_spec = pl.BlockSpec((1, 1, d), lambda b, i: (b, 0, 0))
    else:
        mod_spec = pl.BlockSpec((1, tm, d), lambda b, i: (b, i, 0))
    full = lambda a: pl.BlockSpec(a.shape, lambda b, i: (0,) * a.ndim)
    row = lambda n: pl.BlockSpec((1, tm, n), lambda b, i: (b, i, 0))
    col = lambda n: pl.BlockSpec((1, n, tm), lambda b, i: (b, 0, i))
    ws = [W["wq"], W["wkvT"], W["wg"], W["wz"], W["wx"], W["wdt"], W["wdtT"]]
    outs = pl.pallas_call(
        _proj_body,
        grid=(nb, t // tm),
        in_specs=[row(d), mod_spec, mod_spec, pl.BlockSpec((1, d), lambda b, i: (0, 0))] + [full(a) for a in ws],
        out_specs=[row(NSA_DIM), col(256), col(256), col(256), row(KV_HEADS * LANES), row(D_INNER), row(CONV_DIM),
                   row(LANES), col(SSD_HEADS)],
        out_shape=[jax.ShapeDtypeStruct((nb, t, NSA_DIM), BF16),
                   jax.ShapeDtypeStruct((nb, 256, t), F32),
                   jax.ShapeDtypeStruct((nb, 256, t), F32),
                   jax.ShapeDtypeStruct((nb, 256, t), F32),
                   jax.ShapeDtypeStruct((nb, t, KV_HEADS * LANES), F32),
                   jax.ShapeDtypeStruct((nb, t, D_INNER), F32),
                   jax.ShapeDtypeStruct((nb, t, CONV_DIM), F32),
                   jax.ShapeDtypeStruct((nb, t, LANES), F32),
                   jax.ShapeDtypeStruct((nb, SSD_HEADS, t), F32)],
        compiler_params=_cparams(("arbitrary", "arbitrary")),
        name="in_proj",
    )(x, sc, sh, norm_w.reshape(1, d), *ws)
    names = ("q", "kcT", "ksT", "kwT", "g", "z", "xbc", "dt", "dtT")
    return dict(zip(names, outs))


def _prep_compress(cmp_pe_k, cmp_w1_k, cmp_w2_k, cmp_pe_v, cmp_w1_v, cmp_w2_v):
    w1s = jnp.stack([cmp_w1_k, cmp_w1_v]).reshape(2, 2, CMP_STRIDE, HEAD_DIM, CMP_HID)
    eye = jnp.eye(2, dtype=F32)
    wbd = jnp.einsum("ktldh,kK,vV->lkvdtKVh", w1s, eye, eye).reshape(CMP_STRIDE, 256, 512)
    w2s = jnp.stack([cmp_w2_k, cmp_w2_v])
    w2bd = jnp.einsum("khd,kK,vV->kvhKVd", w2s, eye, eye).reshape(256, 256)
    pes = jnp.stack([cmp_pe_k, cmp_pe_v]).reshape(2, 2, CMP_STRIDE, HEAD_DIM)
    pe_rows = jnp.broadcast_to(jnp.transpose(pes, (1, 2, 0, 3))[:, :, :, None, :],
                               (2, CMP_STRIDE, 2, KV_HEADS, HEAD_DIM)).reshape(2, CMP_STRIDE * 256)
    pecat = jnp.pad(pe_rows, ((0, SUBLANES - 2), (0, 0)))
    perm = np.zeros((LANES, LANES), np.float32)
    for l in range(CMP_STRIDE):
        for n in range(LANES // CMP_STRIDE):
            perm[(LANES // CMP_STRIDE) * l + n, CMP_STRIDE * n + l] = 1.0
    bias = pl.pallas_call(
        _cmp_bias_body,
        out_shape=jax.ShapeDtypeStruct((SUBLANES, 512), F32),
        name="cmp_bias",
    )(pecat, wbd.reshape(CMP_STRIDE * 256, 512))
    return dict(wbd=wbd.astype(BF16), w2bd=w2bd.astype(BF16), bias=bias, perm=jnp.asarray(perm, BF16))


def _cmp_bias_body(pe_ref, w_ref, o_ref):
    o_ref[...] = jnp.dot(pe_ref[...], w_ref[...], preferred_element_type=F32, precision=HIGHEST)


def _compress_body(n_pref, n_slab_refs, slabs_per_ref, *refs):
    refs = refs[n_pref:]
    slab_refs = refs[:n_slab_refs]
    perm_ref, wbd_ref, bias_ref, w2bd_ref, o_ref, z_scr = refs[n_slab_refs:]
    j = pl.program_id(1)
    nh = z_scr.shape[1]
    per_slab = LANES // CMP_STRIDE
    g_tot = n_slab_refs * slabs_per_ref
    base = pl.multiple_of(j * (per_slab * g_tot), per_slab)
    perm = perm_ref[...]
    for ri in range(n_slab_refs):
        for si in range(slabs_per_ref):
            slab = slab_refs[ri][0][:, si * LANES:(si + 1) * LANES].astype(BF16)
            xp = _nt_dot(perm, slab)
            g = ri * slabs_per_ref + si
            for l in range(CMP_STRIDE):
                z_scr[l, pl.ds(base + per_slab * g, per_slab), :] = xp[per_slab * l:per_slab * (l + 1), :]

    @pl.when(j == pl.num_programs(1) - 1)
    def _():
        acc = jnp.zeros((nh, 512), F32)
        for l in range(CMP_STRIDE):
            acc = acc + jnp.dot(z_scr[l].astype(BF16), wbd_ref[l], preferred_element_type=F32)
        lead = acc[:, :256] + bias_ref[0:1, :256]
        tail = acc[:, 256:] + bias_ref[1:2, 256:]
        hid = _silu(lead + pltpu.roll(tail, nh - 1, 0))
        out = jnp.dot(hid.astype(BF16), w2bd_ref[...], preferred_element_type=F32)
        row = lax.broadcasted_iota(jnp.int32, out.shape, 0)
        o_ref[0] = jnp.where(row < nh - 1, out, 0.0)


def _compress_prompt(kcT, C):
    nb, _, t = kcT.shape
    nh = t // CMP_STRIDE
    g = min(8, t // LANES)
    nsteps = t // (LANES * g)
    full = lambda a: pl.BlockSpec(a.shape, lambda b, j: (0,) * a.ndim)
    return pl.pallas_call(
        functools.partial(_compress_body, 0, 1, g),
        grid=(nb, nsteps),
        in_specs=[pl.BlockSpec((1, 256, LANES * g), lambda b, j: (b, 0, j)),
                  full(C["perm"]), full(C["wbd"]), full(C["bias"]), full(C["w2bd"])],
        out_specs=pl.BlockSpec((1, nh, 256), lambda b, j: (b, 0, 0)),
        out_shape=jax.ShapeDtypeStruct((nb, nh, 256), F32),
        scratch_shapes=[pltpu.VMEM((CMP_STRIDE, nh, 256), F32)],
        compiler_params=_cparams(("arbitrary", "arbitrary")),
        name="compress_prompt",
    )(kcT, C["perm"], C["wbd"], C["bias"], C["w2bd"])


def _compress_paged(pool, page_table, C):
    nb, n_pages = page_table.shape
    nh = n_pages * (PAGE_SIZE // CMP_STRIDE)
    g = min(64, n_pages)
    nsteps = n_pages // g
    full = lambda a: pl.BlockSpec(a.shape, lambda b, j, pt: (0,) * a.ndim)
    page_spec = lambda k: pl.BlockSpec((1, 256, LANES), lambda b, j, pt: (pt[b, j * g + k], 0, 0))
    return pl.pallas_call(
        functools.partial(_compress_body, 1, g, 1),
        grid_spec=pltpu.PrefetchScalarGridSpec(
            num_scalar_prefetch=1,
            grid=(nb, nsteps),
            in_specs=[page_spec(k) for k in range(g)] + [full(C["perm"]), full(C["wbd"]), full(C["bias"]), full(C["w2bd"])],
            out_specs=pl.BlockSpec((1, nh, 256), lambda b, j, pt: (b, 0, 0)),
            scratch_shapes=[pltpu.VMEM((CMP_STRIDE, nh, 256), F32)],
        ),
        out_shape=jax.ShapeDtypeStruct((nb, nh, 256), F32),
        compiler_params=_cparams(("arbitrary", "arbitrary")),
        name="compress_paged",
    )(page_table, *([pool] * g), C["perm"], C["wbd"], C["bias"], C["w2bd"])


def _alibi_slope(head):
    return float(2.0 ** (-8.0 * (head + 1) / NSA_HEADS)) * LOG2E


def _overlap_T(nc, ns):
    cst = np.arange(nc)[None, :] * CMP_STRIDE
    sst = np.arange(ns)[:, None] * SEL_BLK
    ov = np.clip(np.minimum(cst + CMP_BLK, sst + SEL_BLK) - np.maximum(cst, sst), 0, None).astype(np.float32) / CMP_BLK
    return jnp.asarray(ov, F32)


def _masked_softmax(s, mask):
    s = jnp.where(mask, s, NEG_INF)
    p = jnp.exp2(s - jnp.max(s, axis=-1, keepdims=True)) * mask.astype(F32)
    return p / jnp.maximum(jnp.sum(p, axis=-1, keepdims=True), 1e-30)


def _topk_mask(grp, k, by_rounds=False):
    ngrp = len(grp)
    sub = lax.broadcasted_iota(jnp.int32, grp[0].shape, 0)
    one, zero = jnp.float32(1.0), jnp.float32(0.0)
    if by_rounds:
        rowidx = [sub + SUBLANES * c for c in range(ngrp)]
        vals = list(grp)
        keep = [jnp.zeros(grp[0].shape, F32) for _ in range(ngrp)]
        for _ in range(k):
            top = functools.reduce(jnp.maximum, vals)
            top = jnp.max(top, axis=0, keepdims=True)
            first = functools.reduce(jnp.minimum, [jnp.where(v == top, r, ngrp * SUBLANES) for v, r in zip(vals, rowidx)])
            first = jnp.min(first, axis=0, keepdims=True)
            for c in range(ngrp):
                hit = rowidx[c] == first
                keep[c] = jnp.where(hit, one, keep[c])
                vals[c] = jnp.where(hit, -jnp.inf, vals[c])
        return keep
    cnt =[jnp.zeros(grp[0].shape, F32) for _ in range(ngrp)]
    for j in range(ngrp * SUBLANES):
        a, r = divmod(j, SUBLANES)
        row = grp[a][r:r + 1, :]
        for c in range(ngrp):
            if c < a:
                beats = jnp.where(row > grp[c], one, zero)
            elif c > a:
                beats = jnp.where(row >= grp[c], one, zero)
            else:
                beats = jnp.where(sub > r, jnp.where(row >= grp[c], one, zero), jnp.where(row > grp[c], one, zero))
            cnt[c] = cnt[c] + beats
    return [jnp.where(c < float(k), one, zero) for c in cnt]


def _select_blocks(imp, cur):
    ns = imp.shape[0]
    jrow = lax.broadcasted_iota(jnp.int32, imp.shape, 0)
    imp = jnp.where(jrow == cur, FORCE_CUR, jnp.where(jrow == 0, FORCE_SINK, imp))
    imp = jnp.where(jrow <= cur, imp, NEG_INF)
    assert ns % SUBLANES == 0
    grp = [imp[SUBLANES * a:SUBLANES * (a + 1)] for a in range(ns // SUBLANES)]
    return jnp.concatenate(_topk_mask(grp, N_SEL), axis=0)


def _cmp_attn_body(q_ref, kcvc_ref, ovT_ref, eye_ref, ocmp_ref, sel_ref):
    i = pl.program_id(1)
    tq = q_ref.shape[1]
    nh = kcvc_ref.shape[1]
    t0 = i * tq
    qpos_col = t0 + lax.broadcasted_iota(jnp.int32, (tq, 1), 0)
    cend = lax.broadcasted_iota(jnp.int32, (1, nh), 1) * CMP_STRIDE + (CMP_BLK - 1)
    mask = cend <= qpos_col
    dist = (qpos_col - cend).astype(F32)
    qpos_row = t0 + lax.broadcasted_iota(jnp.int32, (1, tq), 1)
    cur = qpos_row // SEL_BLK
    kcvc = kcvc_ref[0]
    for kvh in range(KV_HEADS):
        kc = kcvc[:, kvh * HEAD_DIM:(kvh + 1) * HEAD_DIM].astype(BF16)
        vc = kcvc[:, KV_DIM + kvh * HEAD_DIM:KV_DIM + (kvh + 1) * HEAD_DIM].astype(BF16)
        psum = jnp.zeros((tq, nh), F32)
        for g in range(GQA):
            head = kvh * GQA + g
            qg = q_ref[0, :, head * HEAD_DIM:(head + 1) * HEAD_DIM]
            s = _nt_dot(qg, kc) - _alibi_slope(head) * dist
            p = _masked_softmax(s, mask)
            ocmp_ref[0, :, head * HEAD_DIM:(head + 1) * HEAD_DIM] = jnp.dot(p.astype(BF16), vc, preferred_element_type=F32)
            psum = psum + p
        impT = lax.dot_general(ovT_ref[...], psum, (((1,), (1,)), ((), ())), preferred_element_type=F32,
                               precision=HIGHEST)
        selT = _select_blocks(impT, cur)
        sel = _nt_dot(eye_ref[...], selT.astype(BF16))
        sel_ref[0, kvh] = sel.astype(BF16)


def _cmp_attn_prompt(q, kcvc, tq):
    nb, t, _ = q.shape
    nh = kcvc.shape[1]
    ns = t // SEL_BLK
    ovT = _overlap_T(nh, ns)
    eye = jnp.eye(tq, dtype=BF16)
    return pl.pallas_call(
        _cmp_attn_body,
        grid=(nb, t // tq),
        in_specs=[pl.BlockSpec((1, tq, NSA_DIM), lambda b, i: (b, i, 0)),
                  pl.BlockSpec((1, nh, 256), lambda b, i: (b, 0, 0)),
                  pl.BlockSpec(ovT.shape, lambda b, i: (0, 0)),
                  pl.BlockSpec(eye.shape, lambda b, i: (0, 0))],
        out_specs=[pl.BlockSpec((1, tq, NSA_DIM), lambda b, i: (b, i, 0)),
                   pl.BlockSpec((1, KV_HEADS, tq, ns), lambda b, i: (b, 0, i, 0))],
        out_shape=[jax.ShapeDtypeStruct((nb, t, NSA_DIM), F32),
                   jax.ShapeDtypeStruct((nb, KV_HEADS, t, ns), BF16)],
        compiler_params=_cparams(("arbitrary", "arbitrary")),
        name="cmp_attn",
    )(q, kcvc, ovT, eye)


def _head_rows(qrow, kvh):
    rows = [qrow[:, (kvh * GQA + g) * HEAD_DIM:(kvh * GQA + g + 1) * HEAD_DIM] for g in range(GQA)]
    return jnp.concatenate(rows + [jnp.zeros((SUBLANES - GQA, HEAD_DIM), qrow.dtype)], axis=0)


def _slope_col(kvh):
    r = lax.broadcasted_iota(jnp.int32, (SUBLANES, 1), 0)
    col = jnp.zeros((SUBLANES, 1), F32)
    for g in range(GQA):
        col = jnp.where(r == g, _alibi_slope(kvh * GQA + g), col)
    return col


def _cmp_attn_sample_body(past, q_ref, kcvc_ref, ov_ref, ocmp_ref, imp_ref):
    nh = kcvc_ref.shape[1]
    nsl = ov_ref.shape[1]
    cend = lax.broadcasted_iota(jnp.int32, (1, nh), 1) * CMP_STRIDE + (CMP_BLK - 1)
    mask = cend <= past
    dist = (past - cend).astype(F32)
    kcvc = kcvc_ref[0]
    qrow = q_ref[0]
    for kvh in range(KV_HEADS):
        kc = kcvc[:, kvh * HEAD_DIM:(kvh + 1) * HEAD_DIM].astype(BF16)
        vc = kcvc[:, KV_DIM + kvh * HEAD_DIM:KV_DIM + (kvh + 1) * HEAD_DIM].astype(BF16)
        s = _nt_dot(_head_rows(qrow, kvh), kc) - _slope_col(kvh) * dist
        p = _masked_softmax(s, mask)
        o = jnp.dot(p.astype(BF16), vc, preferred_element_type=F32)
        for g in range(GQA):
            head = kvh * GQA + g
            ocmp_ref[0, :, head * HEAD_DIM:(head + 1) * HEAD_DIM] = o[g:g + 1, :]
        psum = jnp.broadcast_to(jnp.sum(p[0:GQA], axis=0, keepdims=True), (SUBLANES, nh))
        imp = jnp.dot(psum, ov_ref[...], preferred_element_type=F32, precision=HIGHEST)
        imp_ref[0, :, kvh * nsl:(kvh + 1) * nsl] = imp[0:1, :]


def _select_sample_body(past, ns_pad, imp_ref, sel_ref):
    nb = imp_ref.shape[0]
    nsl = imp_ref.shape[1] // KV_HEADS
    cur = jnp.full((1, nb), past // SEL_BLK, jnp.int32)
    for kvh in range(KV_HEADS):
        impT = jnp.transpose(imp_ref[:, kvh * nsl:(kvh + 1) * nsl])
        selT = _select_blocks(impT[0:ns_pad], cur)
        selT = jnp.concatenate([selT, jnp.zeros((nsl - ns_pad, nb), F32)], axis=0)
        sel_ref[:, kvh * nsl:(kvh + 1) * nsl] = jnp.transpose(selT).astype(jnp.int32)


def _cmp_attn_sample(q3, kcvc, past):
    nb = q3.shape[0]
    nh = kcvc.shape[1]
    ns = past // SEL_BLK + 1
    ns_pad = -(-ns // SUBLANES) * SUBLANES
    nsl = -(-ns // LANES) * LANES
    ov = jnp.pad(_overlap_T(nh, ns), ((0, nsl - ns), (0, 0))).T
    ocmp, imp = pl.pallas_call(
        functools.partial(_cmp_attn_sample_body, past),
        grid=(nb,),
        in_specs=[pl.BlockSpec((1, 1, NSA_DIM), lambda b: (b, 0, 0)),
                  pl.BlockSpec((1, nh, 256), lambda b: (b, 0, 0)),
                  pl.BlockSpec(ov.shape, lambda b: (0, 0))],
        out_specs=[pl.BlockSpec((1, 1, NSA_DIM), lambda b: (b, 0, 0)),
                   pl.BlockSpec((1, 1, KV_HEADS * nsl), lambda b: (b, 0, 0))],
        out_shape=[jax.ShapeDtypeStruct((nb, 1, NSA_DIM), F32),
                   jax.ShapeDtypeStruct((nb, 1, KV_HEADS * nsl), F32)],
        compiler_params=_cparams(("arbitrary",)),
        name="cmp_attn_sample",
    )(q3, kcvc, ov)
    sel = pl.pallas_call(
        functools.partial(_select_sample_body, past, ns_pad),
        out_shape=jax.ShapeDtypeStruct((nb, KV_HEADS * nsl), jnp.int32),
        name="select_sample",
    )(imp.reshape(nb, KV_HEADS * nsl))
    return ocmp, sel


N_SPLIT = 3


def _pos_rows(n, start=0):
    tab = np.zeros((HEAD_DIM, n), np.float32)
    k = start + np.arange(n)
    tab[0:N_SPLIT] = k // SEL_BLK
    tab[N_SPLIT:2 * N_SPLIT] = k % SEL_BLK
    return jnp.asarray(tab, BF16)


def _slope_rows():
    bf = lambda v: np.asarray(v, dtype=BF16).astype(np.float32)
    tab = np.zeros((KV_HEADS, SUBLANES, HEAD_DIM), np.float32)
    for k in range(KV_HEADS):
        for g in range(GQA):
            for c, val in enumerate((SEL_BLK * _alibi_slope(k * GQA + g), _alibi_slope(k * GQA + g))):
                rest = np.float32(val)
                for j in range(N_SPLIT):
                    piece = bf(rest)
                    tab[k, g, c * N_SPLIT + j] = piece
                    rest = np.float32(rest - piece)
    return jnp.asarray(tab, F32)


def _block_expand(ns, n):
    return jnp.asarray((np.arange(n)[None, :] // SEL_BLK == np.arange(ns)[:, None]).astype(np.float32), BF16)


def _flash_step(q4, kT_aug, vT, bias, m_scr, acc_scr):
    n, tk = q4.shape[0], kT_aug.shape[1]
    s = jnp.dot(q4, kT_aug, preferred_element_type=F32)
    rb = bias.shape[0]
    if rb in (1, n):
        s = s + bias
    else:
        s = (s.reshape(n // rb, rb, tk) + bias[None]).reshape(n, tk)
    m_old = m_scr[...]
    m_new = jnp.maximum(m_old, jnp.max(s, axis=-1, keepdims=True))
    alpha = jnp.exp2(m_old - m_new)
    p = jnp.exp2(s - jnp.concatenate([m_new] * (tk // LANES), axis=1))
    v_ones = jnp.concatenate([vT, jnp.ones((LANES - HEAD_DIM, tk), BF16)], axis=0)
    acc_scr[...] = alpha * acc_scr[...] + _nt_dot(p.astype(BF16), v_ones)
    m_scr[...] = m_new


def _flash_result(acc_scr):
    acc = acc_scr[...]
    return acc[:, :HEAD_DIM] / jnp.maximum(acc[:, HEAD_DIM:], 1e-30)


def _sel_win_body(tk, q_ref, sel_ref, g_ref, ocmp_ref, ksT_ref, vsT_ref, kwT_ref, vwT_ref, pos_ref, exp_ref, slope_ref,
                  o_ref, m_scr, acc_scr):
    i = pl.program_id(2)
    tq = q_ref.shape[1]
    t0 = i * tq
    qpos = t0 + lax.broadcasted_iota(jnp.int32, (tq, 1), 0)
    q4 = jnp.concatenate(
        [jnp.concatenate([q_ref[0, :, g * HEAD_DIM:(g + 1) * HEAD_DIM],
                          jnp.broadcast_to(slope_ref[0, g:g + 1, :], (tq, HEAD_DIM)).astype(BF16)], axis=1)
         for g in range(GQA)], axis=0)

    def reset():
        m_scr[...] = jnp.full(m_scr.shape, NEG_INF, F32)
        acc_scr[...] = jnp.zeros(acc_scr.shape, F32)

    def finish():
        return _flash_result(acc_scr)

    reset()
    sel = sel_ref[0, 0]

    def sel_step(kt, carry):
        k0 = pl.multiple_of(kt * tk, tk)
        kpos = k0 + lax.broadcasted_iota(jnp.int32, (1, tk), 1)
        chosen = jnp.dot(sel, exp_ref[:, pl.ds(k0, tk)], preferred_element_type=F32)
        bias = (jnp.where(kpos <= qpos, chosen, 0.0) - 1.0) * (-NEG_INF)
        kT_aug = jnp.concatenate([ksT_ref[0, :, pl.ds(k0, tk)].astype(BF16), pos_ref[:, pl.ds(k0, tk)]], axis=0)
        _flash_step(q4, kT_aug, vsT_ref[0, :, pl.ds(k0, tk)].astype(BF16), bias, m_scr, acc_scr)
        return carry

    lax.fori_loop(0, (t0 + tq + tk - 1) // tk, sel_step, 0)
    o_sel = finish()

    reset()
    wk = WINDOW + tq
    k0 = pl.multiple_of(jnp.maximum(t0 - WINDOW, 0), tq)
    dist = qpos - (k0 + lax.broadcasted_iota(jnp.int32, (1, wk), 1))
    bias = jnp.where(lax.bitcast_convert_type(dist, jnp.uint32) < jnp.uint32(WINDOW), 0.0, NEG_INF)
    kT_aug = jnp.concatenate([kwT_ref[0, :, pl.ds(k0, wk)].astype(BF16), pos_ref[:, pl.ds(k0, wk)]], axis=0)
    _flash_step(q4, kT_aug, vwT_ref[0, :, pl.ds(k0, wk)].astype(BF16), bias, m_scr, acc_scr)
    o_win = finish()
    gates = g_ref[0]
    for g in range(GQA):
        rows = slice(g * tq, (g + 1) * tq)
        cols = slice(g * HEAD_DIM, (g + 1) * HEAD_DIM)
        o = (gates[:, 3 * g:3 * g + 1] * ocmp_ref[0, :, cols] + gates[:, 3 * g + 1:3 * g + 2] * o_sel[rows]
             + gates[:, 3 * g + 2:3 * g + 3] * o_win[rows])
        o_ref[0, :, cols] = o.astype(o_ref.dtype)


def _sel_win_prompt(q, sel, gates, ocmp, ksT, kwT, tq, tk):
    nb, t, _ = q.shape
    ns = sel.shape[-1]
    assert t % tk == 0 and t % tq == 0 and WINDOW % tq == 0
    pos, expand, slopes = _pos_rows(t), _block_expand(ns, t), _slope_rows()
    grp = GQA * HEAD_DIM
    kv_spec = lambda which: pl.BlockSpec((1, HEAD_DIM, t), lambda b, k, i: (b, which * KV_HEADS + k, 0))
    return pl.pallas_call(
        functools.partial(_sel_win_body, tk),
        grid=(nb, KV_HEADS, t // tq),
        in_specs=[pl.BlockSpec((1, tq, grp), lambda b, k, i: (b, i, k)),
                  pl.BlockSpec((1, 1, tq, ns), lambda b, k, i: (b, k, i, 0)),
                  pl.BlockSpec((1, tq, LANES), lambda b, k, i: (b, i, k)),
                  pl.BlockSpec((1, tq, grp), lambda b, k, i: (b, i, k)),
                  kv_spec(0), kv_spec(1), kv_spec(0), kv_spec(1),
                  pl.BlockSpec(pos.shape, lambda b, k, i: (0, 0)),
                  pl.BlockSpec(expand.shape, lambda b, k, i: (0, 0)),
                  pl.BlockSpec((1, SUBLANES, HEAD_DIM), lambda b, k, i: (k, 0, 0))],
        out_specs=pl.BlockSpec((1, tq, grp), lambda b, k, i: (b, i, k)),
        out_shape=jax.ShapeDtypeStruct((nb, t, NSA_DIM), BF16),
        scratch_shapes=[pltpu.VMEM((GQA * tq, LANES), F32), pltpu.VMEM((GQA * tq, LANES), F32)],
        compiler_params=_cparams(("arbitrary", "arbitrary", "arbitrary")),
        name="sel_win_attn",
    )(q, sel, gates, ocmp, ksT, ksT, kwT, kwT, pos, expand, slopes)


def _sel_win_sample_body(past, g_pages, pt_ref, sel_ref, q_ref, g_ref, ocmp_ref, *refs):
    page_refs = refs[:g_pages]
    (win_ref, ksn_ref, kwn_ref, pos_ref, wpos_ref, slope_ref, o_ref, wout_ref, m_scr, acc_scr) = refs[g_pages:]
    b = pl.program_id(0)
    j = pl.program_id(1)
    ns_pad = sel_ref.shape[1] // KV_HEADS
    qrow = q_ref[0]
    q4 = [jnp.concatenate([_head_rows(qrow, k), slope_ref[k].astype(BF16)], axis=1) for k in range(KV_HEADS)]

    @pl.when(j == 0)
    def _():
        m_scr[...] = jnp.full(m_scr.shape, NEG_INF, F32)
        acc_scr[...] = jnp.zeros(acc_scr.shape, F32)

    width = g_pages * PAGE_SIZE
    lane_blk = lax.broadcasted_iota(jnp.int32, (1, width), 1) // SEL_BLK
    row0 = lax.broadcasted_iota(jnp.int32, (HEAD_DIM, 1), 0) < N_SPLIT
    blk0 = j * (width // SEL_BLK)
    pos = (pos_ref[...].astype(F32) + jnp.where(row0, blk0.astype(F32), 0.0)).astype(BF16)
    for k in range(KV_HEADS):
        kT = jnp.concatenate([r[0, k * HEAD_DIM:(k + 1) * HEAD_DIM, :] for r in page_refs], axis=1).astype(BF16)
        vT = jnp.concatenate([r[0, KV_DIM + k * HEAD_DIM:KV_DIM + (k + 1) * HEAD_DIM, :] for r in page_refs],
                             axis=1).astype(BF16)
        bias = jnp.full((1, width), NEG_INF, F32)
        for blk in range(width // SEL_BLK):
            chosen = sel_ref[b, k * ns_pad + blk0 + blk] > 0
            bias = jnp.where(lane_blk == blk, jnp.where(chosen, 0.0, NEG_INF), bias)
        _flash_step(q4[k], jnp.concatenate([kT, pos], axis=0), vT, bias, m_scr.at[k], acc_scr.at[k])

    @pl.when(j == pl.num_programs(1) - 1)
    def _():
        nb = ksn_ref.shape[2]
        pick = lax.broadcasted_iota(jnp.int32, (1, nb), 1) == b
        ks_new = jnp.sum(jnp.where(pick, ksn_ref[0], 0.0), axis=1, keepdims=True)
        kw_new = jnp.sum(jnp.where(pick, kwn_ref[0], 0.0), axis=1, keepdims=True)
        lane = lax.broadcasted_iota(jnp.int32, (1, LANES), 1)
        tile_new = jnp.where(lane == 0, ks_new, 0.0).astype(BF16)
        pos_new = jnp.where(row0 & (lane == 0), float(past // SEL_BLK), 0.0).astype(BF16)
        bias_new = jnp.where(lane == 0, 0.0, NEG_INF)
        wlane = lax.broadcasted_iota(jnp.int32, (1, win_ref.shape[2]), 1)
        wout = jnp.where(wlane == win_ref.shape[2] - 1, kw_new, pltpu.roll(win_ref[0], win_ref.shape[2] - 1, 1))
        wout_ref[0] = wout
        woutb = wout.astype(BF16)
        gates = g_ref[0]
        for k in range(KV_HEADS):
            ksl = slice(k * HEAD_DIM, (k + 1) * HEAD_DIM)
            vsl = slice(KV_DIM + k * HEAD_DIM, KV_DIM + (k + 1) * HEAD_DIM)
            _flash_step(q4[k], jnp.concatenate([tile_new[ksl], pos_new], axis=0), tile_new[vsl], bias_new,
                        m_scr.at[k], acc_scr.at[k])
            o_sel = _flash_result(acc_scr.at[k])
            m_scr[k] = jnp.full(m_scr.shape[1:], NEG_INF, F32)
            acc_scr[k] = jnp.zeros(acc_scr.shape[1:], F32)
            _flash_step(q4[k], jnp.concatenate([woutb[ksl], wpos_ref[...]], axis=0), woutb[vsl],
                        jnp.zeros((1, win_ref.shape[2]), F32), m_scr.at[k], acc_scr.at[k])
            o_win = _flash_result(acc_scr.at[k])
            for g in range(GQA):
                head = k * GQA + g
                cols = slice(head * HEAD_DIM, (head + 1) * HEAD_DIM)
                c0 = k * LANES + 3 * g
                o = (gates[:, c0:c0 + 1] * ocmp_ref[0, :, cols] + gates[:, c0 + 1:c0 + 2] * o_sel[g:g + 1, :]
                     + gates[:, c0 + 2:c0 + 3] * o_win[g:g + 1, :])
                o_ref[0, :, cols] = o.astype(o_ref.dtype)


def _sel_win_sample(q3, g3, ocmp, sel, pool_sel, page_table, win, ksT_new, kwT_new, past):
    nb, n_pages = page_table.shape
    wlen = win.shape[2]
    assert wlen == WINDOW and past >= WINDOW
    g = min(64, n_pages)
    pos, wpos, slopes = _pos_rows(g * PAGE_SIZE), _pos_rows(wlen, past - wlen + 1), _slope_rows()
    full = lambda a: pl.BlockSpec(a.shape, lambda b, j, pt, sl: (0,) * a.ndim)
    row = lambda n: pl.BlockSpec((1, 1, n), lambda b, j, pt, sl: (b, 0, 0))
    page_spec = lambda k: pl.BlockSpec((1, 256, PAGE_SIZE), lambda b, j, pt, sl: (pt[b, j * g + k], 0, 0))
    wspec = pl.BlockSpec((1, 256, wlen), lambda b, j, pt, sl: (b, 0, 0))
    return pl.pallas_call(
        functools.partial(_sel_win_sample_body, past, g),
        grid_spec=pltpu.PrefetchScalarGridSpec(
            num_scalar_prefetch=2,
            grid=(nb, n_pages // g),
            in_specs=[row(NSA_DIM), row(KV_HEADS * LANES), row(NSA_DIM)] + [page_spec(k) for k in range(g)]
                     + [wspec, full(ksT_new), full(kwT_new), full(pos), full(wpos), full(slopes)],
            out_specs=[row(NSA_DIM), wspec],
            scratch_shapes=[pltpu.VMEM((KV_HEADS, SUBLANES, LANES), F32), pltpu.VMEM((KV_HEADS, SUBLANES, LANES), F32)],
        ),
        out_shape=[jax.ShapeDtypeStruct((nb, 1, NSA_DIM), BF16), jax.ShapeDtypeStruct((nb, 256, wlen), F32)],
        compiler_params=_cparams(("arbitrary", "arbitrary")),
        name="sel_win_sample",
    )(page_table, sel, q3, g3, ocmp, *([pool_sel] * g), win, ksT_new, kwT_new, pos,
      wpos, slopes)


def _softplus(v):
    return jnp.maximum(v, 0.0) + jnp.log1p(jnp.exp(-jnp.abs(v)))


def _tn_dot(a, b):
    return lax.dot_general(a, b, (((0,), (0,)), ((), ())), preferred_element_type=F32)


def _prep_ssd(conv_w, conv_b, dt_bias, a_log, d_skip, ssd_norm):
    padl = lambda v: jnp.pad(v.reshape(1, -1), ((0, 0), (0, LANES - v.shape[0])))
    L = SSD_CHUNK
    tril = jnp.asarray(np.tril(np.ones((L, L), np.float32)))
    return dict(conv_w=conv_w, conv_b=conv_b.reshape(1, -1), dtb_row=padl(dt_bias), dtb_col=dt_bias.reshape(-1, 1),
                alog_row=padl(a_log), alog_col=a_log.reshape(-1, 1), dskip=padl(d_skip), norm=ssd_norm.reshape(1, -1),
                tril=tril, triu=tril.T)


def _ssd_chunk(u, z, dt, dtT, h_prev, P):
    a_row = -jnp.exp(P["alog_row"][...])
    a_col = -jnp.exp(P["alog_col"][...])
    acum = jnp.dot(P["tril"][...], dt * a_row, preferred_element_type=F32, precision=HIGHEST)
    acumT = jnp.dot(dtT * a_col, P["triu"][...], preferred_element_type=F32, precision=HIGHEST)
    L = u.shape[0]
    li = lax.broadcasted_iota(jnp.int32, (L, L), 0)
    si = lax.broadcasted_iota(jnp.int32, (L, L), 1)
    causal = li >= si
    gn = SSD_GROUPS * D_STATE
    ys, hs = [], []
    per = SSD_HEADS // SSD_GROUPS
    for g in range(SSD_GROUPS):
        bm = u[:, D_INNER + g * D_STATE:D_INNER + (g + 1) * D_STATE]
        cm = u[:, D_INNER + gn + g * D_STATE:D_INNER + gn + (g + 1) * D_STATE]
        bmb = bm.astype(BF16)
        cb = _nt_dot(cm.astype(BF16), bmb)
        for e in range(per):
            h = g * per + e
            ac = acum[:, h:h + 1]
            seg = ac - acumT[h:h + 1, :]
            decay = jnp.where(causal, jnp.exp(jnp.where(causal, seg, 0.0)), 0.0)
            xs = u[:, h * SSD_HEAD_DIM:(h + 1) * SSD_HEAD_DIM]
            xdt = xs * dt[:, h:h + 1]
            y = jnp.dot((cb * decay).astype(BF16), xdt.astype(BF16), preferred_element_type=F32)
            a_last = acum[L - 1:L, h:h + 1]
            st = _tn_dot((xdt * jnp.exp(a_last - ac)).astype(BF16), bmb)
            y = y + _nt_dot((cm * jnp.exp(ac)).astype(BF16), h_prev[h].astype(BF16))
            hs.append(jnp.exp(a_last) * h_prev[h] + st)
            ys.append(y + P["dskip"][:, h:h + 1] * xs)
    return ys, hs


def _ssd_finish(ys, z, norm_w):
    y = jnp.concatenate(ys, axis=1) * _silu(z)
    ms = jnp.mean(y * y, axis=-1, keepdims=True)
    return y * lax.rsqrt(ms + RMS_EPS) * norm_w


def _ssd_prompt_body(xbc_ref, z_ref, dt_ref, dtT_ref, cw_ref, cb_ref, dtbr_ref, dtbc_ref, alr_ref, alc_ref, dsk_ref,
                     nrm_ref, tril_ref, triu_ref, y_ref, hout_ref, xpad_scr, h_scr):
    c = pl.program_id(1)
    L = xbc_ref.shape[1]

    @pl.when(c == 0)
    def _():
        xpad_scr[0:SUBLANES, :] = jnp.zeros((SUBLANES, xpad_scr.shape[1]), F32)
        h_scr[...] = jnp.zeros(h_scr.shape, F32)

    xt = xbc_ref[0]
    xpad_scr[SUBLANES:SUBLANES + L, :] = xt
    conv = cb_ref[...] + xpad_scr[SUBLANES - (CONV_W - 1):SUBLANES - (CONV_W - 1) + L, :] * cw_ref[0:1, :]
    for k in range(1, CONV_W):
        o = SUBLANES - (CONV_W - 1) + k
        conv = conv + xpad_scr[o:o + L, :] * cw_ref[k:k + 1, :]
    xpad_scr[0:SUBLANES, :] = xt[L - SUBLANES:L, :]
    u = _silu(conv)
    dt = _softplus(dt_ref[0] + dtbr_ref[...])
    dtT = _softplus(dtT_ref[0] + dtbc_ref[...])
    P = dict(alog_row=alr_ref, alog_col=alc_ref, tril=tril_ref, triu=triu_ref, dskip=dsk_ref[...])
    ys, hs = _ssd_chunk(u, z_ref[0], dt, dtT, [h_scr[h] for h in range(SSD_HEADS)], P)
    for h in range(SSD_HEADS):
        h_scr[h] = hs[h]
    y_ref[0] = _ssd_finish(ys, z_ref[0], nrm_ref[...]).astype(y_ref.dtype)

    @pl.when(c == pl.num_programs(1) - 1)
    def _():
        hout_ref[0] = h_scr[...]


def _ssd_prompt(xbc, z, dt, dtT, SP):
    nb, t, cd = xbc.shape
    L = SSD_CHUNK
    assert t % L == 0
    full = lambda a: pl.BlockSpec(a.shape, lambda b, c: (0,) * a.ndim)
    names = ("conv_w", "conv_b", "dtb_row", "dtb_col", "alog_row", "alog_col", "dskip", "norm", "tril", "triu")
    ps = [SP[n] for n in names]
    return pl.pallas_call(
        _ssd_prompt_body,
        grid=(nb, t // L),
        in_specs=[pl.BlockSpec((1, L, cd), lambda b, c: (b, c, 0)),
                  pl.BlockSpec((1, L, D_INNER), lambda b, c: (b, c, 0)),
                  pl.BlockSpec((1, L, LANES), lambda b, c: (b, c, 0)),
                  pl.BlockSpec((1, SSD_HEADS, L), lambda b, c: (b, 0, c))] + [full(a) for a in ps],
        out_specs=[pl.BlockSpec((1, L, D_INNER), lambda b, c: (b, c, 0)),
                   pl.BlockSpec((1, SSD_HEADS, SSD_HEAD_DIM, D_STATE), lambda b, c: (b, 0, 0, 0))],
        out_shape=[jax.ShapeDtypeStruct((nb, t, D_INNER), BF16),
                   jax.ShapeDtypeStruct((nb, SSD_HEADS, SSD_HEAD_DIM, D_STATE), F32)],
        scratch_shapes=[pltpu.VMEM((SUBLANES + L, cd), F32), pltpu.VMEM((SSD_HEADS, SSD_HEAD_DIM, D_STATE), F32)],
        compiler_params=_cparams(("arbitrary", "arbitrary")),
        name="ssd_prompt",
    )(xbc, z, dt, dtT, *ps)


def _ssd_sample_body(cs_ref, xbc_ref, z_ref, dt_ref, h0_ref, cw_ref, cb_ref, dtb_ref, al_ref, dsk_ref, nrm_ref, eye_ref,
                     y_ref, cso_ref, h_ref):
    nseq = xbc_ref.shape[0]
    xn = xbc_ref[...]
    conv = cb_ref[...] + xn * cw_ref[CONV_W - 1:CONV_W, :]
    for k in range(CONV_W - 1):
        conv = conv + cs_ref[k] * cw_ref[k:k + 1, :]
        if k > 0:
            cso_ref[k - 1] = cs_ref[k]
    cso_ref[CONV_W - 2] = xn
    u = _silu(conv)
    dt = _softplus(dt_ref[...] + dtb_ref[...])
    decay = jnp.exp(dt * (-jnp.exp(al_ref[...])))
    eye = eye_ref[...]
    gn = SSD_GROUPS * D_STATE
    per = SSD_HEADS // SSD_GROUPS
    rows = []
    for s in range(nseq):
        ys = []
        for h in range(SSD_HEADS):
            g = h // per
            xs = u[s:s + 1, h * SSD_HEAD_DIM:(h + 1) * SSD_HEAD_DIM]
            bm = u[s:s + 1, D_INNER + g * D_STATE:D_INNER + (g + 1) * D_STATE]
            cm = u[s:s + 1, D_INNER + gn + g * D_STATE:D_INNER + gn + (g + 1) * D_STATE]
            xcol = jnp.sum(eye * xs, axis=1, keepdims=True)
            hn = decay[s:s + 1, h:h + 1] * h0_ref[s, h] + (dt[s:s + 1, h:h + 1] * xcol) * bm
            h_ref[s, h] = hn
            ycol = jnp.sum(hn * cm, axis=1, keepdims=True)
            ys.append(jnp.sum(eye * ycol, axis=0, keepdims=True) + dsk_ref[:, h:h + 1] * xs)
        rows.append(jnp.concatenate(ys, axis=1))
    y = jnp.concatenate(rows, axis=0) * _silu(z_ref[...])
    ms = jnp.mean(y * y, axis=-1, keepdims=True)
    y_ref[...] = (y * lax.rsqrt(ms + RMS_EPS) * nrm_ref[...]).astype(y_ref.dtype)


def _ssd_sample(conv_state, xbc, z, dt, h0, SP):
    nb, cd = xbc.shape
    ts = SUBLANES
    assert nb % ts == 0
    eye = jnp.eye(SSD_HEAD_DIM, dtype=F32)
    names = ("conv_w", "conv_b", "dtb_row", "alog_row", "dskip", "norm")
    ps = [SP[n] for n in names] + [eye]
    full = lambda a: pl.BlockSpec(a.shape, lambda i: (0,) * a.ndim)
    st = pl.BlockSpec((ts, SSD_HEADS, SSD_HEAD_DIM, D_STATE), lambda i: (i, 0, 0, 0))
    cs = pl.BlockSpec((CONV_W - 1, ts, cd), lambda i: (0, i, 0))
    row = lambda n: pl.BlockSpec((ts, n), lambda i: (i, 0))
    return pl.pallas_call(
        _ssd_sample_body,
        grid=(nb // ts,),
        in_specs=[cs, row(cd), row(D_INNER), row(LANES), st] + [full(a) for a in ps],
        out_specs=[row(D_INNER), cs, st],
        out_shape=[jax.ShapeDtypeStruct((nb, D_INNER), BF16),
                   jax.ShapeDtypeStruct((CONV_W - 1, nb, cd), F32),
                   jax.ShapeDtypeStruct(h0.shape, F32)],
        compiler_params=_cparams(("arbitrary",)),
        name="ssd_sample",
    )(conv_state, xbc, z, dt, h0, *ps)


def _pack_bf16_pairs(v):
    m = v.shape[1] // 2
    hi = pltpu.bitcast(v[:, :m].astype(BF16).astype(F32), jnp.uint32)
    lo = pltpu.bitcast(v[:, m:].astype(BF16).astype(F32), jnp.uint32)
    return hi | (lo >> 16)


def _unpack_pairs_f32(w):
    return pltpu.bitcast(w & jnp.uint32(0xFFFF0000), F32), pltpu.bitcast(w << 16, F32)


def _unpack_bf16_pairs(w):
    hi, lo = _unpack_pairs_f32(w)
    return hi.astype(BF16), lo.astype(BF16)


def _route(logitsT, bias_col):
    s = jax.nn.sigmoid(logitsT)
    sb = s + bias_col
    per = N_EXPERTS // N_EXPERT_GROUPS
    assert per == SUBLANES
    grp = [sb[per * a:per * (a + 1)] for a in range(N_EXPERT_GROUPS)]
    sub = lax.broadcasted_iota(jnp.int32, grp[0].shape, 0)
    gs = []
    for ga in grp:
        m1 = jnp.max(ga, axis=0, keepdims=True)
        first = jnp.min(jnp.where(ga == m1, sub, per), axis=0, keepdims=True)
        m2 = jnp.max(jnp.where(sub == first, NEG_INF, ga), axis=0, keepdims=True)
        gs.append(m1 + m2)
    gmask = _topk_mask([jnp.concatenate(gs, axis=0)], TOPK_GROUPS)[0]
    masked = [jnp.where(gmask[a:a + 1, :] > 0.5, grp[a], NEG_INF) for a in range(N_EXPERT_GROUPS)]
    sel = jnp.concatenate(_topk_mask(masked, TOP_K, by_rounds=True), axis=0)
    w = s * sel
    w = w / jnp.sum(w, axis=0, keepdims=True) * ROUTED_SCALE
    return sel, w


def _post_mix_body(x_ref, on_ref, ys_ref, g1_ref, sc_ref, sh_ref, nw_ref, wo_ref, rw_ref, rb_ref,
                   x1_ref, hp_ref, selT_ref, wT_ref, cnt_ref):
    first = (pl.program_id(0) == 0) & (pl.program_id(1) == 0)
    half = wo_ref.shape[0] // 2
    mix = (jnp.dot(on_ref[0], wo_ref[0:half, :], preferred_element_type=F32)
           + jnp.dot(ys_ref[0], wo_ref[half:, :], preferred_element_type=F32))
    x1 = x_ref[0] + g1_ref[0] * mix
    x1_ref[0] = x1
    ms = jnp.mean(x1 * x1, axis=-1, keepdims=True)
    h = x1 * lax.rsqrt(ms + RMS_EPS) * nw_ref[...]
    h = h * (1.0 + sc_ref[0]) + sh_ref[0]
    hp_ref[0] = _pack_bf16_pairs(h)
    logitsT = lax.dot_general(rw_ref[...], h, (((1,), (1,)), ((), ())), preferred_element_type=F32,
                              precision=HIGHEST)
    sel, w = _route(logitsT, rb_ref[...])
    selT_ref[...] = sel.astype(selT_ref.dtype)
    wT_ref[...] = w

    @pl.when(first)
    def _():
        cnt_ref[...] = jnp.zeros(cnt_ref.shape, F32)

    cnt_ref[...] += jnp.broadcast_to(jnp.sum(sel, axis=1, keepdims=True), cnt_ref.shape)


def _post_mix(x, o_nsa, y_ssd, g1, sc2, sh2, norm_w, w_out_b, router_wT, router_bias, tm):
    nb, t, d = x.shape
    mt = g1.shape[1]
    nt = t // tm
    assert t % tm == 0 and (mt == 1 or mt == t)
    if mt == 1:
        mod_spec = pl.BlockSpec((1, 1, d), lambda b, i: (b, 0, 0))
    else:
        mod_spec = pl.BlockSpec((1, tm, d), lambda b, i: (b, i, 0))
    row = lambda n: pl.BlockSpec((1, tm, n), lambda b, i: (b, i, 0))
    full = lambda a: pl.BlockSpec(a.shape, lambda b, i: (0,) * a.ndim)
    tok = lambda: pl.BlockSpec((N_EXPERTS, tm), lambda b, i: (0, b * nt + i))
    rb = router_bias.reshape(N_EXPERTS, 1)
    nw = norm_w.reshape(1, d)
    return pl.pallas_call(
        _post_mix_body,
        grid=(nb, nt),
        in_specs=[row(d), row(NSA_DIM), row(D_INNER), mod_spec, mod_spec, mod_spec, full(nw), full(w_out_b),
                  full(router_wT), full(rb)],
        out_specs=[row(d), row(d // 2), tok(), tok(), pl.BlockSpec((N_EXPERTS, LANES), lambda b, i: (0, 0))],
        out_shape=[jax.ShapeDtypeStruct((nb, t, d), F32),
                   jax.ShapeDtypeStruct((nb, t, d // 2), jnp.uint32),
                   jax.ShapeDtypeStruct((N_EXPERTS, nb * t), BF16),
                   jax.ShapeDtypeStruct((N_EXPERTS, nb * t), F32),
                   jax.ShapeDtypeStruct((N_EXPERTS, LANES), F32)],
        compiler_params=_cparams(("arbitrary", "arbitrary")),
        name="post_mix",
    )(x, o_nsa, y_ssd, g1, sc2, sh2, nw, w_out_b, router_wT, rb)


MOE_SHIFT_LARGE = 10
MOE_SHIFT_SMALL = 8


def _moe_rows(n_tok, shift):
    n_blocks = (n_tok * TOP_K >> shift) + N_EXPERTS
    n_blocks_pad = -(-n_blocks // LANES) * LANES
    return n_blocks, n_blocks_pad


def _plan_body(shift, selT_ref, wT_ref, cnt_ref, triu_ref, tril_ref, eye_ref, dest_ref, w8_ref, be_ref, fill_ref,
               carry_scr, pstart_scr):
    step = pl.program_id(0)
    ne = N_EXPERTS
    block = 1 << shift

    @pl.when(step == 0)
    def _():
        cnt = cnt_ref[...]
        cnt_i = cnt.astype(jnp.int32)
        padded = (((cnt_i + (block - 1)) >> shift) << shift).astype(F32)
        pstart = jnp.dot(tril_ref[...].astype(F32), padded, preferred_element_type=F32, precision=HIGHEST)
        pstart_scr[...] = pstart
        carry_scr[...] = jnp.zeros(carry_scr.shape, F32)
        pend = pstart + padded
        nbp = be_ref.shape[1]
        starts = (lax.broadcasted_iota(jnp.int32, (1, nbp), 1) * block).astype(F32)
        below = jnp.where(pend[:, 0:1] <= starts, 1.0, 0.0)
        be_ref[...] = jnp.minimum(jnp.sum(below, axis=0, keepdims=True), float(ne - 1)).astype(jnp.int32)
        eye = eye_ref[...]
        to_row = lambda col: jnp.sum(col * eye, axis=0, keepdims=True)
        n_used = jnp.max(pend, axis=0, keepdims=True) * (1.0 / block)
        rows = jnp.concatenate([to_row(pstart + cnt), to_row(padded - cnt), n_used,
                                jnp.zeros((SUBLANES - 3, LANES), F32)], axis=0)
        fill_ref[...] = rows.astype(jnp.int32)

    sel = selT_ref[...]
    self32 = sel.astype(F32)
    rank = jnp.dot(sel, triu_ref[...], preferred_element_type=F32) + carry_scr[:, 0:1]
    carry_scr[...] += jnp.broadcast_to(jnp.sum(self32, axis=1, keepdims=True), carry_scr.shape)
    dest = pstart_scr[:, 0:1] + rank
    slot = jnp.dot(tril_ref[...], sel, preferred_element_type=F32)
    w = wT_ref[...]
    drows, wrows = [], []
    for k in range(TOP_K):
        pick = jnp.where(slot == float(k), self32, 0.0)
        drows.append(jnp.sum(pick * dest, axis=0, keepdims=True))
        wrows.append(jnp.sum(pick * w, axis=0, keepdims=True))
    dest_ref[...] = jnp.concatenate(drows, axis=0).astype(jnp.int32)
    w8_ref[...] = jnp.concatenate(wrows, axis=0)


def _moe_plan(selT, wT, cnt, tile, shift):
    ne, n = selT.shape
    assert n % tile == 0
    _, nbp = _moe_rows(n, shift)
    triu = jnp.asarray(np.triu(np.ones((tile, tile), np.float32), 1), BF16)
    tril = jnp.asarray(np.tril(np.ones((ne, ne), np.float32), -1), BF16)
    eye = jnp.asarray(np.eye(ne, LANES, dtype=np.float32))
    full = lambda a: pl.BlockSpec(a.shape, lambda i: (0,) * a.ndim)
    return pl.pallas_call(
        functools.partial(_plan_body, shift),
        grid=(n // tile,),
        in_specs=[pl.BlockSpec((ne, tile), lambda i: (0, i)), pl.BlockSpec((ne, tile), lambda i: (0, i)),
                  full(cnt), full(triu), full(tril), full(eye)],
        out_specs=[pl.BlockSpec((TOP_K, tile), lambda i: (0, i)), pl.BlockSpec((TOP_K, tile), lambda i: (0, i)),
                   pl.BlockSpec((1, nbp), lambda i: (0, 0)), pl.BlockSpec((SUBLANES, LANES), lambda i: (0, 0))],
        out_shape=[jax.ShapeDtypeStruct((TOP_K, n), jnp.int32), jax.ShapeDtypeStruct((TOP_K, n), F32),
                   jax.ShapeDtypeStruct((1, nbp), jnp.int32), jax.ShapeDtypeStruct((SUBLANES, LANES), jnp.int32)],
        scratch_shapes=[pltpu.VMEM((ne, LANES), F32), pltpu.VMEM((ne, LANES), F32)],
        compiler_params=_cparams(("arbitrary",)),
        name="moe_plan",
    )(selT, wT, cnt, triu, tril, eye)


def _fill_pieces(shift):
    return tuple(1 << s for s in reversed(range(shift)))


def _fill_padding(fill_ref, xd_ref, zero_scr, zsem, wait):
    def per_expert(e, carry):
        start = fill_ref[0, e]
        n = fill_ref[1, e]
        head = n & (SUBLANES - 1)
        for r in range(SUBLANES - 1):
            @pl.when(r < head)
            def _():
                cp = pltpu.make_async_copy(zero_scr.at[pl.ds(0, 1)], xd_ref.at[pl.ds(start + r, 1)], zsem)
                cp.wait() if wait else cp.start()

        cur = start + head
        for p in _fill_pieces(zero_scr.shape[0].bit_length()):
            if p < SUBLANES:
                continue
            hit = (n & p) != 0

            @pl.when(hit)
            def _():
                off = pl.multiple_of(cur, SUBLANES)
                cp = pltpu.make_async_copy(zero_scr.at[pl.ds(0, p)], xd_ref.at[pl.ds(off, p)], zsem)
                cp.wait() if wait else cp.start()

            cur = cur + jnp.where(hit, p, 0)
        return carry

    lax.fori_loop(0, N_EXPERTS, per_expert, 0)


def _dispatch_body(dest_ref, fill_ref, hp_ref, xd_ref, zero_scr, sem, zsem):
    step = pl.program_id(0)
    tile = hp_ref.shape[0]

    def row_copy(t, k):
        return pltpu.make_async_copy(hp_ref.at[pl.ds(t, 1)], xd_ref.at[pl.ds(dest_ref[k, t], 1)], sem)

    @pl.when(step == 0)
    def _():
        zero_scr[...] = jnp.zeros(zero_scr.shape, zero_scr.dtype)
        _fill_padding(fill_ref, xd_ref, zero_scr, zsem, False)
        _fill_padding(fill_ref, xd_ref, zero_scr, zsem, True)

    def issue(t, carry):
        for k in range(TOP_K):
            row_copy(t, k).start(priority=k % 2)
        return carry

    def drain(t, carry):
        for k in range(TOP_K):
            row_copy(t, k).wait()
        return carry

    lax.fori_loop(0, tile, issue, 0)
    lax.fori_loop(0, tile, drain, 0)


def _moe_dispatch(hp, dest8, fill, tile, shift):
    n, m = hp.shape
    n_blocks, _ = _moe_rows(n, shift)
    nr = n_blocks << shift
    return pl.pallas_call(
        _dispatch_body,
        grid=(n // tile,),
        in_specs=[pl.BlockSpec((TOP_K, tile), lambda i: (0, i), memory_space=pltpu.SMEM),
                  pl.BlockSpec(memory_space=pltpu.SMEM),
                  pl.BlockSpec((tile, m), lambda i: (i, 0))],
        out_specs=pl.BlockSpec(memory_space=pl.ANY),
        out_shape=jax.ShapeDtypeStruct((nr, m), jnp.uint32),
        scratch_shapes=[pltpu.VMEM((_fill_pieces(shift)[0], m), jnp.uint32), pltpu.SemaphoreType.DMA(()),
                        pltpu.SemaphoreType.DMA(())],
        compiler_params=_cparams(("arbitrary",)),
        name="moe_dispatch",
    )(dest8, fill, hp)


def _swiglu_packed(xw, w1, w3, w2):
    xa, xb = _unpack_bf16_pairs(xw)
    half = xa.shape[1]
    mm = lambda w: (jnp.dot(xa, w[0:half, :], preferred_element_type=F32)
                    + jnp.dot(xb, w[half:, :], preferred_element_type=F32))
    hid = _silu(mm(w1)) * mm(w3)
    return jnp.dot(hid.astype(BF16), w2, preferred_element_type=F32)


def _experts_body(be_ref, nu_ref, xd_ref, w1_ref, w3_ref, w2_ref, yd_ref, w1_scr, w3_scr, w2_scr):
    i = pl.program_id(0)
    last = jnp.minimum(i, nu_ref[0] - 1)
    fresh = (i == 0) | (be_ref[last] != be_ref[jnp.maximum(last - 1, 0)])

    @pl.when(fresh)
    def _():
        w1_scr[...] = w1_ref[0].astype(BF16)
        w3_scr[...] = w3_ref[0].astype(BF16)
        w2_scr[...] = w2_ref[0].astype(BF16)

    @pl.when(i < nu_ref[0])
    def _():
        yd_ref[...] = _pack_bf16_pairs(_swiglu_packed(xd_ref[...], w1_scr[...], w3_scr[...], w2_scr[...]))

    @pl.when(i >= nu_ref[0])
    def _():
        yd_ref[...] = jnp.zeros(yd_ref.shape, yd_ref.dtype)


def _moe_experts(xd, block_e, n_used, w1b, w3b, w2b, shift):
    nr, m = xd.shape
    n_blocks = nr >> shift
    block = 1 << shift
    d, f = w1b.shape[1:]
    clamp = lambda i, nu: jnp.minimum(i, nu[0] - 1)
    return pl.pallas_call(
        _experts_body,
        grid_spec=pltpu.PrefetchScalarGridSpec(
            num_scalar_prefetch=2,
            grid=(n_blocks,),
            in_specs=[pl.BlockSpec((block, m), lambda i, be, nu: (clamp(i, nu), 0)),
                      pl.BlockSpec((1, d, f), lambda i, be, nu: (be[clamp(i, nu)], 0, 0)),
                      pl.BlockSpec((1, d, f), lambda i, be, nu: (be[clamp(i, nu)], 0, 0)),
                      pl.BlockSpec((1, f, d), lambda i, be, nu: (be[clamp(i, nu)], 0, 0))],
            out_specs=pl.BlockSpec((block, m), lambda i, be, nu: (i, 0)),
            scratch_shapes=[pltpu.VMEM((d, f), BF16), pltpu.VMEM((d, f), BF16), pltpu.VMEM((f, d), BF16)],
        ),
        out_shape=jax.ShapeDtypeStruct((nr, m), jnp.uint32),
        compiler_params=_cparams(("arbitrary",)),
        name="moe_experts",
    )(block_e, n_used, xd, w1b, w3b, w2b)


def _combine_body(dest_ref, w8_ref, hp_ref, x1_ref, g2_ref, eye_ref, sw1_ref, sw3_ref, sw2_ref, nf_ref, yd_ref,
                  o_ref, ybuf, sem):
    tile = hp_ref.shape[1]

    def row_copy(t, k):
        return pltpu.make_async_copy(yd_ref.at[pl.ds(dest_ref[k, t], 1)], ybuf.at[k, pl.ds(t, 1)], sem)

    def issue(t, carry):
        for k in range(TOP_K):
            row_copy(t, k).start(priority=k % 2)
        return carry

    def drain(t, carry):
        for k in range(TOP_K):
            row_copy(t, k).wait()
        return carry

    lax.fori_loop(0, tile, issue, 0)
    shared = _swiglu_packed(hp_ref[0], sw1_ref[...], sw3_ref[...], sw2_ref[...])
    w_rows = lax.dot_general(eye_ref[...], w8_ref[...], (((1,), (1,)), ((), ())), preferred_element_type=F32,
                             precision=HIGHEST)
    lax.fori_loop(0, tile, drain, 0)
    half = ybuf.shape[2]
    acc_a = jnp.zeros((tile, half), F32)
    acc_b = jnp.zeros((tile, half), F32)
    for k in range(TOP_K):
        ya, yb = _unpack_pairs_f32(ybuf[k])
        wk = w_rows[:, k:k + 1]
        acc_a = acc_a + wk * ya
        acc_b = acc_b + wk * yb
    routed = jnp.concatenate([acc_a, acc_b], axis=1)
    x2 = x1_ref[0] + g2_ref[0] * (routed + shared)
    ms = jnp.mean(x2 * x2, axis=-1, keepdims=True)
    o_ref[0] = x2 * lax.rsqrt(ms + RMS_EPS) * nf_ref[...]


def _moe_combine(dest8, w8, hp, x1, g2, yd, sw1b, sw3b, sw2b, norm_f, tile):
    nb, t, d = x1.shape
    nt = t // tile
    mt = g2.shape[1]
    assert t % tile == 0 and (mt == 1 or mt == t)
    if mt == 1:
        mod_spec = pl.BlockSpec((1, 1, d), lambda b, i: (b, 0, 0))
    else:
        mod_spec = pl.BlockSpec((1, tile, d), lambda b, i: (b, i, 0))
    eye = jnp.eye(tile, dtype=F32)
    nf = norm_f.reshape(1, d)
    full = lambda a: pl.BlockSpec(a.shape, lambda b, i: (0,) * a.ndim)
    return pl.pallas_call(
        _combine_body,
        grid=(nb, nt),
        in_specs=[pl.BlockSpec((TOP_K, tile), lambda b, i: (0, b * nt + i), memory_space=pltpu.SMEM),
                  pl.BlockSpec((TOP_K, tile), lambda b, i: (0, b * nt + i)),
                  pl.BlockSpec((1, tile, d // 2), lambda b, i: (b, i, 0)),
                  pl.BlockSpec((1, tile, d), lambda b, i: (b, i, 0)),
                  mod_spec, full(eye), full(sw1b), full(sw3b), full(sw2b), full(nf),
                  pl.BlockSpec(memory_space=pl.ANY)],
        out_specs=pl.BlockSpec((1, tile, d), lambda b, i: (b, i, 0)),
        out_shape=jax.ShapeDtypeStruct((nb, t, d), F32),
        scratch_shapes=[pltpu.VMEM((TOP_K, tile, d // 2), jnp.uint32), pltpu.SemaphoreType.DMA(())],
        compiler_params=_cparams(("arbitrary", "arbitrary")),
        name="moe_combine",
    )(dest8, w8, hp, x1, g2, eye, sw1b, sw3b, sw2b, nf, yd)


SC_CHUNK = 128
SC_WORKERS = 32


def _sc_workers():
    info = plsc.get_sparse_core_info()
    assert info.num_cores * info.num_subcores == SC_WORKERS
    return info.num_cores, info.num_subcores


def _sc_dispatch(hp, dest8, nr):
    n, m = hp.shape
    nc, nsub = _sc_workers()
    per_w = n // (nc * nsub)
    assert n % (nc * nsub * SC_CHUNK) == 0
    mesh = plsc.VectorSubcoreMesh(core_axis_name="c", subcore_axis_name="s")

    @functools.partial(
        pl.kernel, mesh=mesh, out_type=jax.ShapeDtypeStruct((nr, m), hp.dtype),
        scratch_types=[pltpu.VMEM((TOP_K, SC_CHUNK), jnp.int32), pltpu.VMEM((SC_CHUNK, m), hp.dtype),
                       pltpu.SemaphoreType.DMA])
    def scatter_rows(hp_hbm, dest_hbm, xd_hbm, idx_v, rows_v, sem):
        wid = lax.axis_index("s") * nc + lax.axis_index("c")

        @pl.loop(0, per_w // SC_CHUNK)
        def _(c):
            base = pl.multiple_of(wid * per_w + c * SC_CHUNK, SC_CHUNK)
            pltpu.sync_copy(hp_hbm.at[pl.ds(base, SC_CHUNK)], rows_v)
            pltpu.sync_copy(dest_hbm.at[:, pl.ds(base, SC_CHUNK)], idx_v)
            copies = [pltpu.async_copy(rows_v, xd_hbm.at[idx_v.at[k]], sem) for k in range(TOP_K)]
            for cp in copies:
                cp.wait()

    return scatter_rows(hp, dest8)


def _sc_gather(yd, dest8):
    _, m = yd.shape
    n = dest8.shape[1]
    nc, nsub = _sc_workers()
    per_w = n // (nc * nsub)
    assert n % (nc * nsub * SC_CHUNK) == 0
    mesh = plsc.VectorSubcoreMesh(core_axis_name="c", subcore_axis_name="s")

    @functools.partial(
        pl.kernel, mesh=mesh, out_type=jax.ShapeDtypeStruct((TOP_K, n, m), yd.dtype),
        scratch_types=[pltpu.VMEM((TOP_K, SC_CHUNK), jnp.int32), pltpu.VMEM((SC_CHUNK, m), yd.dtype),
                       pltpu.SemaphoreType.DMA])
    def gather_rows(yd_hbm, dest_hbm, out_hbm, idx_v, rows_v, sem):
        wid = lax.axis_index("s") * nc + lax.axis_index("c")

        @pl.loop(0, per_w // SC_CHUNK)
        def _(c):
            base = pl.multiple_of(wid * per_w + c * SC_CHUNK, SC_CHUNK)
            pltpu.sync_copy(dest_hbm.at[:, pl.ds(base, SC_CHUNK)], idx_v)
            for k in range(TOP_K):
                pltpu.async_copy(yd_hbm.at[idx_v.at[k]], rows_v, sem).wait()
                pltpu.sync_copy(rows_v, out_hbm.at[k, pl.ds(base, SC_CHUNK)])

    return gather_rows(yd, dest8)


def _fill_body(fill_ref, xd_in_ref, after_ref, xd_ref, zero_scr, zsem):
    del xd_in_ref, after_ref
    zero_scr[...] = jnp.zeros(zero_scr.shape, zero_scr.dtype)
    for wait in (False, True):
        _fill_padding(fill_ref, xd_ref, zero_scr, zsem, wait)


def _moe_fill(xd, fill, after, shift):
    return pl.pallas_call(
        _fill_body,
        in_specs=[pl.BlockSpec(memory_space=pltpu.SMEM), pl.BlockSpec(memory_space=pl.ANY),
                  pl.BlockSpec(memory_space=pl.ANY)],
        out_specs=pl.BlockSpec(memory_space=pl.ANY),
        out_shape=jax.ShapeDtypeStruct(xd.shape, xd.dtype),
        scratch_shapes=[pltpu.VMEM((_fill_pieces(shift)[0], xd.shape[1]), xd.dtype), pltpu.SemaphoreType.DMA(())],
        input_output_aliases={1: 0},
        name="moe_fill",
    )(fill, xd, after)


def _combine_dense_body(w8_ref, hp_ref, x1_ref, g2_ref, eye_ref, sw1_ref, sw3_ref, sw2_ref, nf_ref, ybuf_ref, o_ref):
    tile = hp_ref.shape[1]
    shared = _swiglu_packed(hp_ref[0], sw1_ref[...], sw3_ref[...], sw2_ref[...])
    w_rows = lax.dot_general(eye_ref[...], w8_ref[...], (((1,), (1,)), ((), ())), preferred_element_type=F32,
                             precision=HIGHEST)
    half = ybuf_ref.shape[2]
    acc_a = jnp.zeros((tile, half), F32)
    acc_b = jnp.zeros((tile, half), F32)
    for k in range(TOP_K):
        ya, yb = _unpack_pairs_f32(ybuf_ref[k])
        wk = w_rows[:, k:k + 1]
        acc_a = acc_a + wk * ya
        acc_b = acc_b + wk * yb
    routed = jnp.concatenate([acc_a, acc_b], axis=1)
    x2 = x1_ref[0] + g2_ref[0] * (routed + shared)
    ms = jnp.mean(x2 * x2, axis=-1, keepdims=True)
    o_ref[0] = x2 * lax.rsqrt(ms + RMS_EPS) * nf_ref[...]


def _moe_combine_dense(w8, hp, x1, g2, ybuf, sw1b, sw3b, sw2b, norm_f, tile):
    nb, t, d = x1.shape
    nt = t // tile
    assert t % tile == 0 and g2.shape[1] == 1
    eye = jnp.eye(tile, dtype=F32)
    nf = norm_f.reshape(1, d)
    full = lambda a: pl.BlockSpec(a.shape, lambda b, i: (0,) * a.ndim)
    return pl.pallas_call(
        _combine_dense_body,
        grid=(nb, nt),
        in_specs=[pl.BlockSpec((TOP_K, tile), lambda b, i: (0, b * nt + i)),
                  pl.BlockSpec((1, tile, d // 2), lambda b, i: (b, i, 0)),
                  pl.BlockSpec((1, tile, d), lambda b, i: (b, i, 0)),
                  pl.BlockSpec((1, 1, d), lambda b, i: (b, 0, 0)),
                  full(eye), full(sw1b), full(sw3b), full(sw2b), full(nf),
                  pl.BlockSpec((TOP_K, tile, d // 2), lambda b, i: (0, b * nt + i, 0))],
        out_specs=pl.BlockSpec((1, tile, d), lambda b, i: (b, i, 0)),
        out_shape=jax.ShapeDtypeStruct((nb, t, d), F32),
        compiler_params=_cparams(("arbitrary", "arbitrary")),
        name="moe_combine_dense",
    )(w8, hp, x1, g2, eye, sw1b, sw3b, sw2b, nf, ybuf)


def _moe(x1, hp, selT, wT, cnt, g2, EW, norm_f, tile, on_sparsecore, after=None):
    nb, t, d = x1.shape
    n = nb * t
    shift = MOE_SHIFT_LARGE if n * TOP_K >= N_EXPERTS << MOE_SHIFT_LARGE else MOE_SHIFT_SMALL
    dest8, w8, block_e, fill = _moe_plan(selT, wT, cnt, tile, shift)
    n_blocks, _ = _moe_rows(n, shift)
    hp2 = hp.reshape(n, d // 2)
    if on_sparsecore:
        xd = _moe_fill(_sc_dispatch(hp2, dest8, n_blocks << shift), fill, after, shift)
    else:
        xd = _moe_dispatch(hp2, dest8, fill, tile, shift)
    yd = _moe_experts(xd, block_e[0, :n_blocks], fill[2, 0:1], EW["w1"], EW["w3"], EW["w2"], shift)
    if on_sparsecore:
        return _moe_combine_dense(w8, hp, x1, g2, _sc_gather(yd, dest8), EW["sw1"], EW["sw3"], EW["sw2"], norm_f, tile)
    return _moe_combine(dest8, w8, hp, x1, g2, yd, EW["sw1"], EW["sw3"], EW["sw2"], norm_f, tile)


def kernel(x_prompt, x_sample, c_prompt, c_sample, cache_kv_cmp, cache_kv_sel, cache_kv_win, state_conv, state_ssm, page_table, w_ada, b_ada, norm_mix, norm_ffn, w_in, cmp_pe_k, cmp_w1_k, cmp_w2_k, cmp_pe_v, cmp_w1_v, cmp_w2_v, conv_w, conv_b, dt_bias, a_log, d_skip, ssd_norm, w_out, router_w, router_bias, exp_w1, exp_w3, exp_w2, sh_w1, sh_w3, sh_w2, norm_f):
    nb, t, d = x_prompt.shape
    ndb = x_sample.shape[0]
    c_all = jnp.concatenate([c_prompt, c_sample], axis=0)
    mod = _modulation(c_all, w_ada[0], b_ada[0]).reshape(nb + ndb, 6, d)
    mod_p = [mod[:nb, k][:, None, :] for k in range(6)]
    mod_s = [mod[nb:, k][None, :, :] for k in range(6)]
    W = _prep_w_in(w_in[0])
    P = _in_proj(x_prompt, mod_p[1], mod_p[0], norm_mix[0], W, ROW_TILE)
    S = _in_proj(x_sample.reshape(1, ndb, d), mod_s[1], mod_s[0], norm_mix[0], W, ndb)
    C = _prep_compress(cmp_pe_k[0], cmp_w1_k[0], cmp_w2_k[0], cmp_pe_v[0], cmp_w1_v[0], cmp_w2_v[0])
    kcvc_p = _compress_prompt(P["kcT"], C)
    ocmp_p, sel_p = _cmp_attn_prompt(P["q"], kcvc_p, ROW_TILE)
    o_nsa_p = _sel_win_prompt(P["q"], sel_p, P["g"], ocmp_p, P["ksT"], P["kwT"], ATTN_Q_TILE, ATTN_K_TILE)
    SP = _prep_ssd(conv_w[0], conv_b[0], dt_bias[0], a_log[0], d_skip[0], ssd_norm[0])
    y_ssd_p, ssm_p = _ssd_prompt(P["xbc"], P["z"], P["dt"], P["dtT"], SP)
    w_out_b = w_out[0].astype(BF16)
    router_wT = router_w[0].T
    EW = dict(w1=exp_w1[0], w3=exp_w3[0], w2=exp_w2[0],
              sw1=sh_w1[0].astype(BF16), sw3=sh_w3[0].astype(BF16), sw2=sh_w2[0].astype(BF16))
    x1_p, hp_p, selT_p, wT_p, cnt_p = _post_mix(x_prompt, o_nsa_p, y_ssd_p, mod_p[2], mod_p[4], mod_p[3], norm_ffn[0],
                                                 w_out_b, router_wT, router_bias[0], 2 * ROW_TILE)
    past = page_table.shape[1] * PAGE_SIZE
    to_pages = lambda c: jnp.transpose(c, (0, 2, 3, 4, 1)).reshape(c.shape[0], 256, c.shape[1])
    pool_cmp, pool_sel, win = to_pages(cache_kv_cmp[0]), to_pages(cache_kv_sel[0]), to_pages(cache_kv_win[0])
    kcvc_s = _compress_paged(pool_cmp, page_table, C)

    big = (nb * t) % (SC_WORKERS * SC_CHUNK) == 0
    y_prompt = _moe(x1_p, hp_p, selT_p, wT_p, cnt_p, mod_p[5], EW, norm_f, ROW_TILE, big, after=kcvc_s)

    q3 = S["q"].reshape(ndb, 1, NSA_DIM)
    ocmp_s, sel_s = _cmp_attn_sample(q3, kcvc_s, past)
    o_nsa_s, win_s = _sel_win_sample(q3, S["g"].reshape(ndb, 1, KV_HEADS * LANES), ocmp_s, sel_s, pool_sel, page_table,
                                     win, S["ksT"], S["kwT"], past)
    y_ssd_s, conv_s, ssm_s = _ssd_sample(jnp.transpose(state_conv[0], (1, 0, 2)), S["xbc"][0], S["z"][0], S["dt"][0],
                                         state_ssm[0], SP)
    x1_s, hp_s, selT_s, wT_s, cnt_s = _post_mix(x_sample.reshape(1, ndb, d), o_nsa_s.reshape(1, ndb, NSA_DIM),
                                                 y_ssd_s.reshape(1, ndb, D_INNER), mod_s[2], mod_s[4], mod_s[3],
                                                 norm_ffn[0], w_out_b, router_wT, router_bias[0], ndb)
    y_sample = _moe(x1_s, hp_s, selT_s, wT_s, cnt_s, mod_s[5], EW, norm_f, ndb, False).reshape(ndb, 1, d)

    from_cm = lambda a: jnp.transpose(a.reshape(a.shape[0], 2, KV_HEADS, HEAD_DIM, a.shape[2]), (0, 4, 1, 2, 3))[None]
    tw = min(WINDOW, t)
    return (y_prompt, y_sample,
            from_cm(P["kcT"]), from_cm(P["ksT"]), from_cm(P["kwT"][:, :, t - tw:]),
            P["xbc"][:, t - (CONV_W - 1):, :][None], ssm_p[None],
            from_cm(jnp.transpose(S["kcT"], (2, 1, 0))), from_cm(jnp.transpose(S["ksT"], (2, 1, 0))), from_cm(win_s),
            jnp.transpose(conv_s, (1, 0, 2))[None], ssm_s[None])
```
